```python
import jax, jax.numpy as jnp
from jax import lax
import numpy as np

D_MODEL = 1024
BATCH = 32
SEQ = 256
DEPTH = 2
DEC_BATCH = 8
DEC_SEQ = 1024
PAST_LEN = 256

GRID_W = 64
CONV_DIM = 512
CONV_K = 3
NA_HEADS = 8
NA_HD = 64
NA_WIN_R = 8
NA_WIN_C = 16
MLA_HEADS = 8
MLA_NOPE = 64
MLA_ROPE = 32
MLA_V = 64
Q_LORA = 256
KV_LORA = 128
FFN_DIM = 2816
N_BRANCH = 3
N_MOD = 9
ROPE_BASE = 10000.0
EPS = 1e-6
Q_BLOCK = 128
NEG_INF = -1e30
MLA_SCALE = (MLA_NOPE + MLA_ROPE) ** -0.5
NA_SCALE = NA_HD ** -0.5
IN_SPLITS = (CONV_DIM, CONV_DIM, CONV_DIM,
             NA_HEADS * NA_HD, NA_HEADS * NA_HD, NA_HEADS * NA_HD,
             Q_LORA, KV_LORA, MLA_ROPE,
             D_MODEL, D_MODEL, D_MODEL)
IN_DIM = 3 * CONV_DIM + 3 * NA_HEADS * NA_HD + Q_LORA + KV_LORA + MLA_ROPE + N_BRANCH * D_MODEL

kernel_name = "hybrid_diffusion_conv_na_mla_step"


def rmsnorm(x, g):
    xf = x.astype(jnp.float32)
    y = xf * lax.rsqrt(jnp.mean(xf * xf, axis=-1, keepdims=True) + EPS)
    return (y * g.astype(jnp.float32)).astype(x.dtype)


def modulate(h, shift, scale):
    return h * (1 + scale) + shift


def swiglu(h, w_gate, w_up, w_down):
    return (jax.nn.silu(h @ w_gate) * (h @ w_up)) @ w_down


def split_cols(u, sizes):
    out, off = [], 0
    for n in sizes:
        out.append(u[..., off:off + n])
        off += n
    return out


def to_heads(t, n_heads):
    b, s, _ = t.shape
    return t.reshape(b, s, n_heads, -1).transpose(0, 2, 1, 3)


def from_heads(t):
    b, h, s, d = t.shape
    return t.transpose(0, 2, 1, 3).reshape(b, s, h * d)


def short_conv(u, w, bias):
    s = u.shape[1]
    pad = CONV_K // 2
    up = jnp.pad(u, ((0, 0), (pad, pad), (0, 0)))
    y = bias
    for i in range(CONV_K):
        y = y + up[:, i:i + s] * w[i]
    return y


def axial_rope(x, rows, cols):
    half = MLA_ROPE // 2
    nf = half // 2
    inv = 1.0 / (ROPE_BASE ** (jnp.arange(nf, dtype=jnp.float32) / nf))

    def rot(xh, pos):
        ang = pos.astype(jnp.float32)[:, None] * inv[None, :]
        cos = jnp.cos(ang).astype(x.dtype)
        sin = jnp.sin(ang).astype(x.dtype)
        x1, x2 = xh[..., :nf], xh[..., nf:]
        return jnp.concatenate([x1 * cos - x2 * sin, x1 * sin + x2 * cos], axis=-1)

    return jnp.concatenate([rot(x[..., :half], rows), rot(x[..., half:], cols)], axis=-1)


def attend_blocked(q, k, v, scale):
    b, h, sq, d = q.shape
    nb = sq // Q_BLOCK
    qb = q.reshape(b, h, nb, Q_BLOCK, d).transpose(2, 0, 1, 3, 4)

    def one(qi):
        s = jnp.einsum('bhqd,bhkd->bhqk', qi, k).astype(jnp.float32) * scale
        p = jax.nn.softmax(s, axis=-1).astype(v.dtype)
        return jnp.einsum('bhqk,bhkd->bhqd', p, v)

    o = lax.map(one, qb)
    return o.transpose(1, 2, 0, 3, 4).reshape(b, h, sq, v.shape[-1])


def neighbourhood_attention(q, k, v, k_ctx, v_ctx, rpb):
    b, h, s, hd = q.shape
    rows = s // GRID_W
    kr = min(NA_WIN_R, rows)
    r = jnp.arange(rows)
    col = jnp.arange(GRID_W)
    row_idx = jnp.clip(r - kr // 2, 0, rows - kr)[:, None] + jnp.arange(kr)[None, :]
    col_start = jnp.clip(col - NA_WIN_C // 2, 0, GRID_W - NA_WIN_C)
    col_rel = col[None, :] - col_start[:, None]
    col_in = (col_rel >= 0) & (col_rel < NA_WIN_C)
    dr = row_idx - r[:, None] + (NA_WIN_R - 1)
    dc = jnp.clip(col[None, :] - col[:, None] + (NA_WIN_C - 1), 0, 2 * NA_WIN_C - 2)
    bias = rpb[:, dr[:, None, :, None], dc[None, :, None, :]].astype(jnp.float32)
    qg = q.reshape(b, h, rows, GRID_W, hd)
    kg = k.reshape(b, h, rows, GRID_W, hd)[:, :, row_idx]
    vg = v.reshape(b, h, rows, GRID_W, hd)[:, :, row_idx]
    s_loc = jnp.einsum('bhrqd,bhrikd->bhrqik', qg, kg).astype(jnp.float32) * NA_SCALE + bias[None]
    s_loc = jnp.where(col_in[:, None, :], s_loc, NEG_INF)
    s_loc = s_loc.reshape(b, h, rows, GRID_W, kr * GRID_W)
    s_ctx = jnp.einsum('bhrqd,bhpd->bhrqp', qg, k_ctx).astype(jnp.float32) * NA_SCALE
    prob = jax.nn.softmax(jnp.concatenate([s_loc, s_ctx], axis=-1), axis=-1).astype(v.dtype)
    p_loc = prob[..., :kr * GRID_W].reshape(b, h, rows, GRID_W, kr, GRID_W)
    p_ctx = prob[..., kr * GRID_W:]
    o = (jnp.einsum('bhrqik,bhrikd->bhrqd', p_loc, vg)
         + jnp.einsum('bhrqp,bhpd->bhrqd', p_ctx, v_ctx))
    return o.reshape(b, h, s, hd)


def mla_expand(c_kv, k_r, w_ukv):
    kv = to_heads(c_kv @ w_ukv, MLA_HEADS)
    k_nope, v = kv[..., :MLA_NOPE], kv[..., MLA_NOPE:]
    b, h, s, _ = k_nope.shape
    k_rope = jnp.broadcast_to(k_r[:, None], (b, h, s, MLA_ROPE))
    return jnp.concatenate([k_nope, k_rope], axis=-1), v


def layer(x, mod, p, ctx=None):
    s = x.shape[1]
    m = lambda i: mod[:, :, i]
    ng = p['norm_g']
    h = modulate(rmsnorm(x, ng[0]), m(0), m(1))
    x = x + 0.5 * m(2) * swiglu(h, p['w_ffn1_gate'], p['w_ffn1_up'], p['w_ffn1_down'])
    h = modulate(rmsnorm(x, ng[1]), m(3), m(4))
    (b_g, c_g, x_c, q_na, k_na, v_na, c_q, c_kv, k_r,
     g_conv, g_na, g_mla) = split_cols(h @ p['w_in'], IN_SPLITS)
    y_conv = b_g * short_conv(c_g * x_c, p['conv_w'], p['conv_b'])
    q_na, k_na, v_na = to_heads(q_na, NA_HEADS), to_heads(k_na, NA_HEADS), to_heads(v_na, NA_HEADS)
    q_m = to_heads(rmsnorm(c_q, p['mla_qnorm']) @ p['w_uq'], MLA_HEADS)
    c_kv = rmsnorm(c_kv, p['mla_kvnorm'])
    if ctx is None:
        o_na = attend_blocked(q_na, k_na, v_na, NA_SCALE)
        k_m, v_m = mla_expand(c_kv, k_r, p['w_ukv'])
        o_m = attend_blocked(q_m, k_m, v_m, MLA_SCALE)
        new = (k_na, v_na, c_kv, k_r)
    else:
        ck_na, cv_na, cc_kv, ck_r = ctx
        o_na = neighbourhood_attention(q_na, k_na, v_na, ck_na, cv_na, p['na_rpb'])
        t = jnp.arange(s)
        rows, cols = t // GRID_W, t % GRID_W
        q_m = jnp.concatenate([q_m[..., :MLA_NOPE], axial_rope(q_m[..., MLA_NOPE:], rows, cols)], axis=-1)
        k_lat, v_lat = mla_expand(c_kv, axial_rope(k_r, rows, cols), p['w_ukv'])
        k_ctx, v_ctx = mla_expand(cc_kv, ck_r, p['w_ukv'])
        o_m = attend_blocked(q_m, jnp.concatenate([k_ctx, k_lat], axis=2),
                             jnp.concatenate([v_ctx, v_lat], axis=2), MLA_SCALE)
        new = None
    z = (jax.nn.sigmoid(g_conv) * (y_conv @ p['w_conv_out'])
         + jax.nn.sigmoid(g_na) * (from_heads(o_na) @ p['w_na_out'])
         + jax.nn.sigmoid(g_mla) * (from_heads(o_m) @ p['w_mla_out']))
    x = x + m(5) * (z @ p['w_o'])
    h = modulate(rmsnorm(x, ng[2]), m(6), m(7))
    x = x + 0.5 * m(8) * swiglu(h, p['w_ffn2_gate'], p['w_ffn2_up'], p['w_ffn2_down'])
    return x, new


def setup_inputs(seed: int = 0) -> dict:
    key = jax.random.key(seed)
    ks = jax.random.split(key, 32)
    f32 = jnp.float32

    def nrm(k, shape, scale):
        return jax.random.normal(k, shape, f32) * scale

    D, L, F = D_MODEL, DEPTH, FFN_DIM
    return {
        'x_prompt': nrm(ks[0], (BATCH, SEQ, D), 1.0),
        'x_sample': nrm(ks[1], (DEC_BATCH, DEC_SEQ, D), 1.0),
        'cache_na_k': nrm(ks[2], (DEC_BATCH, L, NA_HEADS, PAST_LEN, NA_HD), 1.0),
        'cache_na_v': nrm(ks[3], (DEC_BATCH, L, NA_HEADS, PAST_LEN, NA_HD), 1.0),
        'cache_mla_ckv': nrm(ks[4], (DEC_BATCH, L, PAST_LEN, KV_LORA), 1.0),
        'cache_mla_krope': nrm(ks[5], (DEC_BATCH, L, PAST_LEN, MLA_ROPE), 1.0),
        'c': nrm(ks[6], (DEC_BATCH, D), 1.0),
        'c_ctx': nrm(ks[7], (D,), 1.0),
        'w_ada': nrm(ks[8], (L, D, N_MOD * D), 0.3 * D ** -0.5),
        'b_ada': nrm(ks[9], (L, N_MOD * D), 0.01),
        'norm_g': 1.0 + nrm(ks[10], (L, 3, D), 0.02),
        'w_ffn1_gate': nrm(ks[11], (L, D, F), D ** -0.5),
        'w_ffn1_up': nrm(ks[12], (L, D, F), D ** -0.5),
        'w_ffn1_down': nrm(ks[13], (L, F, D), F ** -0.5),
        'w_ffn2_gate': nrm(ks[14], (L, D, F), D ** -0.5),
        'w_ffn2_up': nrm(ks[15], (L, D, F), D ** -0.5),
        'w_ffn2_down': nrm(ks[16], (L, F, D), F ** -0.5),
        'w_in': nrm(ks[17], (L, D, IN_DIM), D ** -0.5),
        'conv_w': nrm(ks[18], (L, CONV_K, CONV_DIM), CONV_K ** -0.5),
        'conv_b': nrm(ks[19], (L, CONV_DIM), 0.01),
        'na_rpb': nrm(ks[20], (L, NA_HEADS, 2 * NA_WIN_R - 1, 2 * NA_WIN_C - 1), 0.1),
        'mla_qnorm': 1.0 + nrm(ks[21], (L, Q_LORA), 0.02),
        'w_uq': nrm(ks[22], (L, Q_LORA, MLA_HEADS * (MLA_NOPE + MLA_ROPE)), Q_LORA ** -0.5),
        'mla_kvnorm': 1.0 + nrm(ks[23], (L, KV_LORA), 0.02),
        'w_ukv': nrm(ks[24], (L, KV_LORA, MLA_HEADS * (MLA_NOPE + MLA_V)), KV_LORA ** -0.5),
        'w_conv_out': nrm(ks[25], (L, CONV_DIM, D), CONV_DIM ** -0.5),
        'w_na_out': nrm(ks[26], (L, NA_HEADS * NA_HD, D), (NA_HEADS * NA_HD) ** -0.5),
        'w_mla_out': nrm(ks[27], (L, MLA_HEADS * MLA_V, D), (MLA_HEADS * MLA_V) ** -0.5),
        'w_o': nrm(ks[28], (L, D, D), D ** -0.5),
        'final_g': 1.0 + nrm(ks[29], (D,), 0.02),
    }


def reference(x_prompt, x_sample, cache_na_k, cache_na_v, cache_mla_ckv, cache_mla_krope, c, c_ctx,
              w_ada, b_ada, norm_g, w_ffn1_gate, w_ffn1_up, w_ffn1_down, w_ffn2_gate, w_ffn2_up, w_ffn2_down,
              w_in, conv_w, conv_b, na_rpb, mla_qnorm, w_uq, mla_kvnorm, w_ukv,
              w_conv_out, w_na_out, w_mla_out, w_o, final_g):
    xp, xs = x_prompt, x_sample
    new_k, new_v, new_ckv, new_kr = [], [], [], []
    for l in range(DEPTH):
        p = dict(norm_g=norm_g[l], w_ffn1_gate=w_ffn1_gate[l], w_ffn1_up=w_ffn1_up[l], w_ffn1_down=w_ffn1_down[l],
                 w_ffn2_gate=w_ffn2_gate[l], w_ffn2_up=w_ffn2_up[l], w_ffn2_down=w_ffn2_down[l],
                 w_in=w_in[l], conv_w=conv_w[l], conv_b=conv_b[l], na_rpb=na_rpb[l],
                 mla_qnorm=mla_qnorm[l], w_uq=w_uq[l], mla_kvnorm=mla_kvnorm[l], w_ukv=w_ukv[l],
                 w_conv_out=w_conv_out[l], w_na_out=w_na_out[l], w_mla_out=w_mla_out[l], w_o=w_o[l])
        mod_ctx = (jax.nn.silu(c_ctx) @ w_ada[l] + b_ada[l]).reshape(1, 1, N_MOD, D_MODEL)
        mod_lat = (jax.nn.silu(c) @ w_ada[l] + b_ada[l]).reshape(-1, 1, N_MOD, D_MODEL)
        xp, (k_na, v_na, ckv, kr) = layer(xp, mod_ctx, p)
        xs, _ = layer(xs, mod_lat, p, (cache_na_k[:, l], cache_na_v[:, l], cache_mla_ckv[:, l], cache_mla_krope[:, l]))
        new_k.append(k_na)
        new_v.append(v_na)
        new_ckv.append(ckv)
        new_kr.append(kr)
    y_prompt = rmsnorm(xp, final_g)
    y_sample = rmsnorm(xs, final_g)
    return (y_prompt, y_sample, jnp.stack(new_k, axis=1), jnp.stack(new_v, axis=1),
            jnp.stack(new_ckv, axis=1), jnp.stack(new_kr, axis=1))
```

```python
import functools

import jax
import jax.numpy as jnp
import numpy as np
from jax import lax
from jax.experimental import pallas as pl
from jax.experimental.pallas import tpu as pltpu

D_MODEL = 1024
BATCH = 32
SEQ = 256
DEPTH = 2
DEC_BATCH = 8
DEC_SEQ = 1024
PAST_LEN = 256
GRID_W = 64
CONV_DIM = 512
CONV_K = 3
NA_HEADS = 8
NA_HD = 64
NA_WIN_R = 8
NA_WIN_C = 16
MLA_HEADS = 8
MLA_NOPE = 64
MLA_ROPE = 32
MLA_V = 64
Q_LORA = 256
KV_LORA = 128
FFN_DIM = 2816
N_MOD = 9
ROPE_BASE = 10000.0
EPS = 1e-6
NEG_INF = -1e30
MLA_SCALE = (MLA_NOPE + MLA_ROPE) ** -0.5
NA_SCALE = NA_HD ** -0.5

HEADS = 8
HEAD_V = 64
MLA_QK_PAD = 128
FFN_CHUNK = 256
Q_CHUNK = 256
ATTN_ROWS = 1024
MOD_ROWS = 16
VMEM_LIMIT = 56 * 1024 * 1024

BF16 = jnp.bfloat16
F32 = jnp.float32


def _dot(a, b):
    return jnp.dot(a, b, preferred_element_type=F32)


def _dot_nt(a, b):
    return lax.dot_general(a, b, (((1,), (1,)), ((), ())), preferred_element_type=F32)


def _rms(x, g):
    return x * lax.rsqrt(jnp.mean(x * x, axis=-1, keepdims=True) + EPS) * g


def _params(n_axes):
    return pltpu.CompilerParams(dimension_semantics=("arbitrary",) * n_axes,
                                vmem_limit_bytes=VMEM_LIMIT)


def _resident(shape, index):
    return pl.BlockSpec(shape, lambda *_: index, pipeline_mode=pl.Buffered(1))


def _mod_body(c_ref, w_ref, b_ref, o_ref):
    c = c_ref[...]
    a = c * jax.nn.sigmoid(c)
    o_ref[...] = jnp.dot(a, w_ref[...], precision=lax.Precision.HIGHEST,
                         preferred_element_type=F32) + b_ref[...]


def _modulation(c_all, w_ada, b_ada):
    n_col = N_MOD * D_MODEL
    tn = D_MODEL
    return pl.pallas_call(
        _mod_body,
        out_shape=jax.ShapeDtypeStruct((DEPTH, MOD_ROWS, n_col), F32),
        grid=(DEPTH, n_col // tn),
        in_specs=[pl.BlockSpec((MOD_ROWS, D_MODEL), lambda l, j: (0, 0)),
                  pl.BlockSpec((None, D_MODEL, tn), lambda l, j: (l, 0, j)),
                  pl.BlockSpec((None, 1, tn), lambda l, j: (l, 0, j))],
        out_specs=pl.BlockSpec((None, MOD_ROWS, tn), lambda l, j: (l, 0, j)),
        compiler_params=_params(2),
        name="modulation",
    )(c_all, w_ada, b_ada.reshape(DEPTH, 1, n_col))


def _ffn_body(x_ref, mod_ref, ng_ref, wg_ref, wu_ref, wd_ref, *rest, mod_off, final):
    if final:
        fg_ref, o_ref, a_scr = rest
    else:
        o_ref, a_scr = rest
    x = x_ref[...]
    mod = mod_ref[...]
    shift = mod[mod_off:mod_off + 1]
    scale = mod[mod_off + 1:mod_off + 2]
    gate = mod[mod_off + 2:mod_off + 3]
    h = (_rms(x, ng_ref[...]) * (1 + scale) + shift).astype(BF16)
    for f in range(FFN_DIM // FFN_CHUNK):
        cols = slice(f * FFN_CHUNK, (f + 1) * FFN_CHUNK)
        g = _dot(h, wg_ref[:, cols])
        u = _dot(h, wu_ref[:, cols])
        a_scr[:, cols] = (g * jax.nn.sigmoid(g) * u).astype(BF16)
    y = _dot(a_scr[...], wd_ref[...])
    out = x + 0.5 * gate * y
    if final:
        out = _rms(out, fg_ref[...])
    o_ref[...] = out


def _ffn(x, mod, rows_per_mod, ng, wg, wu, wd, layer, mod_off, final_g=None, tm=512):
    t = x.shape[0]
    tiles_per_mod = rows_per_mod // tm
    final = final_g is not None
    in_specs = [pl.BlockSpec((tm, D_MODEL), lambda i: (i, 0)),
                pl.BlockSpec((None, N_MOD, D_MODEL), lambda i: (i // tiles_per_mod, 0, 0)),
                _resident((1, D_MODEL), (0, 0)),
                _resident((None, D_MODEL, FFN_DIM), (layer, 0, 0)),
                _resident((None, D_MODEL, FFN_DIM), (layer, 0, 0)),
                _resident((None, FFN_DIM, D_MODEL), (layer, 0, 0))]
    args = [x, mod, ng, wg, wu, wd]
    if final:
        in_specs.append(_resident((1, D_MODEL), (0, 0)))
        args.append(final_g)
    return pl.pallas_call(
        functools.partial(_ffn_body, mod_off=mod_off, final=final),
        out_shape=jax.ShapeDtypeStruct((t, D_MODEL), F32),
        grid=(t // tm,),
        in_specs=in_specs,
        out_specs=pl.BlockSpec((tm, D_MODEL), lambda i: (i, 0)),
        scratch_shapes=[pltpu.VMEM((tm, FFN_DIM), BF16)],
        compiler_params=_params(1),
        name="ffn",
    )(*args)


def _proj_body(x_ref, mod_ref, ng_ref, wa_ref, wb_ref, cw_ref, cb_ref, qn_ref, kvn_ref,
               wq_ref, wka_ref, wkb_ref, wuv_ref, *rest, seq_len, rope):
    if rope:
        (qc_ref, qs_ref, kc_ref, ks_ref,
         yc_ref, qna_ref, kna_ref, vna_ref, qm_ref, km_ref, vm_ref) = rest
    else:
        (yc_ref, qna_ref, kna_ref, vna_ref, qm_ref, km_ref, vm_ref, ckv_ref, kr_ref) = rest
    tm = x_ref.shape[0]
    x = x_ref[...]
    mod = mod_ref[...]
    h = (_rms(x, ng_ref[...]) * (1 + mod[4:5]) + mod[3:4]).astype(BF16)

    u = _dot(h, wa_ref[:, 0:3 * CONV_DIM])
    v = u[:, CONV_DIM:2 * CONV_DIM] * u[:, 2 * CONV_DIM:3 * CONV_DIM]
    pos = lax.broadcasted_iota(jnp.int32, (tm, 1), 0) % seq_len
    v_prev = jnp.where(pos == 0, 0.0, pltpu.roll(v, 1, 0))
    v_next = jnp.where(pos == seq_len - 1, 0.0, pltpu.roll(v, tm - 1, 0))
    cw = cw_ref[...]
    y = cb_ref[...] + v_prev * cw[0:1]
    y = y + v * cw[1:2]
    y = y + v_next * cw[2:3]
    yc_ref[...] = (u[:, 0:CONV_DIM] * y).astype(BF16)

    hd = HEADS * NA_HD
    u = _dot(h, wa_ref[:, 3 * CONV_DIM:3 * CONV_DIM + 3 * hd])
    qna_ref[...] = (u[:, 0:hd] * NA_SCALE).astype(BF16)
    kna_ref[...] = u[:, hd:2 * hd].astype(kna_ref.dtype)
    vna_ref[...] = u[:, 2 * hd:3 * hd].astype(vna_ref.dtype)

    u = _dot(h, wb_ref[...])
    cq = _rms(u[:, 0:Q_LORA], qn_ref[...]).astype(BF16)
    ckv = _rms(u[:, Q_LORA:Q_LORA + KV_LORA], kvn_ref[...])
    kr = u[:, Q_LORA + KV_LORA:Q_LORA + KV_LORA + MLA_ROPE]
    q2 = _dot(cq, wq_ref[...])
    if rope:
        kr_sw = u[:, Q_LORA + KV_LORA + MLA_ROPE:Q_LORA + KV_LORA + 2 * MLA_ROPE]
        kr_rot = kr * kc_ref[...] + kr_sw * ks_ref[...]
        qc = qc_ref[...]
        qs = qs_ref[...]
        width = HEADS * MLA_QK_PAD
        for hh in range(HEADS):
            cols = slice(hh * MLA_QK_PAD, (hh + 1) * MLA_QK_PAD)
            sw_cols = slice(width + hh * MLA_QK_PAD, width + (hh + 1) * MLA_QK_PAD)
            qm_ref[:, cols] = ((q2[:, cols] * qc + q2[:, sw_cols] * qs) * MLA_SCALE).astype(BF16)
    else:
        kr_rot = kr
        qm_ref[...] = (q2 * MLA_SCALE).astype(BF16)
        ckv_ref[...] = ckv
        kr_ref[...] = kr
    ckv_b = ckv.astype(BF16)
    km_ref[...] = (_dot(ckv_b, wka_ref[...]) + _dot(kr_rot.astype(BF16), wkb_ref[...])).astype(BF16)
    vm_ref[...] = _dot(ckv_b, wuv_ref[...]).astype(BF16)


def _proj(x, mod, rows_per_mod, w, layer, seq_len, rope, tables, tm):
    t = x.shape[0]
    tiles_per_mod = rows_per_mod // tm
    hd = HEADS * NA_HD
    qk_w = HEADS * MLA_QK_PAD
    wb_cols = w["wb"].shape[-1]
    wq_cols = w["wq"].shape[-1]
    row = lambda n: pl.BlockSpec((tm, n), lambda i: (i, 0))
    in_specs = [row(D_MODEL),
                pl.BlockSpec((None, N_MOD, D_MODEL), lambda i: (i // tiles_per_mod, 0, 0)),
                _resident((None, 1, D_MODEL), (layer, 0, 0)),
                _resident((None, D_MODEL, 3 * CONV_DIM + 3 * hd), (layer, 0, 0)),
                _resident((None, D_MODEL, wb_cols), (layer, 0, 0)),
                _resident((None, CONV_K, CONV_DIM), (layer, 0, 0)),
                _resident((None, 1, CONV_DIM), (layer, 0, 0)),
                _resident((None, 1, Q_LORA), (layer, 0, 0)),
                _resident((None, 1, KV_LORA), (layer, 0, 0)),
                _resident((None, Q_LORA, wq_cols), (layer, 0, 0)),
                _resident((None, KV_LORA, qk_w), (layer, 0, 0)),
                _resident((None, MLA_ROPE, qk_w), (layer, 0, 0)),
                _resident((None, KV_LORA, HEADS * MLA_V), (layer, 0, 0))]
    args = [x, mod, w["ng1"], w["wa"], w["wb"], w["conv_w"], w["conv_b"], w["qnorm"], w["kvnorm"],
            w["wq"], w["wka"], w["wkb"], w["wuv"]]
    kv_dtype = BF16 if rope else F32
    out_shape = [jax.ShapeDtypeStruct((t, CONV_DIM), BF16),
                 jax.ShapeDtypeStruct((t, hd), BF16),
                 jax.ShapeDtypeStruct((t, hd), kv_dtype),
                 jax.ShapeDtypeStruct((t, hd), kv_dtype),
                 jax.ShapeDtypeStruct((t, qk_w), BF16),
                 jax.ShapeDtypeStruct((t, qk_w), BF16),
                 jax.ShapeDtypeStruct((t, HEADS * MLA_V), BF16)]
    out_specs = [row(CONV_DIM), row(hd), row(hd), row(hd), row(qk_w), row(qk_w), row(HEADS * MLA_V)]
    if rope:
        assert tm == seq_len
        in_specs += [_resident((seq_len, MLA_QK_PAD), (0, 0)), _resident((seq_len, MLA_QK_PAD), (0, 0)),
                     _resident((seq_len, MLA_ROPE), (0, 0)), _resident((seq_len, MLA_ROPE), (0, 0))]
        args += list(tables)
    else:
        out_shape += [jax.ShapeDtypeStruct((t, KV_LORA), F32), jax.ShapeDtypeStruct((t, MLA_ROPE), F32)]
        out_specs += [row(KV_LORA), row(MLA_ROPE)]
    return pl.pallas_call(
        functools.partial(_proj_body, seq_len=seq_len, rope=rope),
        out_shape=out_shape,
        grid=(t // tm,),
        in_specs=in_specs,
        out_specs=out_specs,
        compiler_params=_params(1),
        name="mixer_proj",
    )(*args)


def _ctxkv_body(ckv_ref, kr_ref, wka_ref, wkb_ref, wuv_ref, k_ref, v_ref):
    ckv = ckv_ref[...].astype(BF16)
    kr = kr_ref[...].astype(BF16)
    k_ref[...] = (_dot(ckv, wka_ref[...]) + _dot(kr, wkb_ref[...])).astype(BF16)
    v_ref[...] = _dot(ckv, wuv_ref[...]).astype(BF16)


def _ctx_kv(cache_ckv, cache_kr, wka, wkb, wuv):
    qk_w = HEADS * MLA_QK_PAD
    n = DEC_BATCH * PAST_LEN
    return pl.pallas_call(
        _ctxkv_body,
        out_shape=[jax.ShapeDtypeStruct((DEPTH, n, qk_w), BF16),
                   jax.ShapeDtypeStruct((DEPTH, n, HEADS * MLA_V), BF16)],
        grid=(DEPTH, DEC_BATCH),
        in_specs=[pl.BlockSpec((None, None, PAST_LEN, KV_LORA), lambda l, b: (b, l, 0, 0)),
                  pl.BlockSpec((None, None, PAST_LEN, MLA_ROPE), lambda l, b: (b, l, 0, 0)),
                  pl.BlockSpec((None, KV_LORA, qk_w), lambda l, b: (l, 0, 0)),
                  pl.BlockSpec((None, MLA_ROPE, qk_w), lambda l, b: (l, 0, 0)),
                  pl.BlockSpec((None, KV_LORA, HEADS * MLA_V), lambda l, b: (l, 0, 0))],
        out_specs=[pl.BlockSpec((None, PAST_LEN, qk_w), lambda l, b: (l, b, 0)),
                   pl.BlockSpec((None, PAST_LEN, HEADS * MLA_V), lambda l, b: (l, b, 0))],
        compiler_params=_params(2),
        name="ctx_kv",
    )(cache_ckv, cache_kr, wka, wkb, wuv)


def _attn_body(*refs, dq, seq_q, has_ctx, has_bias):
    q_ref, k_ref, v_ref = refs[:3]
    rest = list(refs[3:])
    k2_ref = v2_ref = bias_ref = None
    if has_ctx:
        k2_ref, v2_ref = rest[:2]
        rest = rest[2:]
    if has_bias:
        bias_ref = rest[0]
        rest = rest[1:]
    (o_ref,) = rest
    rows = q_ref.shape[0]
    for sub in range(rows // seq_q):
        base = sub * seq_q
        for j in range(2):
            qk_cols = slice(j * dq, (j + 1) * dq)
            v_cols = slice(j * HEAD_V, (j + 1) * HEAD_V)
            k = k_ref[base:base + seq_q, qk_cols].astype(BF16)
            v = v_ref[base:base + seq_q, v_cols].astype(BF16)
            if has_ctx:
                k2 = k2_ref[:, qk_cols]
                v2 = v2_ref[:, v_cols]
            for c in range(seq_q // Q_CHUNK):
                r0 = base + c * Q_CHUNK
                q = q_ref[r0:r0 + Q_CHUNK, qk_cols]
                s = _dot_nt(q, k)
                if has_bias:
                    s = s + bias_ref[j, c * Q_CHUNK:(c + 1) * Q_CHUNK, :]
                m = jnp.max(s, axis=-1, keepdims=True)
                if has_ctx:
                    s2 = _dot_nt(q, k2)
                    m = jnp.maximum(m, jnp.max(s2, axis=-1, keepdims=True))
                p = jnp.exp(s - m)
                den = jnp.sum(p, axis=-1, keepdims=True)
                o = _dot(p.astype(BF16), v)
                if has_ctx:
                    p2 = jnp.exp(s2 - m)
                    den = den + jnp.sum(p2, axis=-1, keepdims=True)
                    o = o + _dot(p2.astype(BF16), v2)
                o_ref[r0:r0 + Q_CHUNK, v_cols] = (o / den).astype(BF16)


def _attention(q, k, v, dq, seq_q, k_ctx=None, v_ctx=None, bias=None):
    t = q.shape[0]
    has_ctx = k_ctx is not None
    has_bias = bias is not None
    blocks_per_seq = max(seq_q // ATTN_ROWS, 1)
    assert blocks_per_seq == 1
    in_specs = [pl.BlockSpec((ATTN_ROWS, 2 * dq), lambda hp, i: (i, hp)),
                pl.BlockSpec((ATTN_ROWS, 2 * dq), lambda hp, i: (i, hp)),
                pl.BlockSpec((ATTN_ROWS, 2 * HEAD_V), lambda hp, i: (i, hp))]
    args = [q, k, v]
    if has_ctx:
        assert seq_q == ATTN_ROWS
        in_specs += [pl.BlockSpec((PAST_LEN, 2 * dq), lambda hp, i: (i, hp)),
                     pl.BlockSpec((PAST_LEN, 2 * HEAD_V), lambda hp, i: (i, hp))]
        args += [k_ctx, v_ctx]
    if has_bias:
        in_specs.append(pl.BlockSpec((2, seq_q, seq_q), lambda hp, i: (hp, 0, 0)))
        args.append(bias)
    return pl.pallas_call(
        functools.partial(_attn_body, dq=dq, seq_q=seq_q, has_ctx=has_ctx, has_bias=has_bias),
        out_shape=jax.ShapeDtypeStruct((t, HEADS * HEAD_V), BF16),
        grid=(HEADS // 2, t // ATTN_ROWS),
        in_specs=in_specs,
        out_specs=pl.BlockSpec((ATTN_ROWS, 2 * HEAD_V), lambda hp, i: (i, hp)),
        compiler_params=_params(2),
        name="attention",
    )(*args)


def _mix_body(x_ref, yc_ref, ona_ref, om_ref, mod_ref, ng_ref, wg_ref, wco_ref, wno_ref, wmo_ref,
              wo_ref, o_ref):
    x = x_ref[...]
    mod = mod_ref[...]
    h = (_rms(x, ng_ref[...]) * (1 + mod[4:5]) + mod[3:4]).astype(BF16)
    z = jax.nn.sigmoid(_dot(h, wg_ref[:, 0:D_MODEL])) * _dot(yc_ref[...], wco_ref[...])
    z = z + jax.nn.sigmoid(_dot(h, wg_ref[:, D_MODEL:2 * D_MODEL])) * _dot(ona_ref[...], wno_ref[...])
    z = z + jax.nn.sigmoid(_dot(h, wg_ref[:, 2 * D_MODEL:3 * D_MODEL])) * _dot(om_ref[...], wmo_ref[...])
    o_ref[...] = x + mod[5:6] * _dot(z.astype(BF16), wo_ref[...])


def _mix(x, yc, ona, om, mod, rows_per_mod, w, layer, tm=512):
    t = x.shape[0]
    tiles_per_mod = rows_per_mod // tm
    row = lambda n: pl.BlockSpec((tm, n), lambda i: (i, 0))
    return pl.pallas_call(
        _mix_body,
        out_shape=jax.ShapeDtypeStruct((t, D_MODEL), F32),
        grid=(t // tm,),
        in_specs=[row(D_MODEL), row(CONV_DIM), row(HEADS * NA_HD), row(HEADS * MLA_V),
                  pl.BlockSpec((None, N_MOD, D_MODEL), lambda i: (i // tiles_per_mod, 0, 0)),
                  _resident((None, 1, D_MODEL), (layer, 0, 0)),
                  _resident((None, D_MODEL, 3 * D_MODEL), (layer, 0, 0)),
                  _resident((None, CONV_DIM, D_MODEL), (layer, 0, 0)),
                  _resident((None, HEADS * NA_HD, D_MODEL), (layer, 0, 0)),
                  _resident((None, HEADS * MLA_V, D_MODEL), (layer, 0, 0)),
                  _resident((None, D_MODEL, D_MODEL), (layer, 0, 0))],
        out_specs=row(D_MODEL),
        compiler_params=_params(1),
        name="mixer_out",
    )(x, yc, ona, om, mod, w["ng1"], w["wgate"], w["wco"], w["wno"], w["wmo"], w["wo"])


def _rope_tables():
    half = MLA_ROPE // 2
    nf = half // 2
    inv = 1.0 / (ROPE_BASE ** (jnp.arange(nf, dtype=F32) / nf))
    t = jnp.arange(DEC_SEQ)
    rows = (t // GRID_W).astype(F32)[:, None] * inv[None, :]
    cols = (t % GRID_W).astype(F32)[:, None] * inv[None, :]
    cos = jnp.concatenate([jnp.cos(rows), jnp.cos(rows), jnp.cos(cols), jnp.cos(cols)], axis=-1)
    sin = jnp.concatenate([jnp.sin(rows), jnp.sin(rows), jnp.sin(cols), jnp.sin(cols)], axis=-1)
    pad = MLA_QK_PAD - MLA_NOPE - MLA_ROPE
    q_cos = jnp.concatenate([jnp.ones((DEC_SEQ, MLA_NOPE), F32), cos, jnp.zeros((DEC_SEQ, pad), F32)], axis=-1)
    q_sin = jnp.concatenate([jnp.zeros((DEC_SEQ, MLA_NOPE), F32), sin, jnp.zeros((DEC_SEQ, pad), F32)], axis=-1)
    return q_cos, q_sin, cos, sin


def _rope_swap(w):
    nf = MLA_ROPE // 4
    a, b, c, d = (w[..., i * nf:(i + 1) * nf] for i in range(4))
    return jnp.concatenate([-b, a, -d, c], axis=-1)


def _na_bias(rpb):
    rows = DEC_SEQ // GRID_W
    n_dr = 2 * NA_WIN_R - 1
    n_dc = 2 * NA_WIN_C - 1
    edge = GRID_W - NA_WIN_C
    ext = jnp.concatenate([jnp.repeat(rpb[..., :1], edge, axis=-1), rpb,
                           jnp.repeat(rpb[..., -1:], edge, axis=-1)], axis=-1)
    off = GRID_W - 1
    tc = jnp.stack([ext[..., off - c:off - c + GRID_W] for c in range(GRID_W)], axis=-2)
    fill = jnp.full(tc.shape[:2] + (rows - NA_WIN_R,) + tc.shape[3:], NEG_INF, F32)
    ext_r = jnp.concatenate([fill, tc, fill], axis=2)
    off_r = rows - 1
    tr = jnp.stack([ext_r[:, :, off_r - r:off_r - r + rows] for r in range(rows)], axis=2)
    tr = tr.transpose(0, 1, 2, 4, 3, 5)
    r = np.arange(rows)
    c = np.arange(GRID_W)
    r_start = np.clip(r - NA_WIN_R // 2, 0, rows - NA_WIN_R)
    c_start = np.clip(c - NA_WIN_C // 2, 0, GRID_W - NA_WIN_C)
    r_in = (r[None, :] >= r_start[:, None]) & (r[None, :] < r_start[:, None] + NA_WIN_R)
    c_in = (c[None, :] >= c_start[:, None]) & (c[None, :] < c_start[:, None] + NA_WIN_C)
    mask = r_in[:, None, :, None] & c_in[None, :, None, :]
    assert n_dr == 2 * NA_WIN_R - 1 and n_dc == ext.shape[-1] - 2 * edge
    out = jnp.where(jnp.asarray(mask), tr, NEG_INF)
    return out.reshape(DEPTH, HEADS, DEC_SEQ, DEC_SEQ)


def _pack_weights(w_in, w_uq, w_ukv):
    c3 = 3 * CONV_DIM + 3 * HEADS * NA_HD
    wa = w_in[..., :c3]
    w_cq = w_in[..., c3:c3 + Q_LORA]
    w_ckv = w_in[..., c3 + Q_LORA:c3 + Q_LORA + KV_LORA]
    w_kr = w_in[..., c3 + Q_LORA + KV_LORA:c3 + Q_LORA + KV_LORA + MLA_ROPE]
    g0 = c3 + Q_LORA + KV_LORA + MLA_ROPE
    wgate = w_in[..., g0:]
    used = Q_LORA + KV_LORA + 2 * MLA_ROPE
    wb_lat = jnp.concatenate([w_cq, w_ckv, w_kr, _rope_swap(w_kr),
                              jnp.zeros(w_in.shape[:-1] + (512 - used,), F32)], axis=-1)
    wb_ctx = jnp.concatenate([w_cq, w_ckv, w_kr,
                              jnp.zeros(w_in.shape[:-1] + (512 - used + MLA_ROPE,), F32)], axis=-1)
    uq = w_uq.reshape(DEPTH, Q_LORA, MLA_HEADS, MLA_NOPE + MLA_ROPE)
    pad = MLA_QK_PAD - MLA_NOPE - MLA_ROPE
    zp = jnp.zeros(uq.shape[:-1] + (pad,), F32)
    zn = jnp.zeros(uq.shape[:-1] + (MLA_NOPE,), F32)
    q_ext = jnp.concatenate([uq, zp], axis=-1).reshape(DEPTH, Q_LORA, MLA_HEADS * MLA_QK_PAD)
    q_sw = jnp.concatenate([zn, _rope_swap(uq[..., MLA_NOPE:]), zp], axis=-1).reshape(q_ext.shape)
    ukv = w_ukv.reshape(DEPTH, KV_LORA, MLA_HEADS, MLA_NOPE + MLA_V)
    zk = jnp.zeros(ukv.shape[:-1] + (MLA_QK_PAD - MLA_NOPE,), F32)
    wka = jnp.concatenate([ukv[..., :MLA_NOPE], zk], axis=-1).reshape(DEPTH, KV_LORA, MLA_HEADS * MLA_QK_PAD)
    eye = jnp.concatenate([jnp.zeros((MLA_ROPE, MLA_NOPE), F32), jnp.eye(MLA_ROPE, dtype=F32),
                           jnp.zeros((MLA_ROPE, pad), F32)], axis=-1)
    wkb = jnp.broadcast_to(jnp.tile(eye, (1, MLA_HEADS))[None], (DEPTH, MLA_ROPE, MLA_HEADS * MLA_QK_PAD))
    wuv = ukv[..., MLA_NOPE:].reshape(DEPTH, KV_LORA, MLA_HEADS * MLA_V)
    b = lambda a: a.astype(BF16)
    return dict(wa=b(wa), wb_lat=b(wb_lat), wb_ctx=b(wb_ctx), wgate=b(wgate),
                wq_lat=b(jnp.concatenate([q_ext, q_sw], axis=-1)), wq_ctx=b(q_ext),
                wka=b(wka), wkb=b(wkb), wuv=b(wuv))


def kernel(x_prompt, x_sample, cache_na_k, cache_na_v, cache_mla_ckv, cache_mla_krope, c, c_ctx,
           w_ada, b_ada, norm_g, w_ffn1_gate, w_ffn1_up, w_ffn1_down, w_ffn2_gate, w_ffn2_up, w_ffn2_down,
           w_in, conv_w, conv_b, na_rpb, mla_qnorm, w_uq, mla_kvnorm, w_ukv,
           w_conv_out, w_na_out, w_mla_out, w_o, final_g):
    b16 = lambda a: a.astype(BF16)
    packed = _pack_weights(w_in, w_uq, w_ukv)
    shared = dict(conv_w=conv_w, conv_b=conv_b.reshape(DEPTH, 1, CONV_DIM),
                  qnorm=mla_qnorm.reshape(DEPTH, 1, Q_LORA), kvnorm=mla_kvnorm.reshape(DEPTH, 1, KV_LORA),
                  ng1=norm_g[:, 1:2], wa=packed["wa"], wka=packed["wka"], wkb=packed["wkb"], wuv=packed["wuv"],
                  wgate=packed["wgate"], wco=b16(w_conv_out), wno=b16(w_na_out), wmo=b16(w_mla_out), wo=b16(w_o))
    w_ctx = dict(shared, wb=packed["wb_ctx"], wq=packed["wq_ctx"])
    w_lat = dict(shared, wb=packed["wb_lat"], wq=packed["wq_lat"])
    ffn_w = [(b16(w_ffn1_gate), b16(w_ffn1_up), b16(w_ffn1_down)),
             (b16(w_ffn2_gate), b16(w_ffn2_up), b16(w_ffn2_down))]
    final_row = final_g.reshape(1, D_MODEL)

    c_all = jnp.concatenate([c_ctx[None], c, jnp.zeros((MOD_ROWS - 1 - DEC_BATCH, D_MODEL), F32)], axis=0)
    mod = _modulation(c_all, w_ada, b_ada).reshape(DEPTH, MOD_ROWS, N_MOD, D_MODEL)

    tables = _rope_tables()
    na_bias = _na_bias(na_rpb)
    ctx_k_mla, ctx_v_mla = _ctx_kv(cache_mla_ckv, cache_mla_krope, packed["wka"], packed["wkb"], packed["wuv"])
    n_ctx = DEC_BATCH * PAST_LEN
    to_rows = lambda a: b16(a.transpose(1, 0, 3, 2, 4).reshape(DEPTH, n_ctx, HEADS * NA_HD))
    ctx_k_na = to_rows(cache_na_k)
    ctx_v_na = to_rows(cache_na_v)

    xp = x_prompt.reshape(BATCH * SEQ, D_MODEL)
    xs = x_sample.reshape(DEC_BATCH * DEC_SEQ, D_MODEL)
    n_p = BATCH * SEQ
    new_k, new_v, new_ckv, new_kr = [], [], [], []
    for l in range(DEPTH):
        mod_p = mod[l, 0:1]
        mod_s = mod[l, 1:1 + DEC_BATCH]
        last = l == DEPTH - 1
        xp = _ffn(xp, mod_p, n_p, norm_g[l, 0:1], *ffn_w[0], l, 0)
        yc, qna, kna, vna, qm, km, vm, ckv, kr = _proj(xp, mod_p, n_p, w_ctx, l, SEQ, False, None, 512)
        ona = _attention(qna, kna, vna, NA_HD, SEQ)
        om = _attention(qm, km, vm, MLA_QK_PAD, SEQ)
        xp = _mix(xp, yc, ona, om, mod_p, n_p, w_ctx, l)
        xp = _ffn(xp, mod_p, n_p, norm_g[l, 2:3], *ffn_w[1], l, 6, final_row if last else None)
        new_k.append(kna.reshape(BATCH, SEQ, NA_HEADS, NA_HD).transpose(0, 2, 1, 3))
        new_v.append(vna.reshape(BATCH, SEQ, NA_HEADS, NA_HD).transpose(0, 2, 1, 3))
        new_ckv.append(ckv.reshape(BATCH, SEQ, KV_LORA))
        new_kr.append(kr.reshape(BATCH, SEQ, MLA_ROPE))
        xs = _ffn(xs, mod_s, DEC_SEQ, norm_g[l, 0:1], *ffn_w[0], l, 0)
        yc, qna, kna, vna, qm, km, vm = _proj(xs, mod_s, DEC_SEQ, w_lat, l, DEC_SEQ, True, tables, DEC_SEQ)
        ona = _attention(qna, kna, vna, NA_HD, DEC_SEQ, ctx_k_na[l], ctx_v_na[l], na_bias[l])
        om = _attention(qm, km, vm, MLA_QK_PAD, DEC_SEQ, ctx_k_mla[l], ctx_v_mla[l])
        xs = _mix(xs, yc, ona, om, mod_s, DEC_SEQ, w_lat, l)
        xs = _ffn(xs, mod_s, DEC_SEQ, norm_g[l, 2:3], *ffn_w[1], l, 6, final_row if last else None)
    return (xp.reshape(BATCH, SEQ, D_MODEL), xs.reshape(DEC_BATCH, DEC_SEQ, D_MODEL),
            jnp.stack(new_k, axis=1), jnp.stack(new_v, axis=1),
            jnp.stack(new_ckv, axis=1), jnp.stack(new_kr, axis=1))
```

```python
import functools

import jax
import jax.numpy as jnp
import numpy as np
from jax import lax
from jax.experimental import pallas as pl
from jax.experimental.pallas import tpu as pltpu

D_MODEL = 1024
BATCH = 32
SEQ = 256
DEPTH = 2
DEC_BATCH = 8
DEC_SEQ = 1024
PAST_LEN = 256
GRID_W = 64
CONV_DIM = 512
CONV_K = 3
NA_HEADS = 8
NA_HD = 64
NA_WIN_R = 8
NA_WIN_C = 16
MLA_HEADS = 8
MLA_NOPE = 64
MLA_ROPE = 32
MLA_V = 64
Q_LORA = 256
KV_LORA = 128
FFN_DIM = 2816
N_MOD = 9
ROPE_BASE = 10000.0
EPS = 1e-6
NEG_INF = -1e30
MLA_SCALE = (MLA_NOPE + MLA_ROPE) ** -0.5
NA_SCALE = NA_HD ** -0.5

HEADS = 8
HEAD_V = 64
HD = HEADS * NA_HD
MLA_QK_PAD = 128
MLA_QK_W = HEADS * MLA_QK_PAD
FFN_CHUNK = 256
Q_CHUNK = 256
MOD_ROWS = 16
VMEM_LIMIT = 56 * 1024 * 1024
NA_WINDOWS = ((0, 512), (0, 768), (256, 768), (512, 512))
NA_WINDOW_MAX = 768

BF16 = jnp.bfloat16
F32 = jnp.float32


def _dot(a, b):
    return jnp.dot(a, b, preferred_element_type=F32)


def _dot_nt(a, b):
    return lax.dot_general(a, b, (((1,), (1,)), ((), ())), preferred_element_type=F32)


def _rms(x, g):
    return x * lax.rsqrt(jnp.mean(x * x, axis=-1, keepdims=True) + EPS) * g


def _params(n_axes):
    return pltpu.CompilerParams(dimension_semantics=("arbitrary",) * n_axes,
                                vmem_limit_bytes=VMEM_LIMIT)


def _resident(shape, index):
    return pl.BlockSpec(shape, lambda *_: index, pipeline_mode=pl.Buffered(1))


def _mod_body(c_ref, w_ref, b_ref, o_ref):
    c = c_ref[...]
    a = c * jax.nn.sigmoid(c)
    o_ref[...] = jnp.dot(a, w_ref[...], precision=lax.Precision.HIGHEST,
                         preferred_element_type=F32) + b_ref[...]


def _modulation(c_all, w_ada, b_ada):
    n_col = N_MOD * D_MODEL
    tn = D_MODEL
    return pl.pallas_call(
        _mod_body,
        out_shape=jax.ShapeDtypeStruct((DEPTH, MOD_ROWS, n_col), F32),
        grid=(DEPTH, n_col // tn),
        in_specs=[pl.BlockSpec((MOD_ROWS, D_MODEL), lambda l, j: (0, 0)),
                  pl.BlockSpec((None, D_MODEL, tn), lambda l, j: (l, 0, j)),
                  pl.BlockSpec((None, 1, tn), lambda l, j: (l, 0, j))],
        out_specs=pl.BlockSpec((None, MOD_ROWS, tn), lambda l, j: (l, 0, j)),
        compiler_params=_params(2),
        name="modulation",
    )(c_all, w_ada, b_ada.reshape(DEPTH, 1, n_col))


def _ffn_body(x_ref, mod_ref, ng_ref, wg_ref, wu_ref, wd_ref, *rest, mod_off, final):
    if final:
        fg_ref, o_ref, a_scr = rest
    else:
        o_ref, a_scr = rest
    x = x_ref[...]
    mod = mod_ref[...]
    shift = mod[mod_off:mod_off + 1]
    scale = mod[mod_off + 1:mod_off + 2]
    gate = mod[mod_off + 2:mod_off + 3]
    h = (_rms(x, ng_ref[...]) * (1 + scale) + shift).astype(BF16)
    for f in range(FFN_DIM // FFN_CHUNK):
        cols = slice(f * FFN_CHUNK, (f + 1) * FFN_CHUNK)
        g = _dot(h, wg_ref[:, cols])
        u = _dot(h, wu_ref[:, cols])
        a_scr[:, cols] = (g * jax.nn.sigmoid(g) * u).astype(BF16)
    y = _dot(a_scr[...], wd_ref[...])
    out = x + 0.5 * gate * y
    if final:
        out = _rms(out, fg_ref[...])
    o_ref[...] = out


def _ffn(x, mod, rows_per_mod, ng, wg, wu, wd, layer, mod_off, final_g=None, tm=512):
    t = x.shape[0]
    tiles_per_mod = rows_per_mod // tm
    final = final_g is not None
    in_specs = [pl.BlockSpec((tm, D_MODEL), lambda i: (i, 0)),
                pl.BlockSpec((None, N_MOD, D_MODEL), lambda i: (i // tiles_per_mod, 0, 0)),
                _resident((1, D_MODEL), (0, 0)),
                _resident((None, D_MODEL, FFN_DIM), (layer, 0, 0)),
                _resident((None, D_MODEL, FFN_DIM), (layer, 0, 0)),
                _resident((None, FFN_DIM, D_MODEL), (layer, 0, 0))]
    args = [x, mod, ng, wg, wu, wd]
    if final:
        in_specs.append(_resident((1, D_MODEL), (0, 0)))
        args.append(final_g)
    return pl.pallas_call(
        functools.partial(_ffn_body, mod_off=mod_off, final=final),
        out_shape=jax.ShapeDtypeStruct((t, D_MODEL), F32),
        grid=(t // tm,),
        in_specs=in_specs,
        out_specs=pl.BlockSpec((tm, D_MODEL), lambda i: (i, 0)),
        scratch_shapes=[pltpu.VMEM((tm, FFN_DIM), BF16)],
        compiler_params=_params(1),
        name="ffn",
    )(*args)


def _proj_body(x_ref, mod_ref, ng_ref, wa_ref, wb_ref, wt_ref, cw_ref, cb_ref, qn_ref, kvn_ref,
               wq_ref, wkat_ref, wkbt_ref, wuvt_ref, *rest, seq_len, latent, n_alias):
    rest = rest[n_alias:]
    if latent:
        (qc_ref, qs_ref, kct_ref, kst_ref,
         yc_ref, qna_ref, knat_ref, vnat_ref, qm_ref, kmt_ref, vmt_ref) = rest
    else:
        (yc_ref, qna_ref, knat_ref, vnat_ref, qm_ref, kmt_ref, vmt_ref, ckv_ref, krt_ref) = rest
    tm = x_ref.shape[0]
    x = x_ref[...]
    mod = mod_ref[...]
    h = (_rms(x, ng_ref[...]) * (1 + mod[4:5]) + mod[3:4]).astype(BF16)

    u = _dot(h, wa_ref[:, 0:3 * CONV_DIM])
    v = u[:, CONV_DIM:2 * CONV_DIM] * u[:, 2 * CONV_DIM:3 * CONV_DIM]
    pos = lax.broadcasted_iota(jnp.int32, (tm, 1), 0) % seq_len
    v_prev = jnp.where(pos == 0, 0.0, pltpu.roll(v, 1, 0))
    v_next = jnp.where(pos == seq_len - 1, 0.0, pltpu.roll(v, tm - 1, 0))
    cw = cw_ref[...]
    y = cb_ref[...] + v_prev * cw[0:1]
    y = y + v * cw[1:2]
    y = y + v_next * cw[2:3]
    yc_ref[...] = (u[:, 0:CONV_DIM] * y).astype(BF16)

    qna_ref[...] = (_dot(h, wa_ref[:, 3 * CONV_DIM:3 * CONV_DIM + HD]) * NA_SCALE).astype(BF16)

    ut = _dot_nt(wt_ref[...], h)
    krt = ut[2 * HD:2 * HD + MLA_ROPE]

    u = _dot(h, wb_ref[...])
    cq = _rms(u[:, 0:Q_LORA], qn_ref[...]).astype(BF16)
    ckv = _rms(u[:, Q_LORA:Q_LORA + KV_LORA], kvn_ref[...])
    ckv_b = ckv.astype(BF16)
    q2 = _dot(cq, wq_ref[...])
    if latent:
        knat_ref[...] = ut[0:HD].astype(BF16)
        vnat_ref[...] = ut[HD:2 * HD].astype(BF16)
        krt = krt * kct_ref[...] + ut[2 * HD + MLA_ROPE:2 * HD + 2 * MLA_ROPE] * kst_ref[...]
        qc = qc_ref[...]
        qs = qs_ref[...]
        for hh in range(HEADS):
            cols = slice(hh * MLA_QK_PAD, (hh + 1) * MLA_QK_PAD)
            sw_cols = slice(MLA_QK_W + hh * MLA_QK_PAD, MLA_QK_W + (hh + 1) * MLA_QK_PAD)
            qm_ref[:, cols] = ((q2[:, cols] * qc + q2[:, sw_cols] * qs) * MLA_SCALE).astype(BF16)
    else:
        qm_ref[...] = (q2 * MLA_SCALE).astype(BF16)
    kmt = _dot_nt(wkat_ref[...], ckv_b) + _dot(wkbt_ref[...], krt.astype(BF16))
    vmt = _dot_nt(wuvt_ref[...], ckv_b)
    if latent:
        kmt_ref[...] = kmt.astype(BF16)
        vmt_ref[...] = vmt.astype(BF16)
    else:
        for b in range(tm // seq_len):
            rows = slice(b * seq_len, (b + 1) * seq_len)
            knat_ref[b] = ut[0:HD, rows].reshape(HEADS, NA_HD, seq_len)
            vnat_ref[b] = ut[HD:2 * HD, rows].reshape(HEADS, NA_HD, seq_len)
            ckv_ref[b] = ckv[rows]
            krt_ref[b] = krt[:, rows]
            kmt_ref[b] = kmt[:, rows].astype(BF16)
            vmt_ref[b] = vmt[:, rows].astype(BF16)


def _proj(x, mod, rows_per_mod, w, layer, seq_len, latent, tables, caches, tm):
    t = x.shape[0]
    n_seq = t // seq_len
    seq_per_tile = tm // seq_len
    tiles_per_mod = rows_per_mod // tm
    wb_cols = w["wb"].shape[-1]
    wq_cols = w["wq"].shape[-1]
    wt_rows = w["wt"].shape[-2]
    row = lambda n: pl.BlockSpec((tm, n), lambda i: (i, 0))
    in_specs = [row(D_MODEL),
                pl.BlockSpec((None, N_MOD, D_MODEL), lambda i: (i // tiles_per_mod, 0, 0)),
                _resident((None, 1, D_MODEL), (layer, 0, 0)),
                _resident((None, D_MODEL, 3 * CONV_DIM + HD), (layer, 0, 0)),
                _resident((None, D_MODEL, wb_cols), (layer, 0, 0)),
                _resident((None, wt_rows, D_MODEL), (layer, 0, 0)),
                _resident((None, CONV_K, CONV_DIM), (layer, 0, 0)),
                _resident((None, 1, CONV_DIM), (layer, 0, 0)),
                _resident((None, 1, Q_LORA), (layer, 0, 0)),
                _resident((None, 1, KV_LORA), (layer, 0, 0)),
                _resident((None, Q_LORA, wq_cols), (layer, 0, 0)),
                _resident((None, MLA_QK_W, KV_LORA), (layer, 0, 0)),
                _resident((None, MLA_QK_W, MLA_ROPE), (layer, 0, 0)),
                _resident((None, HD, KV_LORA), (layer, 0, 0))]
    args = [x, mod, w["ng1"], w["wa"], w["wb"], w["wt"], w["conv_w"], w["conv_b"], w["qnorm"], w["kvnorm"],
            w["wq"], w["wkat"], w["wkbt"], w["wuvt"]]
    out_shape = [jax.ShapeDtypeStruct((t, CONV_DIM), BF16),
                 jax.ShapeDtypeStruct((t, HD), BF16)]
    out_specs = [row(CONV_DIM), row(HD)]
    aliases = {}
    n_alias = 0
    if latent:
        assert tm == seq_len
        in_specs += [_resident((seq_len, MLA_QK_PAD), (0, 0)), _resident((seq_len, MLA_QK_PAD), (0, 0)),
                     _resident((MLA_ROPE, seq_len), (0, 0)), _resident((MLA_ROPE, seq_len), (0, 0))]
        args += list(tables)
        seq_blk = lambda n: pl.BlockSpec((None, n, seq_len), lambda i: (i, 0, 0))
        out_shape += [jax.ShapeDtypeStruct((n_seq, HD, seq_len), BF16),
                      jax.ShapeDtypeStruct((n_seq, HD, seq_len), BF16),
                      jax.ShapeDtypeStruct((t, MLA_QK_W), BF16),
                      jax.ShapeDtypeStruct((n_seq, MLA_QK_W, seq_len), BF16),
                      jax.ShapeDtypeStruct((n_seq, HD, seq_len), BF16)]
        out_specs += [seq_blk(HD), seq_blk(HD), row(MLA_QK_W), seq_blk(MLA_QK_W), seq_blk(HD)]
    else:
        if caches is not None:
            n_alias = len(caches)
            in_specs += [pl.BlockSpec(memory_space=pl.ANY)] * n_alias
            args += list(caches)
            aliases = {len(args) - n_alias + k: 2 + (0, 1, 5, 6)[k] for k in range(n_alias)}
        cache_blk = lambda *dims: pl.BlockSpec((seq_per_tile, None) + dims,
                                               lambda i: (i, layer) + (0,) * len(dims))
        seq_blk = lambda n: pl.BlockSpec((seq_per_tile, n, seq_len), lambda i: (i, 0, 0))
        out_shape += [jax.ShapeDtypeStruct((n_seq, DEPTH, HEADS, NA_HD, seq_len), F32),
                      jax.ShapeDtypeStruct((n_seq, DEPTH, HEADS, NA_HD, seq_len), F32),
                      jax.ShapeDtypeStruct((t, MLA_QK_W), BF16),
                      jax.ShapeDtypeStruct((n_seq, MLA_QK_W, seq_len), BF16),
                      jax.ShapeDtypeStruct((n_seq, HD, seq_len), BF16),
                      jax.ShapeDtypeStruct((n_seq, DEPTH, seq_len, KV_LORA), F32),
                      jax.ShapeDtypeStruct((n_seq, DEPTH, MLA_ROPE, seq_len), F32)]
        out_specs += [cache_blk(HEADS, NA_HD, seq_len), cache_blk(HEADS, NA_HD, seq_len), row(MLA_QK_W),
                      seq_blk(MLA_QK_W), seq_blk(HD), cache_blk(seq_len, KV_LORA), cache_blk(MLA_ROPE, seq_len)]
    return pl.pallas_call(
        functools.partial(_proj_body, seq_len=seq_len, latent=latent, n_alias=n_alias),
        out_shape=out_shape,
        grid=(t // tm,),
        in_specs=in_specs,
        out_specs=out_specs,
        input_output_aliases=aliases,
        compiler_params=_params(1),
        name="mixer_proj",
    )(*args)


def _ctxkv_body(ckv_ref, krt_ref, wkat_ref, wkbt_ref, wuvt_ref, k_ref, v_ref):
    ckv = ckv_ref[...].astype(BF16)
    krt = krt_ref[...].astype(BF16)
    k_ref[...] = (_dot_nt(wkat_ref[...], ckv) + _dot(wkbt_ref[...], krt)).astype(BF16)
    v_ref[...] = _dot_nt(wuvt_ref[...], ckv).astype(BF16)


def _ctx_kv(cache_ckv, cache_krt, wkat, wkbt, wuvt):
    return pl.pallas_call(
        _ctxkv_body,
        out_shape=[jax.ShapeDtypeStruct((DEPTH, DEC_BATCH, MLA_QK_W, PAST_LEN), BF16),
                   jax.ShapeDtypeStruct((DEPTH, DEC_BATCH, HD, PAST_LEN), BF16)],
        grid=(DEPTH, DEC_BATCH),
        in_specs=[pl.BlockSpec((None, None, PAST_LEN, KV_LORA), lambda l, b: (b, l, 0, 0)),
                  pl.BlockSpec((None, None, MLA_ROPE, PAST_LEN), lambda l, b: (b, l, 0, 0)),
                  pl.BlockSpec((None, MLA_QK_W, KV_LORA), lambda l, b: (l, 0, 0)),
                  pl.BlockSpec((None, MLA_QK_W, MLA_ROPE), lambda l, b: (l, 0, 0)),
                  pl.BlockSpec((None, HD, KV_LORA), lambda l, b: (l, 0, 0))],
        out_specs=[pl.BlockSpec((None, None, MLA_QK_W, PAST_LEN), lambda l, b: (l, b, 0, 0)),
                   pl.BlockSpec((None, None, HD, PAST_LEN), lambda l, b: (l, b, 0, 0))],
        compiler_params=_params(2),
        name="ctx_kv",
    )(cache_ckv, cache_krt, wkat, wkbt, wuvt)


def _softmax_pv(s, vt):
    m = jnp.max(s, axis=-1, keepdims=True)
    p = jnp.exp(s - m)
    den = jnp.sum(p, axis=-1, keepdims=True)
    return (_dot_nt(p.astype(BF16), vt) / den).astype(BF16)


def _attn_ctx_body(qna_ref, knat_ref, vnat_ref, qm_ref, kmt_ref, vmt_ref, ona_ref, om_ref, *, seq_len):
    for b in range(qna_ref.shape[0] // seq_len):
        rows = slice(b * seq_len, (b + 1) * seq_len)
        for hh in range(HEADS):
            v_cols = slice(hh * HEAD_V, (hh + 1) * HEAD_V)
            s = _dot(qna_ref[rows, hh * NA_HD:(hh + 1) * NA_HD], knat_ref[b, hh].astype(BF16))
            ona_ref[rows, v_cols] = _softmax_pv(s, vnat_ref[b, hh].astype(BF16))
            qk = slice(hh * MLA_QK_PAD, (hh + 1) * MLA_QK_PAD)
            s = _dot(qm_ref[rows, qk], kmt_ref[b, qk, :])
            om_ref[rows, v_cols] = _softmax_pv(s, vmt_ref[b, v_cols, :])


def _attn_ctx(qna, knat, vnat, qm, kmt, vmt, layer, seq_len, tm=512):
    t = qna.shape[0]
    nb = tm // seq_len
    row = lambda n: pl.BlockSpec((tm, n), lambda i: (i, 0))
    cache_blk = pl.BlockSpec((nb, None, HEADS, NA_HD, seq_len), lambda i: (i, layer, 0, 0, 0))
    seq_blk = lambda n: pl.BlockSpec((nb, n, seq_len), lambda i: (i, 0, 0))
    return pl.pallas_call(
        functools.partial(_attn_ctx_body, seq_len=seq_len),
        out_shape=[jax.ShapeDtypeStruct((t, HD), BF16), jax.ShapeDtypeStruct((t, HD), BF16)],
        grid=(t // tm,),
        in_specs=[row(HD), cache_blk, cache_blk, row(MLA_QK_W), seq_blk(MLA_QK_W), seq_blk(HD)],
        out_specs=[row(HD), row(HD)],
        compiler_params=_params(1),
        name="attn_ctx",
    )(qna, knat, vnat, qm, kmt, vmt)


def _attn_lat_body(qna_ref, knat_ref, vnat_ref, kctx_ref, vctx_ref, bias_ref,
                   qm_ref, kmt_ref, vmt_ref, kmctx_ref, vmctx_ref, ona_ref, om_ref):
    cat = lambda a, b: jnp.concatenate([a, b], axis=1)
    for j in range(2):
        v_cols = slice(j * HEAD_V, (j + 1) * HEAD_V)
        na = slice(j * NA_HD, (j + 1) * NA_HD)
        kc = kctx_ref[j].astype(BF16)
        vc = vctx_ref[j].astype(BF16)
        for c, (start, count) in enumerate(NA_WINDOWS):
            rows = slice(c * Q_CHUNK, (c + 1) * Q_CHUNK)
            keys = slice(start, start + count)
            q = qna_ref[rows, na]
            s = cat(_dot(q, kc), _dot(q, knat_ref[na, keys]) + bias_ref[j, c, :, 0:count])
            ona_ref[rows, v_cols] = _softmax_pv(s, cat(vc, vnat_ref[v_cols, keys]))
        qk = slice(j * MLA_QK_PAD, (j + 1) * MLA_QK_PAD)
        kt = cat(kmctx_ref[qk, :], kmt_ref[qk, :])
        vt = cat(vmctx_ref[v_cols, :], vmt_ref[v_cols, :])
        for c in range(DEC_SEQ // Q_CHUNK):
            rows = slice(c * Q_CHUNK, (c + 1) * Q_CHUNK)
            om_ref[rows, v_cols] = _softmax_pv(_dot(qm_ref[rows, qk], kt), vt)


def _attn_lat(qna, knat, vnat, kctx, vctx, bias, qm, kmt, vmt, kmctx, vmctx, layer):
    t = qna.shape[0]
    s = DEC_SEQ
    return pl.pallas_call(
        _attn_lat_body,
        out_shape=[jax.ShapeDtypeStruct((t, HD), BF16), jax.ShapeDtypeStruct((t, HD), BF16)],
        grid=(HEADS // 2, t // s),
        in_specs=[pl.BlockSpec((s, 2 * NA_HD), lambda hp, b: (b, hp)),
                  pl.BlockSpec((None, 2 * NA_HD, s), lambda hp, b: (b, hp, 0)),
                  pl.BlockSpec((None, 2 * HEAD_V, s), lambda hp, b: (b, hp, 0)),
                  pl.BlockSpec((None, None, 2, NA_HD, PAST_LEN), lambda hp, b: (b, layer, hp, 0, 0)),
                  pl.BlockSpec((None, None, 2, NA_HD, PAST_LEN), lambda hp, b: (b, layer, hp, 0, 0)),
                  pl.BlockSpec((None, 2, len(NA_WINDOWS), Q_CHUNK, NA_WINDOW_MAX),
                               lambda hp, b: (layer, hp, 0, 0, 0)),
                  pl.BlockSpec((s, 2 * MLA_QK_PAD), lambda hp, b: (b, hp)),
                  pl.BlockSpec((None, 2 * MLA_QK_PAD, s), lambda hp, b: (b, hp, 0)),
                  pl.BlockSpec((None, 2 * HEAD_V, s), lambda hp, b: (b, hp, 0)),
                  pl.BlockSpec((None, None, 2 * MLA_QK_PAD, PAST_LEN), lambda hp, b: (layer, b, hp, 0)),
                  pl.BlockSpec((None, None, 2 * HEAD_V, PAST_LEN), lambda hp, b: (layer, b, hp, 0))],
        out_specs=[pl.BlockSpec((s, 2 * HEAD_V), lambda hp, b: (b, hp)),
                   pl.BlockSpec((s, 2 * HEAD_V), lambda hp, b: (b, hp))],
        compiler_params=_params(2),
        name="attn_lat",
    )(qna, knat, vnat, kctx, vctx, bias, qm, kmt, vmt, kmctx, vmctx)


def _mix_body(x_ref, yc_ref, ona_ref, om_ref, mod_ref, ng_ref, wg_ref, wco_ref, wno_ref, wmo_ref,
              wo_ref, o_ref):
    x = x_ref[...]
    mod = mod_ref[...]
    h = (_rms(x, ng_ref[...]) * (1 + mod[4:5]) + mod[3:4]).astype(BF16)
    z = jax.nn.sigmoid(_dot(h, wg_ref[:, 0:D_MODEL])) * _dot(yc_ref[...], wco_ref[...])
    z = z + jax.nn.sigmoid(_dot(h, wg_ref[:, D_MODEL:2 * D_MODEL])) * _dot(ona_ref[...], wno_ref[...])
    z = z + jax.nn.sigmoid(_dot(h, wg_ref[:, 2 * D_MODEL:3 * D_MODEL])) * _dot(om_ref[...], wmo_ref[...])
    o_ref[...] = x + mod[5:6] * _dot(z.astype(BF16), wo_ref[...])


def _mix(x, yc, ona, om, mod, rows_per_mod, w, layer, tm=512):
    t = x.shape[0]
    tiles_per_mod = rows_per_mod // tm
    row = lambda n: pl.BlockSpec((tm, n), lambda i: (i, 0))
    return pl.pallas_call(
        _mix_body,
        out_shape=jax.ShapeDtypeStruct((t, D_MODEL), F32),
        grid=(t // tm,),
        in_specs=[row(D_MODEL), row(CONV_DIM), row(HD), row(HD),
                  pl.BlockSpec((None, N_MOD, D_MODEL), lambda i: (i // tiles_per_mod, 0, 0)),
                  _resident((None, 1, D_MODEL), (layer, 0, 0)),
                  _resident((None, D_MODEL, 3 * D_MODEL), (layer, 0, 0)),
                  _resident((None, CONV_DIM, D_MODEL), (layer, 0, 0)),
                  _resident((None, HD, D_MODEL), (layer, 0, 0)),
                  _resident((None, HD, D_MODEL), (layer, 0, 0)),
                  _resident((None, D_MODEL, D_MODEL), (layer, 0, 0))],
        out_specs=row(D_MODEL),
        compiler_params=_params(1),
        name="mixer_out",
    )(x, yc, ona, om, mod, w["ng1"], w["wgate"], w["wco"], w["wno"], w["wmo"], w["wo"])


def _rope_tables():
    half = MLA_ROPE // 2
    nf = half // 2
    inv = 1.0 / (ROPE_BASE ** (jnp.arange(nf, dtype=F32) / nf))
    t = jnp.arange(DEC_SEQ)
    rows = (t // GRID_W).astype(F32)[:, None] * inv[None, :]
    cols = (t % GRID_W).astype(F32)[:, None] * inv[None, :]
    cos = jnp.concatenate([jnp.cos(rows), jnp.cos(rows), jnp.cos(cols), jnp.cos(cols)], axis=-1)
    sin = jnp.concatenate([jnp.sin(rows), jnp.sin(rows), jnp.sin(cols), jnp.sin(cols)], axis=-1)
    pad = MLA_QK_PAD - MLA_NOPE - MLA_ROPE
    q_cos = jnp.concatenate([jnp.ones((DEC_SEQ, MLA_NOPE), F32), cos, jnp.zeros((DEC_SEQ, pad), F32)], axis=-1)
    q_sin = jnp.concatenate([jnp.zeros((DEC_SEQ, MLA_NOPE), F32), sin, jnp.zeros((DEC_SEQ, pad), F32)], axis=-1)
    return q_cos, q_sin, cos.T, sin.T


def _rope_swap(w):
    nf = MLA_ROPE // 4
    a, b, c, d = (w[..., i * nf:(i + 1) * nf] for i in range(4))
    return jnp.concatenate([-b, a, -d, c], axis=-1)


def _na_bias(rpb):
    rows = DEC_SEQ // GRID_W
    rows_per_chunk = Q_CHUNK // GRID_W
    key_rows = NA_WINDOW_MAX // GRID_W
    n_dr = 2 * NA_WIN_R - 1
    n_dc = 2 * NA_WIN_C - 1
    col = np.arange(GRID_W)
    c_start = np.clip(col - NA_WIN_C // 2, 0, GRID_W - NA_WIN_C)
    c_in = (col[None, :] >= c_start[:, None]) & (col[None, :] < c_start[:, None] + NA_WIN_C)
    dc = np.clip(col[None, :] - col[:, None] + (NA_WIN_C - 1), 0, n_dc - 1)
    pick_dc = (dc[None] == np.arange(n_dc)[:, None, None]).astype(np.float32)
    by_col = jnp.einsum("lhdj,jqk->lhdqk", rpb, jnp.asarray(pick_dc), precision=lax.Precision.HIGHEST)
    by_col = jnp.where(jnp.asarray(c_in), by_col, NEG_INF)
    r_start = np.clip(np.arange(rows) - NA_WIN_R // 2, 0, rows - NA_WIN_R)
    dr = np.zeros((len(NA_WINDOWS), rows_per_chunk, key_rows), np.int32)
    r_in = np.zeros(dr.shape, bool)
    for c, (start, count) in enumerate(NA_WINDOWS):
        for rl in range(rows_per_chunk):
            r = c * rows_per_chunk + rl
            assert start // GRID_W <= r_start[r] and r_start[r] + NA_WIN_R <= (start + count) // GRID_W
            for kl in range(count // GRID_W):
                rk = start // GRID_W + kl
                r_in[c, rl, kl] = r_start[r] <= rk < r_start[r] + NA_WIN_R
                dr[c, rl, kl] = np.clip(rk - r + NA_WIN_R - 1, 0, n_dr - 1)
    g = jnp.take(by_col, jnp.asarray(dr.reshape(-1)), axis=2)
    g = g.reshape(DEPTH, HEADS, len(NA_WINDOWS), rows_per_chunk, key_rows, GRID_W, GRID_W)
    g = g.transpose(0, 1, 2, 3, 5, 4, 6)
    g = jnp.where(jnp.asarray(r_in)[:, :, None, :, None], g, NEG_INF)
    return g.reshape(DEPTH, HEADS, len(NA_WINDOWS), Q_CHUNK, NA_WINDOW_MAX)


def _pack_weights(w_in, w_uq, w_ukv):
    c3 = 3 * CONV_DIM + 3 * HD
    t_last = lambda a: jnp.swapaxes(a, -1, -2)
    wa = w_in[..., :3 * CONV_DIM + HD]
    w_kv = w_in[..., 3 * CONV_DIM + HD:c3]
    w_lora = w_in[..., c3:c3 + Q_LORA + KV_LORA]
    w_kr = w_in[..., c3 + Q_LORA + KV_LORA:c3 + Q_LORA + KV_LORA + MLA_ROPE]
    wgate = w_in[..., c3 + Q_LORA + KV_LORA + MLA_ROPE:]
    wt = t_last(jnp.concatenate([w_kv, w_kr, _rope_swap(w_kr)], axis=-1))
    uq = w_uq.reshape(DEPTH, Q_LORA, MLA_HEADS, MLA_NOPE + MLA_ROPE)
    pad = MLA_QK_PAD - MLA_NOPE - MLA_ROPE
    zp = jnp.zeros(uq.shape[:-1] + (pad,), F32)
    zn = jnp.zeros(uq.shape[:-1] + (MLA_NOPE,), F32)
    q_ext = jnp.concatenate([uq, zp], axis=-1).reshape(DEPTH, Q_LORA, MLA_QK_W)
    q_sw = jnp.concatenate([zn, _rope_swap(uq[..., MLA_NOPE:]), zp], axis=-1).reshape(q_ext.shape)
    ukv = w_ukv.reshape(DEPTH, KV_LORA, MLA_HEADS, MLA_NOPE + MLA_V)
    zk = jnp.zeros(ukv.shape[:-1] + (MLA_QK_PAD - MLA_NOPE,), F32)
    wka = jnp.concatenate([ukv[..., :MLA_NOPE], zk], axis=-1).reshape(DEPTH, KV_LORA, MLA_QK_W)
    eye = jnp.concatenate([jnp.zeros((MLA_ROPE, MLA_NOPE), F32), jnp.eye(MLA_ROPE, dtype=F32),
                           jnp.zeros((MLA_ROPE, pad), F32)], axis=-1)
    wkb = jnp.broadcast_to(jnp.tile(eye, (1, MLA_HEADS))[None], (DEPTH, MLA_ROPE, MLA_QK_W))
    wuv = ukv[..., MLA_NOPE:].reshape(DEPTH, KV_LORA, HD)
    b = lambda a: a.astype(BF16)
    return dict(wa=b(wa), wb=b(w_lora), wt=b(wt), wgate=b(wgate),
                wq_lat=b(jnp.concatenate([q_ext, q_sw], axis=-1)), wq_ctx=b(q_ext),
                wkat=b(t_last(wka)), wkbt=b(t_last(wkb)), wuvt=b(t_last(wuv)))


def kernel(x_prompt, x_sample, cache_na_k, cache_na_v, cache_mla_ckv, cache_mla_krope, c, c_ctx,
           w_ada, b_ada, norm_g, w_ffn1_gate, w_ffn1_up, w_ffn1_down, w_ffn2_gate, w_ffn2_up, w_ffn2_down,
           w_in, conv_w, conv_b, na_rpb, mla_qnorm, w_uq, mla_kvnorm, w_ukv,
           w_conv_out, w_na_out, w_mla_out, w_o, final_g):
    b16 = lambda a: a.astype(BF16)
    t_last = lambda a: jnp.swapaxes(a, -1, -2)
    packed = _pack_weights(w_in, w_uq, w_ukv)
    shared = dict(conv_w=conv_w, conv_b=conv_b.reshape(DEPTH, 1, CONV_DIM),
                  qnorm=mla_qnorm.reshape(DEPTH, 1, Q_LORA), kvnorm=mla_kvnorm.reshape(DEPTH, 1, KV_LORA),
                  ng1=norm_g[:, 1:2], wco=b16(w_conv_out), wno=b16(w_na_out), wmo=b16(w_mla_out), wo=b16(w_o),
                  **{k: packed[k] for k in ("wa", "wb", "wt", "wgate", "wkat", "wkbt", "wuvt")})
    w_ctx = dict(shared, wq=packed["wq_ctx"])
    w_lat = dict(shared, wq=packed["wq_lat"])
    ffn_w = [(b16(w_ffn1_gate), b16(w_ffn1_up), b16(w_ffn1_down)),
             (b16(w_ffn2_gate), b16(w_ffn2_up), b16(w_ffn2_down))]
    final_row = final_g.reshape(1, D_MODEL)

    c_all = jnp.concatenate([c_ctx[None], c, jnp.zeros((MOD_ROWS - 1 - DEC_BATCH, D_MODEL), F32)], axis=0)
    mod = _modulation(c_all, w_ada, b_ada).reshape(DEPTH, MOD_ROWS, N_MOD, D_MODEL)

    tables = _rope_tables()
    na_bias = _na_bias(na_rpb)
    ctx_k_na = t_last(cache_na_k)
    ctx_v_na = t_last(cache_na_v)
    ctx_k_mla, ctx_v_mla = _ctx_kv(cache_mla_ckv, t_last(cache_mla_krope),
                                   packed["wkat"], packed["wkbt"], packed["wuvt"])

    xp = x_prompt.reshape(BATCH * SEQ, D_MODEL)
    xs = x_sample.reshape(DEC_BATCH * DEC_SEQ, D_MODEL)
    n_p = BATCH * SEQ
    caches = None
    for l in range(DEPTH):
        mod_p = mod[l, 0:1]
        mod_s = mod[l, 1:1 + DEC_BATCH]
        last = l == DEPTH - 1
        xp = _ffn(xp, mod_p, n_p, norm_g[l, 0:1], *ffn_w[0], l, 0)
        yc, qna, knat, vnat, qm, kmt, vmt, ckv, krt = _proj(xp, mod_p, n_p, w_ctx, l, SEQ, False, None, caches, 512)
        caches = (knat, vnat, ckv, krt)
        ona, om = _attn_ctx(qna, knat, vnat, qm, kmt, vmt, l, SEQ)
        xp = _mix(xp, yc, ona, om, mod_p, n_p, w_ctx, l)
        xp = _ffn(xp, mod_p, n_p, norm_g[l, 2:3], *ffn_w[1], l, 6, final_row if last else None)
        xs = _ffn(xs, mod_s, DEC_SEQ, norm_g[l, 0:1], *ffn_w[0], l, 0)
        yc, qna, knat, vnat, qm, kmt, vmt = _proj(xs, mod_s, DEC_SEQ, w_lat, l, DEC_SEQ, True, tables, None, DEC_SEQ)
        ona, om = _attn_lat(qna, knat, vnat, ctx_k_na, ctx_v_na, na_bias, qm, kmt, vmt, ctx_k_mla, ctx_v_mla, l)
        xs = _mix(xs, yc, ona, om, mod_s, DEC_SEQ, w_lat, l)
        xs = _ffn(xs, mod_s, DEC_SEQ, norm_g[l, 2:3], *ffn_w[1], l, 6, final_row if last else None)
    new_kt, new_vt, new_ckv, new_krt = caches
    return (xp.reshape(BATCH, SEQ, D_MODEL), xs.reshape(DEC_BATCH, DEC_SEQ, D_MODEL),
            t_last(new_kt), t_last(new_vt), new_ckv, t_last(new_krt))
```

```python
import functools

import jax
import jax.numpy as jnp
import numpy as np
from jax import lax
from jax.experimental import pallas as pl
from jax.experimental.pallas import tpu as pltpu

D_MODEL = 1024
BATCH = 32
SEQ = 256
DEPTH = 2
DEC_BATCH = 8
DEC_SEQ = 1024
PAST_LEN = 256
GRID_W = 64
CONV_DIM = 512
CONV_K = 3
NA_HEADS = 8
NA_HD = 64
NA_WIN_R = 8
NA_WIN_C = 16
MLA_HEADS = 8
MLA_NOPE = 64
MLA_ROPE = 32
MLA_V = 64
Q_LORA = 256
KV_LORA = 128
FFN_DIM = 2816
N_MOD = 9
ROPE_BASE = 10000.0
EPS = 1e-6
NEG_INF = -1e30
MLA_SCALE = (MLA_NOPE + MLA_ROPE) ** -0.5
NA_SCALE = NA_HD ** -0.5

HEADS = 8
HEAD_V = 64
HD = HEADS * NA_HD
MLA_QK_PAD = 128
MLA_QK_W = HEADS * MLA_QK_PAD
FFN_CHUNK = 256
Q_CHUNK = 256
MOD_ROWS = 16
VMEM_LIMIT = 56 * 1024 * 1024
NA_WINDOWS = ((0, 512), (0, 768), (256, 768), (512, 512))
NA_DR_MASKED = 2 * NA_WIN_R - 1


def _na_block_pairs():
    rows = DEC_SEQ // GRID_W
    r_start = np.clip(np.arange(rows) - NA_WIN_R // 2, 0, rows - NA_WIN_R)
    pairs, index = [], []
    for c, (start, count) in enumerate(NA_WINDOWS):
        index.append([])
        for rl in range(Q_CHUNK // GRID_W):
            r = c * (Q_CHUNK // GRID_W) + rl
            assert start // GRID_W <= r_start[r] and r_start[r] + NA_WIN_R <= (start + count) // GRID_W
            index[c].append([])
            for kp in range(count // (2 * GRID_W)):
                pair = []
                for rk in (start // GRID_W + 2 * kp, start // GRID_W + 2 * kp + 1):
                    inside = r_start[r] <= rk < r_start[r] + NA_WIN_R
                    pair.append(int(rk - r + NA_WIN_R - 1) if inside else NA_DR_MASKED)
                pair = tuple(pair)
                if pair not in pairs:
                    pairs.append(pair)
                index[c][rl].append(pairs.index(pair))
    return tuple(pairs), index


NA_BLOCK_PAIRS, NA_BLOCK_INDEX = _na_block_pairs()

BF16 = jnp.bfloat16
F32 = jnp.float32


def _dot(a, b):
    return jnp.dot(a, b, preferred_element_type=F32)


def _dot_nt(a, b):
    return lax.dot_general(a, b, (((1,), (1,)), ((), ())), preferred_element_type=F32)


def _rms(x, g):
    return x * lax.rsqrt(jnp.mean(x * x, axis=-1, keepdims=True) + EPS) * g


def _params(n_axes):
    return pltpu.CompilerParams(dimension_semantics=("arbitrary",) * n_axes,
                                vmem_limit_bytes=VMEM_LIMIT)


def _resident(shape, index):
    return pl.BlockSpec(shape, lambda *_: index, pipeline_mode=pl.Buffered(1))


def _mod_body(c_ref, w_ref, b_ref, o_ref):
    c = c_ref[...]
    a = c * jax.nn.sigmoid(c)
    o_ref[...] = jnp.dot(a, w_ref[...], precision=lax.Precision.HIGHEST,
                         preferred_element_type=F32) + b_ref[...]


def _modulation(c_all, w_ada, b_ada):
    n_col = N_MOD * D_MODEL
    tn = n_col // 4
    return pl.pallas_call(
        _mod_body,
        out_shape=jax.ShapeDtypeStruct((DEPTH, MOD_ROWS, n_col), F32),
        grid=(DEPTH, n_col // tn),
        in_specs=[pl.BlockSpec((MOD_ROWS, D_MODEL), lambda l, j: (0, 0)),
                  pl.BlockSpec((None, D_MODEL, tn), lambda l, j: (l, 0, j)),
                  pl.BlockSpec((None, 1, tn), lambda l, j: (l, 0, j))],
        out_specs=pl.BlockSpec((None, MOD_ROWS, tn), lambda l, j: (l, 0, j)),
        compiler_params=_params(2),
        name="modulation",
    )(c_all, w_ada, b_ada.reshape(DEPTH, 1, n_col))


def _ffn_body(x_ref, mod_ref, ng_ref, wg_ref, wu_ref, wd_ref, *rest, mod_off, final):
    if final:
        fg_ref, o_ref, a_scr = rest
    else:
        o_ref, a_scr = rest
    x = x_ref[...]
    mod = mod_ref[...]
    shift = mod[mod_off:mod_off + 1]
    scale = mod[mod_off + 1:mod_off + 2]
    gate = mod[mod_off + 2:mod_off + 3]
    h = (_rms(x, ng_ref[...]) * (1 + scale) + shift).astype(BF16)
    for f in range(FFN_DIM // FFN_CHUNK):
        cols = slice(f * FFN_CHUNK, (f + 1) * FFN_CHUNK)
        g = _dot(h, wg_ref[:, cols])
        u = _dot(h, wu_ref[:, cols])
        a_scr[:, cols] = (g * jax.nn.sigmoid(g) * u).astype(BF16)
    y = _dot(a_scr[...], wd_ref[...])
    out = x + 0.5 * gate * y
    if final:
        out = _rms(out, fg_ref[...])
    o_ref[...] = out


def _ffn(x, mod, rows_per_mod, ng, wg, wu, wd, layer, mod_off, final_g=None, tm=512):
    t = x.shape[0]
    tiles_per_mod = rows_per_mod // tm
    final = final_g is not None
    in_specs = [pl.BlockSpec((tm, D_MODEL), lambda i: (i, 0)),
                pl.BlockSpec((None, N_MOD, D_MODEL), lambda i: (i // tiles_per_mod, 0, 0)),
                _resident((1, D_MODEL), (0, 0)),
                _resident((None, D_MODEL, FFN_DIM), (layer, 0, 0)),
                _resident((None, D_MODEL, FFN_DIM), (layer, 0, 0)),
                _resident((None, FFN_DIM, D_MODEL), (layer, 0, 0))]
    args = [x, mod, ng, wg, wu, wd]
    if final:
        in_specs.append(_resident((1, D_MODEL), (0, 0)))
        args.append(final_g)
    return pl.pallas_call(
        functools.partial(_ffn_body, mod_off=mod_off, final=final),
        out_shape=jax.ShapeDtypeStruct((t, D_MODEL), F32),
        grid=(t // tm,),
        in_specs=in_specs,
        out_specs=pl.BlockSpec((tm, D_MODEL), lambda i: (i, 0)),
        scratch_shapes=[pltpu.VMEM((tm, FFN_DIM), BF16)],
        compiler_params=_params(1),
        name="ffn",
    )(*args)


def _proj_body(x_ref, mod_ref, ng_ref, wa_ref, wb_ref, wt_ref, cw_ref, cb_ref, qn_ref, kvn_ref,
               wq_ref, wkat_ref, wkbt_ref, wuvt_ref, *rest, seq_len, latent, n_alias):
    rest = rest[n_alias:]
    if latent:
        (qc_ref, qs_ref, kct_ref, kst_ref,
         yc_ref, qna_ref, knat_ref, vnat_ref, qm_ref, kmt_ref, vmt_ref) = rest
    else:
        (yc_ref, qna_ref, knat_ref, vnat_ref, qm_ref, kmt_ref, vmt_ref, ckv_ref, krt_ref) = rest
    tm = x_ref.shape[0]
    x = x_ref[...]
    mod = mod_ref[...]
    h = (_rms(x, ng_ref[...]) * (1 + mod[4:5]) + mod[3:4]).astype(BF16)

    u = _dot(h, wa_ref[:, 0:3 * CONV_DIM])
    v = u[:, CONV_DIM:2 * CONV_DIM] * u[:, 2 * CONV_DIM:3 * CONV_DIM]
    pos = lax.broadcasted_iota(jnp.int32, (tm, 1), 0) % seq_len
    v_prev = jnp.where(pos == 0, 0.0, pltpu.roll(v, 1, 0))
    v_next = jnp.where(pos == seq_len - 1, 0.0, pltpu.roll(v, tm - 1, 0))
    cw = cw_ref[...]
    y = cb_ref[...] + v_prev * cw[0:1]
    y = y + v * cw[1:2]
    y = y + v_next * cw[2:3]
    yc_ref[...] = (u[:, 0:CONV_DIM] * y).astype(BF16)

    qna_ref[...] = (_dot(h, wa_ref[:, 3 * CONV_DIM:3 * CONV_DIM + HD]) * NA_SCALE).astype(BF16)

    ut = _dot_nt(wt_ref[...], h)
    krt = ut[2 * HD:2 * HD + MLA_ROPE]

    u = _dot(h, wb_ref[...])
    cq = _rms(u[:, 0:Q_LORA], qn_ref[...]).astype(BF16)
    ckv = _rms(u[:, Q_LORA:Q_LORA + KV_LORA], kvn_ref[...])
    ckv_b = ckv.astype(BF16)
    q2 = _dot(cq, wq_ref[...])
    if latent:
        knat_ref[...] = ut[0:HD].astype(BF16)
        vnat_ref[...] = ut[HD:2 * HD].astype(BF16)
        krt = krt * kct_ref[...] + ut[2 * HD + MLA_ROPE:2 * HD + 2 * MLA_ROPE] * kst_ref[...]
        qc = qc_ref[...]
        qs = qs_ref[...]
        for hh in range(HEADS):
            cols = slice(hh * MLA_QK_PAD, (hh + 1) * MLA_QK_PAD)
            sw_cols = slice(MLA_QK_W + hh * MLA_QK_PAD, MLA_QK_W + (hh + 1) * MLA_QK_PAD)
            qm_ref[:, cols] = ((q2[:, cols] * qc + q2[:, sw_cols] * qs) * MLA_SCALE).astype(BF16)
    else:
        qm_ref[...] = (q2 * MLA_SCALE).astype(BF16)
    kmt = _dot_nt(wkat_ref[...], ckv_b) + _dot(wkbt_ref[...], krt.astype(BF16))
    vmt = _dot_nt(wuvt_ref[...], ckv_b)
    if latent:
        kmt_ref[...] = kmt.astype(BF16)
        vmt_ref[...] = vmt.astype(BF16)
    else:
        for b in range(tm // seq_len):
            rows = slice(b * seq_len, (b + 1) * seq_len)
            knat_ref[b] = ut[0:HD, rows].reshape(HEADS, NA_HD, seq_len)
            vnat_ref[b] = ut[HD:2 * HD, rows].reshape(HEADS, NA_HD, seq_len)
            ckv_ref[b] = ckv[rows]
            krt_ref[b] = krt[:, rows]
            kmt_ref[b] = kmt[:, rows].astype(BF16)
            vmt_ref[b] = vmt[:, rows].astype(BF16)


def _proj(x, mod, rows_per_mod, w, layer, seq_len, latent, tables, caches, tm):
    t = x.shape[0]
    n_seq = t // seq_len
    seq_per_tile = tm // seq_len
    tiles_per_mod = rows_per_mod // tm
    wb_cols = w["wb"].shape[-1]
    wq_cols = w["wq"].shape[-1]
    wt_rows = w["wt"].shape[-2]
    row = lambda n: pl.BlockSpec((tm, n), lambda i: (i, 0))
    in_specs = [row(D_MODEL),
                pl.BlockSpec((None, N_MOD, D_MODEL), lambda i: (i // tiles_per_mod, 0, 0)),
                _resident((None, 1, D_MODEL), (layer, 0, 0)),
                _resident((None, D_MODEL, 3 * CONV_DIM + HD), (layer, 0, 0)),
                _resident((None, D_MODEL, wb_cols), (layer, 0, 0)),
                _resident((None, wt_rows, D_MODEL), (layer, 0, 0)),
                _resident((None, CONV_K, CONV_DIM), (layer, 0, 0)),
                _resident((None, 1, CONV_DIM), (layer, 0, 0)),
                _resident((None, 1, Q_LORA), (layer, 0, 0)),
                _resident((None, 1, KV_LORA), (layer, 0, 0)),
                _resident((None, Q_LORA, wq_cols), (layer, 0, 0)),
                _resident((None, MLA_QK_W, KV_LORA), (layer, 0, 0)),
                _resident((None, MLA_QK_W, MLA_ROPE), (layer, 0, 0)),
                _resident((None, HD, KV_LORA), (layer, 0, 0))]
    args = [x, mod, w["ng1"], w["wa"], w["wb"], w["wt"], w["conv_w"], w["conv_b"], w["qnorm"], w["kvnorm"],
            w["wq"], w["wkat"], w["wkbt"], w["wuvt"]]
    out_shape = [jax.ShapeDtypeStruct((t, CONV_DIM), BF16),
                 jax.ShapeDtypeStruct((t, HD), BF16)]
    out_specs = [row(CONV_DIM), row(HD)]
    aliases = {}
    n_alias = 0
    if latent:
        assert tm == seq_len
        in_specs += [_resident((seq_len, MLA_QK_PAD), (0, 0)), _resident((seq_len, MLA_QK_PAD), (0, 0)),
                     _resident((MLA_ROPE, seq_len), (0, 0)), _resident((MLA_ROPE, seq_len), (0, 0))]
        args += list(tables)
        seq_blk = lambda n: pl.BlockSpec((None, n, seq_len), lambda i: (i, 0, 0))
        out_shape += [jax.ShapeDtypeStruct((n_seq, HD, seq_len), BF16),
                      jax.ShapeDtypeStruct((n_seq, HD, seq_len), BF16),
                      jax.ShapeDtypeStruct((t, MLA_QK_W), BF16),
                      jax.ShapeDtypeStruct((n_seq, MLA_QK_W, seq_len), BF16),
                      jax.ShapeDtypeStruct((n_seq, HD, seq_len), BF16)]
        out_specs += [seq_blk(HD), seq_blk(HD), row(MLA_QK_W), seq_blk(MLA_QK_W), seq_blk(HD)]
    else:
        if caches is not None:
            n_alias = len(caches)
            in_specs += [pl.BlockSpec(memory_space=pl.ANY)] * n_alias
            args += list(caches)
            aliases = {len(args) - n_alias + k: 2 + (0, 1, 5, 6)[k] for k in range(n_alias)}
        cache_blk = lambda *dims: pl.BlockSpec((seq_per_tile, None) + dims,
                                               lambda i: (i, layer) + (0,) * len(dims))
        seq_blk = lambda n: pl.BlockSpec((seq_per_tile, n, seq_len), lambda i: (i, 0, 0))
        out_shape += [jax.ShapeDtypeStruct((n_seq, DEPTH, HEADS, NA_HD, seq_len), F32),
                      jax.ShapeDtypeStruct((n_seq, DEPTH, HEADS, NA_HD, seq_len), F32),
                      jax.ShapeDtypeStruct((t, MLA_QK_W), BF16),
                      jax.ShapeDtypeStruct((n_seq, MLA_QK_W, seq_len), BF16),
                      jax.ShapeDtypeStruct((n_seq, HD, seq_len), BF16),
                      jax.ShapeDtypeStruct((n_seq, DEPTH, seq_len, KV_LORA), F32),
                      jax.ShapeDtypeStruct((n_seq, DEPTH, MLA_ROPE, seq_len), F32)]
        out_specs += [cache_blk(HEADS, NA_HD, seq_len), cache_blk(HEADS, NA_HD, seq_len), row(MLA_QK_W),
                      seq_blk(MLA_QK_W), seq_blk(HD), cache_blk(seq_len, KV_LORA), cache_blk(MLA_ROPE, seq_len)]
    return pl.pallas_call(
        functools.partial(_proj_body, seq_len=seq_len, latent=latent, n_alias=n_alias),
        out_shape=out_shape,
        grid=(t // tm,),
        in_specs=in_specs,
        out_specs=out_specs,
        input_output_aliases=aliases,
        compiler_params=_params(1),
        name="mixer_proj",
    )(*args)


def _ctxkv_body(ckv_ref, krt_ref, wkat_ref, wkbt_ref, wuvt_ref, k_ref, v_ref):
    ckv = ckv_ref[...].astype(BF16)
    krt = krt_ref[...].astype(BF16)
    k_ref[...] = (_dot_nt(wkat_ref[...], ckv) + _dot(wkbt_ref[...], krt)).astype(BF16)
    v_ref[...] = _dot_nt(wuvt_ref[...], ckv).astype(BF16)


def _ctx_kv(cache_ckv, cache_krt, wkat, wkbt, wuvt):
    return pl.pallas_call(
        _ctxkv_body,
        out_shape=[jax.ShapeDtypeStruct((DEPTH, DEC_BATCH, MLA_QK_W, PAST_LEN), BF16),
                   jax.ShapeDtypeStruct((DEPTH, DEC_BATCH, HD, PAST_LEN), BF16)],
        grid=(DEPTH, DEC_BATCH),
        in_specs=[pl.BlockSpec((None, None, PAST_LEN, KV_LORA), lambda l, b: (b, l, 0, 0)),
                  pl.BlockSpec((None, None, MLA_ROPE, PAST_LEN), lambda l, b: (b, l, 0, 0)),
                  pl.BlockSpec((None, MLA_QK_W, KV_LORA), lambda l, b: (l, 0, 0)),
                  pl.BlockSpec((None, MLA_QK_W, MLA_ROPE), lambda l, b: (l, 0, 0)),
                  pl.BlockSpec((None, HD, KV_LORA), lambda l, b: (l, 0, 0))],
        out_specs=[pl.BlockSpec((None, None, MLA_QK_W, PAST_LEN), lambda l, b: (l, b, 0, 0)),
                   pl.BlockSpec((None, None, HD, PAST_LEN), lambda l, b: (l, b, 0, 0))],
        compiler_params=_params(2),
        name="ctx_kv",
    )(cache_ckv, cache_krt, wkat, wkbt, wuvt)


def _softmax_pv(s, vt):
    m = jnp.max(s, axis=-1, keepdims=True)
    p = jnp.exp(s - m)
    den = jnp.sum(p, axis=-1, keepdims=True)
    return _dot_nt(p.astype(BF16), vt) / den


def _pair_slot(x, j):
    z = jnp.zeros_like(x)
    return jnp.concatenate([x, z] if j == 0 else [z, x], axis=0)


def _attn_ctx_body(qna_ref, knat_ref, vnat_ref, qm_ref, kmt_ref, vmt_ref, ona_ref, om_ref, *, seq_len):
    for b in range(qna_ref.shape[0] // seq_len):
        rows = slice(b * seq_len, (b + 1) * seq_len)
        for hp in range(HEADS // 2):
            pair = slice(hp * 2 * HEAD_V, (hp + 1) * 2 * HEAD_V)
            q = qna_ref[rows, pair]
            o_na = o_m = None
            for j in range(2):
                hh = 2 * hp + j
                kt = _pair_slot(knat_ref[b, hh].astype(BF16), j)
                vt = _pair_slot(vnat_ref[b, hh].astype(BF16), j)
                o = _softmax_pv(_dot(q, kt), vt)
                o_na = o if j == 0 else o_na + o
                qk = slice(hh * MLA_QK_PAD, (hh + 1) * MLA_QK_PAD)
                vt = _pair_slot(vmt_ref[b, hh * HEAD_V:(hh + 1) * HEAD_V, :], j)
                o = _softmax_pv(_dot(qm_ref[rows, qk], kmt_ref[b, qk, :]), vt)
                o_m = o if j == 0 else o_m + o
            ona_ref[rows, pair] = o_na.astype(BF16)
            om_ref[rows, pair] = o_m.astype(BF16)


def _attn_ctx(qna, knat, vnat, qm, kmt, vmt, layer, seq_len, tm=512):
    t = qna.shape[0]
    nb = tm // seq_len
    row = lambda n: pl.BlockSpec((tm, n), lambda i: (i, 0))
    cache_blk = pl.BlockSpec((nb, None, HEADS, NA_HD, seq_len), lambda i: (i, layer, 0, 0, 0))
    seq_blk = lambda n: pl.BlockSpec((nb, n, seq_len), lambda i: (i, 0, 0))
    return pl.pallas_call(
        functools.partial(_attn_ctx_body, seq_len=seq_len),
        out_shape=[jax.ShapeDtypeStruct((t, HD), BF16), jax.ShapeDtypeStruct((t, HD), BF16)],
        grid=(t // tm,),
        in_specs=[row(HD), cache_blk, cache_blk, row(MLA_QK_W), seq_blk(MLA_QK_W), seq_blk(HD)],
        out_specs=[row(HD), row(HD)],
        compiler_params=_params(1),
        name="attn_ctx",
    )(qna, knat, vnat, qm, kmt, vmt)


def _attn_lat_body(qna_ref, knat_ref, vnat_ref, kctx_ref, vctx_ref, bias_ref,
                   qm_ref, kmt_ref, vmt_ref, kmctx_ref, vmctx_ref, ona_ref, om_ref):
    cat = lambda *a: jnp.concatenate(a, axis=1)
    head = lambda j: slice(j * HEAD_V, (j + 1) * HEAD_V)
    kc = [_pair_slot(kctx_ref[j].astype(BF16), j) for j in range(2)]
    vc = [_pair_slot(vctx_ref[j].astype(BF16), j) for j in range(2)]
    for c, (start, count) in enumerate(NA_WINDOWS):
        rows = slice(c * Q_CHUNK, (c + 1) * Q_CHUNK)
        keys = slice(start, start + count)
        q = qna_ref[rows, :]
        for j in range(2):
            bias = jnp.concatenate(
                [cat(*[bias_ref[j, p] for p in NA_BLOCK_INDEX[c][rl]]) for rl in range(Q_CHUNK // GRID_W)], axis=0)
            s = cat(_dot(q, kc[j]), _dot(q, _pair_slot(knat_ref[head(j), keys], j)) + bias)
            o = _softmax_pv(s, cat(vc[j], _pair_slot(vnat_ref[head(j), keys], j)))
            o_na = o if j == 0 else o_na + o
        ona_ref[rows, :] = o_na.astype(BF16)
    kt, vt = [], []
    for j in range(2):
        qk = slice(j * MLA_QK_PAD, (j + 1) * MLA_QK_PAD)
        kt.append(cat(kmctx_ref[qk, :], kmt_ref[qk, :]))
        vt.append(_pair_slot(cat(vmctx_ref[head(j), :], vmt_ref[head(j), :]), j))
    for c in range(DEC_SEQ // Q_CHUNK):
        rows = slice(c * Q_CHUNK, (c + 1) * Q_CHUNK)
        for j in range(2):
            o = _softmax_pv(_dot(qm_ref[rows, j * MLA_QK_PAD:(j + 1) * MLA_QK_PAD], kt[j]), vt[j])
            o_m = o if j == 0 else o_m + o
        om_ref[rows, :] = o_m.astype(BF16)


def _attn_lat(qna, knat, vnat, kctx, vctx, bias, qm, kmt, vmt, kmctx, vmctx, layer):
    t = qna.shape[0]
    s = DEC_SEQ
    return pl.pallas_call(
        _attn_lat_body,
        out_shape=[jax.ShapeDtypeStruct((t, HD), BF16), jax.ShapeDtypeStruct((t, HD), BF16)],
        grid=(HEADS // 2, t // s),
        in_specs=[pl.BlockSpec((s, 2 * NA_HD), lambda hp, b: (b, hp)),
                  pl.BlockSpec((None, 2 * NA_HD, s), lambda hp, b: (b, hp, 0)),
                  pl.BlockSpec((None, 2 * HEAD_V, s), lambda hp, b: (b, hp, 0)),
                  pl.BlockSpec((None, None, 2, NA_HD, PAST_LEN), lambda hp, b: (b, layer, hp, 0, 0)),
                  pl.BlockSpec((None, None, 2, NA_HD, PAST_LEN), lambda hp, b: (b, layer, hp, 0, 0)),
                  pl.BlockSpec((None, 2, len(NA_BLOCK_PAIRS), GRID_W, 2 * GRID_W),
                               lambda hp, b: (layer, hp, 0, 0, 0)),
                  pl.BlockSpec((s, 2 * MLA_QK_PAD), lambda hp, b: (b, hp)),
                  pl.BlockSpec((None, 2 * MLA_QK_PAD, s), lambda hp, b: (b, hp, 0)),
                  pl.BlockSpec((None, 2 * HEAD_V, s), lambda hp, b: (b, hp, 0)),
                  pl.BlockSpec((None, None, 2 * MLA_QK_PAD, PAST_LEN), lambda hp, b: (layer, b, hp, 0)),
                  pl.BlockSpec((None, None, 2 * HEAD_V, PAST_LEN), lambda hp, b: (layer, b, hp, 0))],
        out_specs=[pl.BlockSpec((s, 2 * HEAD_V), lambda hp, b: (b, hp)),
                   pl.BlockSpec((s, 2 * HEAD_V), lambda hp, b: (b, hp))],
        compiler_params=_params(2),
        name="attn_lat",
    )(qna, knat, vnat, kctx, vctx, bias, qm, kmt, vmt, kmctx, vmctx)


def _mix_body(x_ref, yc_ref, ona_ref, om_ref, mod_ref, ng_ref, wg_ref, wco_ref, wno_ref, wmo_ref,
              wo_ref, o_ref):
    x = x_ref[...]
    mod = mod_ref[...]
    h = (_rms(x, ng_ref[...]) * (1 + mod[4:5]) + mod[3:4]).astype(BF16)
    z = jax.nn.sigmoid(_dot(h, wg_ref[:, 0:D_MODEL])) * _dot(yc_ref[...], wco_ref[...])
    z = z + jax.nn.sigmoid(_dot(h, wg_ref[:, D_MODEL:2 * D_MODEL])) * _dot(ona_ref[...], wno_ref[...])
    z = z + jax.nn.sigmoid(_dot(h, wg_ref[:, 2 * D_MODEL:3 * D_MODEL])) * _dot(om_ref[...], wmo_ref[...])
    o_ref[...] = x + mod[5:6] * _dot(z.astype(BF16), wo_ref[...])


def _mix(x, yc, ona, om, mod, rows_per_mod, w, layer, tm=512):
    t = x.shape[0]
    tiles_per_mod = rows_per_mod // tm
    row = lambda n: pl.BlockSpec((tm, n), lambda i: (i, 0))
    return pl.pallas_call(
        _mix_body,
        out_shape=jax.ShapeDtypeStruct((t, D_MODEL), F32),
        grid=(t // tm,),
        in_specs=[row(D_MODEL), row(CONV_DIM), row(HD), row(HD),
                  pl.BlockSpec((None, N_MOD, D_MODEL), lambda i: (i // tiles_per_mod, 0, 0)),
                  _resident((None, 1, D_MODEL), (layer, 0, 0)),
                  _resident((None, D_MODEL, 3 * D_MODEL), (layer, 0, 0)),
                  _resident((None, CONV_DIM, D_MODEL), (layer, 0, 0)),
                  _resident((None, HD, D_MODEL), (layer, 0, 0)),
                  _resident((None, HD, D_MODEL), (layer, 0, 0)),
                  _resident((None, D_MODEL, D_MODEL), (layer, 0, 0))],
        out_specs=row(D_MODEL),
        compiler_params=_params(1),
        name="mixer_out",
    )(x, yc, ona, om, mod, w["ng1"], w["wgate"], w["wco"], w["wno"], w["wmo"], w["wo"])


def _rope_tables():
    half = MLA_ROPE // 2
    nf = half // 2
    inv = 1.0 / (ROPE_BASE ** (jnp.arange(nf, dtype=F32) / nf))
    t = jnp.arange(DEC_SEQ)
    rows = (t // GRID_W).astype(F32)[:, None] * inv[None, :]
    cols = (t % GRID_W).astype(F32)[:, None] * inv[None, :]
    cos = jnp.concatenate([jnp.cos(rows), jnp.cos(rows), jnp.cos(cols), jnp.cos(cols)], axis=-1)
    sin = jnp.concatenate([jnp.sin(rows), jnp.sin(rows), jnp.sin(cols), jnp.sin(cols)], axis=-1)
    pad = MLA_QK_PAD - MLA_NOPE - MLA_ROPE
    q_cos = jnp.concatenate([jnp.ones((DEC_SEQ, MLA_NOPE), F32), cos, jnp.zeros((DEC_SEQ, pad), F32)], axis=-1)
    q_sin = jnp.concatenate([jnp.zeros((DEC_SEQ, MLA_NOPE), F32), sin, jnp.zeros((DEC_SEQ, pad), F32)], axis=-1)
    return q_cos, q_sin, cos.T, sin.T


def _rope_swap(w):
    nf = MLA_ROPE // 4
    a, b, c, d = (w[..., i * nf:(i + 1) * nf] for i in range(4))
    return jnp.concatenate([-b, a, -d, c], axis=-1)


def _na_bias(rpb):
    n_dc = 2 * NA_WIN_C - 1
    col = np.arange(GRID_W)
    c_start = np.clip(col - NA_WIN_C // 2, 0, GRID_W - NA_WIN_C)
    c_in = (col[None, :] >= c_start[:, None]) & (col[None, :] < c_start[:, None] + NA_WIN_C)
    dc = np.clip(col[None, :] - col[:, None] + (NA_WIN_C - 1), 0, n_dc - 1)
    pick_dc = (dc[None] == np.arange(n_dc)[:, None, None]).astype(np.float32)
    by_col = jnp.einsum("lhdj,jqk->lhdqk", rpb, jnp.asarray(pick_dc), precision=lax.Precision.HIGHEST)
    by_col = jnp.where(jnp.asarray(c_in), by_col, NEG_INF)
    masked = jnp.full(by_col.shape[:2] + (1, GRID_W, GRID_W), NEG_INF, F32)
    by_col = jnp.concatenate([by_col, masked], axis=2)
    left = jnp.stack([by_col[:, :, a] for a, _ in NA_BLOCK_PAIRS], axis=2)
    right = jnp.stack([by_col[:, :, b] for _, b in NA_BLOCK_PAIRS], axis=2)
    return jnp.concatenate([left, right], axis=-1)


def _pack_weights(w_in, w_uq, w_ukv):
    c3 = 3 * CONV_DIM + 3 * HD
    t_last = lambda a: jnp.swapaxes(a, -1, -2)
    wa = w_in[..., :3 * CONV_DIM + HD]
    w_kv = w_in[..., 3 * CONV_DIM + HD:c3]
    w_lora = w_in[..., c3:c3 + Q_LORA + KV_LORA]
    w_kr = w_in[..., c3 + Q_LORA + KV_LORA:c3 + Q_LORA + KV_LORA + MLA_ROPE]
    wgate = w_in[..., c3 + Q_LORA + KV_LORA + MLA_ROPE:]
    wt = t_last(jnp.concatenate([w_kv, w_kr, _rope_swap(w_kr)], axis=-1))
    uq = w_uq.reshape(DEPTH, Q_LORA, MLA_HEADS, MLA_NOPE + MLA_ROPE)
    pad = MLA_QK_PAD - MLA_NOPE - MLA_ROPE
    zp = jnp.zeros(uq.shape[:-1] + (pad,), F32)
    zn = jnp.zeros(uq.shape[:-1] + (MLA_NOPE,), F32)
    q_ext = jnp.concatenate([uq, zp], axis=-1).reshape(DEPTH, Q_LORA, MLA_QK_W)
    q_sw = jnp.concatenate([zn, _rope_swap(uq[..., MLA_NOPE:]), zp], axis=-1).reshape(q_ext.shape)
    ukv = w_ukv.reshape(DEPTH, KV_LORA, MLA_HEADS, MLA_NOPE + MLA_V)
    zk = jnp.zeros(ukv.shape[:-1] + (MLA_QK_PAD - MLA_NOPE,), F32)
    wka = jnp.concatenate([ukv[..., :MLA_NOPE], zk], axis=-1).reshape(DEPTH, KV_LORA, MLA_QK_W)
    eye = jnp.concatenate([jnp.zeros((MLA_ROPE, MLA_NOPE), F32), jnp.eye(MLA_ROPE, dtype=F32),
                           jnp.zeros((MLA_ROPE, pad), F32)], axis=-1)
    wkb = jnp.broadcast_to(jnp.tile(eye, (1, MLA_HEADS))[None], (DEPTH, MLA_ROPE, MLA_QK_W))
    wuv = ukv[..., MLA_NOPE:].reshape(DEPTH, KV_LORA, HD)
    b = lambda a: a.astype(BF16)
    return dict(wa=b(wa), wb=b(w_lora), wt=b(wt), wgate=b(wgate),
                wq_lat=b(jnp.concatenate([q_ext, q_sw], axis=-1)), wq_ctx=b(q_ext),
                wkat=b(t_last(wka)), wkbt=b(t_last(wkb)), wuvt=b(t_last(wuv)))


def kernel(x_prompt, x_sample, cache_na_k, cache_na_v, cache_mla_ckv, cache_mla_krope, c, c_ctx,
           w_ada, b_ada, norm_g, w_ffn1_gate, w_ffn1_up, w_ffn1_down, w_ffn2_gate, w_ffn2_up, w_ffn2_down,
           w_in, conv_w, conv_b, na_rpb, mla_qnorm, w_uq, mla_kvnorm, w_ukv,
           w_conv_out, w_na_out, w_mla_out, w_o, final_g):
    b16 = lambda a: a.astype(BF16)
    t_last = lambda a: jnp.swapaxes(a, -1, -2)
    packed = _pack_weights(w_in, w_uq, w_ukv)
    shared = dict(conv_w=conv_w, conv_b=conv_b.reshape(DEPTH, 1, CONV_DIM),
                  qnorm=mla_qnorm.reshape(DEPTH, 1, Q_LORA), kvnorm=mla_kvnorm.reshape(DEPTH, 1, KV_LORA),
                  ng1=norm_g[:, 1:2], wco=b16(w_conv_out), wno=b16(w_na_out), wmo=b16(w_mla_out), wo=b16(w_o),
                  **{k: packed[k] for k in ("wa", "wb", "wt", "wgate", "wkat", "wkbt", "wuvt")})
    w_ctx = dict(shared, wq=packed["wq_ctx"])
    w_lat = dict(shared, wq=packed["wq_lat"])
    ffn_w = [(b16(w_ffn1_gate), b16(w_ffn1_up), b16(w_ffn1_down)),
             (b16(w_ffn2_gate), b16(w_ffn2_up), b16(w_ffn2_down))]
    final_row = final_g.reshape(1, D_MODEL)

    c_all = jnp.concatenate([c_ctx[None], c, jnp.zeros((MOD_ROWS - 1 - DEC_BATCH, D_MODEL), F32)], axis=0)
    mod = _modulation(c_all, w_ada, b_ada).reshape(DEPTH, MOD_ROWS, N_MOD, D_MODEL)

    tables = _rope_tables()
    na_bias = _na_bias(na_rpb)
    ctx_k_na = t_last(cache_na_k)
    ctx_v_na = t_last(cache_na_v)
    ctx_k_mla, ctx_v_mla = _ctx_kv(cache_mla_ckv, t_last(cache_mla_krope),
                                   packed["wkat"], packed["wkbt"], packed["wuvt"])

    xp = x_prompt.reshape(BATCH * SEQ, D_MODEL)
    xs = x_sample.reshape(DEC_BATCH * DEC_SEQ, D_MODEL)
    n_p = BATCH * SEQ
    caches = None
    for l in range(DEPTH):
        mod_p = mod[l, 0:1]
        mod_s = mod[l, 1:1 + DEC_BATCH]
        last = l == DEPTH - 1
        xp = _ffn(xp, mod_p, n_p, norm_g[l, 0:1], *ffn_w[0], l, 0)
        yc, qna, knat, vnat, qm, kmt, vmt, ckv, krt = _proj(xp, mod_p, n_p, w_ctx, l, SEQ, False, None, caches, 512)
        caches = (knat, vnat, ckv, krt)
        ona, om = _attn_ctx(qna, knat, vnat, qm, kmt, vmt, l, SEQ)
        xp = _mix(xp, yc, ona, om, mod_p, n_p, w_ctx, l)
        xp = _ffn(xp, mod_p, n_p, norm_g[l, 2:3], *ffn_w[1], l, 6, final_row if last else None)
        xs = _ffn(xs, mod_s, DEC_SEQ, norm_g[l, 0:1], *ffn_w[0], l, 0)
        yc, qna, knat, vnat, qm, kmt, vmt = _proj(xs, mod_s, DEC_SEQ, w_lat, l, DEC_SEQ, True, tables, None, DEC_SEQ)
        ona, om = _attn_lat(qna, knat, vnat, ctx_k_na, ctx_v_na, na_bias, qm, kmt, vmt, ctx_k_mla, ctx_v_mla, l)
        xs = _mix(xs, yc, ona, om, mod_s, DEC_SEQ, w_lat, l)
        xs = _ffn(xs, mod_s, DEC_SEQ, norm_g[l, 2:3], *ffn_w[1], l, 6, final_row if last else None)
    new_kt, new_vt, new_ckv, new_krt = caches
    return (xp.reshape(BATCH, SEQ, D_MODEL), xs.reshape(DEC_BATCH, DEC_SEQ, D_MODEL),
            t_last(new_kt), t_last(new_vt), new_ckv, t_last(new_krt))
```

```python
import functools

import jax
import jax.numpy as jnp
import numpy as np
from jax import lax
from jax.experimental import pallas as pl
from jax.experimental.pallas import tpu as pltpu

D_MODEL = 1024
BATCH = 32
SEQ = 256
DEPTH = 2
DEC_BATCH = 8
DEC_SEQ = 1024
PAST_LEN = 256
GRID_W = 64
CONV_DIM = 512
CONV_K = 3
NA_HEADS = 8
NA_HD = 64
NA_WIN_R = 8
NA_WIN_C = 16
MLA_HEADS = 8
MLA_NOPE = 64
MLA_ROPE = 32
MLA_V = 64
Q_LORA = 256
KV_LORA = 128
FFN_DIM = 2816
N_MOD = 9
ROPE_BASE = 10000.0
EPS = 1e-6
NEG_INF = -1e30
MLA_SCALE = (MLA_NOPE + MLA_ROPE) ** -0.5
NA_SCALE = NA_HD ** -0.5

HEADS = 8
HEAD_V = 64
HD = HEADS * NA_HD
MLA_QK_PAD = 128
MLA_QK_W = HEADS * MLA_QK_PAD
FFN_CHUNK = 256
Q_CHUNK = 256
MOD_ROWS = 16
VMEM_LIMIT = 56 * 1024 * 1024
NA_WINDOWS = ((0, 512), (0, 768), (256, 768), (512, 512))
NA_DR_MASKED = 2 * NA_WIN_R - 1


def _na_block_pairs():
    rows = DEC_SEQ // GRID_W
    r_start = np.clip(np.arange(rows) - NA_WIN_R // 2, 0, rows - NA_WIN_R)
    pairs, index = [], []
    for c, (start, count) in enumerate(NA_WINDOWS):
        index.append([])
        for rl in range(Q_CHUNK // GRID_W):
            r = c * (Q_CHUNK // GRID_W) + rl
            assert start // GRID_W <= r_start[r] and r_start[r] + NA_WIN_R <= (start + count) // GRID_W
            index[c].append([])
            for kp in range(count // (2 * GRID_W)):
                pair = []
                for rk in (start // GRID_W + 2 * kp, start // GRID_W + 2 * kp + 1):
                    inside = r_start[r] <= rk < r_start[r] + NA_WIN_R
                    pair.append(int(rk - r + NA_WIN_R - 1) if inside else NA_DR_MASKED)
                pair = tuple(pair)
                if pair not in pairs:
                    pairs.append(pair)
                index[c][rl].append(pairs.index(pair))
    return tuple(pairs), index


NA_BLOCK_PAIRS, NA_BLOCK_INDEX = _na_block_pairs()

BF16 = jnp.bfloat16
F32 = jnp.float32


def _dot(a, b):
    return jnp.dot(a, b, preferred_element_type=F32)


def _dot_nt(a, b):
    return lax.dot_general(a, b, (((1,), (1,)), ((), ())), preferred_element_type=F32)


def _rms(x, g):
    return x * lax.rsqrt(jnp.mean(x * x, axis=-1, keepdims=True) + EPS) * g


def _params(n_axes):
    return pltpu.CompilerParams(dimension_semantics=("arbitrary",) * n_axes,
                                vmem_limit_bytes=VMEM_LIMIT)


def _resident(shape, index):
    return pl.BlockSpec(shape, lambda *_: index, pipeline_mode=pl.Buffered(1))


def _mod_body(c_ref, w_ref, b_ref, o_ref):
    c = c_ref[...]
    a = c * jax.nn.sigmoid(c)
    o_ref[...] = _dot(a.astype(BF16), w_ref[...].astype(BF16)) + b_ref[...]


def _modulation(c_all, w_ada, b_ada):
    n_col = N_MOD * D_MODEL
    tn = n_col // 4
    return pl.pallas_call(
        _mod_body,
        out_shape=jax.ShapeDtypeStruct((DEPTH, MOD_ROWS, n_col), F32),
        grid=(DEPTH, n_col // tn),
        in_specs=[pl.BlockSpec((MOD_ROWS, D_MODEL), lambda l, j: (0, 0)),
                  pl.BlockSpec((None, D_MODEL, tn), lambda l, j: (l, 0, j)),
                  pl.BlockSpec((None, 1, tn), lambda l, j: (l, 0, j))],
        out_specs=pl.BlockSpec((None, MOD_ROWS, tn), lambda l, j: (l, 0, j)),
        compiler_params=_params(2),
        name="modulation",
    )(c_all, w_ada, b_ada.reshape(DEPTH, 1, n_col))


def _ffn_body(x_ref, mod_ref, ng_ref, wg_ref, wu_ref, wd_ref, *rest, mod_off, final):
    if final:
        fg_ref, o_ref, a_scr = rest
    else:
        o_ref, a_scr = rest
    x = x_ref[...]
    mod = mod_ref[...]
    shift = mod[mod_off:mod_off + 1]
    scale = mod[mod_off + 1:mod_off + 2]
    gate = mod[mod_off + 2:mod_off + 3]
    h = (_rms(x, ng_ref[...]) * (1 + scale) + shift).astype(BF16)
    for f in range(FFN_DIM // FFN_CHUNK):
        cols = slice(f * FFN_CHUNK, (f + 1) * FFN_CHUNK)
        g = _dot(h, wg_ref[:, cols])
        u = _dot(h, wu_ref[:, cols])
        a_scr[:, cols] = (g * jax.nn.sigmoid(g) * u).astype(BF16)
    y = _dot(a_scr[...], wd_ref[...])
    out = x + 0.5 * gate * y
    if final:
        out = _rms(out, fg_ref[...])
    o_ref[...] = out


def _ffn(x, mod, rows_per_mod, ng, wg, wu, wd, layer, mod_off, final_g=None, tm=1024):
    t = x.shape[0]
    tiles_per_mod = rows_per_mod // tm
    final = final_g is not None
    in_specs = [pl.BlockSpec((tm, D_MODEL), lambda i: (i, 0)),
                pl.BlockSpec((None, N_MOD, D_MODEL), lambda i: (i // tiles_per_mod, 0, 0)),
                _resident((1, D_MODEL), (0, 0)),
                _resident((None, D_MODEL, FFN_DIM), (layer, 0, 0)),
                _resident((None, D_MODEL, FFN_DIM), (layer, 0, 0)),
                _resident((None, FFN_DIM, D_MODEL), (layer, 0, 0))]
    args = [x, mod, ng, wg, wu, wd]
    if final:
        in_specs.append(_resident((1, D_MODEL), (0, 0)))
        args.append(final_g)
    return pl.pallas_call(
        functools.partial(_ffn_body, mod_off=mod_off, final=final),
        out_shape=jax.ShapeDtypeStruct((t, D_MODEL), F32),
        grid=(t // tm,),
        in_specs=in_specs,
        out_specs=pl.BlockSpec((tm, D_MODEL), lambda i: (i, 0)),
        scratch_shapes=[pltpu.VMEM((tm, FFN_DIM), BF16)],
        compiler_params=_params(1),
        name="ffn",
    )(*args)


def _proj_body(x_ref, mod_ref, ng_ref, wa_ref, wb_ref, wt_ref, cw_ref, cb_ref, qn_ref, kvn_ref,
               wq_ref, wkat_ref, wkbt_ref, wuvt_ref, *rest, seq_len, latent, n_alias, own_slot):
    rest = rest[n_alias:]

    def put(ref, b, value):
        for k in range(ref.shape[1]):
            ref[b, k] = value if k == own_slot else jnp.zeros_like(value)

    if latent:
        (qc_ref, qs_ref, kct_ref, kst_ref,
         yc_ref, qna_ref, knat_ref, vnat_ref, qm_ref, kmt_ref, vmt_ref) = rest
    else:
        (yc_ref, qna_ref, knat_ref, vnat_ref, qm_ref, kmt_ref, vmt_ref, ckv_ref, krt_ref) = rest
    tm = x_ref.shape[0]
    x = x_ref[...]
    mod = mod_ref[...]
    h = (_rms(x, ng_ref[...]) * (1 + mod[4:5]) + mod[3:4]).astype(BF16)

    u = _dot(h, wa_ref[:, 0:3 * CONV_DIM])
    v = u[:, CONV_DIM:2 * CONV_DIM] * u[:, 2 * CONV_DIM:3 * CONV_DIM]
    pos = lax.broadcasted_iota(jnp.int32, (tm, 1), 0) % seq_len
    v_prev = jnp.where(pos == 0, 0.0, pltpu.roll(v, 1, 0))
    v_next = jnp.where(pos == seq_len - 1, 0.0, pltpu.roll(v, tm - 1, 0))
    cw = cw_ref[...]
    y = cb_ref[...] + v_prev * cw[0:1]
    y = y + v * cw[1:2]
    y = y + v_next * cw[2:3]
    yc_ref[...] = (u[:, 0:CONV_DIM] * y).astype(BF16)

    qna_ref[...] = (_dot(h, wa_ref[:, 3 * CONV_DIM:3 * CONV_DIM + HD]) * NA_SCALE).astype(BF16)

    ut = _dot_nt(wt_ref[...], h)
    krt = ut[2 * HD:2 * HD + MLA_ROPE]

    u = _dot(h, wb_ref[...])
    cq = _rms(u[:, 0:Q_LORA], qn_ref[...]).astype(BF16)
    ckv = _rms(u[:, Q_LORA:Q_LORA + KV_LORA], kvn_ref[...])
    ckv_b = ckv.astype(BF16)
    q2 = _dot(cq, wq_ref[...])
    if latent:
        knat_ref[...] = ut[0:HD].astype(BF16)
        vnat_ref[...] = ut[HD:2 * HD].astype(BF16)
        krt = krt * kct_ref[...] + ut[2 * HD + MLA_ROPE:2 * HD + 2 * MLA_ROPE] * kst_ref[...]
        qc = qc_ref[...]
        qs = qs_ref[...]
        for hh in range(HEADS):
            cols = slice(hh * MLA_QK_PAD, (hh + 1) * MLA_QK_PAD)
            sw_cols = slice(MLA_QK_W + hh * MLA_QK_PAD, MLA_QK_W + (hh + 1) * MLA_QK_PAD)
            qm_ref[:, cols] = ((q2[:, cols] * qc + q2[:, sw_cols] * qs) * MLA_SCALE).astype(BF16)
    else:
        qm_ref[...] = (q2 * MLA_SCALE).astype(BF16)
    kmt = _dot_nt(wkat_ref[...], ckv_b) + _dot(wkbt_ref[...], krt.astype(BF16))
    vmt = _dot_nt(wuvt_ref[...], ckv_b)
    if latent:
        kmt_ref[...] = kmt.astype(BF16)
        vmt_ref[...] = vmt.astype(BF16)
    else:
        for b in range(tm // seq_len):
            rows = slice(b * seq_len, (b + 1) * seq_len)
            put(knat_ref, b, ut[0:HD, rows].reshape(HEADS, NA_HD, seq_len))
            put(vnat_ref, b, ut[HD:2 * HD, rows].reshape(HEADS, NA_HD, seq_len))
            put(ckv_ref, b, ckv[rows])
            put(krt_ref, b, krt[:, rows])
            kmt_ref[b] = kmt[:, rows].astype(BF16)
            vmt_ref[b] = vmt[:, rows].astype(BF16)


def _proj(x, mod, rows_per_mod, w, layer, seq_len, latent, tables, caches, tm):
    t = x.shape[0]
    n_seq = t // seq_len
    seq_per_tile = tm // seq_len
    tiles_per_mod = rows_per_mod // tm
    wb_cols = w["wb"].shape[-1]
    wq_cols = w["wq"].shape[-1]
    wt_rows = w["wt"].shape[-2]
    row = lambda n: pl.BlockSpec((tm, n), lambda i: (i, 0))
    in_specs = [row(D_MODEL),
                pl.BlockSpec((None, N_MOD, D_MODEL), lambda i: (i // tiles_per_mod, 0, 0)),
                _resident((None, 1, D_MODEL), (layer, 0, 0)),
                _resident((None, D_MODEL, 3 * CONV_DIM + HD), (layer, 0, 0)),
                _resident((None, D_MODEL, wb_cols), (layer, 0, 0)),
                _resident((None, wt_rows, D_MODEL), (layer, 0, 0)),
                _resident((None, CONV_K, CONV_DIM), (layer, 0, 0)),
                _resident((None, 1, CONV_DIM), (layer, 0, 0)),
                _resident((None, 1, Q_LORA), (layer, 0, 0)),
                _resident((None, 1, KV_LORA), (layer, 0, 0)),
                _resident((None, Q_LORA, wq_cols), (layer, 0, 0)),
                _resident((None, MLA_QK_W, KV_LORA), (layer, 0, 0)),
                _resident((None, MLA_QK_W, MLA_ROPE), (layer, 0, 0)),
                _resident((None, HD, KV_LORA), (layer, 0, 0))]
    args = [x, mod, w["ng1"], w["wa"], w["wb"], w["wt"], w["conv_w"], w["conv_b"], w["qnorm"], w["kvnorm"],
            w["wq"], w["wkat"], w["wkbt"], w["wuvt"]]
    out_shape = [jax.ShapeDtypeStruct((t, CONV_DIM), BF16),
                 jax.ShapeDtypeStruct((t, HD), BF16)]
    out_specs = [row(CONV_DIM), row(HD)]
    aliases = {}
    n_alias = 0
    own_slot = 0
    if latent:
        assert tm == seq_len
        in_specs += [_resident((seq_len, MLA_QK_PAD), (0, 0)), _resident((seq_len, MLA_QK_PAD), (0, 0)),
                     _resident((MLA_ROPE, seq_len), (0, 0)), _resident((MLA_ROPE, seq_len), (0, 0))]
        args += list(tables)
        seq_blk = lambda n: pl.BlockSpec((None, n, seq_len), lambda i: (i, 0, 0))
        out_shape += [jax.ShapeDtypeStruct((n_seq, HD, seq_len), BF16),
                      jax.ShapeDtypeStruct((n_seq, HD, seq_len), BF16),
                      jax.ShapeDtypeStruct((t, MLA_QK_W), BF16),
                      jax.ShapeDtypeStruct((n_seq, MLA_QK_W, seq_len), BF16),
                      jax.ShapeDtypeStruct((n_seq, HD, seq_len), BF16)]
        out_specs += [seq_blk(HD), seq_blk(HD), row(MLA_QK_W), seq_blk(MLA_QK_W), seq_blk(HD)]
    else:
        if caches is not None:
            n_alias = len(caches)
            in_specs += [pl.BlockSpec(memory_space=pl.ANY)] * n_alias
            args += list(caches)
            aliases = {len(args) - n_alias + k: 2 + (0, 1, 5, 6)[k] for k in range(n_alias)}
            n_slots, first_slot = 1, layer
        else:
            assert layer == 0
            n_slots, first_slot, own_slot = DEPTH, 0, layer
        cache_blk = lambda *dims: pl.BlockSpec((seq_per_tile, n_slots) + dims,
                                               lambda i: (i, first_slot) + (0,) * len(dims))
        seq_blk = lambda n: pl.BlockSpec((seq_per_tile, n, seq_len), lambda i: (i, 0, 0))
        out_shape += [jax.ShapeDtypeStruct((n_seq, DEPTH, HEADS, NA_HD, seq_len), F32),
                      jax.ShapeDtypeStruct((n_seq, DEPTH, HEADS, NA_HD, seq_len), F32),
                      jax.ShapeDtypeStruct((t, MLA_QK_W), BF16),
                      jax.ShapeDtypeStruct((n_seq, MLA_QK_W, seq_len), BF16),
                      jax.ShapeDtypeStruct((n_seq, HD, seq_len), BF16),
                      jax.ShapeDtypeStruct((n_seq, DEPTH, seq_len, KV_LORA), F32),
                      jax.ShapeDtypeStruct((n_seq, DEPTH, MLA_ROPE, seq_len), F32)]
        out_specs += [cache_blk(HEADS, NA_HD, seq_len), cache_blk(HEADS, NA_HD, seq_len), row(MLA_QK_W),
                      seq_blk(MLA_QK_W), seq_blk(HD), cache_blk(seq_len, KV_LORA), cache_blk(MLA_ROPE, seq_len)]
    return pl.pallas_call(
        functools.partial(_proj_body, seq_len=seq_len, latent=latent, n_alias=n_alias, own_slot=own_slot),
        out_shape=out_shape,
        grid=(t // tm,),
        in_specs=in_specs,
        out_specs=out_specs,
        input_output_aliases=aliases,
        compiler_params=_params(1),
        name="mixer_proj",
    )(*args)


def _ctxkv_body(ckv_ref, krt_ref, wkat_ref, wkbt_ref, wuvt_ref, k_ref, v_ref):
    ckv = ckv_ref[...].astype(BF16)
    krt = krt_ref[...].astype(BF16)
    k_ref[...] = (_dot_nt(wkat_ref[...], ckv) + _dot(wkbt_ref[...], krt)).astype(BF16)
    v_ref[...] = _dot_nt(wuvt_ref[...], ckv).astype(BF16)


def _ctx_kv(cache_ckv, cache_krt, wkat, wkbt, wuvt):
    return pl.pallas_call(
        _ctxkv_body,
        out_shape=[jax.ShapeDtypeStruct((DEPTH, DEC_BATCH, MLA_QK_W, PAST_LEN), BF16),
                   jax.ShapeDtypeStruct((DEPTH, DEC_BATCH, HD, PAST_LEN), BF16)],
        grid=(DEPTH, DEC_BATCH),
        in_specs=[pl.BlockSpec((None, None, PAST_LEN, KV_LORA), lambda l, b: (b, l, 0, 0)),
                  pl.BlockSpec((None, None, MLA_ROPE, PAST_LEN), lambda l, b: (b, l, 0, 0)),
                  pl.BlockSpec((None, MLA_QK_W, KV_LORA), lambda l, b: (l, 0, 0)),
                  pl.BlockSpec((None, MLA_QK_W, MLA_ROPE), lambda l, b: (l, 0, 0)),
                  pl.BlockSpec((None, HD, KV_LORA), lambda l, b: (l, 0, 0))],
        out_specs=[pl.BlockSpec((None, None, MLA_QK_W, PAST_LEN), lambda l, b: (l, b, 0, 0)),
                   pl.BlockSpec((None, None, HD, PAST_LEN), lambda l, b: (l, b, 0, 0))],
        compiler_params=_params(2),
        name="ctx_kv",
    )(cache_ckv, cache_krt, wkat, wkbt, wuvt)


def _softmax_pv(s, vt):
    m = jnp.max(s, axis=-1, keepdims=True)
    p = jnp.exp(s - m)
    den = jnp.sum(p, axis=-1, keepdims=True)
    return _dot_nt(p.astype(BF16), vt) / den


def _pair_slot(x, j):
    z = jnp.zeros_like(x)
    return jnp.concatenate([x, z] if j == 0 else [z, x], axis=0)


def _attn_ctx_body(qna_ref, knat_ref, vnat_ref, qm_ref, kmt_ref, vmt_ref, ona_ref, om_ref, *, seq_len):
    for b in range(qna_ref.shape[0] // seq_len):
        rows = slice(b * seq_len, (b + 1) * seq_len)
        for hp in range(HEADS // 2):
            pair = slice(hp * 2 * HEAD_V, (hp + 1) * 2 * HEAD_V)
            q = qna_ref[rows, pair]
            o_na = o_m = None
            for j in range(2):
                hh = 2 * hp + j
                kt = _pair_slot(knat_ref[b, hh].astype(BF16), j)
                vt = _pair_slot(vnat_ref[b, hh].astype(BF16), j)
                o = _softmax_pv(_dot(q, kt), vt)
                o_na = o if j == 0 else o_na + o
                qk = slice(hh * MLA_QK_PAD, (hh + 1) * MLA_QK_PAD)
                vt = _pair_slot(vmt_ref[b, hh * HEAD_V:(hh + 1) * HEAD_V, :], j)
                o = _softmax_pv(_dot(qm_ref[rows, qk], kmt_ref[b, qk, :]), vt)
                o_m = o if j == 0 else o_m + o
            ona_ref[rows, pair] = o_na.astype(BF16)
            om_ref[rows, pair] = o_m.astype(BF16)


def _attn_ctx(qna, knat, vnat, qm, kmt, vmt, layer, seq_len, tm=512):
    t = qna.shape[0]
    nb = tm // seq_len
    row = lambda n: pl.BlockSpec((tm, n), lambda i: (i, 0))
    cache_blk = pl.BlockSpec((nb, None, HEADS, NA_HD, seq_len), lambda i: (i, layer, 0, 0, 0))
    seq_blk = lambda n: pl.BlockSpec((nb, n, seq_len), lambda i: (i, 0, 0))
    return pl.pallas_call(
        functools.partial(_attn_ctx_body, seq_len=seq_len),
        out_shape=[jax.ShapeDtypeStruct((t, HD), BF16), jax.ShapeDtypeStruct((t, HD), BF16)],
        grid=(t // tm,),
        in_specs=[row(HD), cache_blk, cache_blk, row(MLA_QK_W), seq_blk(MLA_QK_W), seq_blk(HD)],
        out_specs=[row(HD), row(HD)],
        compiler_params=_params(1),
        name="attn_ctx",
    )(qna, knat, vnat, qm, kmt, vmt)


def _attn_lat_body(qna_ref, knat_ref, vnat_ref, kctx_ref, vctx_ref, bias_ref,
                   qm_ref, kmt_ref, vmt_ref, kmctx_ref, vmctx_ref, ona_ref, om_ref):
    cat = lambda *a: jnp.concatenate(a, axis=1)
    head = lambda j: slice(j * HEAD_V, (j + 1) * HEAD_V)
    kc = [_pair_slot(kctx_ref[j].astype(BF16), j) for j in range(2)]
    vc = [_pair_slot(vctx_ref[j].astype(BF16), j) for j in range(2)]
    for c, (start, count) in enumerate(NA_WINDOWS):
        rows = slice(c * Q_CHUNK, (c + 1) * Q_CHUNK)
        keys = slice(start, start + count)
        q = qna_ref[rows, :]
        for j in range(2):
            bias = jnp.concatenate(
                [cat(*[bias_ref[j, p] for p in NA_BLOCK_INDEX[c][rl]]) for rl in range(Q_CHUNK // GRID_W)], axis=0)
            s = cat(_dot(q, kc[j]), _dot(q, _pair_slot(knat_ref[head(j), keys], j)) + bias)
            o = _softmax_pv(s, cat(vc[j], _pair_slot(vnat_ref[head(j), keys], j)))
            o_na = o if j == 0 else o_na + o
        ona_ref[rows, :] = o_na.astype(BF16)
    kt, vt = [], []
    for j in range(2):
        qk = slice(j * MLA_QK_PAD, (j + 1) * MLA_QK_PAD)
        kt.append(cat(kmctx_ref[qk, :], kmt_ref[qk, :]))
        vt.append(_pair_slot(cat(vmctx_ref[head(j), :], vmt_ref[head(j), :]), j))
    for c in range(DEC_SEQ // Q_CHUNK):
        rows = slice(c * Q_CHUNK, (c + 1) * Q_CHUNK)
        for j in range(2):
            o = _softmax_pv(_dot(qm_ref[rows, j * MLA_QK_PAD:(j + 1) * MLA_QK_PAD], kt[j]), vt[j])
            o_m = o if j == 0 else o_m + o
        om_ref[rows, :] = o_m.astype(BF16)


def _attn_lat(qna, knat, vnat, kctx, vctx, bias, qm, kmt, vmt, kmctx, vmctx, layer):
    t = qna.shape[0]
    s = DEC_SEQ
    return pl.pallas_call(
        _attn_lat_body,
        out_shape=[jax.ShapeDtypeStruct((t, HD), BF16), jax.ShapeDtypeStruct((t, HD), BF16)],
        grid=(HEADS // 2, t // s),
        in_specs=[pl.BlockSpec((s, 2 * NA_HD), lambda hp, b: (b, hp)),
                  pl.BlockSpec((None, 2 * NA_HD, s), lambda hp, b: (b, hp, 0)),
                  pl.BlockSpec((None, 2 * HEAD_V, s), lambda hp, b: (b, hp, 0)),
                  pl.BlockSpec((None, None, 2, NA_HD, PAST_LEN), lambda hp, b: (b, layer, hp, 0, 0)),
                  pl.BlockSpec((None, None, 2, NA_HD, PAST_LEN), lambda hp, b: (b, layer, hp, 0, 0)),
                  pl.BlockSpec((None, 2, len(NA_BLOCK_PAIRS), GRID_W, 2 * GRID_W),
                               lambda hp, b: (layer, hp, 0, 0, 0)),
                  pl.BlockSpec((s, 2 * MLA_QK_PAD), lambda hp, b: (b, hp)),
                  pl.BlockSpec((None, 2 * MLA_QK_PAD, s), lambda hp, b: (b, hp, 0)),
                  pl.BlockSpec((None, 2 * HEAD_V, s), lambda hp, b: (b, hp, 0)),
                  pl.BlockSpec((None, None, 2 * MLA_QK_PAD, PAST_LEN), lambda hp, b: (layer, b, hp, 0)),
                  pl.BlockSpec((None, None, 2 * HEAD_V, PAST_LEN), lambda hp, b: (layer, b, hp, 0))],
        out_specs=[pl.BlockSpec((s, 2 * HEAD_V), lambda hp, b: (b, hp)),
                   pl.BlockSpec((s, 2 * HEAD_V), lambda hp, b: (b, hp))],
        compiler_params=_params(2),
        name="attn_lat",
    )(qna, knat, vnat, kctx, vctx, bias, qm, kmt, vmt, kmctx, vmctx)


def _mix_body(x_ref, yc_ref, ona_ref, om_ref, mod_ref, ng_ref, wg_ref, wco_ref, wno_ref, wmo_ref,
              wo_ref, o_ref):
    x = x_ref[...]
    mod = mod_ref[...]
    h = (_rms(x, ng_ref[...]) * (1 + mod[4:5]) + mod[3:4]).astype(BF16)
    z = jax.nn.sigmoid(_dot(h, wg_ref[:, 0:D_MODEL])) * _dot(yc_ref[...], wco_ref[...])
    z = z + jax.nn.sigmoid(_dot(h, wg_ref[:, D_MODEL:2 * D_MODEL])) * _dot(ona_ref[...], wno_ref[...])
    z = z + jax.nn.sigmoid(_dot(h, wg_ref[:, 2 * D_MODEL:3 * D_MODEL])) * _dot(om_ref[...], wmo_ref[...])
    o_ref[...] = x + mod[5:6] * _dot(z.astype(BF16), wo_ref[...])


def _mix(x, yc, ona, om, mod, rows_per_mod, w, layer, tm=1024):
    t = x.shape[0]
    tiles_per_mod = rows_per_mod // tm
    row = lambda n: pl.BlockSpec((tm, n), lambda i: (i, 0))
    return pl.pallas_call(
        _mix_body,
        out_shape=jax.ShapeDtypeStruct((t, D_MODEL), F32),
        grid=(t // tm,),
        in_specs=[row(D_MODEL), row(CONV_DIM), row(HD), row(HD),
                  pl.BlockSpec((None, N_MOD, D_MODEL), lambda i: (i // tiles_per_mod, 0, 0)),
                  _resident((None, 1, D_MODEL), (layer, 0, 0)),
                  _resident((None, D_MODEL, 3 * D_MODEL), (layer, 0, 0)),
                  _resident((None, CONV_DIM, D_MODEL), (layer, 0, 0)),
                  _resident((None, HD, D_MODEL), (layer, 0, 0)),
                  _resident((None, HD, D_MODEL), (layer, 0, 0)),
                  _resident((None, D_MODEL, D_MODEL), (layer, 0, 0))],
        out_specs=row(D_MODEL),
        compiler_params=_params(1),
        name="mixer_out",
    )(x, yc, ona, om, mod, w["ng1"], w["wgate"], w["wco"], w["wno"], w["wmo"], w["wo"])


def _rope_tables():
    f32 = np.float32
    half = MLA_ROPE // 2
    nf = half // 2
    inv = (f32(1.0) / (f32(ROPE_BASE) ** (np.arange(nf, dtype=f32) / f32(nf)))).astype(f32)
    t = np.arange(DEC_SEQ)
    rows = (t // GRID_W).astype(f32)[:, None] * inv[None, :]
    cols = (t % GRID_W).astype(f32)[:, None] * inv[None, :]
    cos = np.concatenate([np.cos(rows), np.cos(rows), np.cos(cols), np.cos(cols)], axis=-1).astype(f32)
    sin = np.concatenate([np.sin(rows), np.sin(rows), np.sin(cols), np.sin(cols)], axis=-1).astype(f32)
    pad = MLA_QK_PAD - MLA_NOPE - MLA_ROPE
    q_cos = np.concatenate([np.ones((DEC_SEQ, MLA_NOPE), f32), cos, np.zeros((DEC_SEQ, pad), f32)], axis=-1)
    q_sin = np.concatenate([np.zeros((DEC_SEQ, MLA_NOPE), f32), sin, np.zeros((DEC_SEQ, pad), f32)], axis=-1)
    return tuple(jnp.asarray(a) for a in (q_cos, q_sin, np.ascontiguousarray(cos.T), np.ascontiguousarray(sin.T)))


def _rope_swap(w):
    nf = MLA_ROPE // 4
    a, b, c, d = (w[..., i * nf:(i + 1) * nf] for i in range(4))
    return jnp.concatenate([-b, a, -d, c], axis=-1)


def _na_bias(rpb):
    n_dc = 2 * NA_WIN_C - 1
    col = np.arange(GRID_W)
    c_start = np.clip(col - NA_WIN_C // 2, 0, GRID_W - NA_WIN_C)
    c_in = (col[None, :] >= c_start[:, None]) & (col[None, :] < c_start[:, None] + NA_WIN_C)
    dc = np.clip(col[None, :] - col[:, None] + (NA_WIN_C - 1), 0, n_dc - 1)
    pick_dc = (dc[None] == np.arange(n_dc)[:, None, None]).astype(np.float32)
    n_pairs = len(NA_BLOCK_PAIRS)
    pick_dr = np.zeros((n_pairs, 2, NA_DR_MASKED), np.float32)
    for p, pair in enumerate(NA_BLOCK_PAIRS):
        for side, d in enumerate(pair):
            if d != NA_DR_MASKED:
                pick_dr[p, side, d] = 1.0
    keep = pick_dr.sum(-1).astype(bool)[:, None, :, None] & c_in[None, :, None, :]
    hi = lax.Precision.HIGHEST
    by_row = jnp.einsum("psd,lhdj->lhpsj", jnp.asarray(pick_dr), rpb, precision=hi)
    blocks = jnp.einsum("lhpsj,jqk->lhpqsk", by_row, jnp.asarray(pick_dc), precision=hi)
    blocks = jnp.where(jnp.asarray(keep), blocks, NEG_INF)
    return blocks.reshape(DEPTH, HEADS, n_pairs, GRID_W, 2 * GRID_W)


def _pack_weights(w_in, w_uq, w_ukv):
    c3 = 3 * CONV_DIM + 3 * HD
    t_last = lambda a: jnp.swapaxes(a, -1, -2)
    wa = w_in[..., :3 * CONV_DIM + HD]
    w_kv = w_in[..., 3 * CONV_DIM + HD:c3]
    w_lora = w_in[..., c3:c3 + Q_LORA + KV_LORA]
    w_kr = w_in[..., c3 + Q_LORA + KV_LORA:c3 + Q_LORA + KV_LORA + MLA_ROPE]
    wgate = w_in[..., c3 + Q_LORA + KV_LORA + MLA_ROPE:]
    wt = t_last(jnp.concatenate([w_kv, w_kr, _rope_swap(w_kr)], axis=-1))
    uq = w_uq.reshape(DEPTH, Q_LORA, MLA_HEADS, MLA_NOPE + MLA_ROPE)
    pad = MLA_QK_PAD - MLA_NOPE - MLA_ROPE
    zp = jnp.zeros(uq.shape[:-1] + (pad,), F32)
    zn = jnp.zeros(uq.shape[:-1] + (MLA_NOPE,), F32)
    q_ext = jnp.concatenate([uq, zp], axis=-1).reshape(DEPTH, Q_LORA, MLA_QK_W)
    q_sw = jnp.concatenate([zn, _rope_swap(uq[..., MLA_NOPE:]), zp], axis=-1).reshape(q_ext.shape)
    ukv = w_ukv.reshape(DEPTH, KV_LORA, MLA_HEADS, MLA_NOPE + MLA_V)
    zk = jnp.zeros(ukv.shape[:-1] + (MLA_QK_PAD - MLA_NOPE,), F32)
    wka = jnp.concatenate([ukv[..., :MLA_NOPE], zk], axis=-1).reshape(DEPTH, KV_LORA, MLA_QK_W)
    eye = jnp.concatenate([jnp.zeros((MLA_ROPE, MLA_NOPE), F32), jnp.eye(MLA_ROPE, dtype=F32),
                           jnp.zeros((MLA_ROPE, pad), F32)], axis=-1)
    wkb = jnp.broadcast_to(jnp.tile(eye, (1, MLA_HEADS))[None], (DEPTH, MLA_ROPE, MLA_QK_W))
    wuv = ukv[..., MLA_NOPE:].reshape(DEPTH, KV_LORA, HD)
    b = lambda a: a.astype(BF16)
    return dict(wa=b(wa), wb=b(w_lora), wt=b(wt), wgate=b(wgate),
                wq_lat=b(jnp.concatenate([q_ext, q_sw], axis=-1)), wq_ctx=b(q_ext),
                wkat=b(t_last(wka)), wkbt=b(t_last(wkb)), wuvt=b(t_last(wuv)))


def kernel(x_prompt, x_sample, cache_na_k, cache_na_v, cache_mla_ckv, cache_mla_krope, c, c_ctx,
           w_ada, b_ada, norm_g, w_ffn1_gate, w_ffn1_up, w_ffn1_down, w_ffn2_gate, w_ffn2_up, w_ffn2_down,
           w_in, conv_w, conv_b, na_rpb, mla_qnorm, w_uq, mla_kvnorm, w_ukv,
           w_conv_out, w_na_out, w_mla_out, w_o, final_g):
    b16 = lambda a: a.astype(BF16)
    t_last = lambda a: jnp.swapaxes(a, -1, -2)
    packed = _pack_weights(w_in, w_uq, w_ukv)
    shared = dict(conv_w=conv_w, conv_b=conv_b.reshape(DEPTH, 1, CONV_DIM),
                  qnorm=mla_qnorm.reshape(DEPTH, 1, Q_LORA), kvnorm=mla_kvnorm.reshape(DEPTH, 1, KV_LORA),
                  ng1=norm_g[:, 1:2], wco=b16(w_conv_out), wno=b16(w_na_out), wmo=b16(w_mla_out), wo=b16(w_o),
                  **{k: packed[k] for k in ("wa", "wb", "wt", "wgate", "wkat", "wkbt", "wuvt")})
    w_ctx = dict(shared, wq=packed["wq_ctx"])
    w_lat = dict(shared, wq=packed["wq_lat"])
    ffn_w = [(b16(w_ffn1_gate), b16(w_ffn1_up), b16(w_ffn1_down)),
             (b16(w_ffn2_gate), b16(w_ffn2_up), b16(w_ffn2_down))]
    final_row = final_g.reshape(1, D_MODEL)

    c_all = jnp.concatenate([c_ctx[None], c, jnp.zeros((MOD_ROWS - 1 - DEC_BATCH, D_MODEL), F32)], axis=0)
    mod = _modulation(c_all, w_ada, b_ada).reshape(DEPTH, MOD_ROWS, N_MOD, D_MODEL)

    tables = _rope_tables()
    na_bias = _na_bias(na_rpb)
    ctx_k_na = t_last(cache_na_k)
    ctx_v_na = t_last(cache_na_v)
    ctx_k_mla, ctx_v_mla = _ctx_kv(cache_mla_ckv, t_last(cache_mla_krope),
                                   packed["wkat"], packed["wkbt"], packed["wuvt"])

    xp = x_prompt.reshape(BATCH * SEQ, D_MODEL)
    xs = x_sample.reshape(DEC_BATCH * DEC_SEQ, D_MODEL)
    n_p = BATCH * SEQ
    caches = None
    for l in range(DEPTH):
        mod_p = mod[l, 0:1]
        mod_s = mod[l, 1:1 + DEC_BATCH]
        last = l == DEPTH - 1
        xp = _ffn(xp, mod_p, n_p, norm_g[l, 0:1], *ffn_w[0], l, 0)
        yc, qna, knat, vnat, qm, kmt, vmt, ckv, krt = _proj(xp, mod_p, n_p, w_ctx, l, SEQ, False, None, caches, 512)
        caches = (knat, vnat, ckv, krt)
        ona, om = _attn_ctx(qna, knat, vnat, qm, kmt, vmt, l, SEQ)
        xp = _mix(xp, yc, ona, om, mod_p, n_p, w_ctx, l)
        xp = _ffn(xp, mod_p, n_p, norm_g[l, 2:3], *ffn_w[1], l, 6, final_row if last else None)
        xs = _ffn(xs, mod_s, DEC_SEQ, norm_g[l, 0:1], *ffn_w[0], l, 0)
        yc, qna, knat, vnat, qm, kmt, vmt = _proj(xs, mod_s, DEC_SEQ, w_lat, l, DEC_SEQ, True, tables, None, DEC_SEQ)
        ona, om = _attn_lat(qna, knat, vnat, ctx_k_na, ctx_v_na, na_bias, qm, kmt, vmt, ctx_k_mla, ctx_v_mla, l)
        xs = _mix(xs, yc, ona, om, mod_s, DEC_SEQ, w_lat, l)
        xs = _ffn(xs, mod_s, DEC_SEQ, norm_g[l, 2:3], *ffn_w[1], l, 6, final_row if last else None)
    new_kt, new_vt, new_ckv, new_krt = caches
    return (xp.reshape(BATCH, SEQ, D_MODEL), xs.reshape(DEC_BATCH, DEC_SEQ, D_MODEL),
            t_last(new_kt), t_last(new_vt), new_ckv, t_last(new_krt))
```

```python
import functools

import jax
import jax.numpy as jnp
import numpy as np
from jax import lax
from jax.experimental import pallas as pl
from jax.experimental.pallas import tpu as pltpu

D_MODEL = 1024
BATCH = 32
SEQ = 256
DEPTH = 2
DEC_BATCH = 8
DEC_SEQ = 1024
PAST_LEN = 256
GRID_W = 64
CONV_DIM = 512
CONV_K = 3
NA_HEADS = 8
NA_HD = 64
NA_WIN_R = 8
NA_WIN_C = 16
MLA_HEADS = 8
MLA_NOPE = 64
MLA_ROPE = 32
MLA_V = 64
Q_LORA = 256
KV_LORA = 128
FFN_DIM = 2816
N_MOD = 9
ROPE_BASE = 10000.0
EPS = 1e-6
NEG_INF = -1e30
LOG2_E = 1.4426950408889634
MLA_SCALE = (MLA_NOPE + MLA_ROPE) ** -0.5 * LOG2_E
NA_SCALE = NA_HD ** -0.5 * LOG2_E

HEADS = 8
HEAD_V = 64
HD = HEADS * NA_HD
MLA_QK_PAD = 128
MLA_QK_W = HEADS * MLA_QK_PAD
FFN_CHUNK = 256
Q_CHUNK = 256
MOD_ROWS = 16
VMEM_LIMIT = 56 * 1024 * 1024
NA_WINDOWS = ((0, 512), (0, 768), (256, 768), (512, 512))
NA_DR_MASKED = 2 * NA_WIN_R - 1


def _na_block_pairs():
    rows = DEC_SEQ // GRID_W
    r_start = np.clip(np.arange(rows) - NA_WIN_R // 2, 0, rows - NA_WIN_R)
    pairs, index = [], []
    for c, (start, count) in enumerate(NA_WINDOWS):
        index.append([])
        for rl in range(Q_CHUNK // GRID_W):
            r = c * (Q_CHUNK // GRID_W) + rl
            assert start // GRID_W <= r_start[r] and r_start[r] + NA_WIN_R <= (start + count) // GRID_W
            index[c].append([])
            for kp in range(count // (2 * GRID_W)):
                pair = []
                for rk in (start // GRID_W + 2 * kp, start // GRID_W + 2 * kp + 1):
                    inside = r_start[r] <= rk < r_start[r] + NA_WIN_R
                    pair.append(int(rk - r + NA_WIN_R - 1) if inside else NA_DR_MASKED)
                pair = tuple(pair)
                if pair not in pairs:
                    pairs.append(pair)
                index[c][rl].append(pairs.index(pair))
    return tuple(pairs), index


NA_BLOCK_PAIRS, NA_BLOCK_INDEX = _na_block_pairs()

BF16 = jnp.bfloat16
F32 = jnp.float32


def _dot(a, b):
    return jnp.dot(a, b, preferred_element_type=F32)


def _dot_nt(a, b):
    return lax.dot_general(a, b, (((1,), (1,)), ((), ())), preferred_element_type=F32)


def _rms(x, g):
    return x * lax.rsqrt(jnp.mean(x * x, axis=-1, keepdims=True) + EPS) * g


def _params(n_axes, flags=None):
    return pltpu.CompilerParams(dimension_semantics=("arbitrary",) * n_axes,
                                vmem_limit_bytes=VMEM_LIMIT, flags=flags)


def _resident(shape, index):
    return pl.BlockSpec(shape, lambda *_: index, pipeline_mode=pl.Buffered(1))


def _mod_body(c_ref, w_ref, b_ref, o_ref):
    c = c_ref[...]
    a = c * jax.nn.sigmoid(c)
    o_ref[...] = _dot(a.astype(BF16), w_ref[...].astype(BF16)) + b_ref[...]


def _modulation(c_all, w_ada, b_ada):
    n_col = N_MOD * D_MODEL
    tn = n_col // 4
    return pl.pallas_call(
        _mod_body,
        out_shape=jax.ShapeDtypeStruct((DEPTH, MOD_ROWS, n_col), F32),
        grid=(DEPTH, n_col // tn),
        in_specs=[pl.BlockSpec((MOD_ROWS, D_MODEL), lambda l, j: (0, 0)),
                  pl.BlockSpec((None, D_MODEL, tn), lambda l, j: (l, 0, j)),
                  pl.BlockSpec((None, 1, tn), lambda l, j: (l, 0, j))],
        out_specs=pl.BlockSpec((None, MOD_ROWS, tn), lambda l, j: (l, 0, j)),
        compiler_params=_params(2),
        name="modulation",
    )(c_all, w_ada, b_ada.reshape(DEPTH, 1, n_col))


def _ffn_body(x_ref, mod_ref, ng_ref, wg_ref, wu_ref, wd_ref, *rest, mod_off, final):
    if final:
        fg_ref, o_ref, a_scr = rest
    else:
        o_ref, a_scr = rest
    x = x_ref[...]
    mod = mod_ref[...]
    shift = mod[mod_off:mod_off + 1]
    scale = mod[mod_off + 1:mod_off + 2]
    gate = mod[mod_off + 2:mod_off + 3]
    h = (_rms(x, ng_ref[...]) * (1 + scale) + shift).astype(BF16)
    for f in range(FFN_DIM // FFN_CHUNK):
        cols = slice(f * FFN_CHUNK, (f + 1) * FFN_CHUNK)
        g = _dot(h, wg_ref[:, cols])
        u = _dot(h, wu_ref[:, cols])
        a_scr[:, cols] = (g * jax.nn.sigmoid(g) * u).astype(BF16)
    y = _dot(a_scr[...], wd_ref[...])
    out = x + 0.5 * gate * y
    if final:
        out = _rms(out, fg_ref[...])
    o_ref[...] = out


def _ffn(x, mod, rows_per_mod, ng, wg, wu, wd, layer, mod_off, final_g=None, tm=1024):
    t = x.shape[0]
    tiles_per_mod = rows_per_mod // tm
    final = final_g is not None
    in_specs = [pl.BlockSpec((tm, D_MODEL), lambda i: (i, 0)),
                pl.BlockSpec((None, N_MOD, D_MODEL), lambda i: (i // tiles_per_mod, 0, 0)),
                _resident((1, D_MODEL), (0, 0)),
                _resident((None, D_MODEL, FFN_DIM), (layer, 0, 0)),
                _resident((None, D_MODEL, FFN_DIM), (layer, 0, 0)),
                _resident((None, FFN_DIM, D_MODEL), (layer, 0, 0))]
    args = [x, mod, ng, wg, wu, wd]
    if final:
        in_specs.append(_resident((1, D_MODEL), (0, 0)))
        args.append(final_g)
    return pl.pallas_call(
        functools.partial(_ffn_body, mod_off=mod_off, final=final),
        out_shape=jax.ShapeDtypeStruct((t, D_MODEL), F32),
        grid=(t // tm,),
        in_specs=in_specs,
        out_specs=pl.BlockSpec((tm, D_MODEL), lambda i: (i, 0)),
        scratch_shapes=[pltpu.VMEM((tm, FFN_DIM), BF16)],
        compiler_params=_params(1),
        name="ffn",
    )(*args)


def _proj_body(x_ref, mod_ref, ng_ref, wa_ref, wb_ref, wt_ref, cw_ref, cb_ref, qn_ref, kvn_ref,
               wq_ref, wkat_ref, wkbt_ref, wuvt_ref, *rest, seq_len, latent, n_alias, own_slot):
    rest = rest[n_alias:]

    def put(ref, b, value):
        for k in range(ref.shape[1]):
            ref[b, k] = value if k == own_slot else jnp.zeros_like(value)

    if latent:
        (qc_ref, qs_ref, kct_ref, kst_ref,
         yc_ref, qna_ref, knat_ref, vnat_ref, qm_ref, kmt_ref, vmt_ref) = rest
    else:
        (yc_ref, qna_ref, knat_ref, vnat_ref, qm_ref, kmt_ref, vmt_ref, ckv_ref, krt_ref) = rest
    tm = x_ref.shape[0]
    x = x_ref[...]
    mod = mod_ref[...]
    h = (_rms(x, ng_ref[...]) * (1 + mod[4:5]) + mod[3:4]).astype(BF16)

    u = _dot(h, wa_ref[:, 0:3 * CONV_DIM])
    v = u[:, CONV_DIM:2 * CONV_DIM] * u[:, 2 * CONV_DIM:3 * CONV_DIM]
    pos = lax.broadcasted_iota(jnp.int32, (tm, 1), 0) % seq_len
    v_prev = jnp.where(pos == 0, 0.0, pltpu.roll(v, 1, 0))
    v_next = jnp.where(pos == seq_len - 1, 0.0, pltpu.roll(v, tm - 1, 0))
    cw = cw_ref[...]
    y = cb_ref[...] + v_prev * cw[0:1]
    y = y + v * cw[1:2]
    y = y + v_next * cw[2:3]
    yc_ref[...] = (u[:, 0:CONV_DIM] * y).astype(BF16)

    qna_ref[...] = (_dot(h, wa_ref[:, 3 * CONV_DIM:3 * CONV_DIM + HD]) * NA_SCALE).astype(BF16)

    ut = _dot_nt(wt_ref[...], h)
    krt = ut[2 * HD:2 * HD + MLA_ROPE]

    u = _dot(h, wb_ref[...])
    cq = _rms(u[:, 0:Q_LORA], qn_ref[...]).astype(BF16)
    ckv = _rms(u[:, Q_LORA:Q_LORA + KV_LORA], kvn_ref[...])
    ckv_b = ckv.astype(BF16)
    q2 = _dot(cq, wq_ref[...])
    if latent:
        knat_ref[...] = ut[0:HD].astype(BF16)
        vnat_ref[...] = ut[HD:2 * HD].astype(BF16)
        krt = krt * kct_ref[...] + ut[2 * HD + MLA_ROPE:2 * HD + 2 * MLA_ROPE] * kst_ref[...]
        qc = qc_ref[...]
        qs = qs_ref[...]
        for hh in range(HEADS):
            cols = slice(hh * MLA_QK_PAD, (hh + 1) * MLA_QK_PAD)
            sw_cols = slice(MLA_QK_W + hh * MLA_QK_PAD, MLA_QK_W + (hh + 1) * MLA_QK_PAD)
            qm_ref[:, cols] = ((q2[:, cols] * qc + q2[:, sw_cols] * qs) * MLA_SCALE).astype(BF16)
    else:
        qm_ref[...] = (q2 * MLA_SCALE).astype(BF16)
    kmt = _dot_nt(wkat_ref[...], ckv_b) + _dot(wkbt_ref[...], krt.astype(BF16))
    vmt = _dot_nt(wuvt_ref[...], ckv_b)
    if latent:
        kmt_ref[...] = kmt.astype(BF16)
        vmt_ref[...] = vmt.astype(BF16)
    else:
        for b in range(tm // seq_len):
            rows = slice(b * seq_len, (b + 1) * seq_len)
            put(knat_ref, b, ut[0:HD, rows].reshape(HEADS, NA_HD, seq_len))
            put(vnat_ref, b, ut[HD:2 * HD, rows].reshape(HEADS, NA_HD, seq_len))
            put(ckv_ref, b, ckv[rows])
            put(krt_ref, b, krt[:, rows])
            kmt_ref[b] = kmt[:, rows].astype(BF16)
            vmt_ref[b] = vmt[:, rows].astype(BF16)


def _proj(x, mod, rows_per_mod, w, layer, seq_len, latent, tables, caches, tm):
    t = x.shape[0]
    n_seq = t // seq_len
    seq_per_tile = tm // seq_len
    tiles_per_mod = rows_per_mod // tm
    wb_cols = w["wb"].shape[-1]
    wq_cols = w["wq"].shape[-1]
    wt_rows = w["wt"].shape[-2]
    row = lambda n: pl.BlockSpec((tm, n), lambda i: (i, 0))
    in_specs = [row(D_MODEL),
                pl.BlockSpec((None, N_MOD, D_MODEL), lambda i: (i // tiles_per_mod, 0, 0)),
                _resident((None, 1, D_MODEL), (layer, 0, 0)),
                _resident((None, D_MODEL, 3 * CONV_DIM + HD), (layer, 0, 0)),
                _resident((None, D_MODEL, wb_cols), (layer, 0, 0)),
                _resident((None, wt_rows, D_MODEL), (layer, 0, 0)),
                _resident((None, CONV_K, CONV_DIM), (layer, 0, 0)),
                _resident((None, 1, CONV_DIM), (layer, 0, 0)),
                _resident((None, 1, Q_LORA), (layer, 0, 0)),
                _resident((None, 1, KV_LORA), (layer, 0, 0)),
                _resident((None, Q_LORA, wq_cols), (layer, 0, 0)),
                _resident((None, MLA_QK_W, KV_LORA), (layer, 0, 0)),
                _resident((None, MLA_QK_W, MLA_ROPE), (layer, 0, 0)),
                _resident((None, HD, KV_LORA), (layer, 0, 0))]
    args = [x, mod, w["ng1"], w["wa"], w["wb"], w["wt"], w["conv_w"], w["conv_b"], w["qnorm"], w["kvnorm"],
            w["wq"], w["wkat"], w["wkbt"], w["wuvt"]]
    out_shape = [jax.ShapeDtypeStruct((t, CONV_DIM), BF16),
                 jax.ShapeDtypeStruct((t, HD), BF16)]
    out_specs = [row(CONV_DIM), row(HD)]
    aliases = {}
    n_alias = 0
    own_slot = 0
    if latent:
        assert tm == seq_len
        in_specs += [_resident((seq_len, MLA_QK_PAD), (0, 0)), _resident((seq_len, MLA_QK_PAD), (0, 0)),
                     _resident((MLA_ROPE, seq_len), (0, 0)), _resident((MLA_ROPE, seq_len), (0, 0))]
        args += list(tables)
        seq_blk = lambda n: pl.BlockSpec((None, n, seq_len), lambda i: (i, 0, 0))
        out_shape += [jax.ShapeDtypeStruct((n_seq, HD, seq_len), BF16),
                      jax.ShapeDtypeStruct((n_seq, HD, seq_len), BF16),
                      jax.ShapeDtypeStruct((t, MLA_QK_W), BF16),
                      jax.ShapeDtypeStruct((n_seq, MLA_QK_W, seq_len), BF16),
                      jax.ShapeDtypeStruct((n_seq, HD, seq_len), BF16)]
        out_specs += [seq_blk(HD), seq_blk(HD), row(MLA_QK_W), seq_blk(MLA_QK_W), seq_blk(HD)]
    else:
        if caches is not None:
            n_alias = len(caches)
            in_specs += [pl.BlockSpec(memory_space=pl.ANY)] * n_alias
            args += list(caches)
            aliases = {len(args) - n_alias + k: 2 + (0, 1, 5, 6)[k] for k in range(n_alias)}
            n_slots, first_slot = 1, layer
        else:
            assert layer == 0
            n_slots, first_slot, own_slot = DEPTH, 0, layer
        cache_blk = lambda *dims: pl.BlockSpec((seq_per_tile, n_slots) + dims,
                                               lambda i: (i, first_slot) + (0,) * len(dims))
        seq_blk = lambda n: pl.BlockSpec((seq_per_tile, n, seq_len), lambda i: (i, 0, 0))
        out_shape += [jax.ShapeDtypeStruct((n_seq, DEPTH, HEADS, NA_HD, seq_len), F32),
                      jax.ShapeDtypeStruct((n_seq, DEPTH, HEADS, NA_HD, seq_len), F32),
                      jax.ShapeDtypeStruct((t, MLA_QK_W), BF16),
                      jax.ShapeDtypeStruct((n_seq, MLA_QK_W, seq_len), BF16),
                      jax.ShapeDtypeStruct((n_seq, HD, seq_len), BF16),
                      jax.ShapeDtypeStruct((n_seq, DEPTH, seq_len, KV_LORA), F32),
                      jax.ShapeDtypeStruct((n_seq, DEPTH, MLA_ROPE, seq_len), F32)]
        out_specs += [cache_blk(HEADS, NA_HD, seq_len), cache_blk(HEADS, NA_HD, seq_len), row(MLA_QK_W),
                      seq_blk(MLA_QK_W), seq_blk(HD), cache_blk(seq_len, KV_LORA), cache_blk(MLA_ROPE, seq_len)]
    return pl.pallas_call(
        functools.partial(_proj_body, seq_len=seq_len, latent=latent, n_alias=n_alias, own_slot=own_slot),
        out_shape=out_shape,
        grid=(t // tm,),
        in_specs=in_specs,
        out_specs=out_specs,
        input_output_aliases=aliases,
        compiler_params=_params(1),
        name="mixer_proj",
    )(*args)


def _ctxkv_body(ckv_ref, krt_ref, wkat_ref, wkbt_ref, wuvt_ref, k_ref, v_ref):
    ckv = ckv_ref[...].astype(BF16)
    krt = krt_ref[...].astype(BF16)
    k_ref[...] = (_dot_nt(wkat_ref[...], ckv) + _dot(wkbt_ref[...], krt)).astype(BF16)
    v_ref[...] = _dot_nt(wuvt_ref[...], ckv).astype(BF16)


def _ctx_kv(cache_ckv, cache_krt, wkat, wkbt, wuvt):
    return pl.pallas_call(
        _ctxkv_body,
        out_shape=[jax.ShapeDtypeStruct((DEPTH, DEC_BATCH, MLA_QK_W, PAST_LEN), BF16),
                   jax.ShapeDtypeStruct((DEPTH, DEC_BATCH, HD, PAST_LEN), BF16)],
        grid=(DEPTH, DEC_BATCH),
        in_specs=[pl.BlockSpec((None, None, PAST_LEN, KV_LORA), lambda l, b: (b, l, 0, 0)),
                  pl.BlockSpec((None, None, MLA_ROPE, PAST_LEN), lambda l, b: (b, l, 0, 0)),
                  pl.BlockSpec((None, MLA_QK_W, KV_LORA), lambda l, b: (l, 0, 0)),
                  pl.BlockSpec((None, MLA_QK_W, MLA_ROPE), lambda l, b: (l, 0, 0)),
                  pl.BlockSpec((None, HD, KV_LORA), lambda l, b: (l, 0, 0))],
        out_specs=[pl.BlockSpec((None, None, MLA_QK_W, PAST_LEN), lambda l, b: (l, b, 0, 0)),
                   pl.BlockSpec((None, None, HD, PAST_LEN), lambda l, b: (l, b, 0, 0))],
        compiler_params=_params(2),
        name="ctx_kv",
    )(cache_ckv, cache_krt, wkat, wkbt, wuvt)


def _softmax_pv(s, vt):
    m = jnp.max(s, axis=-1, keepdims=True)
    p = jnp.exp2(s - m)
    den = jnp.sum(p, axis=-1, keepdims=True)
    return _dot_nt(p.astype(BF16), vt) / den


def _pair_slot(x, j):
    z = jnp.zeros_like(x)
    return jnp.concatenate([x, z] if j == 0 else [z, x], axis=0)


def _attn_ctx_body(qna_ref, knat_ref, vnat_ref, qm_ref, kmt_ref, vmt_ref, ona_ref, om_ref, *, seq_len):
    for b in range(qna_ref.shape[0] // seq_len):
        rows = slice(b * seq_len, (b + 1) * seq_len)
        for hp in range(HEADS // 2):
            pair = slice(hp * 2 * HEAD_V, (hp + 1) * 2 * HEAD_V)
            q = qna_ref[rows, pair]
            o_na = o_m = None
            for j in range(2):
                hh = 2 * hp + j
                kt = _pair_slot(knat_ref[b, hh].astype(BF16), j)
                vt = _pair_slot(vnat_ref[b, hh].astype(BF16), j)
                o = _softmax_pv(_dot(q, kt), vt)
                o_na = o if j == 0 else o_na + o
                qk = slice(hh * MLA_QK_PAD, (hh + 1) * MLA_QK_PAD)
                vt = _pair_slot(vmt_ref[b, hh * HEAD_V:(hh + 1) * HEAD_V, :], j)
                o = _softmax_pv(_dot(qm_ref[rows, qk], kmt_ref[b, qk, :]), vt)
                o_m = o if j == 0 else o_m + o
            ona_ref[rows, pair] = o_na.astype(BF16)
            om_ref[rows, pair] = o_m.astype(BF16)


def _attn_ctx(qna, knat, vnat, qm, kmt, vmt, layer, seq_len, tm=512):
    t = qna.shape[0]
    nb = tm // seq_len
    row = lambda n: pl.BlockSpec((tm, n), lambda i: (i, 0))
    cache_blk = pl.BlockSpec((nb, None, HEADS, NA_HD, seq_len), lambda i: (i, layer, 0, 0, 0))
    seq_blk = lambda n: pl.BlockSpec((nb, n, seq_len), lambda i: (i, 0, 0))
    return pl.pallas_call(
        functools.partial(_attn_ctx_body, seq_len=seq_len),
        out_shape=[jax.ShapeDtypeStruct((t, HD), BF16), jax.ShapeDtypeStruct((t, HD), BF16)],
        grid=(t // tm,),
        in_specs=[row(HD), cache_blk, cache_blk, row(MLA_QK_W), seq_blk(MLA_QK_W), seq_blk(HD)],
        out_specs=[row(HD), row(HD)],
        compiler_params=_params(1),
        name="attn_ctx",
    )(qna, knat, vnat, qm, kmt, vmt)


def _attn_lat_body(qna_ref, knat_ref, vnat_ref, kctx_ref, vctx_ref, bias_ref,
                   qm_ref, kmt_ref, vmt_ref, kmctx_ref, vmctx_ref, ona_ref, om_ref):
    cat = lambda *a: jnp.concatenate(a, axis=1)
    head = lambda j: slice(j * HEAD_V, (j + 1) * HEAD_V)
    kc = [_pair_slot(kctx_ref[j].astype(BF16), j) for j in range(2)]
    vc = [_pair_slot(vctx_ref[j].astype(BF16), j) for j in range(2)]
    for c, (start, count) in enumerate(NA_WINDOWS):
        rows = slice(c * Q_CHUNK, (c + 1) * Q_CHUNK)
        keys = slice(start, start + count)
        q = qna_ref[rows, :]
        for j in range(2):
            bias = jnp.concatenate(
                [cat(*[bias_ref[j, p] for p in NA_BLOCK_INDEX[c][rl]]) for rl in range(Q_CHUNK // GRID_W)], axis=0)
            s = cat(_dot(q, kc[j]), _dot(q, _pair_slot(knat_ref[head(j), keys], j)) + bias)
            o = _softmax_pv(s, cat(vc[j], _pair_slot(vnat_ref[head(j), keys], j)))
            o_na = o if j == 0 else o_na + o
        ona_ref[rows, :] = o_na.astype(BF16)
    kt, vt = [], []
    for j in range(2):
        qk = slice(j * MLA_QK_PAD, (j + 1) * MLA_QK_PAD)
        kt.append(cat(kmctx_ref[qk, :], kmt_ref[qk, :]))
        vt.append(_pair_slot(cat(vmctx_ref[head(j), :], vmt_ref[head(j), :]), j))
    for c in range(DEC_SEQ // Q_CHUNK):
        rows = slice(c * Q_CHUNK, (c + 1) * Q_CHUNK)
        for j in range(2):
            o = _softmax_pv(_dot(qm_ref[rows, j * MLA_QK_PAD:(j + 1) * MLA_QK_PAD], kt[j]), vt[j])
            o_m = o if j == 0 else o_m + o
        om_ref[rows, :] = o_m.astype(BF16)


def _attn_lat(qna, knat, vnat, kctx, vctx, bias, qm, kmt, vmt, kmctx, vmctx, layer):
    t = qna.shape[0]
    s = DEC_SEQ
    return pl.pallas_call(
        _attn_lat_body,
        out_shape=[jax.ShapeDtypeStruct((t, HD), BF16), jax.ShapeDtypeStruct((t, HD), BF16)],
        grid=(HEADS // 2, t // s),
        in_specs=[pl.BlockSpec((s, 2 * NA_HD), lambda hp, b: (b, hp)),
                  pl.BlockSpec((None, 2 * NA_HD, s), lambda hp, b: (b, hp, 0)),
                  pl.BlockSpec((None, 2 * HEAD_V, s), lambda hp, b: (b, hp, 0)),
                  pl.BlockSpec((None, None, 2, NA_HD, PAST_LEN), lambda hp, b: (b, layer, hp, 0, 0)),
                  pl.BlockSpec((None, None, 2, NA_HD, PAST_LEN), lambda hp, b: (b, layer, hp, 0, 0)),
                  pl.BlockSpec((None, 2, len(NA_BLOCK_PAIRS), GRID_W, 2 * GRID_W),
                               lambda hp, b: (layer, hp, 0, 0, 0)),
                  pl.BlockSpec((s, 2 * MLA_QK_PAD), lambda hp, b: (b, hp)),
                  pl.BlockSpec((None, 2 * MLA_QK_PAD, s), lambda hp, b: (b, hp, 0)),
                  pl.BlockSpec((None, 2 * HEAD_V, s), lambda hp, b: (b, hp, 0)),
                  pl.BlockSpec((None, None, 2 * MLA_QK_PAD, PAST_LEN), lambda hp, b: (layer, b, hp, 0)),
                  pl.BlockSpec((None, None, 2 * HEAD_V, PAST_LEN), lambda hp, b: (layer, b, hp, 0))],
        out_specs=[pl.BlockSpec((s, 2 * HEAD_V), lambda hp, b: (b, hp)),
                   pl.BlockSpec((s, 2 * HEAD_V), lambda hp, b: (b, hp))],
        compiler_params=_params(2),
        name="attn_lat",
    )(qna, knat, vnat, kctx, vctx, bias, qm, kmt, vmt, kmctx, vmctx)


def _mix_body(x_ref, yc_ref, ona_ref, om_ref, mod_ref, ng_ref, wg_ref, wco_ref, wno_ref, wmo_ref,
              wo_ref, o_ref):
    x = x_ref[...]
    mod = mod_ref[...]
    h = (_rms(x, ng_ref[...]) * (1 + mod[4:5]) + mod[3:4]).astype(BF16)
    z = jax.nn.sigmoid(_dot(h, wg_ref[:, 0:D_MODEL])) * _dot(yc_ref[...], wco_ref[...])
    z = z + jax.nn.sigmoid(_dot(h, wg_ref[:, D_MODEL:2 * D_MODEL])) * _dot(ona_ref[...], wno_ref[...])
    z = z + jax.nn.sigmoid(_dot(h, wg_ref[:, 2 * D_MODEL:3 * D_MODEL])) * _dot(om_ref[...], wmo_ref[...])
    o_ref[...] = x + mod[5:6] * _dot(z.astype(BF16), wo_ref[...])


def _mix(x, yc, ona, om, mod, rows_per_mod, w, layer, tm=1024):
    t = x.shape[0]
    tiles_per_mod = rows_per_mod // tm
    row = lambda n: pl.BlockSpec((tm, n), lambda i: (i, 0))
    return pl.pallas_call(
        _mix_body,
        out_shape=jax.ShapeDtypeStruct((t, D_MODEL), F32),
        grid=(t // tm,),
        in_specs=[row(D_MODEL), row(CONV_DIM), row(HD), row(HD),
                  pl.BlockSpec((None, N_MOD, D_MODEL), lambda i: (i // tiles_per_mod, 0, 0)),
                  _resident((None, 1, D_MODEL), (layer, 0, 0)),
                  _resident((None, D_MODEL, 3 * D_MODEL), (layer, 0, 0)),
                  _resident((None, CONV_DIM, D_MODEL), (layer, 0, 0)),
                  _resident((None, HD, D_MODEL), (layer, 0, 0)),
                  _resident((None, HD, D_MODEL), (layer, 0, 0)),
                  _resident((None, D_MODEL, D_MODEL), (layer, 0, 0))],
        out_specs=row(D_MODEL),
        compiler_params=_params(1),
        name="mixer_out",
    )(x, yc, ona, om, mod, w["ng1"], w["wgate"], w["wco"], w["wno"], w["wmo"], w["wo"])


def _rope_tables():
    f32 = np.float32
    half = MLA_ROPE // 2
    nf = half // 2
    inv = (f32(1.0) / (f32(ROPE_BASE) ** (np.arange(nf, dtype=f32) / f32(nf)))).astype(f32)
    t = np.arange(DEC_SEQ)
    rows = (t // GRID_W).astype(f32)[:, None] * inv[None, :]
    cols = (t % GRID_W).astype(f32)[:, None] * inv[None, :]
    cos = np.concatenate([np.cos(rows), np.cos(rows), np.cos(cols), np.cos(cols)], axis=-1).astype(f32)
    sin = np.concatenate([np.sin(rows), np.sin(rows), np.sin(cols), np.sin(cols)], axis=-1).astype(f32)
    pad = MLA_QK_PAD - MLA_NOPE - MLA_ROPE
    q_cos = np.concatenate([np.ones((DEC_SEQ, MLA_NOPE), f32), cos, np.zeros((DEC_SEQ, pad), f32)], axis=-1)
    q_sin = np.concatenate([np.zeros((DEC_SEQ, MLA_NOPE), f32), sin, np.zeros((DEC_SEQ, pad), f32)], axis=-1)
    return tuple(jnp.asarray(a) for a in (q_cos, q_sin, np.ascontiguousarray(cos.T), np.ascontiguousarray(sin.T)))


def _rope_swap(w):
    nf = MLA_ROPE // 4
    a, b, c, d = (w[..., i * nf:(i + 1) * nf] for i in range(4))
    return jnp.concatenate([-b, a, -d, c], axis=-1)


def _na_bias(rpb):
    n_dc = 2 * NA_WIN_C - 1
    col = np.arange(GRID_W)
    c_start = np.clip(col - NA_WIN_C // 2, 0, GRID_W - NA_WIN_C)
    c_in = (col[None, :] >= c_start[:, None]) & (col[None, :] < c_start[:, None] + NA_WIN_C)
    dc = np.clip(col[None, :] - col[:, None] + (NA_WIN_C - 1), 0, n_dc - 1)
    pick_dc = (dc[None] == np.arange(n_dc)[:, None, None]).astype(np.float32)
    n_pairs = len(NA_BLOCK_PAIRS)
    pick_dr = np.zeros((n_pairs, 2, NA_DR_MASKED), np.float32)
    for p, pair in enumerate(NA_BLOCK_PAIRS):
        for side, d in enumerate(pair):
            if d != NA_DR_MASKED:
                pick_dr[p, side, d] = 1.0
    keep = pick_dr.sum(-1).astype(bool)[:, None, :, None] & c_in[None, :, None, :]
    keep = keep.reshape(n_pairs, GRID_W, 2 * GRID_W)
    pick_side_dc = np.zeros((2, n_dc, GRID_W, 2, GRID_W), np.float32)
    for side in range(2):
        pick_side_dc[side, :, :, side, :] = pick_dc
    pick_side_dc = pick_side_dc.reshape(2, n_dc, GRID_W, 2 * GRID_W)
    hi = lax.Precision.HIGHEST
    by_row = jnp.einsum("psd,lhdj->lhpsj", jnp.asarray(pick_dr), rpb, precision=hi)
    blocks = jnp.einsum("lhpsj,sjqn->lhpqn", by_row, jnp.asarray(pick_side_dc), precision=hi)
    return jnp.where(jnp.asarray(keep), blocks * LOG2_E, NEG_INF)


def _pack_weights(w_in, w_uq, w_ukv):
    c3 = 3 * CONV_DIM + 3 * HD
    t_last = lambda a: jnp.swapaxes(a, -1, -2)
    wa = w_in[..., :3 * CONV_DIM + HD]
    w_kv = w_in[..., 3 * CONV_DIM + HD:c3]
    w_lora = w_in[..., c3:c3 + Q_LORA + KV_LORA]
    w_kr = w_in[..., c3 + Q_LORA + KV_LORA:c3 + Q_LORA + KV_LORA + MLA_ROPE]
    wgate = w_in[..., c3 + Q_LORA + KV_LORA + MLA_ROPE:]
    wt = t_last(jnp.concatenate([w_kv, w_kr, _rope_swap(w_kr)], axis=-1))
    uq = w_uq.reshape(DEPTH, Q_LORA, MLA_HEADS, MLA_NOPE + MLA_ROPE)
    pad = MLA_QK_PAD - MLA_NOPE - MLA_ROPE
    zp = jnp.zeros(uq.shape[:-1] + (pad,), F32)
    zn = jnp.zeros(uq.shape[:-1] + (MLA_NOPE,), F32)
    q_ext = jnp.concatenate([uq, zp], axis=-1).reshape(DEPTH, Q_LORA, MLA_QK_W)
    q_sw = jnp.concatenate([zn, _rope_swap(uq[..., MLA_NOPE:]), zp], axis=-1).reshape(q_ext.shape)
    ukv = w_ukv.reshape(DEPTH, KV_LORA, MLA_HEADS, MLA_NOPE + MLA_V)
    zk = jnp.zeros(ukv.shape[:-1] + (MLA_QK_PAD - MLA_NOPE,), F32)
    wka = jnp.concatenate([ukv[..., :MLA_NOPE], zk], axis=-1).reshape(DEPTH, KV_LORA, MLA_QK_W)
    eye = jnp.concatenate([jnp.zeros((MLA_ROPE, MLA_NOPE), F32), jnp.eye(MLA_ROPE, dtype=F32),
                           jnp.zeros((MLA_ROPE, pad), F32)], axis=-1)
    wkb = jnp.broadcast_to(jnp.tile(eye, (1, MLA_HEADS))[None], (DEPTH, MLA_ROPE, MLA_QK_W))
    wuv = ukv[..., MLA_NOPE:].reshape(DEPTH, KV_LORA, HD)
    b = lambda a: a.astype(BF16)
    return dict(wa=b(wa), wb=b(w_lora), wt=b(wt), wgate=b(wgate),
                wq_lat=b(jnp.concatenate([q_ext, q_sw], axis=-1)), wq_ctx=b(q_ext),
                wkat=b(t_last(wka)), wkbt=b(t_last(wkb)), wuvt=b(t_last(wuv)))


def kernel(x_prompt, x_sample, cache_na_k, cache_na_v, cache_mla_ckv, cache_mla_krope, c, c_ctx,
           w_ada, b_ada, norm_g, w_ffn1_gate, w_ffn1_up, w_ffn1_down, w_ffn2_gate, w_ffn2_up, w_ffn2_down,
           w_in, conv_w, conv_b, na_rpb, mla_qnorm, w_uq, mla_kvnorm, w_ukv,
           w_conv_out, w_na_out, w_mla_out, w_o, final_g):
    b16 = lambda a: a.astype(BF16)
    t_last = lambda a: jnp.swapaxes(a, -1, -2)
    packed = _pack_weights(w_in, w_uq, w_ukv)
    shared = dict(conv_w=conv_w, conv_b=conv_b.reshape(DEPTH, 1, CONV_DIM),
                  qnorm=mla_qnorm.reshape(DEPTH, 1, Q_LORA), kvnorm=mla_kvnorm.reshape(DEPTH, 1, KV_LORA),
                  ng1=norm_g[:, 1:2], wco=b16(w_conv_out), wno=b16(w_na_out), wmo=b16(w_mla_out), wo=b16(w_o),
                  **{k: packed[k] for k in ("wa", "wb", "wt", "wgate", "wkat", "wkbt", "wuvt")})
    w_ctx = dict(shared, wq=packed["wq_ctx"])
    w_lat = dict(shared, wq=packed["wq_lat"])
    ffn_w = [(b16(w_ffn1_gate), b16(w_ffn1_up), b16(w_ffn1_down)),
             (b16(w_ffn2_gate), b16(w_ffn2_up), b16(w_ffn2_down))]
    final_row = final_g.reshape(1, D_MODEL)

    c_all = jnp.concatenate([c_ctx[None], c, jnp.zeros((MOD_ROWS - 1 - DEC_BATCH, D_MODEL), F32)], axis=0)
    mod = _modulation(c_all, w_ada, b_ada).reshape(DEPTH, MOD_ROWS, N_MOD, D_MODEL)

    tables = _rope_tables()
    na_bias = _na_bias(na_rpb)
    ctx_k_na = t_last(cache_na_k)
    ctx_v_na = t_last(cache_na_v)
    ctx_k_mla, ctx_v_mla = _ctx_kv(cache_mla_ckv, t_last(cache_mla_krope),
                                   packed["wkat"], packed["wkbt"], packed["wuvt"])

    xp = x_prompt.reshape(BATCH * SEQ, D_MODEL)
    xs = x_sample.reshape(DEC_BATCH * DEC_SEQ, D_MODEL)
    n_p = BATCH * SEQ
    caches = None
    for l in range(DEPTH):
        mod_p = mod[l, 0:1]
        mod_s = mod[l, 1:1 + DEC_BATCH]
        last = l == DEPTH - 1
        xp = _ffn(xp, mod_p, n_p, norm_g[l, 0:1], *ffn_w[0], l, 0)
        yc, qna, knat, vnat, qm, kmt, vmt, ckv, krt = _proj(xp, mod_p, n_p, w_ctx, l, SEQ, False, None, caches, 512)
        caches = (knat, vnat, ckv, krt)
        ona, om = _attn_ctx(qna, knat, vnat, qm, kmt, vmt, l, SEQ)
        xp = _mix(xp, yc, ona, om, mod_p, n_p, w_ctx, l)
        xp = _ffn(xp, mod_p, n_p, norm_g[l, 2:3], *ffn_w[1], l, 6, final_row if last else None)
        xs = _ffn(xs, mod_s, DEC_SEQ, norm_g[l, 0:1], *ffn_w[0], l, 0)
        yc, qna, knat, vnat, qm, kmt, vmt = _proj(xs, mod_s, DEC_SEQ, w_lat, l, DEC_SEQ, True, tables, None, DEC_SEQ)
        ona, om = _attn_lat(qna, knat, vnat, ctx_k_na, ctx_v_na, na_bias, qm, kmt, vmt, ctx_k_mla, ctx_v_mla, l)
        xs = _mix(xs, yc, ona, om, mod_s, DEC_SEQ, w_lat, l)
        xs = _ffn(xs, mod_s, DEC_SEQ, norm_g[l, 2:3], *ffn_w[1], l, 6, final_row if last else None)
    new_kt, new_vt, new_ckv, new_krt = caches
    return (xp.reshape(BATCH, SEQ, D_MODEL), xs.reshape(DEC_BATCH, DEC_SEQ, D_MODEL),
            t_last(new_kt), t_last(new_vt), new_ckv, t_last(new_krt))
```

```python
import functools

import jax
import jax.numpy as jnp
import numpy as np
from jax import lax
from jax.experimental import pallas as pl
from jax.experimental.pallas import tpu as pltpu

D_MODEL = 1024
BATCH = 32
SEQ = 256
DEPTH = 2
DEC_BATCH = 8
DEC_SEQ = 1024
PAST_LEN = 256
GRID_W = 64
CONV_DIM = 512
CONV_K = 3
NA_HEADS = 8
NA_HD = 64
NA_WIN_R = 8
NA_WIN_C = 16
MLA_HEADS = 8
MLA_NOPE = 64
MLA_ROPE = 32
MLA_V = 64
Q_LORA = 256
KV_LORA = 128
FFN_DIM = 2816
N_MOD = 9
ROPE_BASE = 10000.0
EPS = 1e-6
NEG_INF = -1e30
LOG2_E = 1.4426950408889634
MLA_SCALE = (MLA_NOPE + MLA_ROPE) ** -0.5 * LOG2_E
NA_SCALE = NA_HD ** -0.5 * LOG2_E

HEADS = 8
HEAD_V = 64
HD = HEADS * NA_HD
MLA_QK_PAD = 128
MLA_QK_W = HEADS * MLA_QK_PAD
FFN_CHUNK = 256
W_RING = 3
Q_CHUNK = 256
MOD_ROWS = 16
VMEM_LIMIT = 56 * 1024 * 1024
NA_WINDOWS = ((0, 512), (0, 768), (256, 768), (512, 512))
NA_DR_MASKED = 2 * NA_WIN_R - 1


def _na_block_pairs():
    rows = DEC_SEQ // GRID_W
    r_start = np.clip(np.arange(rows) - NA_WIN_R // 2, 0, rows - NA_WIN_R)
    pairs, index = [], []
    for c, (start, count) in enumerate(NA_WINDOWS):
        index.append([])
        for rl in range(Q_CHUNK // GRID_W):
            r = c * (Q_CHUNK // GRID_W) + rl
            assert start // GRID_W <= r_start[r] and r_start[r] + NA_WIN_R <= (start + count) // GRID_W
            index[c].append([])
            for kp in range(count // (2 * GRID_W)):
                pair = []
                for rk in (start // GRID_W + 2 * kp, start // GRID_W + 2 * kp + 1):
                    inside = r_start[r] <= rk < r_start[r] + NA_WIN_R
                    pair.append(int(rk - r + NA_WIN_R - 1) if inside else NA_DR_MASKED)
                pair = tuple(pair)
                if pair not in pairs:
                    pairs.append(pair)
                index[c][rl].append(pairs.index(pair))
    return tuple(pairs), index


NA_BLOCK_PAIRS, NA_BLOCK_INDEX = _na_block_pairs()

BF16 = jnp.bfloat16
F32 = jnp.float32


def _dot(a, b):
    return jnp.dot(a, b, preferred_element_type=F32)


def _dot_nt(a, b):
    return lax.dot_general(a, b, (((1,), (1,)), ((), ())), preferred_element_type=F32)


def _rms(x, g):
    return x * lax.rsqrt(jnp.mean(x * x, axis=-1, keepdims=True) + EPS) * g


def _params(n_axes, flags=None):
    return pltpu.CompilerParams(dimension_semantics=("arbitrary",) * n_axes,
                                vmem_limit_bytes=VMEM_LIMIT, flags=flags)


def _resident(shape, index):
    return pl.BlockSpec(shape, lambda *_: index, pipeline_mode=pl.Buffered(1))


def _mod_body(c_ref, w_ref, b_ref, o_ref):
    c = c_ref[...]
    a = c * jax.nn.sigmoid(c)
    o_ref[...] = _dot(a.astype(BF16), w_ref[...].astype(BF16)) + b_ref[...]


def _modulation(c_all, w_ada, b_ada):
    n_col = N_MOD * D_MODEL
    tn = n_col // 4
    return pl.pallas_call(
        _mod_body,
        out_shape=jax.ShapeDtypeStruct((DEPTH, MOD_ROWS, n_col), F32),
        grid=(DEPTH, n_col // tn),
        in_specs=[pl.BlockSpec((MOD_ROWS, D_MODEL), lambda l, j: (0, 0)),
                  pl.BlockSpec((None, D_MODEL, tn), lambda l, j: (l, 0, j)),
                  pl.BlockSpec((None, 1, tn), lambda l, j: (l, 0, j))],
        out_specs=pl.BlockSpec((None, MOD_ROWS, tn), lambda l, j: (l, 0, j)),
        compiler_params=_params(2),
        name="modulation",
    )(c_all, w_ada, b_ada.reshape(DEPTH, 1, n_col))


def _ffn_body(x_ref, mod_ref, ng_ref, wg_hbm, wu_hbm, wd_hbm, *rest, layer, mod_off, final):
    if final:
        fg_ref, o_ref, a_scr, wg_bf, wu_bf, wd_bf, stg_g, stg_u, stg_d, sems = rest
    else:
        o_ref, a_scr, wg_bf, wu_bf, wd_bf, stg_g, stg_u, stg_d, sems = rest
    n_chunks = FFN_DIM // FFN_CHUNK
    x = x_ref[...]
    mod = mod_ref[...]
    shift = mod[mod_off:mod_off + 1]
    scale = mod[mod_off + 1:mod_off + 2]
    gate = mod[mod_off + 2:mod_off + 3]
    h = (_rms(x, ng_ref[...]) * (1 + scale) + shift).astype(BF16)

    def chunk_copies(f):
        slot = f % W_RING
        span = pl.ds(f * FFN_CHUNK, FFN_CHUNK)
        return (pltpu.make_async_copy(wg_hbm.at[layer, :, span], stg_g.at[slot], sems.at[0, slot]),
                pltpu.make_async_copy(wu_hbm.at[layer, :, span], stg_u.at[slot], sems.at[1, slot]),
                pltpu.make_async_copy(wd_hbm.at[layer, span, :], stg_d.at[slot], sems.at[2, slot]))

    def swiglu_chunk(f):
        cols = slice(f * FFN_CHUNK, (f + 1) * FFN_CHUNK)
        g = _dot(h, wg_bf[:, cols])
        u = _dot(h, wu_bf[:, cols])
        a_scr[:, cols] = (g * jax.nn.sigmoid(g) * u).astype(BF16)

    def first_step():
        for f in range(W_RING):
            for cp in chunk_copies(f):
                cp.start()
        for f in range(n_chunks):
            slot = f % W_RING
            cols = slice(f * FFN_CHUNK, (f + 1) * FFN_CHUNK)
            for cp in chunk_copies(f):
                cp.wait()
            wg_bf[:, cols] = stg_g[slot].astype(BF16)
            wu_bf[:, cols] = stg_u[slot].astype(BF16)
            wd_bf[cols, :] = stg_d[slot].astype(BF16)
            if f + W_RING < n_chunks:
                for cp in chunk_copies(f + W_RING):
                    cp.start()
            swiglu_chunk(f)

    def later_step():
        for f in range(n_chunks):
            swiglu_chunk(f)

    lax.cond(pl.program_id(0) == 0, first_step, later_step)
    y = _dot(a_scr[...], wd_bf[...])
    out = x + 0.5 * gate * y
    if final:
        out = _rms(out, fg_ref[...])
    o_ref[...] = out


def _ffn(x, mod, rows_per_mod, ng, wg, wu, wd, layer, mod_off, final_g=None, tm=1024):
    t = x.shape[0]
    tiles_per_mod = rows_per_mod // tm
    final = final_g is not None
    in_specs = [pl.BlockSpec((tm, D_MODEL), lambda i: (i, 0)),
                pl.BlockSpec((None, N_MOD, D_MODEL), lambda i: (i // tiles_per_mod, 0, 0)),
                _resident((1, D_MODEL), (0, 0)),
                pl.BlockSpec(memory_space=pl.ANY),
                pl.BlockSpec(memory_space=pl.ANY),
                pl.BlockSpec(memory_space=pl.ANY)]
    args = [x, mod, ng, wg, wu, wd]
    if final:
        in_specs.append(_resident((1, D_MODEL), (0, 0)))
        args.append(final_g)
    return pl.pallas_call(
        functools.partial(_ffn_body, layer=layer, mod_off=mod_off, final=final),
        out_shape=jax.ShapeDtypeStruct((t, D_MODEL), F32),
        grid=(t // tm,),
        in_specs=in_specs,
        out_specs=pl.BlockSpec((tm, D_MODEL), lambda i: (i, 0)),
        scratch_shapes=[pltpu.VMEM((tm, FFN_DIM), BF16),
                        pltpu.VMEM((D_MODEL, FFN_DIM), BF16),
                        pltpu.VMEM((D_MODEL, FFN_DIM), BF16),
                        pltpu.VMEM((FFN_DIM, D_MODEL), BF16),
                        pltpu.VMEM((W_RING, D_MODEL, FFN_CHUNK), F32),
                        pltpu.VMEM((W_RING, D_MODEL, FFN_CHUNK), F32),
                        pltpu.VMEM((W_RING, FFN_CHUNK, D_MODEL), F32),
                        pltpu.SemaphoreType.DMA((3, W_RING))],
        compiler_params=_params(1),
        name="ffn",
    )(*args)


def _proj_body(x_ref, mod_ref, ng_ref, wa_ref, wb_ref, wt_ref, cw_ref, cb_ref, qn_ref, kvn_ref,
               wq_ref, wkat_ref, wkbt_ref, wuvt_ref, *rest, seq_len, latent, n_alias, own_slot):
    rest = rest[n_alias:]

    def put(ref, b, value):
        for k in range(ref.shape[1]):
            ref[b, k] = value if k == own_slot else jnp.zeros_like(value)

    if latent:
        (qc_ref, qs_ref, kct_ref, kst_ref,
         yc_ref, qna_ref, knat_ref, vnat_ref, qm_ref, kmt_ref, vmt_ref) = rest
    else:
        (yc_ref, qna_ref, knat_ref, vnat_ref, qm_ref, kmt_ref, vmt_ref, ckv_ref, krt_ref) = rest
    tm = x_ref.shape[0]
    x = x_ref[...]
    mod = mod_ref[...]
    h = (_rms(x, ng_ref[...]) * (1 + mod[4:5]) + mod[3:4]).astype(BF16)

    u = _dot(h, wa_ref[:, 0:3 * CONV_DIM])
    v = u[:, CONV_DIM:2 * CONV_DIM] * u[:, 2 * CONV_DIM:3 * CONV_DIM]
    pos = lax.broadcasted_iota(jnp.int32, (tm, 1), 0) % seq_len
    v_prev = jnp.where(pos == 0, 0.0, pltpu.roll(v, 1, 0))
    v_next = jnp.where(pos == seq_len - 1, 0.0, pltpu.roll(v, tm - 1, 0))
    cw = cw_ref[...]
    y = cb_ref[...] + v_prev * cw[0:1]
    y = y + v * cw[1:2]
    y = y + v_next * cw[2:3]
    yc_ref[...] = (u[:, 0:CONV_DIM] * y).astype(BF16)

    qna_ref[...] = (_dot(h, wa_ref[:, 3 * CONV_DIM:3 * CONV_DIM + HD]) * NA_SCALE).astype(BF16)

    ut = _dot_nt(wt_ref[...], h)
    krt = ut[2 * HD:2 * HD + MLA_ROPE]

    u = _dot(h, wb_ref[...])
    cq = _rms(u[:, 0:Q_LORA], qn_ref[...]).astype(BF16)
    ckv = _rms(u[:, Q_LORA:Q_LORA + KV_LORA], kvn_ref[...])
    ckv_b = ckv.astype(BF16)
    q2 = _dot(cq, wq_ref[...])
    if latent:
        knat_ref[...] = ut[0:HD].astype(BF16)
        vnat_ref[...] = ut[HD:2 * HD].astype(BF16)
        krt = krt * kct_ref[...] + ut[2 * HD + MLA_ROPE:2 * HD + 2 * MLA_ROPE] * kst_ref[...]
        qc = qc_ref[...]
        qs = qs_ref[...]
        for hh in range(HEADS):
            cols = slice(hh * MLA_QK_PAD, (hh + 1) * MLA_QK_PAD)
            sw_cols = slice(MLA_QK_W + hh * MLA_QK_PAD, MLA_QK_W + (hh + 1) * MLA_QK_PAD)
            qm_ref[:, cols] = ((q2[:, cols] * qc + q2[:, sw_cols] * qs) * MLA_SCALE).astype(BF16)
    else:
        qm_ref[...] = (q2 * MLA_SCALE).astype(BF16)
    kmt = _dot_nt(wkat_ref[...], ckv_b) + _dot(wkbt_ref[...], krt.astype(BF16))
    vmt = _dot_nt(wuvt_ref[...], ckv_b)
    if latent:
        kmt_ref[...] = kmt.astype(BF16)
        vmt_ref[...] = vmt.astype(BF16)
    else:
        for b in range(tm // seq_len):
            rows = slice(b * seq_len, (b + 1) * seq_len)
            put(knat_ref, b, ut[0:HD, rows].reshape(HEADS, NA_HD, seq_len))
            put(vnat_ref, b, ut[HD:2 * HD, rows].reshape(HEADS, NA_HD, seq_len))
            put(ckv_ref, b, ckv[rows])
            put(krt_ref, b, krt[:, rows])
            kmt_ref[b] = kmt[:, rows].astype(BF16)
            vmt_ref[b] = vmt[:, rows].astype(BF16)


def _proj(x, mod, rows_per_mod, w, layer, seq_len, latent, tables, caches, tm):
    t = x.shape[0]
    n_seq = t // seq_len
    seq_per_tile = tm // seq_len
    tiles_per_mod = rows_per_mod // tm
    wb_cols = w["wb"].shape[-1]
    wq_cols = w["wq"].shape[-1]
    wt_rows = w["wt"].shape[-2]
    row = lambda n: pl.BlockSpec((tm, n), lambda i: (i, 0))
    in_specs = [row(D_MODEL),
                pl.BlockSpec((None, N_MOD, D_MODEL), lambda i: (i // tiles_per_mod, 0, 0)),
                _resident((None, 1, D_MODEL), (layer, 0, 0)),
                _resident((None, D_MODEL, 3 * CONV_DIM + HD), (layer, 0, 0)),
                _resident((None, D_MODEL, wb_cols), (layer, 0, 0)),
                _resident((None, wt_rows, D_MODEL), (layer, 0, 0)),
                _resident((None, CONV_K, CONV_DIM), (layer, 0, 0)),
                _resident((None, 1, CONV_DIM), (layer, 0, 0)),
                _resident((None, 1, Q_LORA), (layer, 0, 0)),
                _resident((None, 1, KV_LORA), (layer, 0, 0)),
                _resident((None, Q_LORA, wq_cols), (layer, 0, 0)),
                _resident((None, MLA_QK_W, KV_LORA), (layer, 0, 0)),
                _resident((None, MLA_QK_W, MLA_ROPE), (layer, 0, 0)),
                _resident((None, HD, KV_LORA), (layer, 0, 0))]
    args = [x, mod, w["ng1"], w["wa"], w["wb"], w["wt"], w["conv_w"], w["conv_b"], w["qnorm"], w["kvnorm"],
            w["wq"], w["wkat"], w["wkbt"], w["wuvt"]]
    out_shape = [jax.ShapeDtypeStruct((t, CONV_DIM), BF16),
                 jax.ShapeDtypeStruct((t, HD), BF16)]
    out_specs = [row(CONV_DIM), row(HD)]
    aliases = {}
    n_alias = 0
    own_slot = 0
    if latent:
        assert tm == seq_len
        in_specs += [_resident((seq_len, MLA_QK_PAD), (0, 0)), _resident((seq_len, MLA_QK_PAD), (0, 0)),
                     _resident((MLA_ROPE, seq_len), (0, 0)), _resident((MLA_ROPE, seq_len), (0, 0))]
        args += list(tables)
        seq_blk = lambda n: pl.BlockSpec((None, n, seq_len), lambda i: (i, 0, 0))
        out_shape += [jax.ShapeDtypeStruct((n_seq, HD, seq_len), BF16),
                      jax.ShapeDtypeStruct((n_seq, HD, seq_len), BF16),
                      jax.ShapeDtypeStruct((t, MLA_QK_W), BF16),
                      jax.ShapeDtypeStruct((n_seq, MLA_QK_W, seq_len), BF16),
                      jax.ShapeDtypeStruct((n_seq, HD, seq_len), BF16)]
        out_specs += [seq_blk(HD), seq_blk(HD), row(MLA_QK_W), seq_blk(MLA_QK_W), seq_blk(HD)]
    else:
        if caches is not None:
            n_alias = len(caches)
            in_specs += [pl.BlockSpec(memory_space=pl.ANY)] * n_alias
            args += list(caches)
            aliases = {len(args) - n_alias + k: 2 + (0, 1, 5, 6)[k] for k in range(n_alias)}
            n_slots, first_slot = 1, layer
        else:
            assert layer == 0
            n_slots, first_slot, own_slot = DEPTH, 0, layer
        cache_blk = lambda *dims: pl.BlockSpec((seq_per_tile, n_slots) + dims,
                                               lambda i: (i, first_slot) + (0,) * len(dims))
        seq_blk = lambda n: pl.BlockSpec((seq_per_tile, n, seq_len), lambda i: (i, 0, 0))
        out_shape += [jax.ShapeDtypeStruct((n_seq, DEPTH, HEADS, NA_HD, seq_len), F32),
                      jax.ShapeDtypeStruct((n_seq, DEPTH, HEADS, NA_HD, seq_len), F32),
                      jax.ShapeDtypeStruct((t, MLA_QK_W), BF16),
                      jax.ShapeDtypeStruct((n_seq, MLA_QK_W, seq_len), BF16),
                      jax.ShapeDtypeStruct((n_seq, HD, seq_len), BF16),
                      jax.ShapeDtypeStruct((n_seq, DEPTH, seq_len, KV_LORA), F32),
                      jax.ShapeDtypeStruct((n_seq, DEPTH, MLA_ROPE, seq_len), F32)]
        out_specs += [cache_blk(HEADS, NA_HD, seq_len), cache_blk(HEADS, NA_HD, seq_len), row(MLA_QK_W),
                      seq_blk(MLA_QK_W), seq_blk(HD), cache_blk(seq_len, KV_LORA), cache_blk(MLA_ROPE, seq_len)]
    return pl.pallas_call(
        functools.partial(_proj_body, seq_len=seq_len, latent=latent, n_alias=n_alias, own_slot=own_slot),
        out_shape=out_shape,
        grid=(t // tm,),
        in_specs=in_specs,
        out_specs=out_specs,
        input_output_aliases=aliases,
        compiler_params=_params(1),
        name="mixer_proj",
    )(*args)


def _ctxkv_body(ckv_ref, krt_ref, wkat_ref, wkbt_ref, wuvt_ref, k_ref, v_ref):
    ckv = ckv_ref[...].astype(BF16)
    krt = krt_ref[...].astype(BF16)
    k_ref[...] = (_dot_nt(wkat_ref[...], ckv) + _dot(wkbt_ref[...], krt)).astype(BF16)
    v_ref[...] = _dot_nt(wuvt_ref[...], ckv).astype(BF16)


def _ctx_kv(cache_ckv, cache_krt, wkat, wkbt, wuvt):
    return pl.pallas_call(
        _ctxkv_body,
        out_shape=[jax.ShapeDtypeStruct((DEPTH, DEC_BATCH, MLA_QK_W, PAST_LEN), BF16),
                   jax.ShapeDtypeStruct((DEPTH, DEC_BATCH, HD, PAST_LEN), BF16)],
        grid=(DEPTH, DEC_BATCH),
        in_specs=[pl.BlockSpec((None, None, PAST_LEN, KV_LORA), lambda l, b: (b, l, 0, 0)),
                  pl.BlockSpec((None, None, MLA_ROPE, PAST_LEN), lambda l, b: (b, l, 0, 0)),
                  pl.BlockSpec((None, MLA_QK_W, KV_LORA), lambda l, b: (l, 0, 0)),
                  pl.BlockSpec((None, MLA_QK_W, MLA_ROPE), lambda l, b: (l, 0, 0)),
                  pl.BlockSpec((None, HD, KV_LORA), lambda l, b: (l, 0, 0))],
        out_specs=[pl.BlockSpec((None, None, MLA_QK_W, PAST_LEN), lambda l, b: (l, b, 0, 0)),
                   pl.BlockSpec((None, None, HD, PAST_LEN), lambda l, b: (l, b, 0, 0))],
        compiler_params=_params(2),
        name="ctx_kv",
    )(cache_ckv, cache_krt, wkat, wkbt, wuvt)


def _softmax_pv(s, vt):
    m = jnp.max(s, axis=-1, keepdims=True)
    p = jnp.exp2(s - m)
    den = jnp.sum(p, axis=-1, keepdims=True)
    return _dot_nt(p.astype(BF16), vt) / den


def _pair_slot(x, j):
    z = jnp.zeros_like(x)
    return jnp.concatenate([x, z] if j == 0 else [z, x], axis=0)


def _attn_ctx_body(qna_ref, knat_ref, vnat_ref, qm_ref, kmt_ref, vmt_ref, ona_ref, om_ref, *, seq_len):
    for b in range(qna_ref.shape[0] // seq_len):
        rows = slice(b * seq_len, (b + 1) * seq_len)
        for hp in range(HEADS // 2):
            pair = slice(hp * 2 * HEAD_V, (hp + 1) * 2 * HEAD_V)
            q = qna_ref[rows, pair]
            o_na = o_m = None
            for j in range(2):
                hh = 2 * hp + j
                kt = _pair_slot(knat_ref[b, hh].astype(BF16), j)
                vt = _pair_slot(vnat_ref[b, hh].astype(BF16), j)
                o = _softmax_pv(_dot(q, kt), vt)
                o_na = o if j == 0 else o_na + o
                qk = slice(hh * MLA_QK_PAD, (hh + 1) * MLA_QK_PAD)
                vt = _pair_slot(vmt_ref[b, hh * HEAD_V:(hh + 1) * HEAD_V, :], j)
                o = _softmax_pv(_dot(qm_ref[rows, qk], kmt_ref[b, qk, :]), vt)
                o_m = o if j == 0 else o_m + o
            ona_ref[rows, pair] = o_na.astype(BF16)
            om_ref[rows, pair] = o_m.astype(BF16)


def _attn_ctx(qna, knat, vnat, qm, kmt, vmt, layer, seq_len, tm=512):
    t = qna.shape[0]
    nb = tm // seq_len
    row = lambda n: pl.BlockSpec((tm, n), lambda i: (i, 0))
    cache_blk = pl.BlockSpec((nb, None, HEADS, NA_HD, seq_len), lambda i: (i, layer, 0, 0, 0))
    seq_blk = lambda n: pl.BlockSpec((nb, n, seq_len), lambda i: (i, 0, 0))
    return pl.pallas_call(
        functools.partial(_attn_ctx_body, seq_len=seq_len),
        out_shape=[jax.ShapeDtypeStruct((t, HD), BF16), jax.ShapeDtypeStruct((t, HD), BF16)],
        grid=(t // tm,),
        in_specs=[row(HD), cache_blk, cache_blk, row(MLA_QK_W), seq_blk(MLA_QK_W), seq_blk(HD)],
        out_specs=[row(HD), row(HD)],
        compiler_params=_params(1),
        name="attn_ctx",
    )(qna, knat, vnat, qm, kmt, vmt)


def _attn_lat_body(qna_ref, knat_ref, vnat_ref, kctx_ref, vctx_ref, bias_ref,
                   qm_ref, kmt_ref, vmt_ref, kmctx_ref, vmctx_ref, ona_ref, om_ref):
    cat = lambda *a: jnp.concatenate(a, axis=1)
    head = lambda j: slice(j * HEAD_V, (j + 1) * HEAD_V)
    kc = [_pair_slot(kctx_ref[j].astype(BF16), j) for j in range(2)]
    vc = [_pair_slot(vctx_ref[j].astype(BF16), j) for j in range(2)]
    for c, (start, count) in enumerate(NA_WINDOWS):
        rows = slice(c * Q_CHUNK, (c + 1) * Q_CHUNK)
        keys = slice(start, start + count)
        q = qna_ref[rows, :]
        for j in range(2):
            bias = jnp.concatenate(
                [cat(*[bias_ref[j, p] for p in NA_BLOCK_INDEX[c][rl]]) for rl in range(Q_CHUNK // GRID_W)], axis=0)
            s = cat(_dot(q, kc[j]), _dot(q, _pair_slot(knat_ref[head(j), keys], j)) + bias)
            o = _softmax_pv(s, cat(vc[j], _pair_slot(vnat_ref[head(j), keys], j)))
            o_na = o if j == 0 else o_na + o
        ona_ref[rows, :] = o_na.astype(BF16)
    kt, vt = [], []
    for j in range(2):
        qk = slice(j * MLA_QK_PAD, (j + 1) * MLA_QK_PAD)
        kt.append(cat(kmctx_ref[qk, :], kmt_ref[qk, :]))
        vt.append(_pair_slot(cat(vmctx_ref[head(j), :], vmt_ref[head(j), :]), j))
    for c in range(DEC_SEQ // Q_CHUNK):
        rows = slice(c * Q_CHUNK, (c + 1) * Q_CHUNK)
        for j in range(2):
            o = _softmax_pv(_dot(qm_ref[rows, j * MLA_QK_PAD:(j + 1) * MLA_QK_PAD], kt[j]), vt[j])
            o_m = o if j == 0 else o_m + o
        om_ref[rows, :] = o_m.astype(BF16)


def _attn_lat(qna, knat, vnat, kctx, vctx, bias, qm, kmt, vmt, kmctx, vmctx, layer):
    t = qna.shape[0]
    s = DEC_SEQ
    return pl.pallas_call(
        _attn_lat_body,
        out_shape=[jax.ShapeDtypeStruct((t, HD), BF16), jax.ShapeDtypeStruct((t, HD), BF16)],
        grid=(HEADS // 2, t // s),
        in_specs=[pl.BlockSpec((s, 2 * NA_HD), lambda hp, b: (b, hp)),
                  pl.BlockSpec((None, 2 * NA_HD, s), lambda hp, b: (b, hp, 0)),
                  pl.BlockSpec((None, 2 * HEAD_V, s), lambda hp, b: (b, hp, 0)),
                  pl.BlockSpec((None, None, 2, NA_HD, PAST_LEN), lambda hp, b: (b, layer, hp, 0, 0)),
                  pl.BlockSpec((None, None, 2, NA_HD, PAST_LEN), lambda hp, b: (b, layer, hp, 0, 0)),
                  pl.BlockSpec((None, 2, len(NA_BLOCK_PAIRS), GRID_W, 2 * GRID_W),
                               lambda hp, b: (layer, hp, 0, 0, 0)),
                  pl.BlockSpec((s, 2 * MLA_QK_PAD), lambda hp, b: (b, hp)),
                  pl.BlockSpec((None, 2 * MLA_QK_PAD, s), lambda hp, b: (b, hp, 0)),
                  pl.BlockSpec((None, 2 * HEAD_V, s), lambda hp, b: (b, hp, 0)),
                  pl.BlockSpec((None, None, 2 * MLA_QK_PAD, PAST_LEN), lambda hp, b: (layer, b, hp, 0)),
                  pl.BlockSpec((None, None, 2 * HEAD_V, PAST_LEN), lambda hp, b: (layer, b, hp, 0))],
        out_specs=[pl.BlockSpec((s, 2 * HEAD_V), lambda hp, b: (b, hp)),
                   pl.BlockSpec((s, 2 * HEAD_V), lambda hp, b: (b, hp))],
        compiler_params=_params(2),
        name="attn_lat",
    )(qna, knat, vnat, kctx, vctx, bias, qm, kmt, vmt, kmctx, vmctx)


def _mix_body(x_ref, yc_ref, ona_ref, om_ref, mod_ref, ng_ref, wg_ref, wco_ref, wno_ref, wmo_ref,
              wo_ref, o_ref):
    x = x_ref[...]
    mod = mod_ref[...]
    h = (_rms(x, ng_ref[...]) * (1 + mod[4:5]) + mod[3:4]).astype(BF16)
    z = jax.nn.sigmoid(_dot(h, wg_ref[:, 0:D_MODEL])) * _dot(yc_ref[...], wco_ref[...])
    z = z + jax.nn.sigmoid(_dot(h, wg_ref[:, D_MODEL:2 * D_MODEL])) * _dot(ona_ref[...], wno_ref[...])
    z = z + jax.nn.sigmoid(_dot(h, wg_ref[:, 2 * D_MODEL:3 * D_MODEL])) * _dot(om_ref[...], wmo_ref[...])
    o_ref[...] = x + mod[5:6] * _dot(z.astype(BF16), wo_ref[...])


def _mix(x, yc, ona, om, mod, rows_per_mod, w, layer, tm=1024):
    t = x.shape[0]
    tiles_per_mod = rows_per_mod // tm
    row = lambda n: pl.BlockSpec((tm, n), lambda i: (i, 0))
    return pl.pallas_call(
        _mix_body,
        out_shape=jax.ShapeDtypeStruct((t, D_MODEL), F32),
        grid=(t // tm,),
        in_specs=[row(D_MODEL), row(CONV_DIM), row(HD), row(HD),
                  pl.BlockSpec((None, N_MOD, D_MODEL), lambda i: (i // tiles_per_mod, 0, 0)),
                  _resident((None, 1, D_MODEL), (layer, 0, 0)),
                  _resident((None, D_MODEL, 3 * D_MODEL), (layer, 0, 0)),
                  _resident((None, CONV_DIM, D_MODEL), (layer, 0, 0)),
                  _resident((None, HD, D_MODEL), (layer, 0, 0)),
                  _resident((None, HD, D_MODEL), (layer, 0, 0)),
                  _resident((None, D_MODEL, D_MODEL), (layer, 0, 0))],
        out_specs=row(D_MODEL),
        compiler_params=_params(1),
        name="mixer_out",
    )(x, yc, ona, om, mod, w["ng1"], w["wgate"], w["wco"], w["wno"], w["wmo"], w["wo"])


def _rope_tables():
    f32 = np.float32
    half = MLA_ROPE // 2
    nf = half // 2
    inv = (f32(1.0) / (f32(ROPE_BASE) ** (np.arange(nf, dtype=f32) / f32(nf)))).astype(f32)
    t = np.arange(DEC_SEQ)
    rows = (t // GRID_W).astype(f32)[:, None] * inv[None, :]
    cols = (t % GRID_W).astype(f32)[:, None] * inv[None, :]
    cos = np.concatenate([np.cos(rows), np.cos(rows), np.cos(cols), np.cos(cols)], axis=-1).astype(f32)
    sin = np.concatenate([np.sin(rows), np.sin(rows), np.sin(cols), np.sin(cols)], axis=-1).astype(f32)
    pad = MLA_QK_PAD - MLA_NOPE - MLA_ROPE
    q_cos = np.concatenate([np.ones((DEC_SEQ, MLA_NOPE), f32), cos, np.zeros((DEC_SEQ, pad), f32)], axis=-1)
    q_sin = np.concatenate([np.zeros((DEC_SEQ, MLA_NOPE), f32), sin, np.zeros((DEC_SEQ, pad), f32)], axis=-1)
    return tuple(jnp.asarray(a) for a in (q_cos, q_sin, np.ascontiguousarray(cos.T), np.ascontiguousarray(sin.T)))


def _rope_swap(w):
    nf = MLA_ROPE // 4
    a, b, c, d = (w[..., i * nf:(i + 1) * nf] for i in range(4))
    return jnp.concatenate([-b, a, -d, c], axis=-1)


def _na_bias(rpb):
    n_dc = 2 * NA_WIN_C - 1
    col = np.arange(GRID_W)
    c_start = np.clip(col - NA_WIN_C // 2, 0, GRID_W - NA_WIN_C)
    c_in = (col[None, :] >= c_start[:, None]) & (col[None, :] < c_start[:, None] + NA_WIN_C)
    dc = np.clip(col[None, :] - col[:, None] + (NA_WIN_C - 1), 0, n_dc - 1)
    pick_dc = (dc[None] == np.arange(n_dc)[:, None, None]).astype(np.float32)
    n_pairs = len(NA_BLOCK_PAIRS)
    pick_dr = np.zeros((n_pairs, 2, NA_DR_MASKED), np.float32)
    for p, pair in enumerate(NA_BLOCK_PAIRS):
        for side, d in enumerate(pair):
            if d != NA_DR_MASKED:
                pick_dr[p, side, d] = 1.0
    keep = pick_dr.sum(-1).astype(bool)[:, None, :, None] & c_in[None, :, None, :]
    keep = keep.reshape(n_pairs, GRID_W, 2 * GRID_W)
    pick_side_dc = np.zeros((2, n_dc, GRID_W, 2, GRID_W), np.float32)
    for side in range(2):
        pick_side_dc[side, :, :, side, :] = pick_dc
    pick_side_dc = pick_side_dc.reshape(2, n_dc, GRID_W, 2 * GRID_W)
    hi = lax.Precision.HIGHEST
    by_row = jnp.einsum("psd,lhdj->lhpsj", jnp.asarray(pick_dr), rpb, precision=hi)
    blocks = jnp.einsum("lhpsj,sjqn->lhpqn", by_row, jnp.asarray(pick_side_dc), precision=hi)
    return jnp.where(jnp.asarray(keep), blocks * LOG2_E, NEG_INF)


def _pack_weights(w_in, w_uq, w_ukv):
    c3 = 3 * CONV_DIM + 3 * HD
    t_last = lambda a: jnp.swapaxes(a, -1, -2)
    wa = w_in[..., :3 * CONV_DIM + HD]
    w_kv = w_in[..., 3 * CONV_DIM + HD:c3]
    w_lora = w_in[..., c3:c3 + Q_LORA + KV_LORA]
    w_kr = w_in[..., c3 + Q_LORA + KV_LORA:c3 + Q_LORA + KV_LORA + MLA_ROPE]
    wgate = w_in[..., c3 + Q_LORA + KV_LORA + MLA_ROPE:]
    wt = t_last(jnp.concatenate([w_kv, w_kr, _rope_swap(w_kr)], axis=-1))
    uq = w_uq.reshape(DEPTH, Q_LORA, MLA_HEADS, MLA_NOPE + MLA_ROPE)
    pad = MLA_QK_PAD - MLA_NOPE - MLA_ROPE
    zp = jnp.zeros(uq.shape[:-1] + (pad,), F32)
    zn = jnp.zeros(uq.shape[:-1] + (MLA_NOPE,), F32)
    q_ext = jnp.concatenate([uq, zp], axis=-1).reshape(DEPTH, Q_LORA, MLA_QK_W)
    q_sw = jnp.concatenate([zn, _rope_swap(uq[..., MLA_NOPE:]), zp], axis=-1).reshape(q_ext.shape)
    ukv = w_ukv.reshape(DEPTH, KV_LORA, MLA_HEADS, MLA_NOPE + MLA_V)
    zk = jnp.zeros(ukv.shape[:-1] + (MLA_QK_PAD - MLA_NOPE,), F32)
    wka = jnp.concatenate([ukv[..., :MLA_NOPE], zk], axis=-1).reshape(DEPTH, KV_LORA, MLA_QK_W)
    eye = jnp.concatenate([jnp.zeros((MLA_ROPE, MLA_NOPE), F32), jnp.eye(MLA_ROPE, dtype=F32),
                           jnp.zeros((MLA_ROPE, pad), F32)], axis=-1)
    wkb = jnp.broadcast_to(jnp.tile(eye, (1, MLA_HEADS))[None], (DEPTH, MLA_ROPE, MLA_QK_W))
    wuv = ukv[..., MLA_NOPE:].reshape(DEPTH, KV_LORA, HD)
    b = lambda a: a.astype(BF16)
    return dict(wa=b(wa), wb=b(w_lora), wt=b(wt), wgate=b(wgate),
                wq_lat=b(jnp.concatenate([q_ext, q_sw], axis=-1)), wq_ctx=b(q_ext),
                wkat=b(t_last(wka)), wkbt=b(t_last(wkb)), wuvt=b(t_last(wuv)))


def kernel(x_prompt, x_sample, cache_na_k, cache_na_v, cache_mla_ckv, cache_mla_krope, c, c_ctx,
           w_ada, b_ada, norm_g, w_ffn1_gate, w_ffn1_up, w_ffn1_down, w_ffn2_gate, w_ffn2_up, w_ffn2_down,
           w_in, conv_w, conv_b, na_rpb, mla_qnorm, w_uq, mla_kvnorm, w_ukv,
           w_conv_out, w_na_out, w_mla_out, w_o, final_g):
    b16 = lambda a: a.astype(BF16)
    t_last = lambda a: jnp.swapaxes(a, -1, -2)
    packed = _pack_weights(w_in, w_uq, w_ukv)
    shared = dict(conv_w=conv_w, conv_b=conv_b.reshape(DEPTH, 1, CONV_DIM),
                  qnorm=mla_qnorm.reshape(DEPTH, 1, Q_LORA), kvnorm=mla_kvnorm.reshape(DEPTH, 1, KV_LORA),
                  ng1=norm_g[:, 1:2], wco=b16(w_conv_out), wno=b16(w_na_out), wmo=b16(w_mla_out), wo=b16(w_o),
                  **{k: packed[k] for k in ("wa", "wb", "wt", "wgate", "wkat", "wkbt", "wuvt")})
    w_ctx = dict(shared, wq=packed["wq_ctx"])
    w_lat = dict(shared, wq=packed["wq_lat"])
    ffn_w = [(w_ffn1_gate, w_ffn1_up, w_ffn1_down), (w_ffn2_gate, w_ffn2_up, w_ffn2_down)]
    final_row = final_g.reshape(1, D_MODEL)

    c_all = jnp.concatenate([c_ctx[None], c, jnp.zeros((MOD_ROWS - 1 - DEC_BATCH, D_MODEL), F32)], axis=0)
    mod = _modulation(c_all, w_ada, b_ada).reshape(DEPTH, MOD_ROWS, N_MOD, D_MODEL)

    tables = _rope_tables()
    na_bias = _na_bias(na_rpb)
    ctx_k_na = t_last(cache_na_k)
    ctx_v_na = t_last(cache_na_v)
    ctx_k_mla, ctx_v_mla = _ctx_kv(cache_mla_ckv, t_last(cache_mla_krope),
                                   packed["wkat"], packed["wkbt"], packed["wuvt"])

    xp = x_prompt.reshape(BATCH * SEQ, D_MODEL)
    xs = x_sample.reshape(DEC_BATCH * DEC_SEQ, D_MODEL)
    n_p = BATCH * SEQ
    caches = None
    for l in range(DEPTH):
        mod_p = mod[l, 0:1]
        mod_s = mod[l, 1:1 + DEC_BATCH]
        last = l == DEPTH - 1
        xp = _ffn(xp, mod_p, n_p, norm_g[l, 0:1], *ffn_w[0], l, 0)
        yc, qna, knat, vnat, qm, kmt, vmt, ckv, krt = _proj(xp, mod_p, n_p, w_ctx, l, SEQ, False, None, caches, 512)
        caches = (knat, vnat, ckv, krt)
        ona, om = _attn_ctx(qna, knat, vnat, qm, kmt, vmt, l, SEQ)
        xp = _mix(xp, yc, ona, om, mod_p, n_p, w_ctx, l)
        xp = _ffn(xp, mod_p, n_p, norm_g[l, 2:3], *ffn_w[1], l, 6, final_row if last else None)
        xs = _ffn(xs, mod_s, DEC_SEQ, norm_g[l, 0:1], *ffn_w[0], l, 0)
        yc, qna, knat, vnat, qm, kmt, vmt = _proj(xs, mod_s, DEC_SEQ, w_lat, l, DEC_SEQ, True, tables, None, DEC_SEQ)
        ona, om = _attn_lat(qna, knat, vnat, ctx_k_na, ctx_v_na, na_bias, qm, kmt, vmt, ctx_k_mla, ctx_v_mla, l)
        xs = _mix(xs, yc, ona, om, mod_s, DEC_SEQ, w_lat, l)
        xs = _ffn(xs, mod_s, DEC_SEQ, norm_g[l, 2:3], *ffn_w[1], l, 6, final_row if last else None)
    new_kt, new_vt, new_ckv, new_krt = caches
    return (xp.reshape(BATCH, SEQ, D_MODEL), xs.reshape(DEC_BATCH, DEC_SEQ, D_MODEL),
            t_last(new_kt), t_last(new_vt), new_ckv, t_last(new_krt))
```

```python
import functools

import jax
import jax.numpy as jnp
import numpy as np
from jax import lax
from jax.experimental import pallas as pl
from jax.experimental.pallas import tpu as pltpu

D_MODEL = 1024
BATCH = 32
SEQ = 256
DEPTH = 2
DEC_BATCH = 8
DEC_SEQ = 1024
PAST_LEN = 256
GRID_W = 64
CONV_DIM = 512
CONV_K = 3
NA_HEADS = 8
NA_HD = 64
NA_WIN_R = 8
NA_WIN_C = 16
MLA_HEADS = 8
MLA_NOPE = 64
MLA_ROPE = 32
MLA_V = 64
Q_LORA = 256
KV_LORA = 128
FFN_DIM = 2816
N_MOD = 9
ROPE_BASE = 10000.0
EPS = 1e-6
NEG_INF = -1e30
LOG2_E = 1.4426950408889634
MLA_SCALE = (MLA_NOPE + MLA_ROPE) ** -0.5 * LOG2_E
NA_SCALE = NA_HD ** -0.5 * LOG2_E

HEADS = 8
HEAD_V = 64
HD = HEADS * NA_HD
MLA_QK_PAD = 128
MLA_QK_W = HEADS * MLA_QK_PAD
FFN_CHUNK = 256
Q_CHUNK = 256
MOD_ROWS = 16
VMEM_LIMIT = 56 * 1024 * 1024
NA_WINDOWS = ((0, 512), (0, 768), (256, 768), (512, 512))
NA_DR_MASKED = 2 * NA_WIN_R - 1


def _na_block_pairs():
    rows = DEC_SEQ // GRID_W
    r_start = np.clip(np.arange(rows) - NA_WIN_R // 2, 0, rows - NA_WIN_R)
    pairs, index = [], []
    for c, (start, count) in enumerate(NA_WINDOWS):
        index.append([])
        for rl in range(Q_CHUNK // GRID_W):
            r = c * (Q_CHUNK // GRID_W) + rl
            assert start // GRID_W <= r_start[r] and r_start[r] + NA_WIN_R <= (start + count) // GRID_W
            index[c].append([])
            for kp in range(count // (2 * GRID_W)):
                pair = []
                for rk in (start // GRID_W + 2 * kp, start // GRID_W + 2 * kp + 1):
                    inside = r_start[r] <= rk < r_start[r] + NA_WIN_R
                    pair.append(int(rk - r + NA_WIN_R - 1) if inside else NA_DR_MASKED)
                pair = tuple(pair)
                if pair not in pairs:
                    pairs.append(pair)
                index[c][rl].append(pairs.index(pair))
    return tuple(pairs), index


NA_BLOCK_PAIRS, NA_BLOCK_INDEX = _na_block_pairs()

BF16 = jnp.bfloat16
F32 = jnp.float32


def _dot(a, b):
    return jnp.dot(a, b, preferred_element_type=F32)


def _dot_nt(a, b):
    return lax.dot_general(a, b, (((1,), (1,)), ((), ())), preferred_element_type=F32)


def _rms(x, g):
    return x * lax.rsqrt(jnp.mean(x * x, axis=-1, keepdims=True) + EPS) * g


def _params(n_axes, flags=None):
    return pltpu.CompilerParams(dimension_semantics=("arbitrary",) * n_axes,
                                vmem_limit_bytes=VMEM_LIMIT, flags=flags)


def _resident(shape, index):
    return pl.BlockSpec(shape, lambda *_: index, pipeline_mode=pl.Buffered(1))


def _mod_body(c_ref, w_ref, b_ref, o_ref):
    c = c_ref[...]
    a = c * jax.nn.sigmoid(c)
    o_ref[...] = _dot(a.astype(BF16), w_ref[...].astype(BF16)) + b_ref[...]


def _modulation(c_all, w_ada, b_ada):
    n_col = N_MOD * D_MODEL
    tn = n_col // 4
    return pl.pallas_call(
        _mod_body,
        out_shape=jax.ShapeDtypeStruct((DEPTH, MOD_ROWS, n_col), F32),
        grid=(DEPTH, n_col // tn),
        in_specs=[pl.BlockSpec((MOD_ROWS, D_MODEL), lambda l, j: (0, 0)),
                  pl.BlockSpec((None, D_MODEL, tn), lambda l, j: (l, 0, j)),
                  pl.BlockSpec((None, 1, tn), lambda l, j: (l, 0, j))],
        out_specs=pl.BlockSpec((None, MOD_ROWS, tn), lambda l, j: (l, 0, j)),
        compiler_params=_params(2),
        name="modulation",
    )(c_all, w_ada, b_ada.reshape(DEPTH, 1, n_col))


def _ffn_body(x_ref, mod_ref, ng_ref, wg_ref, wu_ref, wd_ref, *rest, mod_off, final, n_jobs):
    if final:
        fg_ref, rest = rest[0], rest[1:]
    job_in, o_ref, job_out, a_scr = rest[:n_jobs], rest[n_jobs], rest[n_jobs + 1:2 * n_jobs + 1], rest[-1]
    x = x_ref[...]
    mod = mod_ref[...]
    shift = mod[mod_off:mod_off + 1]
    scale = mod[mod_off + 1:mod_off + 2]
    gate = mod[mod_off + 2:mod_off + 3]
    h = (_rms(x, ng_ref[...]) * (1 + scale) + shift).astype(BF16)
    for f in range(FFN_DIM // FFN_CHUNK):
        cols = slice(f * FFN_CHUNK, (f + 1) * FFN_CHUNK)
        g = _dot(h, wg_ref[:, cols])
        u = _dot(h, wu_ref[:, cols])
        a_scr[:, cols] = (g * jax.nn.sigmoid(g) * u).astype(BF16)
    y = _dot(a_scr[...], wd_ref[...])
    out = x + 0.5 * gate * y
    if final:
        out = _rms(out, fg_ref[...])
    o_ref[...] = out
    for src, dst in zip(job_in, job_out):
        dst[...] = src[...].astype(BF16)


def _cast_job(arr, layer, width=None, col_block=0):
    return arr, layer, arr.shape[-1] if width is None else width, col_block


def _ffn(x, mod, rows_per_mod, ng, wg, wu, wd, mod_off, final_g=None, cast_jobs=(), tm=1024):
    t = x.shape[0]
    n_steps = t // tm
    tiles_per_mod = rows_per_mod // tm
    final = final_g is not None
    in_specs = [pl.BlockSpec((tm, D_MODEL), lambda i: (i, 0)),
                pl.BlockSpec((None, N_MOD, D_MODEL), lambda i: (i // tiles_per_mod, 0, 0)),
                _resident((1, D_MODEL), (0, 0)),
                _resident((D_MODEL, FFN_DIM), (0, 0)),
                _resident((D_MODEL, FFN_DIM), (0, 0)),
                _resident((FFN_DIM, D_MODEL), (0, 0))]
    args = [x, mod, ng, wg, wu, wd]
    if final:
        in_specs.append(_resident((1, D_MODEL), (0, 0)))
        args.append(final_g)
    out_shape = [jax.ShapeDtypeStruct((t, D_MODEL), F32)]
    out_specs = [pl.BlockSpec((tm, D_MODEL), lambda i: (i, 0))]
    for arr, layer, width, col_block in cast_jobs:
        rows = arr.shape[1]
        blk = rows // n_steps
        assert blk * n_steps == rows and blk % 16 == 0 and width % 128 == 0
        in_specs.append(pl.BlockSpec((None, blk, width), lambda i, l=layer, cb=col_block: (l, i, cb)))
        args.append(arr)
        out_shape.append(jax.ShapeDtypeStruct((rows, width), BF16))
        out_specs.append(pl.BlockSpec((blk, width), lambda i: (i, 0)))
    outs = pl.pallas_call(
        functools.partial(_ffn_body, mod_off=mod_off, final=final, n_jobs=len(cast_jobs)),
        out_shape=out_shape,
        grid=(n_steps,),
        in_specs=in_specs,
        out_specs=out_specs,
        scratch_shapes=[pltpu.VMEM((tm, FFN_DIM), BF16)],
        compiler_params=_params(1),
        name="ffn",
    )(*args)
    return outs[0], list(outs[1:])


def _proj_body(x_ref, mod_ref, ng_ref, wa_ref, wb_ref, wt_ref, cw_ref, cb_ref, qn_ref, kvn_ref,
               wq_ref, wkat_ref, wkbt_ref, wuvt_ref, *rest, seq_len, latent, n_alias, own_slot):
    rest = rest[n_alias:]

    def put(ref, b, value):
        for k in range(ref.shape[1]):
            ref[b, k] = value if k == own_slot else jnp.zeros_like(value)

    if latent:
        (qc_ref, qs_ref, kct_ref, kst_ref,
         yc_ref, qna_ref, knat_ref, vnat_ref, qm_ref, kmt_ref, vmt_ref) = rest
    else:
        (yc_ref, qna_ref, knat_ref, vnat_ref, qm_ref, kmt_ref, vmt_ref, ckv_ref, krt_ref) = rest
    tm = x_ref.shape[0]
    x = x_ref[...]
    mod = mod_ref[...]
    h = (_rms(x, ng_ref[...]) * (1 + mod[4:5]) + mod[3:4]).astype(BF16)

    u = _dot(h, wa_ref[:, 0:3 * CONV_DIM])
    v = u[:, CONV_DIM:2 * CONV_DIM] * u[:, 2 * CONV_DIM:3 * CONV_DIM]
    pos = lax.broadcasted_iota(jnp.int32, (tm, 1), 0) % seq_len
    v_prev = jnp.where(pos == 0, 0.0, pltpu.roll(v, 1, 0))
    v_next = jnp.where(pos == seq_len - 1, 0.0, pltpu.roll(v, tm - 1, 0))
    cw = cw_ref[...]
    y = cb_ref[...] + v_prev * cw[0:1]
    y = y + v * cw[1:2]
    y = y + v_next * cw[2:3]
    yc_ref[...] = (u[:, 0:CONV_DIM] * y).astype(BF16)

    qna_ref[...] = (_dot(h, wa_ref[:, 3 * CONV_DIM:3 * CONV_DIM + HD]) * NA_SCALE).astype(BF16)

    ut = _dot_nt(wt_ref[...], h)
    krt = ut[2 * HD:2 * HD + MLA_ROPE]

    u = _dot(h, wb_ref[...])
    cq = _rms(u[:, 0:Q_LORA], qn_ref[...]).astype(BF16)
    ckv = _rms(u[:, Q_LORA:Q_LORA + KV_LORA], kvn_ref[...])
    ckv_b = ckv.astype(BF16)
    q2 = _dot(cq, wq_ref[...])
    if latent:
        knat_ref[...] = ut[0:HD].astype(BF16)
        vnat_ref[...] = ut[HD:2 * HD].astype(BF16)
        krt = krt * kct_ref[...] + ut[2 * HD + MLA_ROPE:2 * HD + 2 * MLA_ROPE] * kst_ref[...]
        qc = qc_ref[...]
        qs = qs_ref[...]
        for hh in range(HEADS):
            cols = slice(hh * MLA_QK_PAD, (hh + 1) * MLA_QK_PAD)
            sw_cols = slice(MLA_QK_W + hh * MLA_QK_PAD, MLA_QK_W + (hh + 1) * MLA_QK_PAD)
            qm_ref[:, cols] = ((q2[:, cols] * qc + q2[:, sw_cols] * qs) * MLA_SCALE).astype(BF16)
    else:
        qm_ref[...] = (q2 * MLA_SCALE).astype(BF16)
    kmt = _dot_nt(wkat_ref[...], ckv_b) + _dot(wkbt_ref[...], krt.astype(BF16))
    vmt = _dot_nt(wuvt_ref[...], ckv_b)
    if latent:
        kmt_ref[...] = kmt.astype(BF16)
        vmt_ref[...] = vmt.astype(BF16)
    else:
        for b in range(tm // seq_len):
            rows = slice(b * seq_len, (b + 1) * seq_len)
            put(knat_ref, b, ut[0:HD, rows].reshape(HEADS, NA_HD, seq_len))
            put(vnat_ref, b, ut[HD:2 * HD, rows].reshape(HEADS, NA_HD, seq_len))
            put(ckv_ref, b, ckv[rows])
            put(krt_ref, b, krt[:, rows])
            kmt_ref[b] = kmt[:, rows].astype(BF16)
            vmt_ref[b] = vmt[:, rows].astype(BF16)


def _proj(x, mod, rows_per_mod, w, layer, seq_len, latent, tables, caches, tm):
    t = x.shape[0]
    n_seq = t // seq_len
    seq_per_tile = tm // seq_len
    tiles_per_mod = rows_per_mod // tm
    wb_cols = w["wb"].shape[-1]
    wq_cols = w["wq"].shape[-1]
    wt_rows = w["wt"].shape[-2]
    row = lambda n: pl.BlockSpec((tm, n), lambda i: (i, 0))
    in_specs = [row(D_MODEL),
                pl.BlockSpec((None, N_MOD, D_MODEL), lambda i: (i // tiles_per_mod, 0, 0)),
                _resident((None, 1, D_MODEL), (layer, 0, 0)),
                _resident((D_MODEL, 3 * CONV_DIM + HD), (0, 0)),
                _resident((D_MODEL, wb_cols), (0, 0)),
                _resident((None, wt_rows, D_MODEL), (layer, 0, 0)),
                _resident((None, CONV_K, CONV_DIM), (layer, 0, 0)),
                _resident((None, 1, CONV_DIM), (layer, 0, 0)),
                _resident((None, 1, Q_LORA), (layer, 0, 0)),
                _resident((None, 1, KV_LORA), (layer, 0, 0)),
                _resident((None, Q_LORA, wq_cols), (layer, 0, 0)),
                _resident((None, MLA_QK_W, KV_LORA), (layer, 0, 0)),
                _resident((None, MLA_QK_W, MLA_ROPE), (layer, 0, 0)),
                _resident((None, HD, KV_LORA), (layer, 0, 0))]
    args = [x, mod, w["ng1"], w["wa"], w["wb"], w["wt"], w["conv_w"], w["conv_b"], w["qnorm"], w["kvnorm"],
            w["wq"], w["wkat"], w["wkbt"], w["wuvt"]]
    out_shape = [jax.ShapeDtypeStruct((t, CONV_DIM), BF16),
                 jax.ShapeDtypeStruct((t, HD), BF16)]
    out_specs = [row(CONV_DIM), row(HD)]
    aliases = {}
    n_alias = 0
    own_slot = 0
    if latent:
        assert tm == seq_len
        in_specs += [_resident((seq_len, MLA_QK_PAD), (0, 0)), _resident((seq_len, MLA_QK_PAD), (0, 0)),
                     _resident((MLA_ROPE, seq_len), (0, 0)), _resident((MLA_ROPE, seq_len), (0, 0))]
        args += list(tables)
        seq_blk = lambda n: pl.BlockSpec((None, n, seq_len), lambda i: (i, 0, 0))
        out_shape += [jax.ShapeDtypeStruct((n_seq, HD, seq_len), BF16),
                      jax.ShapeDtypeStruct((n_seq, HD, seq_len), BF16),
                      jax.ShapeDtypeStruct((t, MLA_QK_W), BF16),
                      jax.ShapeDtypeStruct((n_seq, MLA_QK_W, seq_len), BF16),
                      jax.ShapeDtypeStruct((n_seq, HD, seq_len), BF16)]
        out_specs += [seq_blk(HD), seq_blk(HD), row(MLA_QK_W), seq_blk(MLA_QK_W), seq_blk(HD)]
    else:
        if caches is not None:
            n_alias = len(caches)
            in_specs += [pl.BlockSpec(memory_space=pl.ANY)] * n_alias
            args += list(caches)
            aliases = {len(args) - n_alias + k: 2 + (0, 1, 5, 6)[k] for k in range(n_alias)}
            n_slots, first_slot = 1, layer
        else:
            assert layer == 0
            n_slots, first_slot, own_slot = DEPTH, 0, layer
        cache_blk = lambda *dims: pl.BlockSpec((seq_per_tile, n_slots) + dims,
                                               lambda i: (i, first_slot) + (0,) * len(dims))
        seq_blk = lambda n: pl.BlockSpec((seq_per_tile, n, seq_len), lambda i: (i, 0, 0))
        out_shape += [jax.ShapeDtypeStruct((n_seq, DEPTH, HEADS, NA_HD, seq_len), F32),
                      jax.ShapeDtypeStruct((n_seq, DEPTH, HEADS, NA_HD, seq_len), F32),
                      jax.ShapeDtypeStruct((t, MLA_QK_W), BF16),
                      jax.ShapeDtypeStruct((n_seq, MLA_QK_W, seq_len), BF16),
                      jax.ShapeDtypeStruct((n_seq, HD, seq_len), BF16),
                      jax.ShapeDtypeStruct((n_seq, DEPTH, seq_len, KV_LORA), F32),
                      jax.ShapeDtypeStruct((n_seq, DEPTH, MLA_ROPE, seq_len), F32)]
        out_specs += [cache_blk(HEADS, NA_HD, seq_len), cache_blk(HEADS, NA_HD, seq_len), row(MLA_QK_W),
                      seq_blk(MLA_QK_W), seq_blk(HD), cache_blk(seq_len, KV_LORA), cache_blk(MLA_ROPE, seq_len)]
    return pl.pallas_call(
        functools.partial(_proj_body, seq_len=seq_len, latent=latent, n_alias=n_alias, own_slot=own_slot),
        out_shape=out_shape,
        grid=(t // tm,),
        in_specs=in_specs,
        out_specs=out_specs,
        input_output_aliases=aliases,
        compiler_params=_params(1),
        name="mixer_proj",
    )(*args)


def _ctxkv_body(ckv_ref, krt_ref, wkat_ref, wkbt_ref, wuvt_ref, k_ref, v_ref):
    ckv = ckv_ref[...].astype(BF16)
    krt = krt_ref[...].astype(BF16)
    k_ref[...] = (_dot_nt(wkat_ref[...], ckv) + _dot(wkbt_ref[...], krt)).astype(BF16)
    v_ref[...] = _dot_nt(wuvt_ref[...], ckv).astype(BF16)


def _ctx_kv(cache_ckv, cache_krt, wkat, wkbt, wuvt):
    return pl.pallas_call(
        _ctxkv_body,
        out_shape=[jax.ShapeDtypeStruct((DEPTH, DEC_BATCH, MLA_QK_W, PAST_LEN), BF16),
                   jax.ShapeDtypeStruct((DEPTH, DEC_BATCH, HD, PAST_LEN), BF16)],
        grid=(DEPTH, DEC_BATCH),
        in_specs=[pl.BlockSpec((None, None, PAST_LEN, KV_LORA), lambda l, b: (b, l, 0, 0)),
                  pl.BlockSpec((None, None, MLA_ROPE, PAST_LEN), lambda l, b: (b, l, 0, 0)),
                  pl.BlockSpec((None, MLA_QK_W, KV_LORA), lambda l, b: (l, 0, 0)),
                  pl.BlockSpec((None, MLA_QK_W, MLA_ROPE), lambda l, b: (l, 0, 0)),
                  pl.BlockSpec((None, HD, KV_LORA), lambda l, b: (l, 0, 0))],
        out_specs=[pl.BlockSpec((None, None, MLA_QK_W, PAST_LEN), lambda l, b: (l, b, 0, 0)),
                   pl.BlockSpec((None, None, HD, PAST_LEN), lambda l, b: (l, b, 0, 0))],
        compiler_params=_params(2),
        name="ctx_kv",
    )(cache_ckv, cache_krt, wkat, wkbt, wuvt)


def _softmax_pv(s, vt):
    m = jnp.max(s, axis=-1, keepdims=True)
    p = jnp.exp2(s - m)
    den = jnp.sum(p, axis=-1, keepdims=True)
    return _dot_nt(p.astype(BF16), vt) / den


def _pair_slot(x, j):
    z = jnp.zeros_like(x)
    return jnp.concatenate([x, z] if j == 0 else [z, x], axis=0)


def _attn_ctx_body(qna_ref, knat_ref, vnat_ref, qm_ref, kmt_ref, vmt_ref, ona_ref, om_ref, *, seq_len):
    for b in range(qna_ref.shape[0] // seq_len):
        rows = slice(b * seq_len, (b + 1) * seq_len)
        for hp in range(HEADS // 2):
            pair = slice(hp * 2 * HEAD_V, (hp + 1) * 2 * HEAD_V)
            q = qna_ref[rows, pair]
            o_na = o_m = None
            for j in range(2):
                hh = 2 * hp + j
                kt = _pair_slot(knat_ref[b, hh].astype(BF16), j)
                vt = _pair_slot(vnat_ref[b, hh].astype(BF16), j)
                o = _softmax_pv(_dot(q, kt), vt)
                o_na = o if j == 0 else o_na + o
                qk = slice(hh * MLA_QK_PAD, (hh + 1) * MLA_QK_PAD)
                vt = _pair_slot(vmt_ref[b, hh * HEAD_V:(hh + 1) * HEAD_V, :], j)
                o = _softmax_pv(_dot(qm_ref[rows, qk], kmt_ref[b, qk, :]), vt)
                o_m = o if j == 0 else o_m + o
            ona_ref[rows, pair] = o_na.astype(BF16)
            om_ref[rows, pair] = o_m.astype(BF16)


def _attn_ctx(qna, knat, vnat, qm, kmt, vmt, layer, seq_len, tm=512):
    t = qna.shape[0]
    nb = tm // seq_len
    row = lambda n: pl.BlockSpec((tm, n), lambda i: (i, 0))
    cache_blk = pl.BlockSpec((nb, None, HEADS, NA_HD, seq_len), lambda i: (i, layer, 0, 0, 0))
    seq_blk = lambda n: pl.BlockSpec((nb, n, seq_len), lambda i: (i, 0, 0))
    return pl.pallas_call(
        functools.partial(_attn_ctx_body, seq_len=seq_len),
        out_shape=[jax.ShapeDtypeStruct((t, HD), BF16), jax.ShapeDtypeStruct((t, HD), BF16)],
        grid=(t // tm,),
        in_specs=[row(HD), cache_blk, cache_blk, row(MLA_QK_W), seq_blk(MLA_QK_W), seq_blk(HD)],
        out_specs=[row(HD), row(HD)],
        compiler_params=_params(1),
        name="attn_ctx",
    )(qna, knat, vnat, qm, kmt, vmt)


def _attn_lat_body(qna_ref, knat_ref, vnat_ref, kctx_ref, vctx_ref, bias_ref,
                   qm_ref, kmt_ref, vmt_ref, kmctx_ref, vmctx_ref, ona_ref, om_ref):
    cat = lambda *a: jnp.concatenate(a, axis=1)
    head = lambda j: slice(j * HEAD_V, (j + 1) * HEAD_V)
    kc = [_pair_slot(kctx_ref[j].astype(BF16), j) for j in range(2)]
    vc = [_pair_slot(vctx_ref[j].astype(BF16), j) for j in range(2)]
    for c, (start, count) in enumerate(NA_WINDOWS):
        rows = slice(c * Q_CHUNK, (c + 1) * Q_CHUNK)
        keys = slice(start, start + count)
        q = qna_ref[rows, :]
        for j in range(2):
            bias = jnp.concatenate(
                [cat(*[bias_ref[j, p] for p in NA_BLOCK_INDEX[c][rl]]) for rl in range(Q_CHUNK // GRID_W)], axis=0)
            s = cat(_dot(q, kc[j]), _dot(q, _pair_slot(knat_ref[head(j), keys], j)) + bias)
            o = _softmax_pv(s, cat(vc[j], _pair_slot(vnat_ref[head(j), keys], j)))
            o_na = o if j == 0 else o_na + o
        ona_ref[rows, :] = o_na.astype(BF16)
    kt, vt = [], []
    for j in range(2):
        qk = slice(j * MLA_QK_PAD, (j + 1) * MLA_QK_PAD)
        kt.append(cat(kmctx_ref[qk, :], kmt_ref[qk, :]))
        vt.append(_pair_slot(cat(vmctx_ref[head(j), :], vmt_ref[head(j), :]), j))
    for c in range(DEC_SEQ // Q_CHUNK):
        rows = slice(c * Q_CHUNK, (c + 1) * Q_CHUNK)
        for j in range(2):
            o = _softmax_pv(_dot(qm_ref[rows, j * MLA_QK_PAD:(j + 1) * MLA_QK_PAD], kt[j]), vt[j])
            o_m = o if j == 0 else o_m + o
        om_ref[rows, :] = o_m.astype(BF16)


def _attn_lat(qna, knat, vnat, kctx, vctx, bias, qm, kmt, vmt, kmctx, vmctx, layer):
    t = qna.shape[0]
    s = DEC_SEQ
    return pl.pallas_call(
        _attn_lat_body,
        out_shape=[jax.ShapeDtypeStruct((t, HD), BF16), jax.ShapeDtypeStruct((t, HD), BF16)],
        grid=(HEADS // 2, t // s),
        in_specs=[pl.BlockSpec((s, 2 * NA_HD), lambda hp, b: (b, hp)),
                  pl.BlockSpec((None, 2 * NA_HD, s), lambda hp, b: (b, hp, 0)),
                  pl.BlockSpec((None, 2 * HEAD_V, s), lambda hp, b: (b, hp, 0)),
                  pl.BlockSpec((None, None, 2, NA_HD, PAST_LEN), lambda hp, b: (b, layer, hp, 0, 0)),
                  pl.BlockSpec((None, None, 2, NA_HD, PAST_LEN), lambda hp, b: (b, layer, hp, 0, 0)),
                  pl.BlockSpec((None, 2, len(NA_BLOCK_PAIRS), GRID_W, 2 * GRID_W),
                               lambda hp, b: (layer, hp, 0, 0, 0)),
                  pl.BlockSpec((s, 2 * MLA_QK_PAD), lambda hp, b: (b, hp)),
                  pl.BlockSpec((None, 2 * MLA_QK_PAD, s), lambda hp, b: (b, hp, 0)),
                  pl.BlockSpec((None, 2 * HEAD_V, s), lambda hp, b: (b, hp, 0)),
                  pl.BlockSpec((None, None, 2 * MLA_QK_PAD, PAST_LEN), lambda hp, b: (layer, b, hp, 0)),
                  pl.BlockSpec((None, None, 2 * HEAD_V, PAST_LEN), lambda hp, b: (layer, b, hp, 0))],
        out_specs=[pl.BlockSpec((s, 2 * HEAD_V), lambda hp, b: (b, hp)),
                   pl.BlockSpec((s, 2 * HEAD_V), lambda hp, b: (b, hp))],
        compiler_params=_params(2),
        name="attn_lat",
    )(qna, knat, vnat, kctx, vctx, bias, qm, kmt, vmt, kmctx, vmctx)


def _mix_body(x_ref, yc_ref, ona_ref, om_ref, mod_ref, ng_ref, wg_ref, wco_ref, wno_ref, wmo_ref,
              wo_ref, o_ref):
    x = x_ref[...]
    mod = mod_ref[...]
    h = (_rms(x, ng_ref[...]) * (1 + mod[4:5]) + mod[3:4]).astype(BF16)
    z = jax.nn.sigmoid(_dot(h, wg_ref[:, 0:D_MODEL])) * _dot(yc_ref[...], wco_ref[...])
    z = z + jax.nn.sigmoid(_dot(h, wg_ref[:, D_MODEL:2 * D_MODEL])) * _dot(ona_ref[...], wno_ref[...])
    z = z + jax.nn.sigmoid(_dot(h, wg_ref[:, 2 * D_MODEL:3 * D_MODEL])) * _dot(om_ref[...], wmo_ref[...])
    o_ref[...] = x + mod[5:6] * _dot(z.astype(BF16), wo_ref[...])


def _mix(x, yc, ona, om, mod, rows_per_mod, w, layer, tm=1024):
    t = x.shape[0]
    tiles_per_mod = rows_per_mod // tm
    row = lambda n: pl.BlockSpec((tm, n), lambda i: (i, 0))
    return pl.pallas_call(
        _mix_body,
        out_shape=jax.ShapeDtypeStruct((t, D_MODEL), F32),
        grid=(t // tm,),
        in_specs=[row(D_MODEL), row(CONV_DIM), row(HD), row(HD),
                  pl.BlockSpec((None, N_MOD, D_MODEL), lambda i: (i // tiles_per_mod, 0, 0)),
                  _resident((None, 1, D_MODEL), (layer, 0, 0)),
                  _resident((None, D_MODEL, 3 * D_MODEL), (layer, 0, 0)),
                  _resident((CONV_DIM, D_MODEL), (0, 0)),
                  _resident((HD, D_MODEL), (0, 0)),
                  _resident((HD, D_MODEL), (0, 0)),
                  _resident((D_MODEL, D_MODEL), (0, 0))],
        out_specs=row(D_MODEL),
        compiler_params=_params(1),
        name="mixer_out",
    )(x, yc, ona, om, mod, w["ng1"], w["wgate"], w["wco"], w["wno"], w["wmo"], w["wo"])


def _rope_tables():
    f32 = np.float32
    half = MLA_ROPE // 2
    nf = half // 2
    inv = (f32(1.0) / (f32(ROPE_BASE) ** (np.arange(nf, dtype=f32) / f32(nf)))).astype(f32)
    t = np.arange(DEC_SEQ)
    rows = (t // GRID_W).astype(f32)[:, None] * inv[None, :]
    cols = (t % GRID_W).astype(f32)[:, None] * inv[None, :]
    cos = np.concatenate([np.cos(rows), np.cos(rows), np.cos(cols), np.cos(cols)], axis=-1).astype(f32)
    sin = np.concatenate([np.sin(rows), np.sin(rows), np.sin(cols), np.sin(cols)], axis=-1).astype(f32)
    pad = MLA_QK_PAD - MLA_NOPE - MLA_ROPE
    q_cos = np.concatenate([np.ones((DEC_SEQ, MLA_NOPE), f32), cos, np.zeros((DEC_SEQ, pad), f32)], axis=-1)
    q_sin = np.concatenate([np.zeros((DEC_SEQ, MLA_NOPE), f32), sin, np.zeros((DEC_SEQ, pad), f32)], axis=-1)
    return tuple(jnp.asarray(a) for a in (q_cos, q_sin, np.ascontiguousarray(cos.T), np.ascontiguousarray(sin.T)))


def _rope_swap(w):
    nf = MLA_ROPE // 4
    a, b, c, d = (w[..., i * nf:(i + 1) * nf] for i in range(4))
    return jnp.concatenate([-b, a, -d, c], axis=-1)


def _na_bias(rpb):
    n_dc = 2 * NA_WIN_C - 1
    col = np.arange(GRID_W)
    c_start = np.clip(col - NA_WIN_C // 2, 0, GRID_W - NA_WIN_C)
    c_in = (col[None, :] >= c_start[:, None]) & (col[None, :] < c_start[:, None] + NA_WIN_C)
    dc = np.clip(col[None, :] - col[:, None] + (NA_WIN_C - 1), 0, n_dc - 1)
    pick_dc = (dc[None] == np.arange(n_dc)[:, None, None]).astype(np.float32)
    n_pairs = len(NA_BLOCK_PAIRS)
    pick_dr = np.zeros((n_pairs, 2, NA_DR_MASKED), np.float32)
    for p, pair in enumerate(NA_BLOCK_PAIRS):
        for side, d in enumerate(pair):
            if d != NA_DR_MASKED:
                pick_dr[p, side, d] = 1.0
    keep = pick_dr.sum(-1).astype(bool)[:, None, :, None] & c_in[None, :, None, :]
    keep = keep.reshape(n_pairs, GRID_W, 2 * GRID_W)
    pick_side_dc = np.zeros((2, n_dc, GRID_W, 2, GRID_W), np.float32)
    for side in range(2):
        pick_side_dc[side, :, :, side, :] = pick_dc
    pick_side_dc = pick_side_dc.reshape(2, n_dc, GRID_W, 2 * GRID_W)
    hi = lax.Precision.HIGHEST
    by_row = jnp.einsum("psd,lhdj->lhpsj", jnp.asarray(pick_dr), rpb, precision=hi)
    blocks = jnp.einsum("lhpsj,sjqn->lhpqn", by_row, jnp.asarray(pick_side_dc), precision=hi)
    return jnp.where(jnp.asarray(keep), blocks * LOG2_E, NEG_INF)


def _pack_weights(w_in, w_uq, w_ukv):
    c3 = 3 * CONV_DIM + 3 * HD
    t_last = lambda a: jnp.swapaxes(a, -1, -2)
    w_kv = w_in[..., 3 * CONV_DIM + HD:c3]
    w_kr = w_in[..., c3 + Q_LORA + KV_LORA:c3 + Q_LORA + KV_LORA + MLA_ROPE]
    wgate = w_in[..., c3 + Q_LORA + KV_LORA + MLA_ROPE:]
    wt = t_last(jnp.concatenate([w_kv, w_kr, _rope_swap(w_kr)], axis=-1))
    uq = w_uq.reshape(DEPTH, Q_LORA, MLA_HEADS, MLA_NOPE + MLA_ROPE)
    pad = MLA_QK_PAD - MLA_NOPE - MLA_ROPE
    zp = jnp.zeros(uq.shape[:-1] + (pad,), F32)
    zn = jnp.zeros(uq.shape[:-1] + (MLA_NOPE,), F32)
    q_ext = jnp.concatenate([uq, zp], axis=-1).reshape(DEPTH, Q_LORA, MLA_QK_W)
    q_sw = jnp.concatenate([zn, _rope_swap(uq[..., MLA_NOPE:]), zp], axis=-1).reshape(q_ext.shape)
    ukv = w_ukv.reshape(DEPTH, KV_LORA, MLA_HEADS, MLA_NOPE + MLA_V)
    zk = jnp.zeros(ukv.shape[:-1] + (MLA_QK_PAD - MLA_NOPE,), F32)
    wka = jnp.concatenate([ukv[..., :MLA_NOPE], zk], axis=-1).reshape(DEPTH, KV_LORA, MLA_QK_W)
    eye = jnp.concatenate([jnp.zeros((MLA_ROPE, MLA_NOPE), F32), jnp.eye(MLA_ROPE, dtype=F32),
                           jnp.zeros((MLA_ROPE, pad), F32)], axis=-1)
    wkb = jnp.broadcast_to(jnp.tile(eye, (1, MLA_HEADS))[None], (DEPTH, MLA_ROPE, MLA_QK_W))
    wuv = ukv[..., MLA_NOPE:].reshape(DEPTH, KV_LORA, HD)
    b = lambda a: a.astype(BF16)
    return dict(wt=b(wt), wgate=b(wgate),
                wq_lat=b(jnp.concatenate([q_ext, q_sw], axis=-1)), wq_ctx=b(q_ext),
                wkat=b(t_last(wka)), wkbt=b(t_last(wkb)), wuvt=b(t_last(wuv)))


def kernel(x_prompt, x_sample, cache_na_k, cache_na_v, cache_mla_ckv, cache_mla_krope, c, c_ctx,
           w_ada, b_ada, norm_g, w_ffn1_gate, w_ffn1_up, w_ffn1_down, w_ffn2_gate, w_ffn2_up, w_ffn2_down,
           w_in, conv_w, conv_b, na_rpb, mla_qnorm, w_uq, mla_kvnorm, w_ukv,
           w_conv_out, w_na_out, w_mla_out, w_o, final_g):
    b16 = lambda a: a.astype(BF16)
    t_last = lambda a: jnp.swapaxes(a, -1, -2)
    packed = _pack_weights(w_in, w_uq, w_ukv)
    shared = dict(conv_w=conv_w, conv_b=conv_b.reshape(DEPTH, 1, CONV_DIM),
                  qnorm=mla_qnorm.reshape(DEPTH, 1, Q_LORA), kvnorm=mla_kvnorm.reshape(DEPTH, 1, KV_LORA),
                  ng1=norm_g[:, 1:2],
                  **{k: packed[k] for k in ("wt", "wgate", "wkat", "wkbt", "wuvt")})
    ffn1_f32 = (w_ffn1_gate, w_ffn1_up, w_ffn1_down)
    ffn2_f32 = (w_ffn2_gate, w_ffn2_up, w_ffn2_down)
    ffn1_w = {0: tuple(b16(w[0]) for w in ffn1_f32)}
    ffn2_w = {}
    mixer_w = {}
    mixer_srcs = ((w_in, 3 * CONV_DIM + HD, 0), (w_in, Q_LORA + KV_LORA, (3 * CONV_DIM + 3 * HD) // (Q_LORA + KV_LORA)),
                  (w_conv_out, None, 0), (w_na_out, None, 0), (w_mla_out, None, 0), (w_o, None, 0))
    final_row = final_g.reshape(1, D_MODEL)

    c_all = jnp.concatenate([c_ctx[None], c, jnp.zeros((MOD_ROWS - 1 - DEC_BATCH, D_MODEL), F32)], axis=0)
    mod = _modulation(c_all, w_ada, b_ada).reshape(DEPTH, MOD_ROWS, N_MOD, D_MODEL)

    tables = _rope_tables()
    na_bias = _na_bias(na_rpb)
    ctx_k_na = t_last(cache_na_k)
    ctx_v_na = t_last(cache_na_v)
    ctx_k_mla, ctx_v_mla = _ctx_kv(cache_mla_ckv, t_last(cache_mla_krope),
                                   packed["wkat"], packed["wkbt"], packed["wuvt"])

    xp = x_prompt.reshape(BATCH * SEQ, D_MODEL)
    xs = x_sample.reshape(DEC_BATCH * DEC_SEQ, D_MODEL)
    n_p = BATCH * SEQ
    caches = None
    for l in range(DEPTH):
        mod_p = mod[l, 0:1]
        mod_s = mod[l, 1:1 + DEC_BATCH]
        last = l == DEPTH - 1
        jobs = [_cast_job(w, l) for w in ffn2_f32]
        if l == 0:
            jobs += [_cast_job(a, ll, width, cb) for ll in range(DEPTH) for a, width, cb in mixer_srcs]
        xp, cast = _ffn(xp, mod_p, n_p, norm_g[l, 0:1], *ffn1_w[l], 0, cast_jobs=jobs, tm=512)
        ffn2_w[l] = tuple(cast[:3])
        for ll in range(DEPTH if l == 0 else 0):
            names = ("wa", "wb", "wco", "wno", "wmo", "wo")
            mixer_w[ll] = dict(zip(names, cast[3 + len(names) * ll:3 + len(names) * (ll + 1)]))
        w_ctx = dict(shared, wq=packed["wq_ctx"], **mixer_w[l])
        w_lat = dict(shared, wq=packed["wq_lat"], **mixer_w[l])
        yc, qna, knat, vnat, qm, kmt, vmt, ckv, krt = _proj(xp, mod_p, n_p, w_ctx, l, SEQ, False, None, caches, 512)
        caches = (knat, vnat, ckv, krt)
        ona, om = _attn_ctx(qna, knat, vnat, qm, kmt, vmt, l, SEQ)
        xp = _mix(xp, yc, ona, om, mod_p, n_p, w_ctx, l)
        jobs = [] if last else [_cast_job(w, l + 1) for w in ffn1_f32]
        xp, cast = _ffn(xp, mod_p, n_p, norm_g[l, 2:3], *ffn2_w[l], 6, final_row if last else None,
                        cast_jobs=jobs, tm=512 if jobs else 1024)
        if jobs:
            ffn1_w[l + 1] = tuple(cast)
        xs, _ = _ffn(xs, mod_s, DEC_SEQ, norm_g[l, 0:1], *ffn1_w[l], 0)
        yc, qna, knat, vnat, qm, kmt, vmt = _proj(xs, mod_s, DEC_SEQ, w_lat, l, DEC_SEQ, True, tables, None, DEC_SEQ)
        ona, om = _attn_lat(qna, knat, vnat, ctx_k_na, ctx_v_na, na_bias, qm, kmt, vmt, ctx_k_mla, ctx_v_mla, l)
        xs = _mix(xs, yc, ona, om, mod_s, DEC_SEQ, w_lat, l)
        xs, _ = _ffn(xs, mod_s, DEC_SEQ, norm_g[l, 2:3], *ffn2_w[l], 6, final_row if last else None)
    new_kt, new_vt, new_ckv, new_krt = caches
    return (xp.reshape(BATCH, SEQ, D_MODEL), xs.reshape(DEC_BATCH, DEC_SEQ, D_MODEL),
            t_last(new_kt), t_last(new_vt), new_ckv, t_last(new_krt))
```

```python
import functools
import math

import jax
import jax.numpy as jnp
import numpy as np
from jax import lax
from jax.experimental import pallas as pl
from jax.experimental.pallas import tpu as pltpu

D_MODEL = 1024
BATCH = 32
SEQ = 256
DEPTH = 2
DEC_BATCH = 8
DEC_SEQ = 1024
PAST_LEN = 256
GRID_W = 64
CONV_DIM = 512
CONV_K = 3
NA_HEADS = 8
NA_HD = 64
NA_WIN_R = 8
NA_WIN_C = 16
MLA_HEADS = 8
MLA_NOPE = 64
MLA_ROPE = 32
MLA_V = 64
Q_LORA = 256
KV_LORA = 128
FFN_DIM = 2816
N_MOD = 9
ROPE_BASE = 10000.0
EPS = 1e-6
NEG_INF = -1e30
LOG2_E = 1.4426950408889634
MLA_SCALE = (MLA_NOPE + MLA_ROPE) ** -0.5 * LOG2_E
NA_SCALE = NA_HD ** -0.5 * LOG2_E

HEADS = 8
HEAD_V = 64
HD = HEADS * NA_HD
W_IN_KV = 3 * CONV_DIM + HD
W_IN_LORA = W_IN_KV + 2 * HD
W_IN_KR = W_IN_LORA + Q_LORA + KV_LORA
W_IN_GATE = W_IN_KR + MLA_ROPE
MLA_QK_PAD = 128
MLA_QK_W = HEADS * MLA_QK_PAD
FFN_CHUNK = 256
Q_CHUNK = 256
MOD_ROWS = 16
VMEM_LIMIT = 56 * 1024 * 1024
NA_WINDOWS = ((0, 512), (0, 768), (256, 768), (512, 512))
NA_DR_MASKED = 2 * NA_WIN_R - 1


def _na_block_pairs():
    rows = DEC_SEQ // GRID_W
    r_start = np.clip(np.arange(rows) - NA_WIN_R // 2, 0, rows - NA_WIN_R)
    pairs, index = [], []
    for c, (start, count) in enumerate(NA_WINDOWS):
        index.append([])
        for rl in range(Q_CHUNK // GRID_W):
            r = c * (Q_CHUNK // GRID_W) + rl
            assert start // GRID_W <= r_start[r] and r_start[r] + NA_WIN_R <= (start + count) // GRID_W
            index[c].append([])
            for kp in range(count // (2 * GRID_W)):
                pair = []
                for rk in (start // GRID_W + 2 * kp, start // GRID_W + 2 * kp + 1):
                    inside = r_start[r] <= rk < r_start[r] + NA_WIN_R
                    pair.append(int(rk - r + NA_WIN_R - 1) if inside else NA_DR_MASKED)
                pair = tuple(pair)
                if pair not in pairs:
                    pairs.append(pair)
                index[c][rl].append(pairs.index(pair))
    return tuple(pairs), index


NA_BLOCK_PAIRS, NA_BLOCK_INDEX = _na_block_pairs()

BF16 = jnp.bfloat16
F32 = jnp.float32


def _dot(a, b):
    return jnp.dot(a, b, preferred_element_type=F32)


def _dot_nt(a, b):
    return lax.dot_general(a, b, (((1,), (1,)), ((), ())), preferred_element_type=F32)


def _rms(x, g):
    return x * lax.rsqrt(jnp.mean(x * x, axis=-1, keepdims=True) + EPS) * g


def _params(n_axes, flags=None):
    return pltpu.CompilerParams(dimension_semantics=("arbitrary",) * n_axes,
                                vmem_limit_bytes=VMEM_LIMIT, flags=flags)


def _resident(shape, index):
    return pl.BlockSpec(shape, lambda *_: index, pipeline_mode=pl.Buffered(1))


def _mod_body(c_ref, w_ref, b_ref, o_ref):
    c = c_ref[...]
    a = c * jax.nn.sigmoid(c)
    o_ref[...] = _dot(a.astype(BF16), w_ref[...].astype(BF16)) + b_ref[...]


def _modulation(c_all, w_ada, b_ada):
    n_col = N_MOD * D_MODEL
    tn = n_col // 4
    return pl.pallas_call(
        _mod_body,
        out_shape=jax.ShapeDtypeStruct((DEPTH, MOD_ROWS, n_col), F32),
        grid=(DEPTH, n_col // tn),
        in_specs=[pl.BlockSpec((MOD_ROWS, D_MODEL), lambda l, j: (0, 0)),
                  pl.BlockSpec((None, D_MODEL, tn), lambda l, j: (l, 0, j)),
                  pl.BlockSpec((None, 1, tn), lambda l, j: (l, 0, j))],
        out_specs=pl.BlockSpec((None, MOD_ROWS, tn), lambda l, j: (l, 0, j)),
        compiler_params=_params(2),
        name="modulation",
    )(c_all, w_ada, b_ada.reshape(DEPTH, 1, n_col))


def _ffn_body(x_ref, mod_ref, ng_ref, wg_ref, wu_ref, wd_ref, *rest, mod_off, final, n_jobs):
    if final:
        fg_ref, rest = rest[0], rest[1:]
    job_in, o_ref, job_out, a_scr = rest[:n_jobs], rest[n_jobs], rest[n_jobs + 1:2 * n_jobs + 1], rest[-1]
    x = x_ref[...]
    mod = mod_ref[...]
    shift = mod[mod_off:mod_off + 1]
    scale = mod[mod_off + 1:mod_off + 2]
    gate = mod[mod_off + 2:mod_off + 3]
    h = (_rms(x, ng_ref[...]) * (1 + scale) + shift).astype(BF16)
    for f in range(FFN_DIM // FFN_CHUNK):
        cols = slice(f * FFN_CHUNK, (f + 1) * FFN_CHUNK)
        g = _dot(h, wg_ref[:, cols])
        u = _dot(h, wu_ref[:, cols])
        a_scr[:, cols] = (g * jax.nn.sigmoid(g) * u).astype(BF16)
    y = _dot(a_scr[...], wd_ref[...])
    out = x + 0.5 * gate * y
    if final:
        out = _rms(out, fg_ref[...])
    o_ref[...] = out
    for src, dst in zip(job_in, job_out):
        dst[...] = src[...].reshape(dst.shape).astype(BF16)


def _cast_job(arr, layer, row_start=0, n_rows=None):
    return arr, layer, row_start, arr.shape[1] if n_rows is None else n_rows


def _ffn(x, mod, rows_per_mod, ng, wg, wu, wd, mod_off, final_g=None, cast_jobs=(), tm=1024):
    t = x.shape[0]
    n_steps = t // tm
    tiles_per_mod = rows_per_mod // tm
    final = final_g is not None
    in_specs = [pl.BlockSpec((tm, D_MODEL), lambda i: (i, 0)),
                pl.BlockSpec((None, N_MOD, D_MODEL), lambda i: (i // tiles_per_mod, 0, 0)),
                _resident((1, D_MODEL), (0, 0)),
                _resident((D_MODEL, FFN_DIM), (0, 0)),
                _resident((D_MODEL, FFN_DIM), (0, 0)),
                _resident((FFN_DIM, D_MODEL), (0, 0))]
    args = [x, mod, ng, wg, wu, wd]
    if final:
        in_specs.append(_resident((1, D_MODEL), (0, 0)))
        args.append(final_g)
    out_shape = [jax.ShapeDtypeStruct((t, D_MODEL), F32)]
    out_specs = [pl.BlockSpec((tm, D_MODEL), lambda i: (i, 0))]
    for arr, layer, row_start, rows in cast_jobs:
        width = arr.shape[2]
        blk = rows // n_steps
        assert blk * n_steps == rows and blk % 16 == 0 and row_start % 16 == 0
        if row_start % blk == 0:
            spec = pl.BlockSpec((None, blk, width), lambda i, l=layer, b0=row_start // blk: (l, b0 + i, 0))
        else:
            g = math.gcd(row_start, blk)
            spec = pl.BlockSpec((pl.Element(1), pl.Element(blk), pl.Element(width)),
                                lambda i, l=layer, r0=row_start // g, n=blk // g, g=g: (l, (r0 + n * i) * g, 0))
        in_specs.append(spec)
        args.append(arr)
        out_shape.append(jax.ShapeDtypeStruct((rows, width), BF16))
        out_specs.append(pl.BlockSpec((blk, width), lambda i: (i, 0)))
    outs = pl.pallas_call(
        functools.partial(_ffn_body, mod_off=mod_off, final=final, n_jobs=len(cast_jobs)),
        out_shape=out_shape,
        grid=(n_steps,),
        in_specs=in_specs,
        out_specs=out_specs,
        scratch_shapes=[pltpu.VMEM((tm, FFN_DIM), BF16)],
        compiler_params=_params(1),
        name="ffn",
    )(*args)
    return outs[0], list(outs[1:])


def _proj_body(x_ref, mod_ref, ng_ref, wat_ref, wbt_ref, wt_ref, cw_ref, cb_ref, qn_ref, kvn_ref,
               wq_ref, wkat_ref, wkbt_ref, wuvt_ref, *rest, seq_len, latent, n_alias, own_slot):
    rest = rest[n_alias:]

    def put(ref, b, value):
        for k in range(ref.shape[1]):
            ref[b, k] = value if k == own_slot else jnp.zeros_like(value)

    if latent:
        (qc_ref, qs_ref, kct_ref, kst_ref,
         yc_ref, qna_ref, knat_ref, vnat_ref, qm_ref, kmt_ref, vmt_ref) = rest
    else:
        (yc_ref, qna_ref, knat_ref, vnat_ref, qm_ref, kmt_ref, vmt_ref, ckv_ref, krt_ref) = rest
    tm = x_ref.shape[0]
    x = x_ref[...]
    mod = mod_ref[...]
    h = (_rms(x, ng_ref[...]) * (1 + mod[4:5]) + mod[3:4]).astype(BF16)

    u = _dot_nt(h, wat_ref[0:3 * CONV_DIM, :])
    v = u[:, CONV_DIM:2 * CONV_DIM] * u[:, 2 * CONV_DIM:3 * CONV_DIM]
    pos = lax.broadcasted_iota(jnp.int32, (tm, 1), 0) % seq_len
    v_prev = jnp.where(pos == 0, 0.0, pltpu.roll(v, 1, 0))
    v_next = jnp.where(pos == seq_len - 1, 0.0, pltpu.roll(v, tm - 1, 0))
    cw = cw_ref[...]
    y = cb_ref[...] + v_prev * cw[0:1]
    y = y + v * cw[1:2]
    y = y + v_next * cw[2:3]
    yc_ref[...] = (u[:, 0:CONV_DIM] * y).astype(BF16)

    qna_ref[...] = (_dot_nt(h, wat_ref[3 * CONV_DIM:3 * CONV_DIM + HD, :]) * NA_SCALE).astype(BF16)

    ut = _dot_nt(wt_ref[...], h)
    krt = ut[2 * HD:2 * HD + MLA_ROPE]

    u = _dot_nt(h, wbt_ref[...])
    cq = _rms(u[:, 0:Q_LORA], qn_ref[...]).astype(BF16)
    ckv = _rms(u[:, Q_LORA:Q_LORA + KV_LORA], kvn_ref[...])
    ckv_b = ckv.astype(BF16)
    q2 = _dot(cq, wq_ref[...])
    if latent:
        knat_ref[...] = ut[0:HD].astype(BF16)
        vnat_ref[...] = ut[HD:2 * HD].astype(BF16)
        krt = krt * kct_ref[...] + ut[2 * HD + MLA_ROPE:2 * HD + 2 * MLA_ROPE] * kst_ref[...]
        qc = qc_ref[...]
        qs = qs_ref[...]
        for hh in range(HEADS):
            cols = slice(hh * MLA_QK_PAD, (hh + 1) * MLA_QK_PAD)
            sw_cols = slice(MLA_QK_W + hh * MLA_QK_PAD, MLA_QK_W + (hh + 1) * MLA_QK_PAD)
            qm_ref[:, cols] = ((q2[:, cols] * qc + q2[:, sw_cols] * qs) * MLA_SCALE).astype(BF16)
    else:
        qm_ref[...] = (q2 * MLA_SCALE).astype(BF16)
    kmt = _dot_nt(wkat_ref[...], ckv_b) + _dot(wkbt_ref[...], krt.astype(BF16))
    vmt = _dot_nt(wuvt_ref[...], ckv_b)
    if latent:
        kmt_ref[...] = kmt.astype(BF16)
        vmt_ref[...] = vmt.astype(BF16)
    else:
        for b in range(tm // seq_len):
            rows = slice(b * seq_len, (b + 1) * seq_len)
            put(knat_ref, b, ut[0:HD, rows].reshape(HEADS, NA_HD, seq_len))
            put(vnat_ref, b, ut[HD:2 * HD, rows].reshape(HEADS, NA_HD, seq_len))
            put(ckv_ref, b, ckv[rows])
            put(krt_ref, b, krt[:, rows])
            kmt_ref[b] = kmt[:, rows].astype(BF16)
            vmt_ref[b] = vmt[:, rows].astype(BF16)


def _proj(x, mod, rows_per_mod, w, layer, seq_len, latent, tables, caches, tm):
    t = x.shape[0]
    n_seq = t // seq_len
    seq_per_tile = tm // seq_len
    tiles_per_mod = rows_per_mod // tm
    wq_cols = w["wq"].shape[-1]
    wt_rows = w["wt"].shape[-2]
    row = lambda n: pl.BlockSpec((tm, n), lambda i: (i, 0))
    in_specs = [row(D_MODEL),
                pl.BlockSpec((None, N_MOD, D_MODEL), lambda i: (i // tiles_per_mod, 0, 0)),
                _resident((None, 1, D_MODEL), (layer, 0, 0)),
                _resident((3 * CONV_DIM + HD, D_MODEL), (0, 0)),
                _resident((None, Q_LORA + KV_LORA, D_MODEL), (layer, 0, 0)),
                _resident((None, wt_rows, D_MODEL), (layer, 0, 0)),
                _resident((None, CONV_K, CONV_DIM), (layer, 0, 0)),
                _resident((None, 1, CONV_DIM), (layer, 0, 0)),
                _resident((None, 1, Q_LORA), (layer, 0, 0)),
                _resident((None, 1, KV_LORA), (layer, 0, 0)),
                _resident((None, Q_LORA, wq_cols), (layer, 0, 0)),
                _resident((None, MLA_QK_W, KV_LORA), (layer, 0, 0)),
                _resident((None, MLA_QK_W, MLA_ROPE), (layer, 0, 0)),
                _resident((None, HD, KV_LORA), (layer, 0, 0))]
    args = [x, mod, w["ng1"], w["wat"], w["wbt"], w["wt"], w["conv_w"], w["conv_b"], w["qnorm"], w["kvnorm"],
            w["wq"], w["wkat"], w["wkbt"], w["wuvt"]]
    out_shape = [jax.ShapeDtypeStruct((t, CONV_DIM), BF16),
                 jax.ShapeDtypeStruct((t, HD), BF16)]
    out_specs = [row(CONV_DIM), row(HD)]
    aliases = {}
    n_alias = 0
    own_slot = 0
    if latent:
        assert tm == seq_len
        in_specs += [_resident((seq_len, MLA_QK_PAD), (0, 0)), _resident((seq_len, MLA_QK_PAD), (0, 0)),
                     _resident((MLA_ROPE, seq_len), (0, 0)), _resident((MLA_ROPE, seq_len), (0, 0))]
        args += list(tables)
        seq_blk = lambda n: pl.BlockSpec((None, n, seq_len), lambda i: (i, 0, 0))
        out_shape += [jax.ShapeDtypeStruct((n_seq, HD, seq_len), BF16),
                      jax.ShapeDtypeStruct((n_seq, HD, seq_len), BF16),
                      jax.ShapeDtypeStruct((t, MLA_QK_W), BF16),
                      jax.ShapeDtypeStruct((n_seq, MLA_QK_W, seq_len), BF16),
                      jax.ShapeDtypeStruct((n_seq, HD, seq_len), BF16)]
        out_specs += [seq_blk(HD), seq_blk(HD), row(MLA_QK_W), seq_blk(MLA_QK_W), seq_blk(HD)]
    else:
        if caches is not None:
            n_alias = len(caches)
            in_specs += [pl.BlockSpec(memory_space=pl.ANY)] * n_alias
            args += list(caches)
            aliases = {len(args) - n_alias + k: 2 + (0, 1, 5, 6)[k] for k in range(n_alias)}
            n_slots, first_slot = 1, layer
        else:
            assert layer == 0
            n_slots, first_slot, own_slot = DEPTH, 0, layer
        cache_blk = lambda *dims: pl.BlockSpec((seq_per_tile, n_slots) + dims,
                                               lambda i: (i, first_slot) + (0,) * len(dims))
        seq_blk = lambda n: pl.BlockSpec((seq_per_tile, n, seq_len), lambda i: (i, 0, 0))
        out_shape += [jax.ShapeDtypeStruct((n_seq, DEPTH, HEADS, NA_HD, seq_len), F32),
                      jax.ShapeDtypeStruct((n_seq, DEPTH, HEADS, NA_HD, seq_len), F32),
                      jax.ShapeDtypeStruct((t, MLA_QK_W), BF16),
                      jax.ShapeDtypeStruct((n_seq, MLA_QK_W, seq_len), BF16),
                      jax.ShapeDtypeStruct((n_seq, HD, seq_len), BF16),
                      jax.ShapeDtypeStruct((n_seq, DEPTH, seq_len, KV_LORA), F32),
                      jax.ShapeDtypeStruct((n_seq, DEPTH, MLA_ROPE, seq_len), F32)]
        out_specs += [cache_blk(HEADS, NA_HD, seq_len), cache_blk(HEADS, NA_HD, seq_len), row(MLA_QK_W),
                      seq_blk(MLA_QK_W), seq_blk(HD), cache_blk(seq_len, KV_LORA), cache_blk(MLA_ROPE, seq_len)]
    return pl.pallas_call(
        functools.partial(_proj_body, seq_len=seq_len, latent=latent, n_alias=n_alias, own_slot=own_slot),
        out_shape=out_shape,
        grid=(t // tm,),
        in_specs=in_specs,
        out_specs=out_specs,
        input_output_aliases=aliases,
        compiler_params=_params(1),
        name="mixer_proj",
    )(*args)


def _ctxkv_body(ckv_ref, krt_ref, wkat_ref, wkbt_ref, wuvt_ref, k_ref, v_ref):
    ckv = ckv_ref[...].astype(BF16)
    krt = krt_ref[...].astype(BF16)
    k_ref[...] = (_dot_nt(wkat_ref[...], ckv) + _dot(wkbt_ref[...], krt)).astype(BF16)
    v_ref[...] = _dot_nt(wuvt_ref[...], ckv).astype(BF16)


def _ctx_kv(cache_ckv, cache_krt, wkat, wkbt, wuvt):
    return pl.pallas_call(
        _ctxkv_body,
        out_shape=[jax.ShapeDtypeStruct((DEPTH, DEC_BATCH, MLA_QK_W, PAST_LEN), BF16),
                   jax.ShapeDtypeStruct((DEPTH, DEC_BATCH, HD, PAST_LEN), BF16)],
        grid=(DEPTH, DEC_BATCH),
        in_specs=[pl.BlockSpec((None, None, PAST_LEN, KV_LORA), lambda l, b: (b, l, 0, 0)),
                  pl.BlockSpec((None, None, MLA_ROPE, PAST_LEN), lambda l, b: (b, l, 0, 0)),
                  pl.BlockSpec((None, MLA_QK_W, KV_LORA), lambda l, b: (l, 0, 0)),
                  pl.BlockSpec((None, MLA_QK_W, MLA_ROPE), lambda l, b: (l, 0, 0)),
                  pl.BlockSpec((None, HD, KV_LORA), lambda l, b: (l, 0, 0))],
        out_specs=[pl.BlockSpec((None, None, MLA_QK_W, PAST_LEN), lambda l, b: (l, b, 0, 0)),
                   pl.BlockSpec((None, None, HD, PAST_LEN), lambda l, b: (l, b, 0, 0))],
        compiler_params=_params(2),
        name="ctx_kv",
    )(cache_ckv, cache_krt, wkat, wkbt, wuvt)


def _softmax_pv(s, vt):
    m = jnp.max(s, axis=-1, keepdims=True)
    p = jnp.exp2(s - m)
    den = jnp.sum(p, axis=-1, keepdims=True)
    return _dot_nt(p.astype(BF16), vt) / den


def _pair_slot(x, j):
    z = jnp.zeros_like(x)
    return jnp.concatenate([x, z] if j == 0 else [z, x], axis=0)


def _attn_ctx_body(qna_ref, knat_ref, vnat_ref, qm_ref, kmt_ref, vmt_ref, ona_ref, om_ref, *, seq_len):
    for b in range(qna_ref.shape[0] // seq_len):
        rows = slice(b * seq_len, (b + 1) * seq_len)
        for hp in range(HEADS // 2):
            pair = slice(hp * 2 * HEAD_V, (hp + 1) * 2 * HEAD_V)
            q = qna_ref[rows, pair]
            o_na = o_m = None
            for j in range(2):
                hh = 2 * hp + j
                kt = _pair_slot(knat_ref[b, hh].astype(BF16), j)
                vt = _pair_slot(vnat_ref[b, hh].astype(BF16), j)
                o = _softmax_pv(_dot(q, kt), vt)
                o_na = o if j == 0 else o_na + o
                qk = slice(hh * MLA_QK_PAD, (hh + 1) * MLA_QK_PAD)
                vt = _pair_slot(vmt_ref[b, hh * HEAD_V:(hh + 1) * HEAD_V, :], j)
                o = _softmax_pv(_dot(qm_ref[rows, qk], kmt_ref[b, qk, :]), vt)
                o_m = o if j == 0 else o_m + o
            ona_ref[rows, pair] = o_na.astype(BF16)
            om_ref[rows, pair] = o_m.astype(BF16)


def _attn_ctx(qna, knat, vnat, qm, kmt, vmt, layer, seq_len, tm=512):
    t = qna.shape[0]
    nb = tm // seq_len
    row = lambda n: pl.BlockSpec((tm, n), lambda i: (i, 0))
    cache_blk = pl.BlockSpec((nb, None, HEADS, NA_HD, seq_len), lambda i: (i, layer, 0, 0, 0))
    seq_blk = lambda n: pl.BlockSpec((nb, n, seq_len), lambda i: (i, 0, 0))
    return pl.pallas_call(
        functools.partial(_attn_ctx_body, seq_len=seq_len),
        out_shape=[jax.ShapeDtypeStruct((t, HD), BF16), jax.ShapeDtypeStruct((t, HD), BF16)],
        grid=(t // tm,),
        in_specs=[row(HD), cache_blk, cache_blk, row(MLA_QK_W), seq_blk(MLA_QK_W), seq_blk(HD)],
        out_specs=[row(HD), row(HD)],
        compiler_params=_params(1),
        name="attn_ctx",
    )(qna, knat, vnat, qm, kmt, vmt)


def _attn_lat_body(qna_ref, knat_ref, vnat_ref, kctx_ref, vctx_ref, bias_ref,
                   qm_ref, kmt_ref, vmt_ref, kmctx_ref, vmctx_ref, ona_ref, om_ref):
    cat = lambda *a: jnp.concatenate(a, axis=1)
    head = lambda j: slice(j * HEAD_V, (j + 1) * HEAD_V)
    kc = [_pair_slot(kctx_ref[j].astype(BF16), j) for j in range(2)]
    vc = [_pair_slot(vctx_ref[j].astype(BF16), j) for j in range(2)]
    for c, (start, count) in enumerate(NA_WINDOWS):
        rows = slice(c * Q_CHUNK, (c + 1) * Q_CHUNK)
        keys = slice(start, start + count)
        q = qna_ref[rows, :]
        for j in range(2):
            bias = jnp.concatenate(
                [cat(*[bias_ref[j, p] for p in NA_BLOCK_INDEX[c][rl]]) for rl in range(Q_CHUNK // GRID_W)], axis=0)
            s = cat(_dot(q, kc[j]), _dot(q, _pair_slot(knat_ref[head(j), keys], j)) + bias)
            o = _softmax_pv(s, cat(vc[j], _pair_slot(vnat_ref[head(j), keys], j)))
            o_na = o if j == 0 else o_na + o
        ona_ref[rows, :] = o_na.astype(BF16)
    kt, vt = [], []
    for j in range(2):
        qk = slice(j * MLA_QK_PAD, (j + 1) * MLA_QK_PAD)
        kt.append(cat(kmctx_ref[qk, :], kmt_ref[qk, :]))
        vt.append(_pair_slot(cat(vmctx_ref[head(j), :], vmt_ref[head(j), :]), j))
    for c in range(DEC_SEQ // Q_CHUNK):
        rows = slice(c * Q_CHUNK, (c + 1) * Q_CHUNK)
        for j in range(2):
            o = _softmax_pv(_dot(qm_ref[rows, j * MLA_QK_PAD:(j + 1) * MLA_QK_PAD], kt[j]), vt[j])
            o_m = o if j == 0 else o_m + o
        om_ref[rows, :] = o_m.astype(BF16)


def _attn_lat(qna, knat, vnat, kctx, vctx, bias, qm, kmt, vmt, kmctx, vmctx, layer):
    t = qna.shape[0]
    s = DEC_SEQ
    return pl.pallas_call(
        _attn_lat_body,
        out_shape=[jax.ShapeDtypeStruct((t, HD), BF16), jax.ShapeDtypeStruct((t, HD), BF16)],
        grid=(HEADS // 2, t // s),
        in_specs=[pl.BlockSpec((s, 2 * NA_HD), lambda hp, b: (b, hp)),
                  pl.BlockSpec((None, 2 * NA_HD, s), lambda hp, b: (b, hp, 0)),
                  pl.BlockSpec((None, 2 * HEAD_V, s), lambda hp, b: (b, hp, 0)),
                  pl.BlockSpec((None, None, 2, NA_HD, PAST_LEN), lambda hp, b: (b, layer, hp, 0, 0)),
                  pl.BlockSpec((None, None, 2, NA_HD, PAST_LEN), lambda hp, b: (b, layer, hp, 0, 0)),
                  pl.BlockSpec((None, 2, len(NA_BLOCK_PAIRS), GRID_W, 2 * GRID_W),
                               lambda hp, b: (layer, hp, 0, 0, 0)),
                  pl.BlockSpec((s, 2 * MLA_QK_PAD), lambda hp, b: (b, hp)),
                  pl.BlockSpec((None, 2 * MLA_QK_PAD, s), lambda hp, b: (b, hp, 0)),
                  pl.BlockSpec((None, 2 * HEAD_V, s), lambda hp, b: (b, hp, 0)),
                  pl.BlockSpec((None, None, 2 * MLA_QK_PAD, PAST_LEN), lambda hp, b: (layer, b, hp, 0)),
                  pl.BlockSpec((None, None, 2 * HEAD_V, PAST_LEN), lambda hp, b: (layer, b, hp, 0))],
        out_specs=[pl.BlockSpec((s, 2 * HEAD_V), lambda hp, b: (b, hp)),
                   pl.BlockSpec((s, 2 * HEAD_V), lambda hp, b: (b, hp))],
        compiler_params=_params(2),
        name="attn_lat",
    )(qna, knat, vnat, kctx, vctx, bias, qm, kmt, vmt, kmctx, vmctx)


def _mix_body(x_ref, yc_ref, ona_ref, om_ref, mod_ref, ng_ref, wgt_ref, wco_ref, wno_ref, wmo_ref,
              wo_ref, o_ref):
    x = x_ref[...]
    mod = mod_ref[...]
    h = (_rms(x, ng_ref[...]) * (1 + mod[4:5]) + mod[3:4]).astype(BF16)
    gate = lambda k: jax.nn.sigmoid(_dot_nt(h, wgt_ref[k * D_MODEL:(k + 1) * D_MODEL, :]))
    z = gate(0) * _dot(yc_ref[...], wco_ref[...])
    z = z + gate(1) * _dot(ona_ref[...], wno_ref[...])
    z = z + gate(2) * _dot(om_ref[...], wmo_ref[...])
    o_ref[...] = x + mod[5:6] * _dot(z.astype(BF16), wo_ref[...])


def _mix(x, yc, ona, om, mod, rows_per_mod, w, layer, tm=1024):
    t = x.shape[0]
    tiles_per_mod = rows_per_mod // tm
    row = lambda n: pl.BlockSpec((tm, n), lambda i: (i, 0))
    return pl.pallas_call(
        _mix_body,
        out_shape=jax.ShapeDtypeStruct((t, D_MODEL), F32),
        grid=(t // tm,),
        in_specs=[row(D_MODEL), row(CONV_DIM), row(HD), row(HD),
                  pl.BlockSpec((None, N_MOD, D_MODEL), lambda i: (i // tiles_per_mod, 0, 0)),
                  _resident((None, 1, D_MODEL), (layer, 0, 0)),
                  _resident((3 * D_MODEL, D_MODEL), (0, 0)),
                  _resident((CONV_DIM, D_MODEL), (0, 0)),
                  _resident((HD, D_MODEL), (0, 0)),
                  _resident((HD, D_MODEL), (0, 0)),
                  _resident((D_MODEL, D_MODEL), (0, 0))],
        out_specs=row(D_MODEL),
        compiler_params=_params(1),
        name="mixer_out",
    )(x, yc, ona, om, mod, w["ng1"], w["wgt"], w["wco"], w["wno"], w["wmo"], w["wo"])


def _rope_tables():
    f32 = np.float32
    half = MLA_ROPE // 2
    nf = half // 2
    inv = (f32(1.0) / (f32(ROPE_BASE) ** (np.arange(nf, dtype=f32) / f32(nf)))).astype(f32)
    t = np.arange(DEC_SEQ)
    rows = (t // GRID_W).astype(f32)[:, None] * inv[None, :]
    cols = (t % GRID_W).astype(f32)[:, None] * inv[None, :]
    cos = np.concatenate([np.cos(rows), np.cos(rows), np.cos(cols), np.cos(cols)], axis=-1).astype(f32)
    sin = np.concatenate([np.sin(rows), np.sin(rows), np.sin(cols), np.sin(cols)], axis=-1).astype(f32)
    pad = MLA_QK_PAD - MLA_NOPE - MLA_ROPE
    q_cos = np.concatenate([np.ones((DEC_SEQ, MLA_NOPE), f32), cos, np.zeros((DEC_SEQ, pad), f32)], axis=-1)
    q_sin = np.concatenate([np.zeros((DEC_SEQ, MLA_NOPE), f32), sin, np.zeros((DEC_SEQ, pad), f32)], axis=-1)
    return tuple(jnp.asarray(a) for a in (q_cos, q_sin, np.ascontiguousarray(cos.T), np.ascontiguousarray(sin.T)))


def _rope_swap(w):
    nf = MLA_ROPE // 4
    a, b, c, d = (w[..., i * nf:(i + 1) * nf] for i in range(4))
    return jnp.concatenate([-b, a, -d, c], axis=-1)


def _na_bias(rpb):
    n_dc = 2 * NA_WIN_C - 1
    col = np.arange(GRID_W)
    c_start = np.clip(col - NA_WIN_C // 2, 0, GRID_W - NA_WIN_C)
    c_in = (col[None, :] >= c_start[:, None]) & (col[None, :] < c_start[:, None] + NA_WIN_C)
    dc = np.clip(col[None, :] - col[:, None] + (NA_WIN_C - 1), 0, n_dc - 1)
    pick_dc = (dc[None] == np.arange(n_dc)[:, None, None]).astype(np.float32)
    n_pairs = len(NA_BLOCK_PAIRS)
    pick_dr = np.zeros((n_pairs, 2, NA_DR_MASKED), np.float32)
    for p, pair in enumerate(NA_BLOCK_PAIRS):
        for side, d in enumerate(pair):
            if d != NA_DR_MASKED:
                pick_dr[p, side, d] = 1.0
    keep = pick_dr.sum(-1).astype(bool)[:, None, :, None] & c_in[None, :, None, :]
    keep = keep.reshape(n_pairs, GRID_W, 2 * GRID_W)
    pick_side_dc = np.zeros((2, n_dc, GRID_W, 2, GRID_W), np.float32)
    for side in range(2):
        pick_side_dc[side, :, :, side, :] = pick_dc
    pick_side_dc = pick_side_dc.reshape(2, n_dc, GRID_W, 2 * GRID_W)
    hi = lax.Precision.HIGHEST
    by_row = jnp.einsum("psd,lhdj->lhpsj", jnp.asarray(pick_dr), rpb, precision=hi)
    blocks = jnp.einsum("lhpsj,sjqn->lhpqn", by_row, jnp.asarray(pick_side_dc), precision=hi)
    return jnp.where(jnp.asarray(keep), blocks * LOG2_E, NEG_INF)


def _pack_weights(w_int, w_uq, w_ukv):
    t_last = lambda a: jnp.swapaxes(a, -1, -2)
    w_mid = lax.optimization_barrier(w_int[:, W_IN_KV:W_IN_GATE]).astype(BF16)
    w_kvt = w_mid[:, :W_IN_LORA - W_IN_KV]
    w_krt = w_mid[:, W_IN_KR - W_IN_KV:]
    wt = jnp.concatenate([w_kvt, w_krt, t_last(_rope_swap(t_last(w_krt)))], axis=1)
    wbt = w_mid[:, W_IN_LORA - W_IN_KV:W_IN_KR - W_IN_KV]
    uq = w_uq.reshape(DEPTH, Q_LORA, MLA_HEADS, MLA_NOPE + MLA_ROPE)
    pad = MLA_QK_PAD - MLA_NOPE - MLA_ROPE
    zp = jnp.zeros(uq.shape[:-1] + (pad,), F32)
    zn = jnp.zeros(uq.shape[:-1] + (MLA_NOPE,), F32)
    q_ext = jnp.concatenate([uq, zp], axis=-1).reshape(DEPTH, Q_LORA, MLA_QK_W)
    q_sw = jnp.concatenate([zn, _rope_swap(uq[..., MLA_NOPE:]), zp], axis=-1).reshape(q_ext.shape)
    ukv = w_ukv.reshape(DEPTH, KV_LORA, MLA_HEADS, MLA_NOPE + MLA_V)
    zk = jnp.zeros(ukv.shape[:-1] + (MLA_QK_PAD - MLA_NOPE,), F32)
    wka = jnp.concatenate([ukv[..., :MLA_NOPE], zk], axis=-1).reshape(DEPTH, KV_LORA, MLA_QK_W)
    eye = jnp.concatenate([jnp.zeros((MLA_ROPE, MLA_NOPE), F32), jnp.eye(MLA_ROPE, dtype=F32),
                           jnp.zeros((MLA_ROPE, pad), F32)], axis=-1)
    wkb = jnp.broadcast_to(jnp.tile(eye, (1, MLA_HEADS))[None], (DEPTH, MLA_ROPE, MLA_QK_W))
    wuv = ukv[..., MLA_NOPE:].reshape(DEPTH, KV_LORA, HD)
    b = lambda a: a.astype(BF16)
    return dict(wt=b(wt), wbt=b(wbt),
                wq_lat=b(jnp.concatenate([q_ext, q_sw], axis=-1)), wq_ctx=b(q_ext),
                wkat=b(t_last(wka)), wkbt=b(t_last(wkb)), wuvt=b(t_last(wuv)))


def kernel(x_prompt, x_sample, cache_na_k, cache_na_v, cache_mla_ckv, cache_mla_krope, c, c_ctx,
           w_ada, b_ada, norm_g, w_ffn1_gate, w_ffn1_up, w_ffn1_down, w_ffn2_gate, w_ffn2_up, w_ffn2_down,
           w_in, conv_w, conv_b, na_rpb, mla_qnorm, w_uq, mla_kvnorm, w_ukv,
           w_conv_out, w_na_out, w_mla_out, w_o, final_g):
    b16 = lambda a: a.astype(BF16)
    t_last = lambda a: jnp.swapaxes(a, -1, -2)
    w_int = t_last(w_in)
    packed = _pack_weights(w_int, w_uq, w_ukv)
    shared = dict(conv_w=conv_w, conv_b=conv_b.reshape(DEPTH, 1, CONV_DIM),
                  qnorm=mla_qnorm.reshape(DEPTH, 1, Q_LORA), kvnorm=mla_kvnorm.reshape(DEPTH, 1, KV_LORA),
                  ng1=norm_g[:, 1:2],
                  **{k: packed[k] for k in ("wt", "wbt", "wkat", "wkbt", "wuvt")})
    ffn1_f32 = (w_ffn1_gate, w_ffn1_up, w_ffn1_down)
    ffn2_f32 = (w_ffn2_gate, w_ffn2_up, w_ffn2_down)
    ffn1_w = {0: tuple(b16(w[0]) for w in ffn1_f32)}
    ffn2_w = {}
    mixer_w = {}
    mixer_srcs = ((w_int, 0, W_IN_KV), (w_int, W_IN_GATE, 3 * D_MODEL),
                  (w_conv_out, 0, None), (w_na_out, 0, None), (w_mla_out, 0, None), (w_o, 0, None))
    final_row = final_g.reshape(1, D_MODEL)

    c_all = jnp.concatenate([c_ctx[None], c, jnp.zeros((MOD_ROWS - 1 - DEC_BATCH, D_MODEL), F32)], axis=0)
    mod = _modulation(c_all, w_ada, b_ada).reshape(DEPTH, MOD_ROWS, N_MOD, D_MODEL)

    tables = _rope_tables()
    na_bias = _na_bias(na_rpb)
    ctx_k_na = t_last(cache_na_k)
    ctx_v_na = t_last(cache_na_v)
    ctx_k_mla, ctx_v_mla = _ctx_kv(cache_mla_ckv, t_last(cache_mla_krope),
                                   packed["wkat"], packed["wkbt"], packed["wuvt"])

    xp = x_prompt.reshape(BATCH * SEQ, D_MODEL)
    xs = x_sample.reshape(DEC_BATCH * DEC_SEQ, D_MODEL)
    n_p = BATCH * SEQ
    caches = None
    for l in range(DEPTH):
        mod_p = mod[l, 0:1]
        mod_s = mod[l, 1:1 + DEC_BATCH]
        last = l == DEPTH - 1
        jobs = [_cast_job(w, l) for w in ffn2_f32]
        if l == 0:
            jobs += [_cast_job(a, ll, r0, n) for ll in range(DEPTH) for a, r0, n in mixer_srcs]
        xp, cast = _ffn(xp, mod_p, n_p, norm_g[l, 0:1], *ffn1_w[l], 0, cast_jobs=jobs, tm=512)
        ffn2_w[l] = tuple(cast[:3])
        for ll in range(DEPTH if l == 0 else 0):
            names = ("wat", "wgt", "wco", "wno", "wmo", "wo")
            mixer_w[ll] = dict(zip(names, cast[3 + len(names) * ll:3 + len(names) * (ll + 1)]))
        w_ctx = dict(shared, wq=packed["wq_ctx"], **mixer_w[l])
        w_lat = dict(shared, wq=packed["wq_lat"], **mixer_w[l])
        yc, qna, knat, vnat, qm, kmt, vmt, ckv, krt = _proj(xp, mod_p, n_p, w_ctx, l, SEQ, False, None, caches, 512)
        caches = (knat, vnat, ckv, krt)
        ona, om = _attn_ctx(qna, knat, vnat, qm, kmt, vmt, l, SEQ)
        xp = _mix(xp, yc, ona, om, mod_p, n_p, w_ctx, l)
        jobs = [] if last else [_cast_job(w, l + 1) for w in ffn1_f32]
        xp, cast = _ffn(xp, mod_p, n_p, norm_g[l, 2:3], *ffn2_w[l], 6, final_row if last else None,
                        cast_jobs=jobs, tm=512 if jobs else 1024)
        if jobs:
            ffn1_w[l + 1] = tuple(cast)
        xs, _ = _ffn(xs, mod_s, DEC_SEQ, norm_g[l, 0:1], *ffn1_w[l], 0)
        yc, qna, knat, vnat, qm, kmt, vmt = _proj(xs, mod_s, DEC_SEQ, w_lat, l, DEC_SEQ, True, tables, None, DEC_SEQ)
        ona, om = _attn_lat(qna, knat, vnat, ctx_k_na, ctx_v_na, na_bias, qm, kmt, vmt, ctx_k_mla, ctx_v_mla, l)
        xs = _mix(xs, yc, ona, om, mod_s, DEC_SEQ, w_lat, l)
        xs, _ = _ffn(xs, mod_s, DEC_SEQ, norm_g[l, 2:3], *ffn2_w[l], 6, final_row if last else None)
    new_kt, new_vt, new_ckv, new_krt = caches
    return (xp.reshape(BATCH, SEQ, D_MODEL), xs.reshape(DEC_BATCH, DEC_SEQ, D_MODEL),
            t_last(new_kt), t_last(new_vt), new_ckv, t_last(new_krt))
```

```python
import functools
import math

import jax
import jax.numpy as jnp
import numpy as np
from jax import lax
from jax.experimental import pallas as pl
from jax.experimental.pallas import tpu as pltpu

D_MODEL = 1024
BATCH = 32
SEQ = 256
DEPTH = 2
DEC_BATCH = 8
DEC_SEQ = 1024
PAST_LEN = 256
GRID_W = 64
CONV_DIM = 512
CONV_K = 3
NA_HEADS = 8
NA_HD = 64
NA_WIN_R = 8
NA_WIN_C = 16
MLA_HEADS = 8
MLA_NOPE = 64
MLA_ROPE = 32
MLA_V = 64
Q_LORA = 256
KV_LORA = 128
FFN_DIM = 2816
N_MOD = 9
ROPE_BASE = 10000.0
EPS = 1e-6
NEG_INF = -1e30
LOG2_E = 1.4426950408889634
MLA_SCALE = (MLA_NOPE + MLA_ROPE) ** -0.5 * LOG2_E
NA_SCALE = NA_HD ** -0.5 * LOG2_E

HEADS = 8
HEAD_V = 64
HD = HEADS * NA_HD
W_IN_KV = 3 * CONV_DIM + HD
W_IN_LORA = W_IN_KV + 2 * HD
W_IN_KR = W_IN_LORA + Q_LORA + KV_LORA
W_IN_GATE = W_IN_KR + MLA_ROPE
MLA_QK_PAD = 128
MLA_QK_W = HEADS * MLA_QK_PAD
FFN_CHUNK = 256
Q_CHUNK = 256
MLA_Q_ROWS = 128
MOD_ROWS = 16
VMEM_LIMIT = 56 * 1024 * 1024
NA_WINDOWS = ((0, 512), (0, 768), (256, 768), (512, 512))
NA_DR_MASKED = 2 * NA_WIN_R - 1


def _na_block_pairs():
    rows = DEC_SEQ // GRID_W
    r_start = np.clip(np.arange(rows) - NA_WIN_R // 2, 0, rows - NA_WIN_R)
    pairs, index = [], []
    for c, (start, count) in enumerate(NA_WINDOWS):
        index.append([])
        for rl in range(Q_CHUNK // GRID_W):
            r = c * (Q_CHUNK // GRID_W) + rl
            assert start // GRID_W <= r_start[r] and r_start[r] + NA_WIN_R <= (start + count) // GRID_W
            index[c].append([])
            for kp in range(count // (2 * GRID_W)):
                pair = []
                for rk in (start // GRID_W + 2 * kp, start // GRID_W + 2 * kp + 1):
                    inside = r_start[r] <= rk < r_start[r] + NA_WIN_R
                    pair.append(int(rk - r + NA_WIN_R - 1) if inside else NA_DR_MASKED)
                pair = tuple(pair)
                if pair not in pairs:
                    pairs.append(pair)
                index[c][rl].append(pairs.index(pair))
    return tuple(pairs), index


NA_BLOCK_PAIRS, NA_BLOCK_INDEX = _na_block_pairs()

BF16 = jnp.bfloat16
F32 = jnp.float32


def _dot(a, b):
    return jnp.dot(a, b, preferred_element_type=F32)


def _dot_nt(a, b):
    return lax.dot_general(a, b, (((1,), (1,)), ((), ())), preferred_element_type=F32)


def _rms(x, g):
    return x * lax.rsqrt(jnp.mean(x * x, axis=-1, keepdims=True) + EPS) * g


def _params(n_axes, flags=None):
    return pltpu.CompilerParams(dimension_semantics=("arbitrary",) * n_axes,
                                vmem_limit_bytes=VMEM_LIMIT, flags=flags)


def _resident(shape, index):
    return pl.BlockSpec(shape, lambda *_: index, pipeline_mode=pl.Buffered(1))


def _mod_body(c_ref, w_ref, b_ref, o_ref):
    c = c_ref[...]
    a = c * jax.nn.sigmoid(c)
    o_ref[...] = _dot(a.astype(BF16), w_ref[...].astype(BF16)) + b_ref[...]


def _modulation(c_all, w_ada, b_ada):
    n_col = N_MOD * D_MODEL
    tn = n_col // 4
    return pl.pallas_call(
        _mod_body,
        out_shape=jax.ShapeDtypeStruct((DEPTH, MOD_ROWS, n_col), F32),
        grid=(DEPTH, n_col // tn),
        in_specs=[pl.BlockSpec((MOD_ROWS, D_MODEL), lambda l, j: (0, 0)),
                  pl.BlockSpec((None, D_MODEL, tn), lambda l, j: (l, 0, j)),
                  pl.BlockSpec((None, 1, tn), lambda l, j: (l, 0, j))],
        out_specs=pl.BlockSpec((None, MOD_ROWS, tn), lambda l, j: (l, 0, j)),
        compiler_params=_params(2),
        name="modulation",
    )(c_all, w_ada, b_ada.reshape(DEPTH, 1, n_col))


def _ffn_body(x_ref, mod_ref, ng_ref, wg_ref, wu_ref, wd_ref, *rest, mod_off, final, n_jobs):
    if final:
        fg_ref, rest = rest[0], rest[1:]
    job_in, o_ref, job_out, a_scr = rest[:n_jobs], rest[n_jobs], rest[n_jobs + 1:2 * n_jobs + 1], rest[-1]
    x = x_ref[...]
    mod = mod_ref[...]
    shift = mod[mod_off:mod_off + 1]
    scale = mod[mod_off + 1:mod_off + 2]
    gate = mod[mod_off + 2:mod_off + 3]
    h = (_rms(x, ng_ref[...]) * (1 + scale) + shift).astype(BF16)
    for f in range(FFN_DIM // FFN_CHUNK):
        cols = slice(f * FFN_CHUNK, (f + 1) * FFN_CHUNK)
        g = _dot(h, wg_ref[:, cols])
        u = _dot(h, wu_ref[:, cols])
        a_scr[:, cols] = (g * jax.nn.sigmoid(g) * u).astype(BF16)
    y = _dot(a_scr[...], wd_ref[...])
    out = x + 0.5 * gate * y
    if final:
        out = _rms(out, fg_ref[...])
    o_ref[...] = out
    for src, dst in zip(job_in, job_out):
        dst[...] = src[...].reshape(dst.shape).astype(BF16)


def _cast_job(arr, layer, row_start=0, n_rows=None):
    return arr, layer, row_start, arr.shape[1] if n_rows is None else n_rows


def _ffn(x, mod, rows_per_mod, ng, wg, wu, wd, mod_off, final_g=None, cast_jobs=(), tm=1024):
    t = x.shape[0]
    n_steps = t // tm
    tiles_per_mod = rows_per_mod // tm
    final = final_g is not None
    in_specs = [pl.BlockSpec((tm, D_MODEL), lambda i: (i, 0)),
                pl.BlockSpec((None, N_MOD, D_MODEL), lambda i: (i // tiles_per_mod, 0, 0)),
                _resident((1, D_MODEL), (0, 0)),
                _resident((D_MODEL, FFN_DIM), (0, 0)),
                _resident((D_MODEL, FFN_DIM), (0, 0)),
                _resident((FFN_DIM, D_MODEL), (0, 0))]
    args = [x, mod, ng, wg, wu, wd]
    if final:
        in_specs.append(_resident((1, D_MODEL), (0, 0)))
        args.append(final_g)
    out_shape = [jax.ShapeDtypeStruct((t, D_MODEL), F32)]
    out_specs = [pl.BlockSpec((tm, D_MODEL), lambda i: (i, 0))]
    for arr, layer, row_start, rows in cast_jobs:
        width = arr.shape[2]
        blk = rows // n_steps
        assert blk * n_steps == rows and blk % 16 == 0 and row_start % 16 == 0
        if row_start % blk == 0:
            spec = pl.BlockSpec((None, blk, width), lambda i, l=layer, b0=row_start // blk: (l, b0 + i, 0))
        else:
            g = math.gcd(row_start, blk)
            spec = pl.BlockSpec((pl.Element(1), pl.Element(blk), pl.Element(width)),
                                lambda i, l=layer, r0=row_start // g, n=blk // g, g=g: (l, (r0 + n * i) * g, 0))
        in_specs.append(spec)
        args.append(arr)
        out_shape.append(jax.ShapeDtypeStruct((rows, width), BF16))
        out_specs.append(pl.BlockSpec((blk, width), lambda i: (i, 0)))
    outs = pl.pallas_call(
        functools.partial(_ffn_body, mod_off=mod_off, final=final, n_jobs=len(cast_jobs)),
        out_shape=out_shape,
        grid=(n_steps,),
        in_specs=in_specs,
        out_specs=out_specs,
        scratch_shapes=[pltpu.VMEM((tm, FFN_DIM), BF16)],
        compiler_params=_params(1),
        name="ffn",
    )(*args)
    return outs[0], list(outs[1:])


def _proj_body(x_ref, mod_ref, ng_ref, wat_ref, wbt_ref, wt_ref, cw_ref, cb_ref, qn_ref, kvn_ref,
               wq_ref, wkat_ref, wkbt_ref, wuvt_ref, *rest, seq_len, latent, n_alias, own_slot):
    rest = rest[n_alias:]

    def put(ref, b, value):
        for k in range(ref.shape[1]):
            ref[b, k] = value if k == own_slot else jnp.zeros_like(value)

    if latent:
        (qc_ref, qs_ref, kct_ref, kst_ref,
         yc_ref, qna_ref, knat_ref, vnat_ref, qm_ref, kmt_ref, vmt_ref) = rest
    else:
        (yc_ref, qna_ref, knat_ref, vnat_ref, qm_ref, kmt_ref, vmt_ref, ckv_ref, krt_ref) = rest
    tm = x_ref.shape[0]
    x = x_ref[...]
    mod = mod_ref[...]
    h = (_rms(x, ng_ref[...]) * (1 + mod[4:5]) + mod[3:4]).astype(BF16)

    u = _dot_nt(h, wat_ref[0:3 * CONV_DIM, :])
    v = u[:, CONV_DIM:2 * CONV_DIM] * u[:, 2 * CONV_DIM:3 * CONV_DIM]
    pos = lax.broadcasted_iota(jnp.int32, (tm, 1), 0) % seq_len
    v_prev = jnp.where(pos == 0, 0.0, pltpu.roll(v, 1, 0))
    v_next = jnp.where(pos == seq_len - 1, 0.0, pltpu.roll(v, tm - 1, 0))
    cw = cw_ref[...]
    y = cb_ref[...] + v_prev * cw[0:1]
    y = y + v * cw[1:2]
    y = y + v_next * cw[2:3]
    yc_ref[...] = (u[:, 0:CONV_DIM] * y).astype(BF16)

    qna_ref[...] = (_dot_nt(h, wat_ref[3 * CONV_DIM:3 * CONV_DIM + HD, :]) * NA_SCALE).astype(BF16)

    ut = _dot_nt(wt_ref[...], h)
    krt = ut[2 * HD:2 * HD + MLA_ROPE]

    u = _dot_nt(h, wbt_ref[...])
    cq = _rms(u[:, 0:Q_LORA], qn_ref[...]).astype(BF16)
    ckv = _rms(u[:, Q_LORA:Q_LORA + KV_LORA], kvn_ref[...])
    ckv_b = ckv.astype(BF16)
    q2 = _dot(cq, wq_ref[...])
    if latent:
        knat_ref[...] = ut[0:HD].astype(BF16)
        vnat_ref[...] = ut[HD:2 * HD].astype(BF16)
        krt = krt * kct_ref[...] + ut[2 * HD + MLA_ROPE:2 * HD + 2 * MLA_ROPE] * kst_ref[...]
        qc = qc_ref[...]
        qs = qs_ref[...]
        for hh in range(HEADS):
            cols = slice(hh * MLA_QK_PAD, (hh + 1) * MLA_QK_PAD)
            sw_cols = slice(MLA_QK_W + hh * MLA_QK_PAD, MLA_QK_W + (hh + 1) * MLA_QK_PAD)
            qm_ref[:, cols] = ((q2[:, cols] * qc + q2[:, sw_cols] * qs) * MLA_SCALE).astype(BF16)
    else:
        qm_ref[...] = (q2 * MLA_SCALE).astype(BF16)
    kmt = _dot_nt(wkat_ref[...], ckv_b) + _dot(wkbt_ref[...], krt.astype(BF16))
    vmt = _dot_nt(wuvt_ref[...], ckv_b)
    if latent:
        kmt_ref[...] = kmt.astype(BF16)
        vmt_ref[...] = vmt.astype(BF16)
    else:
        for b in range(tm // seq_len):
            rows = slice(b * seq_len, (b + 1) * seq_len)
            put(knat_ref, b, ut[0:HD, rows].reshape(HEADS, NA_HD, seq_len))
            put(vnat_ref, b, ut[HD:2 * HD, rows].reshape(HEADS, NA_HD, seq_len))
            put(ckv_ref, b, ckv[rows])
            put(krt_ref, b, krt[:, rows])
            kmt_ref[b] = kmt[:, rows].astype(BF16)
            vmt_ref[b] = vmt[:, rows].astype(BF16)


def _proj(x, mod, rows_per_mod, w, layer, seq_len, latent, tables, caches, tm):
    t = x.shape[0]
    n_seq = t // seq_len
    seq_per_tile = tm // seq_len
    tiles_per_mod = rows_per_mod // tm
    wq_cols = w["wq"].shape[-1]
    wt_rows = w["wt"].shape[-2]
    row = lambda n: pl.BlockSpec((tm, n), lambda i: (i, 0))
    in_specs = [row(D_MODEL),
                pl.BlockSpec((None, N_MOD, D_MODEL), lambda i: (i // tiles_per_mod, 0, 0)),
                _resident((None, 1, D_MODEL), (layer, 0, 0)),
                _resident((3 * CONV_DIM + HD, D_MODEL), (0, 0)),
                _resident((None, Q_LORA + KV_LORA, D_MODEL), (layer, 0, 0)),
                _resident((None, wt_rows, D_MODEL), (layer, 0, 0)),
                _resident((None, CONV_K, CONV_DIM), (layer, 0, 0)),
                _resident((None, 1, CONV_DIM), (layer, 0, 0)),
                _resident((None, 1, Q_LORA), (layer, 0, 0)),
                _resident((None, 1, KV_LORA), (layer, 0, 0)),
                _resident((None, Q_LORA, wq_cols), (layer, 0, 0)),
                _resident((None, MLA_QK_W, KV_LORA), (layer, 0, 0)),
                _resident((None, MLA_QK_W, MLA_ROPE), (layer, 0, 0)),
                _resident((None, HD, KV_LORA), (layer, 0, 0))]
    args = [x, mod, w["ng1"], w["wat"], w["wbt"], w["wt"], w["conv_w"], w["conv_b"], w["qnorm"], w["kvnorm"],
            w["wq"], w["wkat"], w["wkbt"], w["wuvt"]]
    out_shape = [jax.ShapeDtypeStruct((t, CONV_DIM), BF16),
                 jax.ShapeDtypeStruct((t, HD), BF16)]
    out_specs = [row(CONV_DIM), row(HD)]
    aliases = {}
    n_alias = 0
    own_slot = 0
    if latent:
        assert tm == seq_len
        in_specs += [_resident((seq_len, MLA_QK_PAD), (0, 0)), _resident((seq_len, MLA_QK_PAD), (0, 0)),
                     _resident((MLA_ROPE, seq_len), (0, 0)), _resident((MLA_ROPE, seq_len), (0, 0))]
        args += list(tables)
        seq_blk = lambda n: pl.BlockSpec((None, n, seq_len), lambda i: (i, 0, 0))
        out_shape += [jax.ShapeDtypeStruct((n_seq, HD, seq_len), BF16),
                      jax.ShapeDtypeStruct((n_seq, HD, seq_len), BF16),
                      jax.ShapeDtypeStruct((t, MLA_QK_W), BF16),
                      jax.ShapeDtypeStruct((n_seq, MLA_QK_W, seq_len), BF16),
                      jax.ShapeDtypeStruct((n_seq, HD, seq_len), BF16)]
        out_specs += [seq_blk(HD), seq_blk(HD), row(MLA_QK_W), seq_blk(MLA_QK_W), seq_blk(HD)]
    else:
        if caches is not None:
            n_alias = len(caches)
            in_specs += [pl.BlockSpec(memory_space=pl.ANY)] * n_alias
            args += list(caches)
            aliases = {len(args) - n_alias + k: 2 + (0, 1, 5, 6)[k] for k in range(n_alias)}
            n_slots, first_slot = 1, layer
        else:
            assert layer == 0
            n_slots, first_slot, own_slot = DEPTH, 0, layer
        cache_blk = lambda *dims: pl.BlockSpec((seq_per_tile, n_slots) + dims,
                                               lambda i: (i, first_slot) + (0,) * len(dims))
        seq_blk = lambda n: pl.BlockSpec((seq_per_tile, n, seq_len), lambda i: (i, 0, 0))
        out_shape += [jax.ShapeDtypeStruct((n_seq, DEPTH, HEADS, NA_HD, seq_len), F32),
                      jax.ShapeDtypeStruct((n_seq, DEPTH, HEADS, NA_HD, seq_len), F32),
                      jax.ShapeDtypeStruct((t, MLA_QK_W), BF16),
                      jax.ShapeDtypeStruct((n_seq, MLA_QK_W, seq_len), BF16),
                      jax.ShapeDtypeStruct((n_seq, HD, seq_len), BF16),
                      jax.ShapeDtypeStruct((n_seq, DEPTH, seq_len, KV_LORA), F32),
                      jax.ShapeDtypeStruct((n_seq, DEPTH, MLA_ROPE, seq_len), F32)]
        out_specs += [cache_blk(HEADS, NA_HD, seq_len), cache_blk(HEADS, NA_HD, seq_len), row(MLA_QK_W),
                      seq_blk(MLA_QK_W), seq_blk(HD), cache_blk(seq_len, KV_LORA), cache_blk(MLA_ROPE, seq_len)]
    return pl.pallas_call(
        functools.partial(_proj_body, seq_len=seq_len, latent=latent, n_alias=n_alias, own_slot=own_slot),
        out_shape=out_shape,
        grid=(t // tm,),
        in_specs=in_specs,
        out_specs=out_specs,
        input_output_aliases=aliases,
        compiler_params=_params(1),
        name="mixer_proj",
    )(*args)


def _ctxkv_body(ckv_ref, krt_ref, wkat_ref, wkbt_ref, wuvt_ref, k_ref, v_ref):
    ckv = ckv_ref[...].astype(BF16)
    krt = krt_ref[...].astype(BF16)
    k_ref[...] = (_dot_nt(wkat_ref[...], ckv) + _dot(wkbt_ref[...], krt)).astype(BF16)
    v_ref[...] = _dot_nt(wuvt_ref[...], ckv).astype(BF16)


def _ctx_kv(cache_ckv, cache_krt, wkat, wkbt, wuvt):
    return pl.pallas_call(
        _ctxkv_body,
        out_shape=[jax.ShapeDtypeStruct((DEPTH, DEC_BATCH, MLA_QK_W, PAST_LEN), BF16),
                   jax.ShapeDtypeStruct((DEPTH, DEC_BATCH, HD, PAST_LEN), BF16)],
        grid=(DEPTH, DEC_BATCH),
        in_specs=[pl.BlockSpec((None, None, PAST_LEN, KV_LORA), lambda l, b: (b, l, 0, 0)),
                  pl.BlockSpec((None, None, MLA_ROPE, PAST_LEN), lambda l, b: (b, l, 0, 0)),
                  pl.BlockSpec((None, MLA_QK_W, KV_LORA), lambda l, b: (l, 0, 0)),
                  pl.BlockSpec((None, MLA_QK_W, MLA_ROPE), lambda l, b: (l, 0, 0)),
                  pl.BlockSpec((None, HD, KV_LORA), lambda l, b: (l, 0, 0))],
        out_specs=[pl.BlockSpec((None, None, MLA_QK_W, PAST_LEN), lambda l, b: (l, b, 0, 0)),
                   pl.BlockSpec((None, None, HD, PAST_LEN), lambda l, b: (l, b, 0, 0))],
        compiler_params=_params(2),
        name="ctx_kv",
    )(cache_ckv, cache_krt, wkat, wkbt, wuvt)


def _softmax_pv(s, vt):
    m = jnp.max(s, axis=-1, keepdims=True)
    p = jnp.exp2(s - m)
    den = jnp.sum(p, axis=-1, keepdims=True)
    return _dot_nt(p.astype(BF16), vt) / den


def _pair_slot(x, j):
    z = jnp.zeros_like(x)
    return jnp.concatenate([x, z] if j == 0 else [z, x], axis=0)


def _attn_ctx_body(qna_ref, knat_ref, vnat_ref, qm_ref, kmt_ref, vmt_ref, ona_ref, om_ref, *, seq_len):
    for b in range(qna_ref.shape[0] // seq_len):
        rows = slice(b * seq_len, (b + 1) * seq_len)
        for hp in range(HEADS // 2):
            pair = slice(hp * 2 * HEAD_V, (hp + 1) * 2 * HEAD_V)
            q = qna_ref[rows, pair]
            o_na = o_m = None
            for j in range(2):
                hh = 2 * hp + j
                kt = _pair_slot(knat_ref[b, hh].astype(BF16), j)
                vt = _pair_slot(vnat_ref[b, hh].astype(BF16), j)
                o = _softmax_pv(_dot(q, kt), vt)
                o_na = o if j == 0 else o_na + o
                qk = slice(hh * MLA_QK_PAD, (hh + 1) * MLA_QK_PAD)
                vt = _pair_slot(vmt_ref[b, hh * HEAD_V:(hh + 1) * HEAD_V, :], j)
                o = _softmax_pv(_dot(qm_ref[rows, qk], kmt_ref[b, qk, :]), vt)
                o_m = o if j == 0 else o_m + o
            ona_ref[rows, pair] = o_na.astype(BF16)
            om_ref[rows, pair] = o_m.astype(BF16)


def _attn_ctx(qna, knat, vnat, qm, kmt, vmt, layer, seq_len, tm=512):
    t = qna.shape[0]
    nb = tm // seq_len
    row = lambda n: pl.BlockSpec((tm, n), lambda i: (i, 0))
    cache_blk = pl.BlockSpec((nb, None, HEADS, NA_HD, seq_len), lambda i: (i, layer, 0, 0, 0))
    seq_blk = lambda n: pl.BlockSpec((nb, n, seq_len), lambda i: (i, 0, 0))
    return pl.pallas_call(
        functools.partial(_attn_ctx_body, seq_len=seq_len),
        out_shape=[jax.ShapeDtypeStruct((t, HD), BF16), jax.ShapeDtypeStruct((t, HD), BF16)],
        grid=(t // tm,),
        in_specs=[row(HD), cache_blk, cache_blk, row(MLA_QK_W), seq_blk(MLA_QK_W), seq_blk(HD)],
        out_specs=[row(HD), row(HD)],
        compiler_params=_params(1),
        name="attn_ctx",
    )(qna, knat, vnat, qm, kmt, vmt)


def _attn_lat_body(qna_ref, knat_ref, vnat_ref, kctx_ref, vctx_ref, bias_ref,
                   qm_ref, kmt_ref, vmt_ref, kmctx_ref, vmctx_ref, ona_ref, om_ref):
    cat = lambda *a: jnp.concatenate(a, axis=1)
    head = lambda j: slice(j * HEAD_V, (j + 1) * HEAD_V)
    kc = [_pair_slot(kctx_ref[j].astype(BF16), j) for j in range(2)]
    vc = [_pair_slot(vctx_ref[j].astype(BF16), j) for j in range(2)]
    for c, (start, count) in enumerate(NA_WINDOWS):
        rows = slice(c * Q_CHUNK, (c + 1) * Q_CHUNK)
        keys = slice(start, start + count)
        q = qna_ref[rows, :]
        for j in range(2):
            bias = jnp.concatenate(
                [cat(*[bias_ref[j, p] for p in NA_BLOCK_INDEX[c][rl]]) for rl in range(Q_CHUNK // GRID_W)], axis=0)
            s = cat(_dot(q, kc[j]), _dot(q, _pair_slot(knat_ref[head(j), keys], j)) + bias)
            o = _softmax_pv(s, cat(vc[j], _pair_slot(vnat_ref[head(j), keys], j)))
            o_na = o if j == 0 else o_na + o
        ona_ref[rows, :] = o_na.astype(BF16)
    kt, vt = [], []
    for j in range(2):
        qk = slice(j * MLA_QK_PAD, (j + 1) * MLA_QK_PAD)
        kt.append(cat(kmctx_ref[qk, :], kmt_ref[qk, :]))
        vt.append(_pair_slot(cat(vmctx_ref[head(j), :], vmt_ref[head(j), :]), j))
    for c in range(DEC_SEQ // MLA_Q_ROWS):
        rows = slice(c * MLA_Q_ROWS, (c + 1) * MLA_Q_ROWS)
        for j in range(2):
            o = _softmax_pv(_dot(qm_ref[rows, j * MLA_QK_PAD:(j + 1) * MLA_QK_PAD], kt[j]), vt[j])
            o_m = o if j == 0 else o_m + o
        om_ref[rows, :] = o_m.astype(BF16)


def _attn_lat(qna, knat, vnat, kctx, vctx, bias, qm, kmt, vmt, kmctx, vmctx, layer):
    t = qna.shape[0]
    s = DEC_SEQ
    return pl.pallas_call(
        _attn_lat_body,
        out_shape=[jax.ShapeDtypeStruct((t, HD), BF16), jax.ShapeDtypeStruct((t, HD), BF16)],
        grid=(HEADS // 2, t // s),
        in_specs=[pl.BlockSpec((s, 2 * NA_HD), lambda hp, b: (b, hp)),
                  pl.BlockSpec((None, 2 * NA_HD, s), lambda hp, b: (b, hp, 0)),
                  pl.BlockSpec((None, 2 * HEAD_V, s), lambda hp, b: (b, hp, 0)),
                  pl.BlockSpec((None, None, 2, NA_HD, PAST_LEN), lambda hp, b: (b, layer, hp, 0, 0)),
                  pl.BlockSpec((None, None, 2, NA_HD, PAST_LEN), lambda hp, b: (b, layer, hp, 0, 0)),
                  pl.BlockSpec((None, 2, len(NA_BLOCK_PAIRS), GRID_W, 2 * GRID_W),
                               lambda hp, b: (layer, hp, 0, 0, 0)),
                  pl.BlockSpec((s, 2 * MLA_QK_PAD), lambda hp, b: (b, hp)),
                  pl.BlockSpec((None, 2 * MLA_QK_PAD, s), lambda hp, b: (b, hp, 0)),
                  pl.BlockSpec((None, 2 * HEAD_V, s), lambda hp, b: (b, hp, 0)),
                  pl.BlockSpec((None, None, 2 * MLA_QK_PAD, PAST_LEN), lambda hp, b: (layer, b, hp, 0)),
                  pl.BlockSpec((None, None, 2 * HEAD_V, PAST_LEN), lambda hp, b: (layer, b, hp, 0))],
        out_specs=[pl.BlockSpec((s, 2 * HEAD_V), lambda hp, b: (b, hp)),
                   pl.BlockSpec((s, 2 * HEAD_V), lambda hp, b: (b, hp))],
        compiler_params=_params(2),
        name="attn_lat",
    )(qna, knat, vnat, kctx, vctx, bias, qm, kmt, vmt, kmctx, vmctx)


def _mix_body(x_ref, yc_ref, ona_ref, om_ref, mod_ref, ng_ref, wgt_ref, wco_ref, wno_ref, wmo_ref,
              wo_ref, o_ref):
    x = x_ref[...]
    mod = mod_ref[...]
    h = (_rms(x, ng_ref[...]) * (1 + mod[4:5]) + mod[3:4]).astype(BF16)
    gate = lambda k: jax.nn.sigmoid(_dot_nt(h, wgt_ref[k * D_MODEL:(k + 1) * D_MODEL, :]))
    z = gate(0) * _dot(yc_ref[...], wco_ref[...])
    z = z + gate(1) * _dot(ona_ref[...], wno_ref[...])
    z = z + gate(2) * _dot(om_ref[...], wmo_ref[...])
    o_ref[...] = x + mod[5:6] * _dot(z.astype(BF16), wo_ref[...])


def _mix(x, yc, ona, om, mod, rows_per_mod, w, layer, tm=1024):
    t = x.shape[0]
    tiles_per_mod = rows_per_mod // tm
    row = lambda n: pl.BlockSpec((tm, n), lambda i: (i, 0))
    return pl.pallas_call(
        _mix_body,
        out_shape=jax.ShapeDtypeStruct((t, D_MODEL), F32),
        grid=(t // tm,),
        in_specs=[row(D_MODEL), row(CONV_DIM), row(HD), row(HD),
                  pl.BlockSpec((None, N_MOD, D_MODEL), lambda i: (i // tiles_per_mod, 0, 0)),
                  _resident((None, 1, D_MODEL), (layer, 0, 0)),
                  _resident((3 * D_MODEL, D_MODEL), (0, 0)),
                  _resident((CONV_DIM, D_MODEL), (0, 0)),
                  _resident((HD, D_MODEL), (0, 0)),
                  _resident((HD, D_MODEL), (0, 0)),
                  _resident((D_MODEL, D_MODEL), (0, 0))],
        out_specs=row(D_MODEL),
        compiler_params=_params(1),
        name="mixer_out",
    )(x, yc, ona, om, mod, w["ng1"], w["wgt"], w["wco"], w["wno"], w["wmo"], w["wo"])


def _rope_tables():
    f32 = np.float32
    half = MLA_ROPE // 2
    nf = half // 2
    inv = (f32(1.0) / (f32(ROPE_BASE) ** (np.arange(nf, dtype=f32) / f32(nf)))).astype(f32)
    t = np.arange(DEC_SEQ)
    rows = (t // GRID_W).astype(f32)[:, None] * inv[None, :]
    cols = (t % GRID_W).astype(f32)[:, None] * inv[None, :]
    cos = np.concatenate([np.cos(rows), np.cos(rows), np.cos(cols), np.cos(cols)], axis=-1).astype(f32)
    sin = np.concatenate([np.sin(rows), np.sin(rows), np.sin(cols), np.sin(cols)], axis=-1).astype(f32)
    pad = MLA_QK_PAD - MLA_NOPE - MLA_ROPE
    q_cos = np.concatenate([np.ones((DEC_SEQ, MLA_NOPE), f32), cos, np.zeros((DEC_SEQ, pad), f32)], axis=-1)
    q_sin = np.concatenate([np.zeros((DEC_SEQ, MLA_NOPE), f32), sin, np.zeros((DEC_SEQ, pad), f32)], axis=-1)
    return tuple(jnp.asarray(a) for a in (q_cos, q_sin, np.ascontiguousarray(cos.T), np.ascontiguousarray(sin.T)))


def _rope_swap(w):
    nf = MLA_ROPE // 4
    a, b, c, d = (w[..., i * nf:(i + 1) * nf] for i in range(4))
    return jnp.concatenate([-b, a, -d, c], axis=-1)


def _na_bias(rpb):
    n_dc = 2 * NA_WIN_C - 1
    col = np.arange(GRID_W)
    c_start = np.clip(col - NA_WIN_C // 2, 0, GRID_W - NA_WIN_C)
    c_in = (col[None, :] >= c_start[:, None]) & (col[None, :] < c_start[:, None] + NA_WIN_C)
    dc = np.clip(col[None, :] - col[:, None] + (NA_WIN_C - 1), 0, n_dc - 1)
    pick_dc = (dc[None] == np.arange(n_dc)[:, None, None]).astype(np.float32)
    n_pairs = len(NA_BLOCK_PAIRS)
    pick_dr = np.zeros((n_pairs, 2, NA_DR_MASKED), np.float32)
    for p, pair in enumerate(NA_BLOCK_PAIRS):
        for side, d in enumerate(pair):
            if d != NA_DR_MASKED:
                pick_dr[p, side, d] = 1.0
    keep = pick_dr.sum(-1).astype(bool)[:, None, :, None] & c_in[None, :, None, :]
    keep = keep.reshape(n_pairs, GRID_W, 2 * GRID_W)
    pick_side_dc = np.zeros((2, n_dc, GRID_W, 2, GRID_W), np.float32)
    for side in range(2):
        pick_side_dc[side, :, :, side, :] = pick_dc
    pick_side_dc = pick_side_dc.reshape(2, n_dc, GRID_W, 2 * GRID_W)
    hi = lax.Precision.HIGHEST
    by_row = jnp.einsum("psd,lhdj->lhpsj", jnp.asarray(pick_dr), rpb, precision=hi)
    blocks = jnp.einsum("lhpsj,sjqn->lhpqn", by_row, jnp.asarray(pick_side_dc), precision=hi)
    return jnp.where(jnp.asarray(keep), blocks * LOG2_E, NEG_INF)


def _pack_weights(w_int, w_uq, w_ukv):
    t_last = lambda a: jnp.swapaxes(a, -1, -2)
    w_mid = lax.optimization_barrier(w_int[:, W_IN_KV:W_IN_GATE]).astype(BF16)
    w_kvt = w_mid[:, :W_IN_LORA - W_IN_KV]
    w_krt = w_mid[:, W_IN_KR - W_IN_KV:]
    wt = jnp.concatenate([w_kvt, w_krt, t_last(_rope_swap(t_last(w_krt)))], axis=1)
    wbt = w_mid[:, W_IN_LORA - W_IN_KV:W_IN_KR - W_IN_KV]
    uq = w_uq.reshape(DEPTH, Q_LORA, MLA_HEADS, MLA_NOPE + MLA_ROPE)
    pad = MLA_QK_PAD - MLA_NOPE - MLA_ROPE
    zp = jnp.zeros(uq.shape[:-1] + (pad,), F32)
    zn = jnp.zeros(uq.shape[:-1] + (MLA_NOPE,), F32)
    q_ext = jnp.concatenate([uq, zp], axis=-1).reshape(DEPTH, Q_LORA, MLA_QK_W)
    q_sw = jnp.concatenate([zn, _rope_swap(uq[..., MLA_NOPE:]), zp], axis=-1).reshape(q_ext.shape)
    ukv = w_ukv.reshape(DEPTH, KV_LORA, MLA_HEADS, MLA_NOPE + MLA_V)
    zk = jnp.zeros(ukv.shape[:-1] + (MLA_QK_PAD - MLA_NOPE,), F32)
    wka = jnp.concatenate([ukv[..., :MLA_NOPE], zk], axis=-1).reshape(DEPTH, KV_LORA, MLA_QK_W)
    eye = jnp.concatenate([jnp.zeros((MLA_ROPE, MLA_NOPE), F32), jnp.eye(MLA_ROPE, dtype=F32),
                           jnp.zeros((MLA_ROPE, pad), F32)], axis=-1)
    wkb = jnp.broadcast_to(jnp.tile(eye, (1, MLA_HEADS))[None], (DEPTH, MLA_ROPE, MLA_QK_W))
    wuv = ukv[..., MLA_NOPE:].reshape(DEPTH, KV_LORA, HD)
    b = lambda a: a.astype(BF16)
    return dict(wt=b(wt), wbt=b(wbt),
                wq_lat=b(jnp.concatenate([q_ext, q_sw], axis=-1)), wq_ctx=b(q_ext),
                wkat=b(t_last(wka)), wkbt=b(t_last(wkb)), wuvt=b(t_last(wuv)))


def kernel(x_prompt, x_sample, cache_na_k, cache_na_v, cache_mla_ckv, cache_mla_krope, c, c_ctx,
           w_ada, b_ada, norm_g, w_ffn1_gate, w_ffn1_up, w_ffn1_down, w_ffn2_gate, w_ffn2_up, w_ffn2_down,
           w_in, conv_w, conv_b, na_rpb, mla_qnorm, w_uq, mla_kvnorm, w_ukv,
           w_conv_out, w_na_out, w_mla_out, w_o, final_g):
    b16 = lambda a: a.astype(BF16)
    t_last = lambda a: jnp.swapaxes(a, -1, -2)
    w_int = t_last(w_in)
    packed = _pack_weights(w_int, w_uq, w_ukv)
    shared = dict(conv_w=conv_w, conv_b=conv_b.reshape(DEPTH, 1, CONV_DIM),
                  qnorm=mla_qnorm.reshape(DEPTH, 1, Q_LORA), kvnorm=mla_kvnorm.reshape(DEPTH, 1, KV_LORA),
                  ng1=norm_g[:, 1:2],
                  **{k: packed[k] for k in ("wt", "wbt", "wkat", "wkbt", "wuvt")})
    ffn1_f32 = (w_ffn1_gate, w_ffn1_up, w_ffn1_down)
    ffn2_f32 = (w_ffn2_gate, w_ffn2_up, w_ffn2_down)
    ffn1_w = {0: tuple(b16(w[0]) for w in ffn1_f32)}
    ffn2_w = {}
    mixer_w = {}
    mixer_srcs = ((w_int, 0, W_IN_KV), (w_int, W_IN_GATE, 3 * D_MODEL),
                  (w_conv_out, 0, None), (w_na_out, 0, None), (w_mla_out, 0, None), (w_o, 0, None))
    final_row = final_g.reshape(1, D_MODEL)

    c_all = jnp.concatenate([c_ctx[None], c, jnp.zeros((MOD_ROWS - 1 - DEC_BATCH, D_MODEL), F32)], axis=0)
    mod = _modulation(c_all, w_ada, b_ada).reshape(DEPTH, MOD_ROWS, N_MOD, D_MODEL)

    tables = _rope_tables()
    na_bias = _na_bias(na_rpb)
    ctx_k_na = t_last(cache_na_k)
    ctx_v_na = t_last(cache_na_v)
    ctx_k_mla, ctx_v_mla = _ctx_kv(cache_mla_ckv, t_last(cache_mla_krope),
                                   packed["wkat"], packed["wkbt"], packed["wuvt"])

    xp = x_prompt.reshape(BATCH * SEQ, D_MODEL)
    xs = x_sample.reshape(DEC_BATCH * DEC_SEQ, D_MODEL)
    n_p = BATCH * SEQ
    caches = None
    for l in range(DEPTH):
        mod_p = mod[l, 0:1]
        mod_s = mod[l, 1:1 + DEC_BATCH]
        last = l == DEPTH - 1
        jobs = [_cast_job(w, l) for w in ffn2_f32]
        if l == 0:
            jobs += [_cast_job(a, ll, r0, n) for ll in range(DEPTH) for a, r0, n in mixer_srcs]
        xp, cast = _ffn(xp, mod_p, n_p, norm_g[l, 0:1], *ffn1_w[l], 0, cast_jobs=jobs, tm=512)
        ffn2_w[l] = tuple(cast[:3])
        for ll in range(DEPTH if l == 0 else 0):
            names = ("wat", "wgt", "wco", "wno", "wmo", "wo")
            mixer_w[ll] = dict(zip(names, cast[3 + len(names) * ll:3 + len(names) * (ll + 1)]))
        w_ctx = dict(shared, wq=packed["wq_ctx"], **mixer_w[l])
        w_lat = dict(shared, wq=packed["wq_lat"], **mixer_w[l])
        yc, qna, knat, vnat, qm, kmt, vmt, ckv, krt = _proj(xp, mod_p, n_p, w_ctx, l, SEQ, False, None, caches,
                                                            512 if caches is None else 1024)
        caches = (knat, vnat, ckv, krt)
        ona, om = _attn_ctx(qna, knat, vnat, qm, kmt, vmt, l, SEQ)
        xp = _mix(xp, yc, ona, om, mod_p, n_p, w_ctx, l)
        jobs = [] if last else [_cast_job(w, l + 1) for w in ffn1_f32]
        xp, cast = _ffn(xp, mod_p, n_p, norm_g[l, 2:3], *ffn2_w[l], 6, final_row if last else None,
                        cast_jobs=jobs, tm=512 if jobs else 1024)
        if jobs:
            ffn1_w[l + 1] = tuple(cast)
        xs, _ = _ffn(xs, mod_s, DEC_SEQ, norm_g[l, 0:1], *ffn1_w[l], 0)
        yc, qna, knat, vnat, qm, kmt, vmt = _proj(xs, mod_s, DEC_SEQ, w_lat, l, DEC_SEQ, True, tables, None, DEC_SEQ)
        ona, om = _attn_lat(qna, knat, vnat, ctx_k_na, ctx_v_na, na_bias, qm, kmt, vmt, ctx_k_mla, ctx_v_mla, l)
        xs = _mix(xs, yc, ona, om, mod_s, DEC_SEQ, w_lat, l)
        xs, _ = _ffn(xs, mod_s, DEC_SEQ, norm_g[l, 2:3], *ffn2_w[l], 6, final_row if last else None)
    new_kt, new_vt, new_ckv, new_krt = caches
    return (xp.reshape(BATCH, SEQ, D_MODEL), xs.reshape(DEC_BATCH, DEC_SEQ, D_MODEL),
            t_last(new_kt), t_last(new_vt), new_ckv, t_last(new_krt))
```

```python
import functools
import math

import jax
import jax.numpy as jnp
import numpy as np
from jax import lax
from jax.experimental import pallas as pl
from jax.experimental.pallas import tpu as pltpu

D_MODEL = 1024
BATCH = 32
SEQ = 256
DEPTH = 2
DEC_BATCH = 8
DEC_SEQ = 1024
PAST_LEN = 256
GRID_W = 64
CONV_DIM = 512
CONV_K = 3
NA_HEADS = 8
NA_HD = 64
NA_WIN_R = 8
NA_WIN_C = 16
MLA_HEADS = 8
MLA_NOPE = 64
MLA_ROPE = 32
MLA_V = 64
Q_LORA = 256
KV_LORA = 128
FFN_DIM = 2816
N_MOD = 9
ROPE_BASE = 10000.0
EPS = 1e-6
NEG_INF = -1e30
LOG2_E = 1.4426950408889634
MLA_SCALE = (MLA_NOPE + MLA_ROPE) ** -0.5 * LOG2_E
NA_SCALE = NA_HD ** -0.5 * LOG2_E

N_CTX_ROWS = BATCH * SEQ
N_ROWS = N_CTX_ROWS + DEC_BATCH * DEC_SEQ
HEADS = 8
HEAD_V = 64
HD = HEADS * NA_HD
W_IN_KV = 3 * CONV_DIM + HD
W_IN_LORA = W_IN_KV + 2 * HD
W_IN_KR = W_IN_LORA + Q_LORA + KV_LORA
W_IN_GATE = W_IN_KR + MLA_ROPE
MLA_QK_PAD = 128
MLA_QK_W = HEADS * MLA_QK_PAD
FFN_CHUNK = 256
Q_CHUNK = 256
MLA_Q_ROWS = 128
MOD_ROWS = 16
VMEM_LIMIT = 56 * 1024 * 1024
NA_WINDOWS = ((0, 512), (0, 768), (256, 768), (512, 512))
NA_DR_MASKED = 2 * NA_WIN_R - 1


def _na_block_pairs():
    rows = DEC_SEQ // GRID_W
    r_start = np.clip(np.arange(rows) - NA_WIN_R // 2, 0, rows - NA_WIN_R)
    pairs, index = [], []
    for c, (start, count) in enumerate(NA_WINDOWS):
        index.append([])
        for rl in range(Q_CHUNK // GRID_W):
            r = c * (Q_CHUNK // GRID_W) + rl
            assert start // GRID_W <= r_start[r] and r_start[r] + NA_WIN_R <= (start + count) // GRID_W
            index[c].append([])
            for kp in range(count // (2 * GRID_W)):
                pair = []
                for rk in (start // GRID_W + 2 * kp, start // GRID_W + 2 * kp + 1):
                    inside = r_start[r] <= rk < r_start[r] + NA_WIN_R
                    pair.append(int(rk - r + NA_WIN_R - 1) if inside else NA_DR_MASKED)
                pair = tuple(pair)
                if pair not in pairs:
                    pairs.append(pair)
                index[c][rl].append(pairs.index(pair))
    return tuple(pairs), index


NA_BLOCK_PAIRS, NA_BLOCK_INDEX = _na_block_pairs()

BF16 = jnp.bfloat16
F32 = jnp.float32


def _dot(a, b):
    return jnp.dot(a, b, preferred_element_type=F32)


def _dot_nt(a, b):
    return lax.dot_general(a, b, (((1,), (1,)), ((), ())), preferred_element_type=F32)


def _rms(x, g):
    return x * lax.rsqrt(jnp.mean(x * x, axis=-1, keepdims=True) + EPS) * g


def _params(n_axes, flags=None):
    return pltpu.CompilerParams(dimension_semantics=("arbitrary",) * n_axes,
                                vmem_limit_bytes=VMEM_LIMIT, flags=flags)


def _resident(shape, index):
    return pl.BlockSpec(shape, lambda *_: index, pipeline_mode=pl.Buffered(1))


def _mod_body(c_ref, w_ref, b_ref, o_ref):
    c = c_ref[...]
    a = c * jax.nn.sigmoid(c)
    o_ref[...] = _dot(a.astype(BF16), w_ref[...].astype(BF16)) + b_ref[...]


def _modulation(c_all, w_ada, b_ada):
    n_col = N_MOD * D_MODEL
    tn = n_col // 4
    return pl.pallas_call(
        _mod_body,
        out_shape=jax.ShapeDtypeStruct((DEPTH, MOD_ROWS, n_col), F32),
        grid=(DEPTH, n_col // tn),
        in_specs=[pl.BlockSpec((MOD_ROWS, D_MODEL), lambda l, j: (0, 0)),
                  pl.BlockSpec((None, D_MODEL, tn), lambda l, j: (l, 0, j)),
                  pl.BlockSpec((None, 1, tn), lambda l, j: (l, 0, j))],
        out_specs=pl.BlockSpec((None, MOD_ROWS, tn), lambda l, j: (l, 0, j)),
        compiler_params=_params(2),
        name="modulation",
    )(c_all, w_ada, b_ada.reshape(DEPTH, 1, n_col))


def _rows_specs(x, tm, tile0):
    if isinstance(x, tuple):
        assert tile0 == 0
        n_ctx = x[0].shape[0] // tm
        width = x[0].shape[1]
        return [pl.BlockSpec((tm, width), lambda i: (jnp.minimum(i, n_ctx - 1), 0)),
                pl.BlockSpec((tm, width), lambda i: (jnp.maximum(i - n_ctx, 0), 0))], list(x)
    return [pl.BlockSpec((tm, x.shape[1]), lambda i: (i + tile0, 0))], [x]


def _rows_value(refs, tm):
    if len(refs) == 1:
        return refs[0][...]
    return jnp.where(pl.program_id(0) < N_CTX_ROWS // tm, refs[0][...], refs[1][...])


def _mod_spec(tm, tile0):
    def index(i):
        g = (i + tile0) * tm
        return jnp.where(g < N_CTX_ROWS, 0, 1 + (g - N_CTX_ROWS) // DEC_SEQ), 0, 0
    return pl.BlockSpec((None, N_MOD, D_MODEL), index)


def _ffn_body(*refs, mod_off, final, n_jobs, n_x):
    x_refs, (mod_ref, ng_ref, wg_ref, wu_ref, wd_ref), rest = refs[:n_x], refs[n_x:n_x + 5], refs[n_x + 5:]
    if final:
        fg_ref, rest = rest[0], rest[1:]
    job_in, o_ref, job_out, a_scr = rest[:n_jobs], rest[n_jobs], rest[n_jobs + 1:2 * n_jobs + 1], rest[-1]
    x = _rows_value(x_refs, o_ref.shape[0])
    mod = mod_ref[...]
    shift = mod[mod_off:mod_off + 1]
    scale = mod[mod_off + 1:mod_off + 2]
    gate = mod[mod_off + 2:mod_off + 3]
    h = (_rms(x, ng_ref[...]) * (1 + scale) + shift).astype(BF16)
    for f in range(FFN_DIM // FFN_CHUNK):
        cols = slice(f * FFN_CHUNK, (f + 1) * FFN_CHUNK)
        g = _dot(h, wg_ref[:, cols])
        u = _dot(h, wu_ref[:, cols])
        a_scr[:, cols] = (g * jax.nn.sigmoid(g) * u).astype(BF16)
    y = _dot(a_scr[...], wd_ref[...])
    out = x + 0.5 * gate * y
    if final:
        out = _rms(out, fg_ref[...])
    o_ref[...] = out
    for src, dst in zip(job_in, job_out):
        dst[...] = src[...].reshape(dst.shape).astype(BF16)


def _cast_job(arr, layer, row_start=0, n_rows=None):
    return arr, layer, row_start, arr.shape[1] if n_rows is None else n_rows


def _ffn(x, mod, ng, wg, wu, wd, mod_off, row0=0, n_rows=None, final_g=None, cast_jobs=(), tm=1024):
    t = N_ROWS - row0 if n_rows is None else n_rows
    n_steps = t // tm
    final = final_g is not None
    whole = isinstance(x, tuple) or x.shape[0] == N_ROWS
    x_specs, x_args = _rows_specs(x, tm, row0 // tm if whole else 0)
    in_specs = x_specs + [_mod_spec(tm, row0 // tm),
                          _resident((1, D_MODEL), (0, 0)),
                          _resident((D_MODEL, FFN_DIM), (0, 0)),
                          _resident((D_MODEL, FFN_DIM), (0, 0)),
                          _resident((FFN_DIM, D_MODEL), (0, 0))]
    args = x_args + [mod, ng, wg, wu, wd]
    if final:
        in_specs.append(_resident((1, D_MODEL), (0, 0)))
        args.append(final_g)
    out_shape = [jax.ShapeDtypeStruct((t, D_MODEL), F32)]
    out_specs = [pl.BlockSpec((tm, D_MODEL), lambda i: (i, 0))]
    for arr, layer, row_start, rows in cast_jobs:
        width = arr.shape[2]
        blk = rows // n_steps
        assert blk * n_steps == rows and blk % 16 == 0 and row_start % 16 == 0
        if row_start % blk == 0:
            spec = pl.BlockSpec((None, blk, width), lambda i, l=layer, b0=row_start // blk: (l, b0 + i, 0))
        else:
            g = math.gcd(row_start, blk)
            spec = pl.BlockSpec((pl.Element(1), pl.Element(blk), pl.Element(width)),
                                lambda i, l=layer, r0=row_start // g, n=blk // g, g=g: (l, (r0 + n * i) * g, 0))
        in_specs.append(spec)
        args.append(arr)
        out_shape.append(jax.ShapeDtypeStruct((rows, width), BF16))
        out_specs.append(pl.BlockSpec((blk, width), lambda i: (i, 0)))
    outs = pl.pallas_call(
        functools.partial(_ffn_body, mod_off=mod_off, final=final, n_jobs=len(cast_jobs), n_x=len(x_args)),
        out_shape=out_shape,
        grid=(n_steps,),
        in_specs=in_specs,
        out_specs=out_specs,
        scratch_shapes=[pltpu.VMEM((tm, FFN_DIM), BF16)],
        compiler_params=_params(1),
        name="ffn",
    )(*args)
    return outs[0], list(outs[1:])


def _proj_body(x_ref, mod_ref, ng_ref, wat_ref, wbt_ref, wt_ref, cw_ref, cb_ref, qn_ref, kvn_ref,
               wq_ref, wkat_ref, wkbt_ref, wuvt_ref, *rest, seq_len, latent, n_alias, own_slot):
    rest = rest[n_alias:]

    def put(ref, b, value):
        for k in range(ref.shape[1]):
            ref[b, k] = value if k == own_slot else jnp.zeros_like(value)

    if latent:
        (qc_ref, qs_ref, kct_ref, kst_ref,
         yc_ref, qna_ref, knat_ref, vnat_ref, qm_ref, kmt_ref, vmt_ref) = rest
    else:
        (yc_ref, qna_ref, knat_ref, vnat_ref, qm_ref, kmt_ref, vmt_ref, ckv_ref, krt_ref) = rest
    tm = x_ref.shape[0]
    x = x_ref[...]
    mod = mod_ref[...]
    h = (_rms(x, ng_ref[...]) * (1 + mod[4:5]) + mod[3:4]).astype(BF16)

    u = _dot_nt(h, wat_ref[0:3 * CONV_DIM, :])
    v = u[:, CONV_DIM:2 * CONV_DIM] * u[:, 2 * CONV_DIM:3 * CONV_DIM]
    pos = lax.broadcasted_iota(jnp.int32, (tm, 1), 0) % seq_len
    v_prev = jnp.where(pos == 0, 0.0, pltpu.roll(v, 1, 0))
    v_next = jnp.where(pos == seq_len - 1, 0.0, pltpu.roll(v, tm - 1, 0))
    cw = cw_ref[...]
    y = cb_ref[...] + v_prev * cw[0:1]
    y = y + v * cw[1:2]
    y = y + v_next * cw[2:3]
    yc_ref[...] = (u[:, 0:CONV_DIM] * y).astype(BF16)

    qna_ref[...] = (_dot_nt(h, wat_ref[3 * CONV_DIM:3 * CONV_DIM + HD, :]) * NA_SCALE).astype(BF16)

    ut = _dot_nt(wt_ref[...], h)
    krt = ut[2 * HD:2 * HD + MLA_ROPE]

    u = _dot_nt(h, wbt_ref[...])
    cq = _rms(u[:, 0:Q_LORA], qn_ref[...]).astype(BF16)
    ckv = _rms(u[:, Q_LORA:Q_LORA + KV_LORA], kvn_ref[...])
    ckv_b = ckv.astype(BF16)
    q2 = _dot(cq, wq_ref[...])
    if latent:
        knat_ref[...] = ut[0:HD].astype(BF16)
        vnat_ref[...] = ut[HD:2 * HD].astype(BF16)
        krt = krt * kct_ref[...] + ut[2 * HD + MLA_ROPE:2 * HD + 2 * MLA_ROPE] * kst_ref[...]
        qc = qc_ref[...]
        qs = qs_ref[...]
        for hh in range(HEADS):
            cols = slice(hh * MLA_QK_PAD, (hh + 1) * MLA_QK_PAD)
            sw_cols = slice(MLA_QK_W + hh * MLA_QK_PAD, MLA_QK_W + (hh + 1) * MLA_QK_PAD)
            qm_ref[:, cols] = ((q2[:, cols] * qc + q2[:, sw_cols] * qs) * MLA_SCALE).astype(BF16)
    else:
        qm_ref[...] = (q2 * MLA_SCALE).astype(BF16)
    kmt = _dot_nt(wkat_ref[...], ckv_b) + _dot(wkbt_ref[...], krt.astype(BF16))
    vmt = _dot_nt(wuvt_ref[...], ckv_b)
    if latent:
        kmt_ref[...] = kmt.astype(BF16)
        vmt_ref[...] = vmt.astype(BF16)
    else:
        for b in range(tm // seq_len):
            rows = slice(b * seq_len, (b + 1) * seq_len)
            put(knat_ref, b, ut[0:HD, rows].reshape(HEADS, NA_HD, seq_len))
            put(vnat_ref, b, ut[HD:2 * HD, rows].reshape(HEADS, NA_HD, seq_len))
            put(ckv_ref, b, ckv[rows])
            put(krt_ref, b, krt[:, rows])
            kmt_ref[b] = kmt[:, rows].astype(BF16)
            vmt_ref[b] = vmt[:, rows].astype(BF16)


def _proj(x, mod, w, layer, seq_len, latent, tables, caches, tm):
    row0 = N_CTX_ROWS if latent else 0
    t = N_ROWS - N_CTX_ROWS if latent else N_CTX_ROWS
    x_tile0 = row0 // tm if x.shape[0] == N_ROWS else 0
    n_seq = t // seq_len
    seq_per_tile = tm // seq_len
    wq_cols = w["wq"].shape[-1]
    wt_rows = w["wt"].shape[-2]
    row = lambda n: pl.BlockSpec((tm, n), lambda i: (i, 0))
    in_specs = [pl.BlockSpec((tm, D_MODEL), lambda i: (i + x_tile0, 0)),
                _mod_spec(tm, row0 // tm),
                _resident((None, 1, D_MODEL), (layer, 0, 0)),
                _resident((3 * CONV_DIM + HD, D_MODEL), (0, 0)),
                _resident((None, Q_LORA + KV_LORA, D_MODEL), (layer, 0, 0)),
                _resident((None, wt_rows, D_MODEL), (layer, 0, 0)),
                _resident((None, CONV_K, CONV_DIM), (layer, 0, 0)),
                _resident((None, 1, CONV_DIM), (layer, 0, 0)),
                _resident((None, 1, Q_LORA), (layer, 0, 0)),
                _resident((None, 1, KV_LORA), (layer, 0, 0)),
                _resident((None, Q_LORA, wq_cols), (layer, 0, 0)),
                _resident((None, MLA_QK_W, KV_LORA), (layer, 0, 0)),
                _resident((None, MLA_QK_W, MLA_ROPE), (layer, 0, 0)),
                _resident((None, HD, KV_LORA), (layer, 0, 0))]
    args = [x, mod, w["ng1"], w["wat"], w["wbt"], w["wt"], w["conv_w"], w["conv_b"], w["qnorm"], w["kvnorm"],
            w["wq"], w["wkat"], w["wkbt"], w["wuvt"]]
    out_shape = [jax.ShapeDtypeStruct((t, CONV_DIM), BF16),
                 jax.ShapeDtypeStruct((t, HD), BF16)]
    out_specs = [row(CONV_DIM), row(HD)]
    aliases = {}
    n_alias = 0
    own_slot = 0
    if latent:
        assert tm == seq_len
        in_specs += [_resident((seq_len, MLA_QK_PAD), (0, 0)), _resident((seq_len, MLA_QK_PAD), (0, 0)),
                     _resident((MLA_ROPE, seq_len), (0, 0)), _resident((MLA_ROPE, seq_len), (0, 0))]
        args += list(tables)
        seq_blk = lambda n: pl.BlockSpec((None, n, seq_len), lambda i: (i, 0, 0))
        out_shape += [jax.ShapeDtypeStruct((n_seq, HD, seq_len), BF16),
                      jax.ShapeDtypeStruct((n_seq, HD, seq_len), BF16),
                      jax.ShapeDtypeStruct((t, MLA_QK_W), BF16),
                      jax.ShapeDtypeStruct((n_seq, MLA_QK_W, seq_len), BF16),
                      jax.ShapeDtypeStruct((n_seq, HD, seq_len), BF16)]
        out_specs += [seq_blk(HD), seq_blk(HD), row(MLA_QK_W), seq_blk(MLA_QK_W), seq_blk(HD)]
    else:
        if caches is not None:
            n_alias = len(caches)
            in_specs += [pl.BlockSpec(memory_space=pl.ANY)] * n_alias
            args += list(caches)
            aliases = {len(args) - n_alias + k: 2 + (0, 1, 5, 6)[k] for k in range(n_alias)}
            n_slots, first_slot = 1, layer
        else:
            assert layer == 0
            n_slots, first_slot, own_slot = DEPTH, 0, layer
        cache_blk = lambda *dims: pl.BlockSpec((seq_per_tile, n_slots) + dims,
                                               lambda i: (i, first_slot) + (0,) * len(dims))
        seq_blk = lambda n: pl.BlockSpec((seq_per_tile, n, seq_len), lambda i: (i, 0, 0))
        out_shape += [jax.ShapeDtypeStruct((n_seq, DEPTH, HEADS, NA_HD, seq_len), F32),
                      jax.ShapeDtypeStruct((n_seq, DEPTH, HEADS, NA_HD, seq_len), F32),
                      jax.ShapeDtypeStruct((t, MLA_QK_W), BF16),
                      jax.ShapeDtypeStruct((n_seq, MLA_QK_W, seq_len), BF16),
                      jax.ShapeDtypeStruct((n_seq, HD, seq_len), BF16),
                      jax.ShapeDtypeStruct((n_seq, DEPTH, seq_len, KV_LORA), F32),
                      jax.ShapeDtypeStruct((n_seq, DEPTH, MLA_ROPE, seq_len), F32)]
        out_specs += [cache_blk(HEADS, NA_HD, seq_len), cache_blk(HEADS, NA_HD, seq_len), row(MLA_QK_W),
                      seq_blk(MLA_QK_W), seq_blk(HD), cache_blk(seq_len, KV_LORA), cache_blk(MLA_ROPE, seq_len)]
    return pl.pallas_call(
        functools.partial(_proj_body, seq_len=seq_len, latent=latent, n_alias=n_alias, own_slot=own_slot),
        out_shape=out_shape,
        grid=(t // tm,),
        in_specs=in_specs,
        out_specs=out_specs,
        input_output_aliases=aliases,
        compiler_params=_params(1),
        name="mixer_proj",
    )(*args)


def _ctxkv_body(ckv_ref, krt_ref, wkat_ref, wkbt_ref, wuvt_ref, k_ref, v_ref):
    ckv = ckv_ref[...].astype(BF16)
    krt = krt_ref[...].astype(BF16)
    k_ref[...] = (_dot_nt(wkat_ref[...], ckv) + _dot(wkbt_ref[...], krt)).astype(BF16)
    v_ref[...] = _dot_nt(wuvt_ref[...], ckv).astype(BF16)


def _ctx_kv(cache_ckv, cache_krt, wkat, wkbt, wuvt):
    return pl.pallas_call(
        _ctxkv_body,
        out_shape=[jax.ShapeDtypeStruct((DEPTH, DEC_BATCH, MLA_QK_W, PAST_LEN), BF16),
                   jax.ShapeDtypeStruct((DEPTH, DEC_BATCH, HD, PAST_LEN), BF16)],
        grid=(DEPTH, DEC_BATCH),
        in_specs=[pl.BlockSpec((None, None, PAST_LEN, KV_LORA), lambda l, b: (b, l, 0, 0)),
                  pl.BlockSpec((None, None, MLA_ROPE, PAST_LEN), lambda l, b: (b, l, 0, 0)),
                  pl.BlockSpec((None, MLA_QK_W, KV_LORA), lambda l, b: (l, 0, 0)),
                  pl.BlockSpec((None, MLA_QK_W, MLA_ROPE), lambda l, b: (l, 0, 0)),
                  pl.BlockSpec((None, HD, KV_LORA), lambda l, b: (l, 0, 0))],
        out_specs=[pl.BlockSpec((None, None, MLA_QK_W, PAST_LEN), lambda l, b: (l, b, 0, 0)),
                   pl.BlockSpec((None, None, HD, PAST_LEN), lambda l, b: (l, b, 0, 0))],
        compiler_params=_params(2),
        name="ctx_kv",
    )(cache_ckv, cache_krt, wkat, wkbt, wuvt)


def _softmax_pv(s, vt):
    m = jnp.max(s, axis=-1, keepdims=True)
    p = jnp.exp2(s - m)
    den = jnp.sum(p, axis=-1, keepdims=True)
    return _dot_nt(p.astype(BF16), vt) / den


def _pair_slot(x, j):
    z = jnp.zeros_like(x)
    return jnp.concatenate([x, z] if j == 0 else [z, x], axis=0)


def _attn_ctx_body(qna_ref, knat_ref, vnat_ref, qm_ref, kmt_ref, vmt_ref, ona_ref, om_ref, *, seq_len):
    for b in range(qna_ref.shape[0] // seq_len):
        rows = slice(b * seq_len, (b + 1) * seq_len)
        for hp in range(HEADS // 2):
            pair = slice(hp * 2 * HEAD_V, (hp + 1) * 2 * HEAD_V)
            q = qna_ref[rows, pair]
            o_na = o_m = None
            for j in range(2):
                hh = 2 * hp + j
                kt = _pair_slot(knat_ref[b, hh].astype(BF16), j)
                vt = _pair_slot(vnat_ref[b, hh].astype(BF16), j)
                o = _softmax_pv(_dot(q, kt), vt)
                o_na = o if j == 0 else o_na + o
                qk = slice(hh * MLA_QK_PAD, (hh + 1) * MLA_QK_PAD)
                vt = _pair_slot(vmt_ref[b, hh * HEAD_V:(hh + 1) * HEAD_V, :], j)
                o = _softmax_pv(_dot(qm_ref[rows, qk], kmt_ref[b, qk, :]), vt)
                o_m = o if j == 0 else o_m + o
            ona_ref[rows, pair] = o_na.astype(BF16)
            om_ref[rows, pair] = o_m.astype(BF16)


def _attn_ctx(qna, knat, vnat, qm, kmt, vmt, layer, seq_len, tm=512):
    t = qna.shape[0]
    nb = tm // seq_len
    row = lambda n: pl.BlockSpec((tm, n), lambda i: (i, 0))
    cache_blk = pl.BlockSpec((nb, None, HEADS, NA_HD, seq_len), lambda i: (i, layer, 0, 0, 0))
    seq_blk = lambda n: pl.BlockSpec((nb, n, seq_len), lambda i: (i, 0, 0))
    return pl.pallas_call(
        functools.partial(_attn_ctx_body, seq_len=seq_len),
        out_shape=[jax.ShapeDtypeStruct((t, HD), BF16), jax.ShapeDtypeStruct((t, HD), BF16)],
        grid=(t // tm,),
        in_specs=[row(HD), cache_blk, cache_blk, row(MLA_QK_W), seq_blk(MLA_QK_W), seq_blk(HD)],
        out_specs=[row(HD), row(HD)],
        compiler_params=_params(1),
        name="attn_ctx",
    )(qna, knat, vnat, qm, kmt, vmt)


def _attn_lat_body(qna_ref, knat_ref, vnat_ref, kctx_ref, vctx_ref, bias_ref,
                   qm_ref, kmt_ref, vmt_ref, kmctx_ref, vmctx_ref, ona_ref, om_ref):
    cat = lambda *a: jnp.concatenate(a, axis=1)
    head = lambda j: slice(j * HEAD_V, (j + 1) * HEAD_V)
    kc = [_pair_slot(kctx_ref[j].astype(BF16), j) for j in range(2)]
    vc = [_pair_slot(vctx_ref[j].astype(BF16), j) for j in range(2)]
    for c, (start, count) in enumerate(NA_WINDOWS):
        rows = slice(c * Q_CHUNK, (c + 1) * Q_CHUNK)
        keys = slice(start, start + count)
        q = qna_ref[rows, :]
        for j in range(2):
            bias = jnp.concatenate(
                [cat(*[bias_ref[j, p] for p in NA_BLOCK_INDEX[c][rl]]) for rl in range(Q_CHUNK // GRID_W)], axis=0)
            s = cat(_dot(q, kc[j]), _dot(q, _pair_slot(knat_ref[head(j), keys], j)) + bias)
            o = _softmax_pv(s, cat(vc[j], _pair_slot(vnat_ref[head(j), keys], j)))
            o_na = o if j == 0 else o_na + o
        ona_ref[rows, :] = o_na.astype(BF16)
    kt, vt = [], []
    for j in range(2):
        qk = slice(j * MLA_QK_PAD, (j + 1) * MLA_QK_PAD)
        kt.append(cat(kmctx_ref[qk, :], kmt_ref[qk, :]))
        vt.append(_pair_slot(cat(vmctx_ref[head(j), :], vmt_ref[head(j), :]), j))
    for c in range(DEC_SEQ // MLA_Q_ROWS):
        rows = slice(c * MLA_Q_ROWS, (c + 1) * MLA_Q_ROWS)
        for j in range(2):
            o = _softmax_pv(_dot(qm_ref[rows, j * MLA_QK_PAD:(j + 1) * MLA_QK_PAD], kt[j]), vt[j])
            o_m = o if j == 0 else o_m + o
        om_ref[rows, :] = o_m.astype(BF16)


def _attn_lat(qna, knat, vnat, kctx, vctx, bias, qm, kmt, vmt, kmctx, vmctx, layer):
    t = qna.shape[0]
    s = DEC_SEQ
    return pl.pallas_call(
        _attn_lat_body,
        out_shape=[jax.ShapeDtypeStruct((t, HD), BF16), jax.ShapeDtypeStruct((t, HD), BF16)],
        grid=(HEADS // 2, t // s),
        in_specs=[pl.BlockSpec((s, 2 * NA_HD), lambda hp, b: (b, hp)),
                  pl.BlockSpec((None, 2 * NA_HD, s), lambda hp, b: (b, hp, 0)),
                  pl.BlockSpec((None, 2 * HEAD_V, s), lambda hp, b: (b, hp, 0)),
                  pl.BlockSpec((None, None, 2, NA_HD, PAST_LEN), lambda hp, b: (b, layer, hp, 0, 0)),
                  pl.BlockSpec((None, None, 2, NA_HD, PAST_LEN), lambda hp, b: (b, layer, hp, 0, 0)),
                  pl.BlockSpec((None, 2, len(NA_BLOCK_PAIRS), GRID_W, 2 * GRID_W),
                               lambda hp, b: (layer, hp, 0, 0, 0)),
                  pl.BlockSpec((s, 2 * MLA_QK_PAD), lambda hp, b: (b, hp)),
                  pl.BlockSpec((None, 2 * MLA_QK_PAD, s), lambda hp, b: (b, hp, 0)),
                  pl.BlockSpec((None, 2 * HEAD_V, s), lambda hp, b: (b, hp, 0)),
                  pl.BlockSpec((None, None, 2 * MLA_QK_PAD, PAST_LEN), lambda hp, b: (layer, b, hp, 0)),
                  pl.BlockSpec((None, None, 2 * HEAD_V, PAST_LEN), lambda hp, b: (layer, b, hp, 0))],
        out_specs=[pl.BlockSpec((s, 2 * HEAD_V), lambda hp, b: (b, hp)),
                   pl.BlockSpec((s, 2 * HEAD_V), lambda hp, b: (b, hp))],
        compiler_params=_params(2),
        name="attn_lat",
    )(qna, knat, vnat, kctx, vctx, bias, qm, kmt, vmt, kmctx, vmctx)


def _mix_body(*refs, n_parts):
    tm = refs[-1].shape[0]
    parts, pos = [], 0
    for n in n_parts:
        parts.append(_rows_value(refs[pos:pos + n], tm))
        pos += n
    x, yc, ona, om = parts
    mod_ref, ng_ref, wgt_ref, wco_ref, wno_ref, wmo_ref, wo_ref, o_ref = refs[pos:]
    mod = mod_ref[...]
    h = (_rms(x, ng_ref[...]) * (1 + mod[4:5]) + mod[3:4]).astype(BF16)
    gate = lambda k: jax.nn.sigmoid(_dot_nt(h, wgt_ref[k * D_MODEL:(k + 1) * D_MODEL, :]))
    z = gate(0) * _dot(yc, wco_ref[...])
    z = z + gate(1) * _dot(ona, wno_ref[...])
    z = z + gate(2) * _dot(om, wmo_ref[...])
    o_ref[...] = x + mod[5:6] * _dot(z.astype(BF16), wo_ref[...])


def _mix(x, yc, ona, om, mod, w, layer):
    tm = 512 if isinstance(x, tuple) else 1024
    in_specs, args, n_parts = [], [], []
    for a in (x, yc, ona, om):
        specs, ops = _rows_specs(a, tm, 0)
        in_specs += specs
        args += ops
        n_parts.append(len(ops))
    in_specs += [_mod_spec(tm, 0),
                 _resident((None, 1, D_MODEL), (layer, 0, 0)),
                 _resident((3 * D_MODEL, D_MODEL), (0, 0)),
                 _resident((CONV_DIM, D_MODEL), (0, 0)),
                 _resident((HD, D_MODEL), (0, 0)),
                 _resident((HD, D_MODEL), (0, 0)),
                 _resident((D_MODEL, D_MODEL), (0, 0))]
    args += [mod, w["ng1"], w["wgt"], w["wco"], w["wno"], w["wmo"], w["wo"]]
    return pl.pallas_call(
        functools.partial(_mix_body, n_parts=tuple(n_parts)),
        out_shape=jax.ShapeDtypeStruct((N_ROWS, D_MODEL), F32),
        grid=(N_ROWS // tm,),
        in_specs=in_specs,
        out_specs=pl.BlockSpec((tm, D_MODEL), lambda i: (i, 0)),
        compiler_params=_params(1),
        name="mixer_out",
    )(*args)


def _rope_tables():
    f32 = np.float32
    half = MLA_ROPE // 2
    nf = half // 2
    inv = (f32(1.0) / (f32(ROPE_BASE) ** (np.arange(nf, dtype=f32) / f32(nf)))).astype(f32)
    t = np.arange(DEC_SEQ)
    rows = (t // GRID_W).astype(f32)[:, None] * inv[None, :]
    cols = (t % GRID_W).astype(f32)[:, None] * inv[None, :]
    cos = np.concatenate([np.cos(rows), np.cos(rows), np.cos(cols), np.cos(cols)], axis=-1).astype(f32)
    sin = np.concatenate([np.sin(rows), np.sin(rows), np.sin(cols), np.sin(cols)], axis=-1).astype(f32)
    pad = MLA_QK_PAD - MLA_NOPE - MLA_ROPE
    q_cos = np.concatenate([np.ones((DEC_SEQ, MLA_NOPE), f32), cos, np.zeros((DEC_SEQ, pad), f32)], axis=-1)
    q_sin = np.concatenate([np.zeros((DEC_SEQ, MLA_NOPE), f32), sin, np.zeros((DEC_SEQ, pad), f32)], axis=-1)
    return tuple(jnp.asarray(a) for a in (q_cos, q_sin, np.ascontiguousarray(cos.T), np.ascontiguousarray(sin.T)))


def _rope_swap(w):
    nf = MLA_ROPE // 4
    a, b, c, d = (w[..., i * nf:(i + 1) * nf] for i in range(4))
    return jnp.concatenate([-b, a, -d, c], axis=-1)


def _na_bias(rpb):
    n_dc = 2 * NA_WIN_C - 1
    col = np.arange(GRID_W)
    c_start = np.clip(col - NA_WIN_C // 2, 0, GRID_W - NA_WIN_C)
    c_in = (col[None, :] >= c_start[:, None]) & (col[None, :] < c_start[:, None] + NA_WIN_C)
    dc = np.clip(col[None, :] - col[:, None] + (NA_WIN_C - 1), 0, n_dc - 1)
    pick_dc = (dc[None] == np.arange(n_dc)[:, None, None]).astype(np.float32)
    n_pairs = len(NA_BLOCK_PAIRS)
    pick_dr = np.zeros((n_pairs, 2, NA_DR_MASKED), np.float32)
    for p, pair in enumerate(NA_BLOCK_PAIRS):
        for side, d in enumerate(pair):
            if d != NA_DR_MASKED:
                pick_dr[p, side, d] = 1.0
    keep = pick_dr.sum(-1).astype(bool)[:, None, :, None] & c_in[None, :, None, :]
    keep = keep.reshape(n_pairs, GRID_W, 2 * GRID_W)
    pick_side_dc = np.zeros((2, n_dc, GRID_W, 2, GRID_W), np.float32)
    for side in range(2):
        pick_side_dc[side, :, :, side, :] = pick_dc
    pick_side_dc = pick_side_dc.reshape(2, n_dc, GRID_W, 2 * GRID_W)
    hi = lax.Precision.HIGHEST
    by_row = jnp.einsum("psd,lhdj->lhpsj", jnp.asarray(pick_dr), rpb, precision=hi)
    blocks = jnp.einsum("lhpsj,sjqn->lhpqn", by_row, jnp.asarray(pick_side_dc), precision=hi)
    return jnp.where(jnp.asarray(keep), blocks * LOG2_E, NEG_INF)


def _pack_weights(w_int, w_uq, w_ukv):
    t_last = lambda a: jnp.swapaxes(a, -1, -2)
    w_mid = lax.optimization_barrier(w_int[:, W_IN_KV:W_IN_GATE]).astype(BF16)
    w_kvt = w_mid[:, :W_IN_LORA - W_IN_KV]
    w_krt = w_mid[:, W_IN_KR - W_IN_KV:]
    wt = jnp.concatenate([w_kvt, w_krt, t_last(_rope_swap(t_last(w_krt)))], axis=1)
    wbt = w_mid[:, W_IN_LORA - W_IN_KV:W_IN_KR - W_IN_KV]
    uq = w_uq.reshape(DEPTH, Q_LORA, MLA_HEADS, MLA_NOPE + MLA_ROPE)
    pad = MLA_QK_PAD - MLA_NOPE - MLA_ROPE
    zp = jnp.zeros(uq.shape[:-1] + (pad,), F32)
    zn = jnp.zeros(uq.shape[:-1] + (MLA_NOPE,), F32)
    q_ext = jnp.concatenate([uq, zp], axis=-1).reshape(DEPTH, Q_LORA, MLA_QK_W)
    q_sw = jnp.concatenate([zn, _rope_swap(uq[..., MLA_NOPE:]), zp], axis=-1).reshape(q_ext.shape)
    ukv = w_ukv.reshape(DEPTH, KV_LORA, MLA_HEADS, MLA_NOPE + MLA_V)
    zk = jnp.zeros(ukv.shape[:-1] + (MLA_QK_PAD - MLA_NOPE,), F32)
    wka = jnp.concatenate([ukv[..., :MLA_NOPE], zk], axis=-1).reshape(DEPTH, KV_LORA, MLA_QK_W)
    eye = jnp.concatenate([jnp.zeros((MLA_ROPE, MLA_NOPE), F32), jnp.eye(MLA_ROPE, dtype=F32),
                           jnp.zeros((MLA_ROPE, pad), F32)], axis=-1)
    wkb = jnp.broadcast_to(jnp.tile(eye, (1, MLA_HEADS))[None], (DEPTH, MLA_ROPE, MLA_QK_W))
    wuv = ukv[..., MLA_NOPE:].reshape(DEPTH, KV_LORA, HD)
    b = lambda a: a.astype(BF16)
    return dict(wt=b(wt), wbt=b(wbt),
                wq_lat=b(jnp.concatenate([q_ext, q_sw], axis=-1)), wq_ctx=b(q_ext),
                wkat=b(t_last(wka)), wkbt=b(t_last(wkb)), wuvt=b(t_last(wuv)))


def kernel(x_prompt, x_sample, cache_na_k, cache_na_v, cache_mla_ckv, cache_mla_krope, c, c_ctx,
           w_ada, b_ada, norm_g, w_ffn1_gate, w_ffn1_up, w_ffn1_down, w_ffn2_gate, w_ffn2_up, w_ffn2_down,
           w_in, conv_w, conv_b, na_rpb, mla_qnorm, w_uq, mla_kvnorm, w_ukv,
           w_conv_out, w_na_out, w_mla_out, w_o, final_g):
    b16 = lambda a: a.astype(BF16)
    t_last = lambda a: jnp.swapaxes(a, -1, -2)
    w_int = t_last(w_in)
    packed = _pack_weights(w_int, w_uq, w_ukv)
    shared = dict(conv_w=conv_w, conv_b=conv_b.reshape(DEPTH, 1, CONV_DIM),
                  qnorm=mla_qnorm.reshape(DEPTH, 1, Q_LORA), kvnorm=mla_kvnorm.reshape(DEPTH, 1, KV_LORA),
                  ng1=norm_g[:, 1:2],
                  **{k: packed[k] for k in ("wt", "wbt", "wkat", "wkbt", "wuvt")})
    ffn1_f32 = (w_ffn1_gate, w_ffn1_up, w_ffn1_down)
    ffn2_f32 = (w_ffn2_gate, w_ffn2_up, w_ffn2_down)
    ffn1_w = {0: tuple(b16(w[0]) for w in ffn1_f32)}
    ffn2_w = {}
    mixer_w = {}
    mixer_srcs = ((w_int, 0, W_IN_KV), (w_int, W_IN_GATE, 3 * D_MODEL),
                  (w_conv_out, 0, None), (w_na_out, 0, None), (w_mla_out, 0, None), (w_o, 0, None))
    final_row = final_g.reshape(1, D_MODEL)

    c_all = jnp.concatenate([c_ctx[None], c, jnp.zeros((MOD_ROWS - 1 - DEC_BATCH, D_MODEL), F32)], axis=0)
    mod = _modulation(c_all, w_ada, b_ada).reshape(DEPTH, MOD_ROWS, N_MOD, D_MODEL)

    tables = _rope_tables()
    na_bias = _na_bias(na_rpb)
    ctx_k_na = t_last(cache_na_k)
    ctx_v_na = t_last(cache_na_v)
    ctx_k_mla, ctx_v_mla = _ctx_kv(cache_mla_ckv, t_last(cache_mla_krope),
                                   packed["wkat"], packed["wkbt"], packed["wuvt"])

    xp = x_prompt.reshape(N_CTX_ROWS, D_MODEL)
    xs = x_sample.reshape(N_ROWS - N_CTX_ROWS, D_MODEL)
    caches = None
    x_all = None
    for l in range(DEPTH):
        last = l == DEPTH - 1
        jobs = [_cast_job(w, l) for w in ffn2_f32]
        if l == 0:
            jobs += [_cast_job(a, ll, r0, n) for ll in range(DEPTH) for a, r0, n in mixer_srcs]
            xp, cast = _ffn(xp, mod[l], norm_g[l, 0:1], *ffn1_w[l], 0, n_rows=N_CTX_ROWS, cast_jobs=jobs, tm=512)
            xs, _ = _ffn(xs, mod[l], norm_g[l, 0:1], *ffn1_w[l], 0, row0=N_CTX_ROWS)
            x1 = (xp, xs)
        else:
            x1, cast = _ffn(x_all, mod[l], norm_g[l, 0:1], *ffn1_w[l], 0, cast_jobs=jobs)
            xp = xs = x1
        ffn2_w[l] = tuple(cast[:3])
        for ll in range(DEPTH if l == 0 else 0):
            names = ("wat", "wgt", "wco", "wno", "wmo", "wo")
            mixer_w[ll] = dict(zip(names, cast[3 + len(names) * ll:3 + len(names) * (ll + 1)]))
        w_ctx = dict(shared, wq=packed["wq_ctx"], **mixer_w[l])
        w_lat = dict(shared, wq=packed["wq_lat"], **mixer_w[l])
        yc_p, qna, knat, vnat, qm, kmt, vmt, ckv, krt = _proj(xp, mod[l], w_ctx, l, SEQ, False, None, caches,
                                                              512 if caches is None else 1024)
        caches = (knat, vnat, ckv, krt)
        ona_p, om_p = _attn_ctx(qna, knat, vnat, qm, kmt, vmt, l, SEQ)
        yc_s, qna, knat, vnat, qm, kmt, vmt = _proj(xs, mod[l], w_lat, l, DEC_SEQ, True, tables, None, DEC_SEQ)
        ona_s, om_s = _attn_lat(qna, knat, vnat, ctx_k_na, ctx_v_na, na_bias, qm, kmt, vmt, ctx_k_mla, ctx_v_mla, l)
        x2 = _mix(x1, (yc_p, yc_s), (ona_p, ona_s), (om_p, om_s), mod[l], w_ctx, l)
        if not last:
            x_all, cast = _ffn(x2, mod[l], norm_g[l, 2:3], *ffn2_w[l], 6,
                               cast_jobs=[_cast_job(w, l + 1) for w in ffn1_f32])
            ffn1_w[l + 1] = tuple(cast)
        else:
            yp, _ = _ffn(x2, mod[l], norm_g[l, 2:3], *ffn2_w[l], 6, n_rows=N_CTX_ROWS, final_g=final_row)
            ys, _ = _ffn(x2, mod[l], norm_g[l, 2:3], *ffn2_w[l], 6, row0=N_CTX_ROWS, final_g=final_row)
    new_kt, new_vt, new_ckv, new_krt = caches
    return (yp.reshape(BATCH, SEQ, D_MODEL), ys.reshape(DEC_BATCH, DEC_SEQ, D_MODEL),
            t_last(new_kt), t_last(new_vt), new_ckv, t_last(new_krt))
```

```python
import functools
import math

import jax
import jax.numpy as jnp
import numpy as np
from jax import lax
from jax.experimental import pallas as pl
from jax.experimental.pallas import tpu as pltpu

D_MODEL = 1024
BATCH = 32
SEQ = 256
DEPTH = 2
DEC_BATCH = 8
DEC_SEQ = 1024
PAST_LEN = 256
GRID_W = 64
CONV_DIM = 512
CONV_K = 3
NA_HEADS = 8
NA_HD = 64
NA_WIN_R = 8
NA_WIN_C = 16
MLA_HEADS = 8
MLA_NOPE = 64
MLA_ROPE = 32
MLA_V = 64
Q_LORA = 256
KV_LORA = 128
FFN_DIM = 2816
N_MOD = 9
ROPE_BASE = 10000.0
EPS = 1e-6
NEG_INF = -1e30
LOG2_E = 1.4426950408889634
MLA_SCALE = (MLA_NOPE + MLA_ROPE) ** -0.5 * LOG2_E
NA_SCALE = NA_HD ** -0.5 * LOG2_E

N_CTX_ROWS = BATCH * SEQ
N_ROWS = N_CTX_ROWS + DEC_BATCH * DEC_SEQ
HEADS = 8
HEAD_V = 64
HD = HEADS * NA_HD
W_IN_KV = 3 * CONV_DIM + HD
W_IN_LORA = W_IN_KV + 2 * HD
W_IN_KR = W_IN_LORA + Q_LORA + KV_LORA
W_IN_GATE = W_IN_KR + MLA_ROPE
MLA_QK_PAD = 128
MLA_QK_W = HEADS * MLA_QK_PAD
FFN_CHUNK = 256
CONV_CHUNK = 256
MIX_CHUNK = 256
Q_CHUNK = 256
MLA_Q_ROWS = 128
MOD_ROWS = 16
VMEM_LIMIT = 56 * 1024 * 1024
NA_WINDOWS = ((0, 512), (0, 768), (256, 768), (512, 512))
NA_DR_MASKED = 2 * NA_WIN_R - 1


def _na_block_pairs():
    rows = DEC_SEQ // GRID_W
    r_start = np.clip(np.arange(rows) - NA_WIN_R // 2, 0, rows - NA_WIN_R)
    pairs, index = [], []
    for c, (start, count) in enumerate(NA_WINDOWS):
        index.append([])
        for rl in range(Q_CHUNK // GRID_W):
            r = c * (Q_CHUNK // GRID_W) + rl
            assert start // GRID_W <= r_start[r] and r_start[r] + NA_WIN_R <= (start + count) // GRID_W
            index[c].append([])
            for kp in range(count // (2 * GRID_W)):
                pair = []
                for rk in (start // GRID_W + 2 * kp, start // GRID_W + 2 * kp + 1):
                    inside = r_start[r] <= rk < r_start[r] + NA_WIN_R
                    pair.append(int(rk - r + NA_WIN_R - 1) if inside else NA_DR_MASKED)
                pair = tuple(pair)
                if pair not in pairs:
                    pairs.append(pair)
                index[c][rl].append(pairs.index(pair))
    return tuple(pairs), index


NA_BLOCK_PAIRS, NA_BLOCK_INDEX = _na_block_pairs()

BF16 = jnp.bfloat16
F32 = jnp.float32


def _dot(a, b):
    return jnp.dot(a, b, preferred_element_type=F32)


def _dot_nt(a, b):
    return lax.dot_general(a, b, (((1,), (1,)), ((), ())), preferred_element_type=F32)


def _rms(x, g):
    return x * lax.rsqrt(jnp.mean(x * x, axis=-1, keepdims=True) + EPS) * g


def _params(n_axes, flags=None):
    return pltpu.CompilerParams(dimension_semantics=("arbitrary",) * n_axes,
                                vmem_limit_bytes=VMEM_LIMIT, flags=flags)


def _resident(shape, index):
    return pl.BlockSpec(shape, lambda *_: index, pipeline_mode=pl.Buffered(1))


def _mod_body(c_ref, w_ref, b_ref, o_ref):
    c = c_ref[...]
    a = c * jax.nn.sigmoid(c)
    o_ref[...] = _dot(a.astype(BF16), w_ref[...].astype(BF16)) + b_ref[...]


def _modulation(c_all, w_ada, b_ada):
    n_col = N_MOD * D_MODEL
    tn = n_col // 4
    return pl.pallas_call(
        _mod_body,
        out_shape=jax.ShapeDtypeStruct((DEPTH, MOD_ROWS, n_col), F32),
        grid=(DEPTH, n_col // tn),
        in_specs=[pl.BlockSpec((MOD_ROWS, D_MODEL), lambda l, j: (0, 0)),
                  pl.BlockSpec((None, D_MODEL, tn), lambda l, j: (l, 0, j)),
                  pl.BlockSpec((None, 1, tn), lambda l, j: (l, 0, j))],
        out_specs=pl.BlockSpec((None, MOD_ROWS, tn), lambda l, j: (l, 0, j)),
        compiler_params=_params(2),
        name="modulation",
    )(c_all, w_ada, b_ada.reshape(DEPTH, 1, n_col))


def _rows_specs(x, tm, tile0):
    if isinstance(x, tuple):
        assert tile0 == 0
        n_ctx = x[0].shape[0] // tm
        width = x[0].shape[1]
        return [pl.BlockSpec((tm, width), lambda i: (jnp.minimum(i, n_ctx - 1), 0)),
                pl.BlockSpec((tm, width), lambda i: (jnp.maximum(i - n_ctx, 0), 0))], list(x)
    return [pl.BlockSpec((tm, x.shape[1]), lambda i: (i + tile0, 0))], [x]


def _rows_value(refs, tm):
    if len(refs) == 1:
        return refs[0][...]
    return jnp.where(pl.program_id(0) < N_CTX_ROWS // tm, refs[0][...], refs[1][...])


def _mod_spec(tm, tile0):
    def index(i):
        g = (i + tile0) * tm
        return jnp.where(g < N_CTX_ROWS, 0, 1 + (g - N_CTX_ROWS) // DEC_SEQ), 0, 0
    return pl.BlockSpec((None, N_MOD, D_MODEL), index)


def _ffn_body(*refs, mod_off, final, n_jobs, n_x):
    x_refs, (mod_ref, ng_ref, wg_ref, wu_ref, wd_ref), rest = refs[:n_x], refs[n_x:n_x + 5], refs[n_x + 5:]
    if final:
        fg_ref, rest = rest[0], rest[1:]
    job_in, o_ref, job_out, a_scr = rest[:n_jobs], rest[n_jobs], rest[n_jobs + 1:2 * n_jobs + 1], rest[-1]
    x = _rows_value(x_refs, o_ref.shape[0])
    mod = mod_ref[...]
    shift = mod[mod_off:mod_off + 1]
    scale = mod[mod_off + 1:mod_off + 2]
    gate = mod[mod_off + 2:mod_off + 3]
    h = (_rms(x, ng_ref[...]) * (1 + scale) + shift).astype(BF16)
    for f in range(FFN_DIM // FFN_CHUNK):
        cols = slice(f * FFN_CHUNK, (f + 1) * FFN_CHUNK)
        g = _dot(h, wg_ref[:, cols])
        u = _dot(h, wu_ref[:, cols])
        a_scr[:, cols] = (g * jax.nn.sigmoid(g) * u).astype(BF16)
    y = _dot(a_scr[...], wd_ref[...])
    out = x + 0.5 * gate * y
    if final:
        out = _rms(out, fg_ref[...])
    o_ref[...] = out
    for src, dst in zip(job_in, job_out):
        dst[...] = src[...].reshape(dst.shape).astype(BF16)


def _cast_job(arr, layer, row_start=0, n_rows=None):
    return arr, layer, row_start, arr.shape[1] if n_rows is None else n_rows


def _ffn(x, mod, ng, wg, wu, wd, mod_off, row0=0, n_rows=None, final_g=None, cast_jobs=(), tm=1024):
    t = N_ROWS - row0 if n_rows is None else n_rows
    n_steps = t // tm
    final = final_g is not None
    whole = isinstance(x, tuple) or x.shape[0] == N_ROWS
    x_specs, x_args = _rows_specs(x, tm, row0 // tm if whole else 0)
    in_specs = x_specs + [_mod_spec(tm, row0 // tm),
                          _resident((1, D_MODEL), (0, 0)),
                          _resident((D_MODEL, FFN_DIM), (0, 0)),
                          _resident((D_MODEL, FFN_DIM), (0, 0)),
                          _resident((FFN_DIM, D_MODEL), (0, 0))]
    args = x_args + [mod, ng, wg, wu, wd]
    if final:
        in_specs.append(_resident((1, D_MODEL), (0, 0)))
        args.append(final_g)
    out_shape = [jax.ShapeDtypeStruct((t, D_MODEL), F32)]
    out_specs = [pl.BlockSpec((tm, D_MODEL), lambda i: (i, 0))]
    for arr, layer, row_start, rows in cast_jobs:
        width = arr.shape[2]
        blk = rows // n_steps
        assert blk * n_steps == rows and blk % 16 == 0 and row_start % 16 == 0
        if row_start % blk == 0:
            spec = pl.BlockSpec((None, blk, width), lambda i, l=layer, b0=row_start // blk: (l, b0 + i, 0))
        else:
            g = math.gcd(row_start, blk)
            spec = pl.BlockSpec((pl.Element(1), pl.Element(blk), pl.Element(width)),
                                lambda i, l=layer, r0=row_start // g, n=blk // g, g=g: (l, (r0 + n * i) * g, 0))
        in_specs.append(spec)
        args.append(arr)
        out_shape.append(jax.ShapeDtypeStruct((rows, width), BF16))
        out_specs.append(pl.BlockSpec((blk, width), lambda i: (i, 0)))
    outs = pl.pallas_call(
        functools.partial(_ffn_body, mod_off=mod_off, final=final, n_jobs=len(cast_jobs), n_x=len(x_args)),
        out_shape=out_shape,
        grid=(n_steps,),
        in_specs=in_specs,
        out_specs=out_specs,
        scratch_shapes=[pltpu.VMEM((tm, FFN_DIM), BF16)],
        compiler_params=_params(1),
        name="ffn",
    )(*args)
    return outs[0], list(outs[1:])


def _proj_body(x_ref, mod_ref, ng_ref, wat_ref, wbt_ref, wt_ref, cw_ref, cb_ref, qn_ref, kvn_ref,
               wq_ref, wkat_ref, wkbt_ref, wuvt_ref, *rest, seq_len, latent, n_alias, own_slot):
    rest = rest[n_alias:]

    def put(ref, b, value):
        for k in range(ref.shape[1]):
            ref[b, k] = value if k == own_slot else jnp.zeros_like(value)

    if latent:
        (qc_ref, qs_ref, kct_ref, kst_ref,
         yc_ref, qna_ref, knat_ref, vnat_ref, qm_ref, kmt_ref, vmt_ref) = rest
    else:
        (yc_ref, qna_ref, knat_ref, vnat_ref, qm_ref, kmt_ref, vmt_ref, ckv_ref, krt_ref) = rest
    tm = x_ref.shape[0]
    x = x_ref[...]
    mod = mod_ref[...]
    h = (_rms(x, ng_ref[...]) * (1 + mod[4:5]) + mod[3:4]).astype(BF16)

    pos = lax.broadcasted_iota(jnp.int32, (tm, 1), 0) % seq_len
    cw = cw_ref[...]
    for ch in range(CONV_DIM // CONV_CHUNK):
        cols = slice(ch * CONV_CHUNK, (ch + 1) * CONV_CHUNK)
        part = lambda k: _dot_nt(h, wat_ref[k * CONV_DIM + ch * CONV_CHUNK:k * CONV_DIM + (ch + 1) * CONV_CHUNK, :])
        v = part(1) * part(2)
        v_prev = jnp.where(pos == 0, 0.0, pltpu.roll(v, 1, 0))
        v_next = jnp.where(pos == seq_len - 1, 0.0, pltpu.roll(v, tm - 1, 0))
        y = cb_ref[:, cols] + v_prev * cw[0:1, cols]
        y = y + v * cw[1:2, cols]
        y = y + v_next * cw[2:3, cols]
        yc_ref[:, cols] = (part(0) * y).astype(BF16)

    qna_ref[...] = (_dot_nt(h, wat_ref[3 * CONV_DIM:3 * CONV_DIM + HD, :]) * NA_SCALE).astype(BF16)

    kt_na = _dot_nt(wt_ref[0:HD, :], h)
    ut = _dot_nt(wt_ref[HD:, :], h)
    vt_na = ut[0:HD]
    krt = ut[HD:HD + MLA_ROPE]

    u = _dot_nt(h, wbt_ref[...])
    cq = _rms(u[:, 0:Q_LORA], qn_ref[...]).astype(BF16)
    ckv = _rms(u[:, Q_LORA:Q_LORA + KV_LORA], kvn_ref[...])
    ckv_b = ckv.astype(BF16)
    q2 = _dot(cq, wq_ref[...])
    if latent:
        knat_ref[...] = kt_na.astype(BF16)
        vnat_ref[...] = vt_na.astype(BF16)
        krt = krt * kct_ref[...] + ut[HD + MLA_ROPE:HD + 2 * MLA_ROPE] * kst_ref[...]
        qc = qc_ref[...]
        qs = qs_ref[...]
        for hh in range(HEADS):
            cols = slice(hh * MLA_QK_PAD, (hh + 1) * MLA_QK_PAD)
            sw_cols = slice(MLA_QK_W + hh * MLA_QK_PAD, MLA_QK_W + (hh + 1) * MLA_QK_PAD)
            qm_ref[:, cols] = ((q2[:, cols] * qc + q2[:, sw_cols] * qs) * MLA_SCALE).astype(BF16)
    else:
        qm_ref[...] = (q2 * MLA_SCALE).astype(BF16)
    kmt = _dot_nt(wkat_ref[...], ckv_b) + _dot(wkbt_ref[...], krt.astype(BF16))
    vmt = _dot_nt(wuvt_ref[...], ckv_b)
    if latent:
        kmt_ref[...] = kmt.astype(BF16)
        vmt_ref[...] = vmt.astype(BF16)
    else:
        for b in range(tm // seq_len):
            rows = slice(b * seq_len, (b + 1) * seq_len)
            put(knat_ref, b, kt_na[:, rows].reshape(HEADS, NA_HD, seq_len))
            put(vnat_ref, b, vt_na[:, rows].reshape(HEADS, NA_HD, seq_len))
            put(ckv_ref, b, ckv[rows])
            put(krt_ref, b, krt[:, rows])
            kmt_ref[b] = kmt[:, rows].astype(BF16)
            vmt_ref[b] = vmt[:, rows].astype(BF16)


def _proj(x, mod, w, layer, seq_len, latent, tables, caches, tm):
    row0 = N_CTX_ROWS if latent else 0
    t = N_ROWS - N_CTX_ROWS if latent else N_CTX_ROWS
    x_tile0 = row0 // tm if x.shape[0] == N_ROWS else 0
    n_seq = t // seq_len
    seq_per_tile = tm // seq_len
    wq_cols = w["wq"].shape[-1]
    wt_rows = w["wt"].shape[-2]
    row = lambda n: pl.BlockSpec((tm, n), lambda i: (i, 0))
    in_specs = [pl.BlockSpec((tm, D_MODEL), lambda i: (i + x_tile0, 0)),
                _mod_spec(tm, row0 // tm),
                _resident((None, 1, D_MODEL), (layer, 0, 0)),
                _resident((3 * CONV_DIM + HD, D_MODEL), (0, 0)),
                _resident((None, Q_LORA + KV_LORA, D_MODEL), (layer, 0, 0)),
                _resident((None, wt_rows, D_MODEL), (layer, 0, 0)),
                _resident((None, CONV_K, CONV_DIM), (layer, 0, 0)),
                _resident((None, 1, CONV_DIM), (layer, 0, 0)),
                _resident((None, 1, Q_LORA), (layer, 0, 0)),
                _resident((None, 1, KV_LORA), (layer, 0, 0)),
                _resident((None, Q_LORA, wq_cols), (layer, 0, 0)),
                _resident((None, MLA_QK_W, KV_LORA), (layer, 0, 0)),
                _resident((None, MLA_QK_W, MLA_ROPE), (layer, 0, 0)),
                _resident((None, HD, KV_LORA), (layer, 0, 0))]
    args = [x, mod, w["ng1"], w["wat"], w["wbt"], w["wt"], w["conv_w"], w["conv_b"], w["qnorm"], w["kvnorm"],
            w["wq"], w["wkat"], w["wkbt"], w["wuvt"]]
    out_shape = [jax.ShapeDtypeStruct((t, CONV_DIM), BF16),
                 jax.ShapeDtypeStruct((t, HD), BF16)]
    out_specs = [row(CONV_DIM), row(HD)]
    aliases = {}
    n_alias = 0
    own_slot = 0
    if latent:
        assert tm == seq_len
        in_specs += [_resident((seq_len, MLA_QK_PAD), (0, 0)), _resident((seq_len, MLA_QK_PAD), (0, 0)),
                     _resident((MLA_ROPE, seq_len), (0, 0)), _resident((MLA_ROPE, seq_len), (0, 0))]
        args += list(tables)
        seq_blk = lambda n: pl.BlockSpec((None, n, seq_len), lambda i: (i, 0, 0))
        out_shape += [jax.ShapeDtypeStruct((n_seq, HD, seq_len), BF16),
                      jax.ShapeDtypeStruct((n_seq, HD, seq_len), BF16),
                      jax.ShapeDtypeStruct((t, MLA_QK_W), BF16),
                      jax.ShapeDtypeStruct((n_seq, MLA_QK_W, seq_len), BF16),
                      jax.ShapeDtypeStruct((n_seq, HD, seq_len), BF16)]
        out_specs += [seq_blk(HD), seq_blk(HD), row(MLA_QK_W), seq_blk(MLA_QK_W), seq_blk(HD)]
    else:
        if caches is not None:
            n_alias = len(caches)
            in_specs += [pl.BlockSpec(memory_space=pl.ANY)] * n_alias
            args += list(caches)
            aliases = {len(args) - n_alias + k: 2 + (0, 1, 5, 6)[k] for k in range(n_alias)}
            n_slots, first_slot = 1, layer
        else:
            assert layer == 0
            n_slots, first_slot, own_slot = DEPTH, 0, layer
        cache_blk = lambda *dims: pl.BlockSpec((seq_per_tile, n_slots) + dims,
                                               lambda i: (i, first_slot) + (0,) * len(dims))
        seq_blk = lambda n: pl.BlockSpec((seq_per_tile, n, seq_len), lambda i: (i, 0, 0))
        out_shape += [jax.ShapeDtypeStruct((n_seq, DEPTH, HEADS, NA_HD, seq_len), F32),
                      jax.ShapeDtypeStruct((n_seq, DEPTH, HEADS, NA_HD, seq_len), F32),
                      jax.ShapeDtypeStruct((t, MLA_QK_W), BF16),
                      jax.ShapeDtypeStruct((n_seq, MLA_QK_W, seq_len), BF16),
                      jax.ShapeDtypeStruct((n_seq, HD, seq_len), BF16),
                      jax.ShapeDtypeStruct((n_seq, DEPTH, seq_len, KV_LORA), F32),
                      jax.ShapeDtypeStruct((n_seq, DEPTH, MLA_ROPE, seq_len), F32)]
        out_specs += [cache_blk(HEADS, NA_HD, seq_len), cache_blk(HEADS, NA_HD, seq_len), row(MLA_QK_W),
                      seq_blk(MLA_QK_W), seq_blk(HD), cache_blk(seq_len, KV_LORA), cache_blk(MLA_ROPE, seq_len)]
    return pl.pallas_call(
        functools.partial(_proj_body, seq_len=seq_len, latent=latent, n_alias=n_alias, own_slot=own_slot),
        out_shape=out_shape,
        grid=(t // tm,),
        in_specs=in_specs,
        out_specs=out_specs,
        input_output_aliases=aliases,
        compiler_params=_params(1),
        name="mixer_proj",
    )(*args)


def _softmax_pv(s, vt):
    m = jnp.max(s, axis=-1, keepdims=True)
    p = jnp.exp2(s - m)
    den = jnp.sum(p, axis=-1, keepdims=True)
    return _dot_nt(p.astype(BF16), vt) / den


def _pair_slot(x, j):
    z = jnp.zeros_like(x)
    return jnp.concatenate([x, z] if j == 0 else [z, x], axis=0)


def _attn_ctx_body(qna_ref, knat_ref, vnat_ref, qm_ref, kmt_ref, vmt_ref, ona_ref, om_ref, *, seq_len):
    for b in range(qna_ref.shape[0] // seq_len):
        rows = slice(b * seq_len, (b + 1) * seq_len)
        for hp in range(HEADS // 2):
            pair = slice(hp * 2 * HEAD_V, (hp + 1) * 2 * HEAD_V)
            q = qna_ref[rows, pair]
            o_na = o_m = None
            for j in range(2):
                hh = 2 * hp + j
                kt = _pair_slot(knat_ref[b, hh].astype(BF16), j)
                vt = _pair_slot(vnat_ref[b, hh].astype(BF16), j)
                o = _softmax_pv(_dot(q, kt), vt)
                o_na = o if j == 0 else o_na + o
                qk = slice(hh * MLA_QK_PAD, (hh + 1) * MLA_QK_PAD)
                vt = _pair_slot(vmt_ref[b, hh * HEAD_V:(hh + 1) * HEAD_V, :], j)
                o = _softmax_pv(_dot(qm_ref[rows, qk], kmt_ref[b, qk, :]), vt)
                o_m = o if j == 0 else o_m + o
            ona_ref[rows, pair] = o_na.astype(BF16)
            om_ref[rows, pair] = o_m.astype(BF16)


def _attn_ctx(qna, knat, vnat, qm, kmt, vmt, layer, seq_len, tm=1024):
    t = qna.shape[0]
    nb = tm // seq_len
    row = lambda n: pl.BlockSpec((tm, n), lambda i: (i, 0))
    cache_blk = pl.BlockSpec((nb, None, HEADS, NA_HD, seq_len), lambda i: (i, layer, 0, 0, 0))
    seq_blk = lambda n: pl.BlockSpec((nb, n, seq_len), lambda i: (i, 0, 0))
    return pl.pallas_call(
        functools.partial(_attn_ctx_body, seq_len=seq_len),
        out_shape=[jax.ShapeDtypeStruct((t, HD), BF16), jax.ShapeDtypeStruct((t, HD), BF16)],
        grid=(t // tm,),
        in_specs=[row(HD), cache_blk, cache_blk, row(MLA_QK_W), seq_blk(MLA_QK_W), seq_blk(HD)],
        out_specs=[row(HD), row(HD)],
        compiler_params=_params(1),
        name="attn_ctx",
    )(qna, knat, vnat, qm, kmt, vmt)


def _attn_lat_body(qna_ref, knat_ref, vnat_ref, kctx_ref, vctx_ref, bias_ref,
                   qm_ref, kmt_ref, vmt_ref, cckv_ref, ckrt_ref, wkat_ref, wkbt_ref, wuvt_ref, ona_ref, om_ref):
    cat = lambda *a: jnp.concatenate(a, axis=1)
    head = lambda j: slice(j * HEAD_V, (j + 1) * HEAD_V)
    kc = [_pair_slot(kctx_ref[j].astype(BF16), j) for j in range(2)]
    vc = [_pair_slot(vctx_ref[j].astype(BF16), j) for j in range(2)]
    for c, (start, count) in enumerate(NA_WINDOWS):
        rows = slice(c * Q_CHUNK, (c + 1) * Q_CHUNK)
        keys = slice(start, start + count)
        q = qna_ref[rows, :]
        for j in range(2):
            bias = jnp.concatenate(
                [cat(*[bias_ref[j, p] for p in NA_BLOCK_INDEX[c][rl]]) for rl in range(Q_CHUNK // GRID_W)], axis=0)
            s = cat(_dot(q, kc[j]), _dot(q, _pair_slot(knat_ref[head(j), keys], j)) + bias)
            o = _softmax_pv(s, cat(vc[j], _pair_slot(vnat_ref[head(j), keys], j)))
            o_na = o if j == 0 else o_na + o
        ona_ref[rows, :] = o_na.astype(BF16)
    cckv = cckv_ref[...].astype(BF16)
    km_ctx = (_dot_nt(wkat_ref[...], cckv) + _dot(wkbt_ref[...], ckrt_ref[...].astype(BF16))).astype(BF16)
    vm_ctx = _dot_nt(wuvt_ref[...], cckv).astype(BF16)
    kt, vt = [], []
    for j in range(2):
        qk = slice(j * MLA_QK_PAD, (j + 1) * MLA_QK_PAD)
        kt.append(cat(km_ctx[qk, :], kmt_ref[qk, :]))
        vt.append(_pair_slot(cat(vm_ctx[head(j), :], vmt_ref[head(j), :]), j))
    for c in range(DEC_SEQ // MLA_Q_ROWS):
        rows = slice(c * MLA_Q_ROWS, (c + 1) * MLA_Q_ROWS)
        for j in range(2):
            o = _softmax_pv(_dot(qm_ref[rows, j * MLA_QK_PAD:(j + 1) * MLA_QK_PAD], kt[j]), vt[j])
            o_m = o if j == 0 else o_m + o
        om_ref[rows, :] = o_m.astype(BF16)


def _attn_lat(qna, knat, vnat, kctx, vctx, bias, qm, kmt, vmt, cache_ckv, cache_krt, wkat, wkbt, wuvt, layer):
    t = qna.shape[0]
    s = DEC_SEQ
    return pl.pallas_call(
        _attn_lat_body,
        out_shape=[jax.ShapeDtypeStruct((t, HD), BF16), jax.ShapeDtypeStruct((t, HD), BF16)],
        grid=(HEADS // 2, t // s),
        in_specs=[pl.BlockSpec((s, 2 * NA_HD), lambda hp, b: (b, hp)),
                  pl.BlockSpec((None, 2 * NA_HD, s), lambda hp, b: (b, hp, 0)),
                  pl.BlockSpec((None, 2 * HEAD_V, s), lambda hp, b: (b, hp, 0)),
                  pl.BlockSpec((None, None, 2, NA_HD, PAST_LEN), lambda hp, b: (b, layer, hp, 0, 0)),
                  pl.BlockSpec((None, None, 2, NA_HD, PAST_LEN), lambda hp, b: (b, layer, hp, 0, 0)),
                  pl.BlockSpec((None, 2, len(NA_BLOCK_PAIRS), GRID_W, 2 * GRID_W),
                               lambda hp, b: (layer, hp, 0, 0, 0)),
                  pl.BlockSpec((s, 2 * MLA_QK_PAD), lambda hp, b: (b, hp)),
                  pl.BlockSpec((None, 2 * MLA_QK_PAD, s), lambda hp, b: (b, hp, 0)),
                  pl.BlockSpec((None, 2 * HEAD_V, s), lambda hp, b: (b, hp, 0)),
                  pl.BlockSpec((None, None, PAST_LEN, KV_LORA), lambda hp, b: (b, layer, 0, 0)),
                  pl.BlockSpec((None, None, MLA_ROPE, PAST_LEN), lambda hp, b: (b, layer, 0, 0)),
                  pl.BlockSpec((None, 2 * MLA_QK_PAD, KV_LORA), lambda hp, b: (layer, hp, 0)),
                  pl.BlockSpec((None, 2 * MLA_QK_PAD, MLA_ROPE), lambda hp, b: (layer, hp, 0)),
                  pl.BlockSpec((None, 2 * HEAD_V, KV_LORA), lambda hp, b: (layer, hp, 0))],
        out_specs=[pl.BlockSpec((s, 2 * HEAD_V), lambda hp, b: (b, hp)),
                   pl.BlockSpec((s, 2 * HEAD_V), lambda hp, b: (b, hp))],
        compiler_params=_params(2),
        name="attn_lat",
    )(qna, knat, vnat, kctx, vctx, bias, qm, kmt, vmt, cache_ckv, cache_krt, wkat, wkbt, wuvt)


def _mix_body(*refs, n_parts):
    tm = refs[-1].shape[0]
    parts, pos = [], 0
    for n in n_parts:
        parts.append(_rows_value(refs[pos:pos + n], tm))
        pos += n
    x, yc, ona, om = parts
    mod_ref, ng_ref, wgt_ref, wco_ref, wno_ref, wmo_ref, wo_ref, o_ref, z_scr = refs[pos:]
    mod = mod_ref[...]
    h = (_rms(x, ng_ref[...]) * (1 + mod[4:5]) + mod[3:4]).astype(BF16)
    for ch in range(D_MODEL // MIX_CHUNK):
        cols = slice(ch * MIX_CHUNK, (ch + 1) * MIX_CHUNK)
        gate = lambda k: jax.nn.sigmoid(
            _dot_nt(h, wgt_ref[k * D_MODEL + ch * MIX_CHUNK:k * D_MODEL + (ch + 1) * MIX_CHUNK, :]))
        z = gate(0) * _dot(yc, wco_ref[:, cols])
        z = z + gate(1) * _dot(ona, wno_ref[:, cols])
        z = z + gate(2) * _dot(om, wmo_ref[:, cols])
        z_scr[:, cols] = z.astype(BF16)
    o_ref[...] = x + mod[5:6] * _dot(z_scr[...], wo_ref[...])


def _mix(x, yc, ona, om, mod, w, layer):
    tm = 512 if isinstance(x, tuple) else 1024
    in_specs, args, n_parts = [], [], []
    for a in (x, yc, ona, om):
        specs, ops = _rows_specs(a, tm, 0)
        in_specs += specs
        args += ops
        n_parts.append(len(ops))
    in_specs += [_mod_spec(tm, 0),
                 _resident((None, 1, D_MODEL), (layer, 0, 0)),
                 _resident((3 * D_MODEL, D_MODEL), (0, 0)),
                 _resident((CONV_DIM, D_MODEL), (0, 0)),
                 _resident((HD, D_MODEL), (0, 0)),
                 _resident((HD, D_MODEL), (0, 0)),
                 _resident((D_MODEL, D_MODEL), (0, 0))]
    args += [mod, w["ng1"], w["wgt"], w["wco"], w["wno"], w["wmo"], w["wo"]]
    return pl.pallas_call(
        functools.partial(_mix_body, n_parts=tuple(n_parts)),
        out_shape=jax.ShapeDtypeStruct((N_ROWS, D_MODEL), F32),
        grid=(N_ROWS // tm,),
        in_specs=in_specs,
        out_specs=pl.BlockSpec((tm, D_MODEL), lambda i: (i, 0)),
        scratch_shapes=[pltpu.VMEM((tm, D_MODEL), BF16)],
        compiler_params=_params(1),
        name="mixer_out",
    )(*args)


def _rope_tables():
    f32 = np.float32
    half = MLA_ROPE // 2
    nf = half // 2
    inv = (f32(1.0) / (f32(ROPE_BASE) ** (np.arange(nf, dtype=f32) / f32(nf)))).astype(f32)
    t = np.arange(DEC_SEQ)
    rows = (t // GRID_W).astype(f32)[:, None] * inv[None, :]
    cols = (t % GRID_W).astype(f32)[:, None] * inv[None, :]
    cos = np.concatenate([np.cos(rows), np.cos(rows), np.cos(cols), np.cos(cols)], axis=-1).astype(f32)
    sin = np.concatenate([np.sin(rows), np.sin(rows), np.sin(cols), np.sin(cols)], axis=-1).astype(f32)
    pad = MLA_QK_PAD - MLA_NOPE - MLA_ROPE
    q_cos = np.concatenate([np.ones((DEC_SEQ, MLA_NOPE), f32), cos, np.zeros((DEC_SEQ, pad), f32)], axis=-1)
    q_sin = np.concatenate([np.zeros((DEC_SEQ, MLA_NOPE), f32), sin, np.zeros((DEC_SEQ, pad), f32)], axis=-1)
    return tuple(jnp.asarray(a) for a in (q_cos, q_sin, np.ascontiguousarray(cos.T), np.ascontiguousarray(sin.T)))


def _rope_swap(w):
    nf = MLA_ROPE // 4
    a, b, c, d = (w[..., i * nf:(i + 1) * nf] for i in range(4))
    return jnp.concatenate([-b, a, -d, c], axis=-1)


def _na_bias(rpb):
    n_dc = 2 * NA_WIN_C - 1
    col = np.arange(GRID_W)
    c_start = np.clip(col - NA_WIN_C // 2, 0, GRID_W - NA_WIN_C)
    c_in = (col[None, :] >= c_start[:, None]) & (col[None, :] < c_start[:, None] + NA_WIN_C)
    dc = np.clip(col[None, :] - col[:, None] + (NA_WIN_C - 1), 0, n_dc - 1)
    pick_dc = (dc[None] == np.arange(n_dc)[:, None, None]).astype(np.float32)
    n_pairs = len(NA_BLOCK_PAIRS)
    pick_dr = np.zeros((n_pairs, 2, NA_DR_MASKED), np.float32)
    for p, pair in enumerate(NA_BLOCK_PAIRS):
        for side, d in enumerate(pair):
            if d != NA_DR_MASKED:
                pick_dr[p, side, d] = 1.0
    keep = pick_dr.sum(-1).astype(bool)[:, None, :, None] & c_in[None, :, None, :]
    keep = keep.reshape(n_pairs, GRID_W, 2 * GRID_W)
    pick_side_dc = np.zeros((2, n_dc, GRID_W, 2, GRID_W), np.float32)
    for side in range(2):
        pick_side_dc[side, :, :, side, :] = pick_dc
    pick_side_dc = pick_side_dc.reshape(2, n_dc, GRID_W, 2 * GRID_W)
    hi = lax.Precision.HIGHEST
    by_row = jnp.einsum("psd,lhdj->lhpsj", jnp.asarray(pick_dr), rpb, precision=hi)
    blocks = jnp.einsum("lhpsj,sjqn->lhpqn", by_row, jnp.asarray(pick_side_dc), precision=hi)
    return jnp.where(jnp.asarray(keep), blocks * LOG2_E, NEG_INF)


def _pack_weights(w_int, w_uq, w_ukv):
    t_last = lambda a: jnp.swapaxes(a, -1, -2)
    w_mid = lax.optimization_barrier(w_int[:, W_IN_KV:W_IN_GATE]).astype(BF16)
    w_kvt = w_mid[:, :W_IN_LORA - W_IN_KV]
    w_krt = w_mid[:, W_IN_KR - W_IN_KV:]
    wt = jnp.concatenate([w_kvt, w_krt, t_last(_rope_swap(t_last(w_krt)))], axis=1)
    wbt = w_mid[:, W_IN_LORA - W_IN_KV:W_IN_KR - W_IN_KV]
    uq = w_uq.reshape(DEPTH, Q_LORA, MLA_HEADS, MLA_NOPE + MLA_ROPE)
    pad = MLA_QK_PAD - MLA_NOPE - MLA_ROPE
    zp = jnp.zeros(uq.shape[:-1] + (pad,), F32)
    zn = jnp.zeros(uq.shape[:-1] + (MLA_NOPE,), F32)
    q_ext = jnp.concatenate([uq, zp], axis=-1).reshape(DEPTH, Q_LORA, MLA_QK_W)
    q_sw = jnp.concatenate([zn, _rope_swap(uq[..., MLA_NOPE:]), zp], axis=-1).reshape(q_ext.shape)
    ukv = w_ukv.reshape(DEPTH, KV_LORA, MLA_HEADS, MLA_NOPE + MLA_V)
    zk = jnp.zeros(ukv.shape[:-1] + (MLA_QK_PAD - MLA_NOPE,), F32)
    wka = jnp.concatenate([ukv[..., :MLA_NOPE], zk], axis=-1).reshape(DEPTH, KV_LORA, MLA_QK_W)
    eye = jnp.concatenate([jnp.zeros((MLA_ROPE, MLA_NOPE), F32), jnp.eye(MLA_ROPE, dtype=F32),
                           jnp.zeros((MLA_ROPE, pad), F32)], axis=-1)
    wkb = jnp.broadcast_to(jnp.tile(eye, (1, MLA_HEADS))[None], (DEPTH, MLA_ROPE, MLA_QK_W))
    wuv = ukv[..., MLA_NOPE:].reshape(DEPTH, KV_LORA, HD)
    b = lambda a: a.astype(BF16)
    return dict(wt=b(wt), wbt=b(wbt),
                wq_lat=b(jnp.concatenate([q_ext, q_sw], axis=-1)), wq_ctx=b(q_ext),
                wkat=b(t_last(wka)), wkbt=b(t_last(wkb)), wuvt=b(t_last(wuv)))


def kernel(x_prompt, x_sample, cache_na_k, cache_na_v, cache_mla_ckv, cache_mla_krope, c, c_ctx,
           w_ada, b_ada, norm_g, w_ffn1_gate, w_ffn1_up, w_ffn1_down, w_ffn2_gate, w_ffn2_up, w_ffn2_down,
           w_in, conv_w, conv_b, na_rpb, mla_qnorm, w_uq, mla_kvnorm, w_ukv,
           w_conv_out, w_na_out, w_mla_out, w_o, final_g):
    b16 = lambda a: a.astype(BF16)
    t_last = lambda a: jnp.swapaxes(a, -1, -2)
    w_int = t_last(w_in)
    packed = _pack_weights(w_int, w_uq, w_ukv)
    shared = dict(conv_w=conv_w, conv_b=conv_b.reshape(DEPTH, 1, CONV_DIM),
                  qnorm=mla_qnorm.reshape(DEPTH, 1, Q_LORA), kvnorm=mla_kvnorm.reshape(DEPTH, 1, KV_LORA),
                  ng1=norm_g[:, 1:2],
                  **{k: packed[k] for k in ("wt", "wbt", "wkat", "wkbt", "wuvt")})
    ffn1_f32 = (w_ffn1_gate, w_ffn1_up, w_ffn1_down)
    ffn2_f32 = (w_ffn2_gate, w_ffn2_up, w_ffn2_down)
    ffn1_w = {0: tuple(b16(w[0]) for w in ffn1_f32)}
    ffn2_w = {}
    mixer_w = {}
    mixer_srcs = ((w_int, 0, W_IN_KV), (w_int, W_IN_GATE, 3 * D_MODEL),
                  (w_conv_out, 0, None), (w_na_out, 0, None), (w_mla_out, 0, None), (w_o, 0, None))
    final_row = final_g.reshape(1, D_MODEL)

    c_all = jnp.concatenate([c_ctx[None], c, jnp.zeros((MOD_ROWS - 1 - DEC_BATCH, D_MODEL), F32)], axis=0)
    mod = _modulation(c_all, w_ada, b_ada).reshape(DEPTH, MOD_ROWS, N_MOD, D_MODEL)

    tables = _rope_tables()
    na_bias = _na_bias(na_rpb)
    ctx_k_na = t_last(cache_na_k)
    ctx_v_na = t_last(cache_na_v)
    ctx_krt = t_last(cache_mla_krope)

    xp = x_prompt.reshape(N_CTX_ROWS, D_MODEL)
    xs = x_sample.reshape(N_ROWS - N_CTX_ROWS, D_MODEL)
    caches = None
    x_all = None
    for l in range(DEPTH):
        last = l == DEPTH - 1
        jobs = [_cast_job(w, l) for w in ffn2_f32]
        if l == 0:
            jobs += [_cast_job(a, ll, r0, n) for ll in range(DEPTH) for a, r0, n in mixer_srcs]
            xp, cast = _ffn(xp, mod[l], norm_g[l, 0:1], *ffn1_w[l], 0, n_rows=N_CTX_ROWS, cast_jobs=jobs, tm=512)
            xs, _ = _ffn(xs, mod[l], norm_g[l, 0:1], *ffn1_w[l], 0, row0=N_CTX_ROWS)
            x1 = (xp, xs)
        else:
            x1, cast = _ffn(x_all, mod[l], norm_g[l, 0:1], *ffn1_w[l], 0, cast_jobs=jobs)
            xp = xs = x1
        ffn2_w[l] = tuple(cast[:3])
        for ll in range(DEPTH if l == 0 else 0):
            names = ("wat", "wgt", "wco", "wno", "wmo", "wo")
            mixer_w[ll] = dict(zip(names, cast[3 + len(names) * ll:3 + len(names) * (ll + 1)]))
        w_ctx = dict(shared, wq=packed["wq_ctx"], **mixer_w[l])
        w_lat = dict(shared, wq=packed["wq_lat"], **mixer_w[l])
        yc_p, qna, knat, vnat, qm, kmt, vmt, ckv, krt = _proj(xp, mod[l], w_ctx, l, SEQ, False, None, caches, 1024)
        caches = (knat, vnat, ckv, krt)
        ona_p, om_p = _attn_ctx(qna, knat, vnat, qm, kmt, vmt, l, SEQ)
        yc_s, qna, knat, vnat, qm, kmt, vmt = _proj(xs, mod[l], w_lat, l, DEC_SEQ, True, tables, None, DEC_SEQ)
        ona_s, om_s = _attn_lat(qna, knat, vnat, ctx_k_na, ctx_v_na, na_bias, qm, kmt, vmt, cache_mla_ckv, ctx_krt,
                                packed["wkat"], packed["wkbt"], packed["wuvt"], l)
        x2 = _mix(x1, (yc_p, yc_s), (ona_p, ona_s), (om_p, om_s), mod[l], w_ctx, l)
        if not last:
            x_all, cast = _ffn(x2, mod[l], norm_g[l, 2:3], *ffn2_w[l], 6,
                               cast_jobs=[_cast_job(w, l + 1) for w in ffn1_f32])
            ffn1_w[l + 1] = tuple(cast)
        else:
            yp, _ = _ffn(x2, mod[l], norm_g[l, 2:3], *ffn2_w[l], 6, n_rows=N_CTX_ROWS, final_g=final_row)
            ys, _ = _ffn(x2, mod[l], norm_g[l, 2:3], *ffn2_w[l], 6, row0=N_CTX_ROWS, final_g=final_row)
    new_kt, new_vt, new_ckv, new_krt = caches
    return (yp.reshape(BATCH, SEQ, D_MODEL), ys.reshape(DEC_BATCH, DEC_SEQ, D_MODEL),
            t_last(new_kt), t_last(new_vt), new_ckv, t_last(new_krt))
```

```python
import functools
import math

import jax
import jax.numpy as jnp
import numpy as np
from jax import lax
from jax.experimental import pallas as pl
from jax.experimental.pallas import tpu as pltpu

D_MODEL = 1024
BATCH = 32
SEQ = 256
DEPTH = 2
DEC_BATCH = 8
DEC_SEQ = 1024
PAST_LEN = 256
GRID_W = 64
CONV_DIM = 512
CONV_K = 3
NA_HEADS = 8
NA_HD = 64
NA_WIN_R = 8
NA_WIN_C = 16
MLA_HEADS = 8
MLA_NOPE = 64
MLA_ROPE = 32
MLA_V = 64
Q_LORA = 256
KV_LORA = 128
FFN_DIM = 2816
N_MOD = 9
ROPE_BASE = 10000.0
EPS = 1e-6
NEG_INF = -1e30
LOG2_E = 1.4426950408889634
MLA_SCALE = (MLA_NOPE + MLA_ROPE) ** -0.5 * LOG2_E
NA_SCALE = NA_HD ** -0.5 * LOG2_E

N_CTX_ROWS = BATCH * SEQ
N_ROWS = N_CTX_ROWS + DEC_BATCH * DEC_SEQ
HEADS = 8
HEAD_V = 64
HD = HEADS * NA_HD
W_IN_KV = 3 * CONV_DIM + HD
W_IN_LORA = W_IN_KV + 2 * HD
W_IN_KR = W_IN_LORA + Q_LORA + KV_LORA
W_IN_GATE = W_IN_KR + MLA_ROPE
MLA_QK_PAD = 128
MLA_QK_W = HEADS * MLA_QK_PAD
FFN_CHUNK = 256
CONV_CHUNK = 256
MIX_CHUNK = 256
Q_CHUNK = 256
MLA_Q_ROWS = 128
MOD_ROWS = 16
VMEM_LIMIT = 56 * 1024 * 1024
NA_WINDOWS = ((0, 512), (0, 768), (256, 768), (512, 512))
NA_DR_MASKED = 2 * NA_WIN_R - 1


def _na_block_pairs():
    rows = DEC_SEQ // GRID_W
    r_start = np.clip(np.arange(rows) - NA_WIN_R // 2, 0, rows - NA_WIN_R)
    pairs, index = [], []
    for c, (start, count) in enumerate(NA_WINDOWS):
        index.append([])
        for rl in range(Q_CHUNK // GRID_W):
            r = c * (Q_CHUNK // GRID_W) + rl
            assert start // GRID_W <= r_start[r] and r_start[r] + NA_WIN_R <= (start + count) // GRID_W
            index[c].append([])
            for kp in range(count // (2 * GRID_W)):
                pair = []
                for rk in (start // GRID_W + 2 * kp, start // GRID_W + 2 * kp + 1):
                    inside = r_start[r] <= rk < r_start[r] + NA_WIN_R
                    pair.append(int(rk - r + NA_WIN_R - 1) if inside else NA_DR_MASKED)
                pair = tuple(pair)
                if pair not in pairs:
                    pairs.append(pair)
                index[c][rl].append(pairs.index(pair))
    return tuple(pairs), index


NA_BLOCK_PAIRS, NA_BLOCK_INDEX = _na_block_pairs()

BF16 = jnp.bfloat16
F32 = jnp.float32


def _dot(a, b):
    return jnp.dot(a, b, preferred_element_type=F32)


def _dot_nt(a, b):
    return lax.dot_general(a, b, (((1,), (1,)), ((), ())), preferred_element_type=F32)


def _rms(x, g):
    return x * lax.rsqrt(jnp.mean(x * x, axis=-1, keepdims=True) + EPS) * g


def _params(n_axes, flags=None):
    return pltpu.CompilerParams(dimension_semantics=("arbitrary",) * n_axes,
                                vmem_limit_bytes=VMEM_LIMIT, flags=flags)


def _resident(shape, index):
    return pl.BlockSpec(shape, lambda *_: index, pipeline_mode=pl.Buffered(1))


def _mod_body(c_ref, w_ref, b_ref, o_ref):
    c = c_ref[...]
    a = c * jax.nn.sigmoid(c)
    o_ref[...] = _dot(a.astype(BF16), w_ref[...].astype(BF16)) + b_ref[...]


def _modulation(c_all, w_ada, b_ada):
    n_col = N_MOD * D_MODEL
    tn = n_col // 4
    return pl.pallas_call(
        _mod_body,
        out_shape=jax.ShapeDtypeStruct((DEPTH, MOD_ROWS, n_col), F32),
        grid=(DEPTH, n_col // tn),
        in_specs=[pl.BlockSpec((MOD_ROWS, D_MODEL), lambda l, j: (0, 0)),
                  pl.BlockSpec((None, D_MODEL, tn), lambda l, j: (l, 0, j)),
                  pl.BlockSpec((None, 1, tn), lambda l, j: (l, 0, j))],
        out_specs=pl.BlockSpec((None, MOD_ROWS, tn), lambda l, j: (l, 0, j)),
        compiler_params=_params(2),
        name="modulation",
    )(c_all, w_ada, b_ada.reshape(DEPTH, 1, n_col))


def _rows_specs(x, tm, tile0):
    if isinstance(x, tuple):
        assert tile0 == 0
        n_ctx = x[0].shape[0] // tm
        width = x[0].shape[1]
        return [pl.BlockSpec((tm, width), lambda i: (jnp.minimum(i, n_ctx - 1), 0)),
                pl.BlockSpec((tm, width), lambda i: (jnp.maximum(i - n_ctx, 0), 0))], list(x)
    return [pl.BlockSpec((tm, x.shape[1]), lambda i: (i + tile0, 0))], [x]


def _rows_value(refs, tm):
    if len(refs) == 1:
        return refs[0][...]
    return jnp.where(pl.program_id(0) < N_CTX_ROWS // tm, refs[0][...], refs[1][...])


def _mod_spec(tm, tile0):
    def index(i):
        g = (i + tile0) * tm
        return jnp.where(g < N_CTX_ROWS, 0, 1 + (g - N_CTX_ROWS) // DEC_SEQ), 0, 0
    return pl.BlockSpec((None, N_MOD, D_MODEL), index)


def _ffn_body(*refs, mod_off, final, n_jobs, n_x):
    x_refs, (mod_ref, ng_ref, wg_ref, wu_ref, wd_ref), rest = refs[:n_x], refs[n_x:n_x + 5], refs[n_x + 5:]
    if final:
        fg_ref, rest = rest[0], rest[1:]
    job_in, o_ref, job_out, a_scr = rest[:n_jobs], rest[n_jobs], rest[n_jobs + 1:2 * n_jobs + 1], rest[-1]
    x = _rows_value(x_refs, o_ref.shape[0])
    mod = mod_ref[...]
    shift = mod[mod_off:mod_off + 1]
    scale = mod[mod_off + 1:mod_off + 2]
    gate = mod[mod_off + 2:mod_off + 3]
    h = (_rms(x, ng_ref[...]) * (1 + scale) + shift).astype(BF16)
    for f in range(FFN_DIM // FFN_CHUNK):
        cols = slice(f * FFN_CHUNK, (f + 1) * FFN_CHUNK)
        g = _dot(h, wg_ref[:, cols])
        u = _dot(h, wu_ref[:, cols])
        a_scr[:, cols] = (g * jax.nn.sigmoid(g) * u).astype(BF16)
    y = _dot(a_scr[...], wd_ref[...])
    out = x + 0.5 * gate * y
    if final:
        out = _rms(out, fg_ref[...])
    o_ref[...] = out
    for src, dst in zip(job_in, job_out):
        dst[...] = src[...].reshape(dst.shape).astype(BF16)


def _cast_job(arr, layer, row_start=0, n_rows=None):
    return arr, layer, row_start, arr.shape[1] if n_rows is None else n_rows


def _ffn(x, mod, ng, wg, wu, wd, mod_off, row0=0, n_rows=None, final_g=None, cast_jobs=(), tm=1024):
    t = N_ROWS - row0 if n_rows is None else n_rows
    n_steps = t // tm
    final = final_g is not None
    whole = isinstance(x, tuple) or x.shape[0] == N_ROWS
    x_specs, x_args = _rows_specs(x, tm, row0 // tm if whole else 0)
    in_specs = x_specs + [_mod_spec(tm, row0 // tm),
                          _resident((1, D_MODEL), (0, 0)),
                          _resident((D_MODEL, FFN_DIM), (0, 0)),
                          _resident((D_MODEL, FFN_DIM), (0, 0)),
                          _resident((FFN_DIM, D_MODEL), (0, 0))]
    args = x_args + [mod, ng, wg, wu, wd]
    if final:
        in_specs.append(_resident((1, D_MODEL), (0, 0)))
        args.append(final_g)
    out_shape = [jax.ShapeDtypeStruct((t, D_MODEL), F32)]
    out_specs = [pl.BlockSpec((tm, D_MODEL), lambda i: (i, 0))]
    for arr, layer, row_start, rows in cast_jobs:
        width = arr.shape[2]
        blk = rows // n_steps
        assert blk * n_steps == rows and blk % 16 == 0 and row_start % 16 == 0
        if row_start % blk == 0:
            spec = pl.BlockSpec((None, blk, width), lambda i, l=layer, b0=row_start // blk: (l, b0 + i, 0))
        else:
            g = math.gcd(row_start, blk)
            spec = pl.BlockSpec((pl.Element(1), pl.Element(blk), pl.Element(width)),
                                lambda i, l=layer, r0=row_start // g, n=blk // g, g=g: (l, (r0 + n * i) * g, 0))
        in_specs.append(spec)
        args.append(arr)
        out_shape.append(jax.ShapeDtypeStruct((rows, width), BF16))
        out_specs.append(pl.BlockSpec((blk, width), lambda i: (i, 0)))
    outs = pl.pallas_call(
        functools.partial(_ffn_body, mod_off=mod_off, final=final, n_jobs=len(cast_jobs), n_x=len(x_args)),
        out_shape=out_shape,
        grid=(n_steps,),
        in_specs=in_specs,
        out_specs=out_specs,
        scratch_shapes=[pltpu.VMEM((tm, FFN_DIM), BF16)],
        compiler_params=_params(1),
        name="ffn",
    )(*args)
    return outs[0], list(outs[1:])


def _proj_body(x_ref, mod_ref, ng_ref, wat_ref, wbt_ref, wt_ref, cw_ref, cb_ref, qn_ref, kvn_ref,
               wq_ref, wkat_ref, wkbt_ref, wuvt_ref, *rest, seq_len, latent, n_alias, own_slot):
    rest = rest[n_alias:]

    def put(ref, b, value):
        for k in range(ref.shape[1]):
            ref[b, k] = value if k == own_slot else jnp.zeros_like(value)

    if latent:
        (qc_ref, qs_ref, kct_ref, kst_ref,
         yc_ref, qna_ref, knat_ref, vnat_ref, qm_ref, kmt_ref, vmt_ref) = rest
    else:
        (yc_ref, qna_ref, knat_ref, vnat_ref, qm_ref, kmt_ref, vmt_ref, ckv_ref, krt_ref) = rest
    tm = x_ref.shape[0]
    x = x_ref[...]
    mod = mod_ref[...]
    h = (_rms(x, ng_ref[...]) * (1 + mod[4:5]) + mod[3:4]).astype(BF16)

    pos = lax.broadcasted_iota(jnp.int32, (tm, 1), 0) % seq_len
    cw = cw_ref[...]
    for ch in range(CONV_DIM // CONV_CHUNK):
        cols = slice(ch * CONV_CHUNK, (ch + 1) * CONV_CHUNK)
        part = lambda k: _dot_nt(h, wat_ref[k * CONV_DIM + ch * CONV_CHUNK:k * CONV_DIM + (ch + 1) * CONV_CHUNK, :])
        v = part(1) * part(2)
        v_prev = jnp.where(pos == 0, 0.0, pltpu.roll(v, 1, 0))
        v_next = jnp.where(pos == seq_len - 1, 0.0, pltpu.roll(v, tm - 1, 0))
        y = cb_ref[:, cols] + v_prev * cw[0:1, cols]
        y = y + v * cw[1:2, cols]
        y = y + v_next * cw[2:3, cols]
        yc_ref[:, cols] = (part(0) * y).astype(BF16)

    qna_ref[...] = (_dot_nt(h, wat_ref[3 * CONV_DIM:3 * CONV_DIM + HD, :]) * NA_SCALE).astype(BF16)

    kt_na = _dot_nt(wt_ref[0:HD, :], h)
    ut = _dot_nt(wt_ref[HD:, :], h)
    vt_na = ut[0:HD]
    krt = ut[HD:HD + MLA_ROPE]

    u = _dot_nt(h, wbt_ref[...])
    cq = _rms(u[:, 0:Q_LORA], qn_ref[...]).astype(BF16)
    ckv = _rms(u[:, Q_LORA:Q_LORA + KV_LORA], kvn_ref[...])
    ckv_b = ckv.astype(BF16)
    if latent:
        knat_ref[...] = kt_na.astype(BF16)
        vnat_ref[...] = vt_na.astype(BF16)
        krt = krt * kct_ref[...] + ut[HD + MLA_ROPE:HD + 2 * MLA_ROPE] * kst_ref[...]
        qc = jnp.concatenate([qc_ref[...]] * 2, axis=1)
        qs = jnp.concatenate([qs_ref[...]] * 2, axis=1)
        for hp in range(HEADS // 2):
            cols = slice(hp * 2 * MLA_QK_PAD, (hp + 1) * 2 * MLA_QK_PAD)
            sw_cols = slice(MLA_QK_W + hp * 2 * MLA_QK_PAD, MLA_QK_W + (hp + 1) * 2 * MLA_QK_PAD)
            q_rot = _dot(cq, wq_ref[:, cols]) * qc + _dot(cq, wq_ref[:, sw_cols]) * qs
            qm_ref[:, cols] = (q_rot * MLA_SCALE).astype(BF16)
    else:
        qm_ref[...] = (_dot(cq, wq_ref[...]) * MLA_SCALE).astype(BF16)
    kmt = _dot_nt(wkat_ref[...], ckv_b) + _dot(wkbt_ref[...], krt.astype(BF16))
    vmt = _dot_nt(wuvt_ref[...], ckv_b)
    if latent:
        kmt_ref[...] = kmt.astype(BF16)
        vmt_ref[...] = vmt.astype(BF16)
    else:
        for b in range(tm // seq_len):
            rows = slice(b * seq_len, (b + 1) * seq_len)
            put(knat_ref, b, kt_na[:, rows].reshape(HEADS, NA_HD, seq_len))
            put(vnat_ref, b, vt_na[:, rows].reshape(HEADS, NA_HD, seq_len))
            put(ckv_ref, b, ckv[rows])
            put(krt_ref, b, krt[:, rows])
            kmt_ref[b] = kmt[:, rows].astype(BF16)
            vmt_ref[b] = vmt[:, rows].astype(BF16)


def _proj(x, mod, w, layer, seq_len, latent, tables, caches, tm):
    row0 = N_CTX_ROWS if latent else 0
    t = N_ROWS - N_CTX_ROWS if latent else N_CTX_ROWS
    x_tile0 = row0 // tm if x.shape[0] == N_ROWS else 0
    n_seq = t // seq_len
    seq_per_tile = tm // seq_len
    wq_cols = w["wq"].shape[-1]
    wt_rows = w["wt"].shape[-2]
    row = lambda n: pl.BlockSpec((tm, n), lambda i: (i, 0))
    in_specs = [pl.BlockSpec((tm, D_MODEL), lambda i: (i + x_tile0, 0)),
                _mod_spec(tm, row0 // tm),
                _resident((None, 1, D_MODEL), (layer, 0, 0)),
                _resident((3 * CONV_DIM + HD, D_MODEL), (0, 0)),
                _resident((None, Q_LORA + KV_LORA, D_MODEL), (layer, 0, 0)),
                _resident((None, wt_rows, D_MODEL), (layer, 0, 0)),
                _resident((None, CONV_K, CONV_DIM), (layer, 0, 0)),
                _resident((None, 1, CONV_DIM), (layer, 0, 0)),
                _resident((None, 1, Q_LORA), (layer, 0, 0)),
                _resident((None, 1, KV_LORA), (layer, 0, 0)),
                _resident((None, Q_LORA, wq_cols), (layer, 0, 0)),
                _resident((None, MLA_QK_W, KV_LORA), (layer, 0, 0)),
                _resident((None, MLA_QK_W, MLA_ROPE), (layer, 0, 0)),
                _resident((None, HD, KV_LORA), (layer, 0, 0))]
    args = [x, mod, w["ng1"], w["wat"], w["wbt"], w["wt"], w["conv_w"], w["conv_b"], w["qnorm"], w["kvnorm"],
            w["wq"], w["wkat"], w["wkbt"], w["wuvt"]]
    out_shape = [jax.ShapeDtypeStruct((t, CONV_DIM), BF16),
                 jax.ShapeDtypeStruct((t, HD), BF16)]
    out_specs = [row(CONV_DIM), row(HD)]
    aliases = {}
    n_alias = 0
    own_slot = 0
    if latent:
        assert tm == seq_len
        in_specs += [_resident((seq_len, MLA_QK_PAD), (0, 0)), _resident((seq_len, MLA_QK_PAD), (0, 0)),
                     _resident((MLA_ROPE, seq_len), (0, 0)), _resident((MLA_ROPE, seq_len), (0, 0))]
        args += list(tables)
        seq_blk = lambda n: pl.BlockSpec((None, n, seq_len), lambda i: (i, 0, 0))
        out_shape += [jax.ShapeDtypeStruct((n_seq, HD, seq_len), BF16),
                      jax.ShapeDtypeStruct((n_seq, HD, seq_len), BF16),
                      jax.ShapeDtypeStruct((t, MLA_QK_W), BF16),
                      jax.ShapeDtypeStruct((n_seq, MLA_QK_W, seq_len), BF16),
                      jax.ShapeDtypeStruct((n_seq, HD, seq_len), BF16)]
        out_specs += [seq_blk(HD), seq_blk(HD), row(MLA_QK_W), seq_blk(MLA_QK_W), seq_blk(HD)]
    else:
        if caches is not None:
            n_alias = len(caches)
            in_specs += [pl.BlockSpec(memory_space=pl.ANY)] * n_alias
            args += list(caches)
            aliases = {len(args) - n_alias + k: 2 + (0, 1, 5, 6)[k] for k in range(n_alias)}
            n_slots, first_slot = 1, layer
        else:
            assert layer == 0
            n_slots, first_slot, own_slot = DEPTH, 0, layer
        cache_blk = lambda *dims: pl.BlockSpec((seq_per_tile, n_slots) + dims,
                                               lambda i: (i, first_slot) + (0,) * len(dims))
        seq_blk = lambda n: pl.BlockSpec((seq_per_tile, n, seq_len), lambda i: (i, 0, 0))
        out_shape += [jax.ShapeDtypeStruct((n_seq, DEPTH, HEADS, NA_HD, seq_len), F32),
                      jax.ShapeDtypeStruct((n_seq, DEPTH, HEADS, NA_HD, seq_len), F32),
                      jax.ShapeDtypeStruct((t, MLA_QK_W), BF16),
                      jax.ShapeDtypeStruct((n_seq, MLA_QK_W, seq_len), BF16),
                      jax.ShapeDtypeStruct((n_seq, HD, seq_len), BF16),
                      jax.ShapeDtypeStruct((n_seq, DEPTH, seq_len, KV_LORA), F32),
                      jax.ShapeDtypeStruct((n_seq, DEPTH, MLA_ROPE, seq_len), F32)]
        out_specs += [cache_blk(HEADS, NA_HD, seq_len), cache_blk(HEADS, NA_HD, seq_len), row(MLA_QK_W),
                      seq_blk(MLA_QK_W), seq_blk(HD), cache_blk(seq_len, KV_LORA), cache_blk(MLA_ROPE, seq_len)]
    return pl.pallas_call(
        functools.partial(_proj_body, seq_len=seq_len, latent=latent, n_alias=n_alias, own_slot=own_slot),
        out_shape=out_shape,
        grid=(t // tm,),
        in_specs=in_specs,
        out_specs=out_specs,
        input_output_aliases=aliases,
        compiler_params=_params(1),
        name="mixer_proj",
    )(*args)


def _softmax_pv(s, vt):
    m = jnp.max(s, axis=-1, keepdims=True)
    p = jnp.exp2(s - m)
    den = jnp.sum(p, axis=-1, keepdims=True)
    return _dot_nt(p.astype(BF16), vt) / den


def _pair_slot(x, j):
    z = jnp.zeros_like(x)
    return jnp.concatenate([x, z] if j == 0 else [z, x], axis=0)


def _attn_ctx_body(qna_ref, knat_ref, vnat_ref, qm_ref, kmt_ref, vmt_ref, ona_ref, om_ref, *, seq_len):
    for b in range(qna_ref.shape[0] // seq_len):
        rows = slice(b * seq_len, (b + 1) * seq_len)
        for hp in range(HEADS // 2):
            pair = slice(hp * 2 * HEAD_V, (hp + 1) * 2 * HEAD_V)
            q = qna_ref[rows, pair]
            o_na = o_m = None
            for j in range(2):
                hh = 2 * hp + j
                kt = _pair_slot(knat_ref[b, hh].astype(BF16), j)
                vt = _pair_slot(vnat_ref[b, hh].astype(BF16), j)
                o = _softmax_pv(_dot(q, kt), vt)
                o_na = o if j == 0 else o_na + o
                qk = slice(hh * MLA_QK_PAD, (hh + 1) * MLA_QK_PAD)
                vt = _pair_slot(vmt_ref[b, hh * HEAD_V:(hh + 1) * HEAD_V, :], j)
                o = _softmax_pv(_dot(qm_ref[rows, qk], kmt_ref[b, qk, :]), vt)
                o_m = o if j == 0 else o_m + o
            ona_ref[rows, pair] = o_na.astype(BF16)
            om_ref[rows, pair] = o_m.astype(BF16)


def _attn_ctx(qna, knat, vnat, qm, kmt, vmt, layer, seq_len, tm=1024):
    t = qna.shape[0]
    nb = tm // seq_len
    row = lambda n: pl.BlockSpec((tm, n), lambda i: (i, 0))
    cache_blk = pl.BlockSpec((nb, None, HEADS, NA_HD, seq_len), lambda i: (i, layer, 0, 0, 0))
    seq_blk = lambda n: pl.BlockSpec((nb, n, seq_len), lambda i: (i, 0, 0))
    return pl.pallas_call(
        functools.partial(_attn_ctx_body, seq_len=seq_len),
        out_shape=[jax.ShapeDtypeStruct((t, HD), BF16), jax.ShapeDtypeStruct((t, HD), BF16)],
        grid=(t // tm,),
        in_specs=[row(HD), cache_blk, cache_blk, row(MLA_QK_W), seq_blk(MLA_QK_W), seq_blk(HD)],
        out_specs=[row(HD), row(HD)],
        compiler_params=_params(1),
        name="attn_ctx",
    )(qna, knat, vnat, qm, kmt, vmt)


def _attn_lat_body(qna_ref, knat_ref, vnat_ref, kctx_ref, vctx_ref, bias_ref,
                   qm_ref, kmt_ref, vmt_ref, cckv_ref, ckrt_ref, wkat_ref, wkbt_ref, wuvt_ref, ona_ref, om_ref):
    cat = lambda *a: jnp.concatenate(a, axis=1)
    head = lambda j: slice(j * HEAD_V, (j + 1) * HEAD_V)
    kc = [_pair_slot(kctx_ref[j].astype(BF16), j) for j in range(2)]
    vc = [_pair_slot(vctx_ref[j].astype(BF16), j) for j in range(2)]
    for c, (start, count) in enumerate(NA_WINDOWS):
        rows = slice(c * Q_CHUNK, (c + 1) * Q_CHUNK)
        keys = slice(start, start + count)
        q = qna_ref[rows, :]
        for j in range(2):
            bias = jnp.concatenate(
                [cat(*[bias_ref[j, p] for p in NA_BLOCK_INDEX[c][rl]]) for rl in range(Q_CHUNK // GRID_W)], axis=0)
            s = cat(_dot(q, kc[j]), _dot(q, _pair_slot(knat_ref[head(j), keys], j)) + bias)
            o = _softmax_pv(s, cat(vc[j], _pair_slot(vnat_ref[head(j), keys], j)))
            o_na = o if j == 0 else o_na + o
        ona_ref[rows, :] = o_na.astype(BF16)
    cckv = cckv_ref[...].astype(BF16)
    km_ctx = (_dot_nt(wkat_ref[...], cckv) + _dot(wkbt_ref[...], ckrt_ref[...].astype(BF16))).astype(BF16)
    vm_ctx = _dot_nt(wuvt_ref[...], cckv).astype(BF16)
    kt, vt = [], []
    for j in range(2):
        qk = slice(j * MLA_QK_PAD, (j + 1) * MLA_QK_PAD)
        kt.append(cat(km_ctx[qk, :], kmt_ref[qk, :]))
        vt.append(_pair_slot(cat(vm_ctx[head(j), :], vmt_ref[head(j), :]), j))
    for c in range(DEC_SEQ // MLA_Q_ROWS):
        rows = slice(c * MLA_Q_ROWS, (c + 1) * MLA_Q_ROWS)
        for j in range(2):
            o = _softmax_pv(_dot(qm_ref[rows, j * MLA_QK_PAD:(j + 1) * MLA_QK_PAD], kt[j]), vt[j])
            o_m = o if j == 0 else o_m + o
        om_ref[rows, :] = o_m.astype(BF16)


def _attn_lat(qna, knat, vnat, kctx, vctx, bias, qm, kmt, vmt, cache_ckv, cache_krt, wkat, wkbt, wuvt, layer):
    t = qna.shape[0]
    s = DEC_SEQ
    return pl.pallas_call(
        _attn_lat_body,
        out_shape=[jax.ShapeDtypeStruct((t, HD), BF16), jax.ShapeDtypeStruct((t, HD), BF16)],
        grid=(HEADS // 2, t // s),
        in_specs=[pl.BlockSpec((s, 2 * NA_HD), lambda hp, b: (b, hp)),
                  pl.BlockSpec((None, 2 * NA_HD, s), lambda hp, b: (b, hp, 0)),
                  pl.BlockSpec((None, 2 * HEAD_V, s), lambda hp, b: (b, hp, 0)),
                  pl.BlockSpec((None, None, 2, NA_HD, PAST_LEN), lambda hp, b: (b, layer, hp, 0, 0)),
                  pl.BlockSpec((None, None, 2, NA_HD, PAST_LEN), lambda hp, b: (b, layer, hp, 0, 0)),
                  pl.BlockSpec((None, 2, len(NA_BLOCK_PAIRS), GRID_W, 2 * GRID_W),
                               lambda hp, b: (layer, hp, 0, 0, 0)),
                  pl.BlockSpec((s, 2 * MLA_QK_PAD), lambda hp, b: (b, hp)),
                  pl.BlockSpec((None, 2 * MLA_QK_PAD, s), lambda hp, b: (b, hp, 0)),
                  pl.BlockSpec((None, 2 * HEAD_V, s), lambda hp, b: (b, hp, 0)),
                  pl.BlockSpec((None, None, PAST_LEN, KV_LORA), lambda hp, b: (b, layer, 0, 0)),
                  pl.BlockSpec((None, None, MLA_ROPE, PAST_LEN), lambda hp, b: (b, layer, 0, 0)),
                  pl.BlockSpec((None, 2 * MLA_QK_PAD, KV_LORA), lambda hp, b: (layer, hp, 0)),
                  pl.BlockSpec((None, 2 * MLA_QK_PAD, MLA_ROPE), lambda hp, b: (layer, hp, 0)),
                  pl.BlockSpec((None, 2 * HEAD_V, KV_LORA), lambda hp, b: (layer, hp, 0))],
        out_specs=[pl.BlockSpec((s, 2 * HEAD_V), lambda hp, b: (b, hp)),
                   pl.BlockSpec((s, 2 * HEAD_V), lambda hp, b: (b, hp))],
        compiler_params=_params(2),
        name="attn_lat",
    )(qna, knat, vnat, kctx, vctx, bias, qm, kmt, vmt, cache_ckv, cache_krt, wkat, wkbt, wuvt)


def _mix_body(*refs, n_parts):
    tm = refs[-1].shape[0]
    parts, pos = [], 0
    for n in n_parts:
        parts.append(_rows_value(refs[pos:pos + n], tm))
        pos += n
    x, yc, ona, om = parts
    mod_ref, ng_ref, wgt_ref, wco_ref, wno_ref, wmo_ref, wo_ref, o_ref, z_scr = refs[pos:]
    mod = mod_ref[...]
    h = (_rms(x, ng_ref[...]) * (1 + mod[4:5]) + mod[3:4]).astype(BF16)
    for ch in range(D_MODEL // MIX_CHUNK):
        cols = slice(ch * MIX_CHUNK, (ch + 1) * MIX_CHUNK)
        gate = lambda k: jax.nn.sigmoid(
            _dot_nt(h, wgt_ref[k * D_MODEL + ch * MIX_CHUNK:k * D_MODEL + (ch + 1) * MIX_CHUNK, :]))
        z = gate(0) * _dot(yc, wco_ref[:, cols])
        z = z + gate(1) * _dot(ona, wno_ref[:, cols])
        z = z + gate(2) * _dot(om, wmo_ref[:, cols])
        z_scr[:, cols] = z.astype(BF16)
    o_ref[...] = x + mod[5:6] * _dot(z_scr[...], wo_ref[...])


def _mix(x, yc, ona, om, mod, w, layer):
    tm = 512 if isinstance(x, tuple) else 1024
    in_specs, args, n_parts = [], [], []
    for a in (x, yc, ona, om):
        specs, ops = _rows_specs(a, tm, 0)
        in_specs += specs
        args += ops
        n_parts.append(len(ops))
    in_specs += [_mod_spec(tm, 0),
                 _resident((None, 1, D_MODEL), (layer, 0, 0)),
                 _resident((3 * D_MODEL, D_MODEL), (0, 0)),
                 _resident((CONV_DIM, D_MODEL), (0, 0)),
                 _resident((HD, D_MODEL), (0, 0)),
                 _resident((HD, D_MODEL), (0, 0)),
                 _resident((D_MODEL, D_MODEL), (0, 0))]
    args += [mod, w["ng1"], w["wgt"], w["wco"], w["wno"], w["wmo"], w["wo"]]
    return pl.pallas_call(
        functools.partial(_mix_body, n_parts=tuple(n_parts)),
        out_shape=jax.ShapeDtypeStruct((N_ROWS, D_MODEL), F32),
        grid=(N_ROWS // tm,),
        in_specs=in_specs,
        out_specs=pl.BlockSpec((tm, D_MODEL), lambda i: (i, 0)),
        scratch_shapes=[pltpu.VMEM((tm, D_MODEL), BF16)],
        compiler_params=_params(1),
        name="mixer_out",
    )(*args)


def _rope_tables():
    f32 = np.float32
    half = MLA_ROPE // 2
    nf = half // 2
    inv = (f32(1.0) / (f32(ROPE_BASE) ** (np.arange(nf, dtype=f32) / f32(nf)))).astype(f32)
    t = np.arange(DEC_SEQ)
    rows = (t // GRID_W).astype(f32)[:, None] * inv[None, :]
    cols = (t % GRID_W).astype(f32)[:, None] * inv[None, :]
    cos = np.concatenate([np.cos(rows), np.cos(rows), np.cos(cols), np.cos(cols)], axis=-1).astype(f32)
    sin = np.concatenate([np.sin(rows), np.sin(rows), np.sin(cols), np.sin(cols)], axis=-1).astype(f32)
    pad = MLA_QK_PAD - MLA_NOPE - MLA_ROPE
    q_cos = np.concatenate([np.ones((DEC_SEQ, MLA_NOPE), f32), cos, np.zeros((DEC_SEQ, pad), f32)], axis=-1)
    q_sin = np.concatenate([np.zeros((DEC_SEQ, MLA_NOPE), f32), sin, np.zeros((DEC_SEQ, pad), f32)], axis=-1)
    return tuple(jnp.asarray(a) for a in (q_cos, q_sin, np.ascontiguousarray(cos.T), np.ascontiguousarray(sin.T)))


def _rope_swap(w):
    nf = MLA_ROPE // 4
    a, b, c, d = (w[..., i * nf:(i + 1) * nf] for i in range(4))
    return jnp.concatenate([-b, a, -d, c], axis=-1)


def _na_bias(rpb):
    n_dc = 2 * NA_WIN_C - 1
    col = np.arange(GRID_W)
    c_start = np.clip(col - NA_WIN_C // 2, 0, GRID_W - NA_WIN_C)
    c_in = (col[None, :] >= c_start[:, None]) & (col[None, :] < c_start[:, None] + NA_WIN_C)
    dc = np.clip(col[None, :] - col[:, None] + (NA_WIN_C - 1), 0, n_dc - 1)
    pick_dc = (dc[None] == np.arange(n_dc)[:, None, None]).astype(np.float32)
    n_pairs = len(NA_BLOCK_PAIRS)
    pick_dr = np.zeros((n_pairs, 2, NA_DR_MASKED), np.float32)
    for p, pair in enumerate(NA_BLOCK_PAIRS):
        for side, d in enumerate(pair):
            if d != NA_DR_MASKED:
                pick_dr[p, side, d] = 1.0
    keep = pick_dr.sum(-1).astype(bool)[:, None, :, None] & c_in[None, :, None, :]
    keep = keep.reshape(n_pairs, GRID_W, 2 * GRID_W)
    pick_side_dc = np.zeros((2, n_dc, GRID_W, 2, GRID_W), np.float32)
    for side in range(2):
        pick_side_dc[side, :, :, side, :] = pick_dc
    pick_side_dc = pick_side_dc.reshape(2, n_dc, GRID_W, 2 * GRID_W)
    hi = lax.Precision.HIGHEST
    by_row = jnp.einsum("psd,lhdj->lhpsj", jnp.asarray(pick_dr), rpb, precision=hi)
    blocks = jnp.einsum("lhpsj,sjqn->lhpqn", by_row, jnp.asarray(pick_side_dc), precision=hi)
    return jnp.where(jnp.asarray(keep), blocks * LOG2_E, NEG_INF)


def _pack_weights(w_int, w_uq, w_ukv):
    t_last = lambda a: jnp.swapaxes(a, -1, -2)
    w_mid = lax.optimization_barrier(w_int[:, W_IN_KV:W_IN_GATE]).astype(BF16)
    w_kvt = w_mid[:, :W_IN_LORA - W_IN_KV]
    w_krt = w_mid[:, W_IN_KR - W_IN_KV:]
    wt = jnp.concatenate([w_kvt, w_krt, t_last(_rope_swap(t_last(w_krt)))], axis=1)
    wbt = w_mid[:, W_IN_LORA - W_IN_KV:W_IN_KR - W_IN_KV]
    uq = w_uq.reshape(DEPTH, Q_LORA, MLA_HEADS, MLA_NOPE + MLA_ROPE)
    pad = MLA_QK_PAD - MLA_NOPE - MLA_ROPE
    zp = jnp.zeros(uq.shape[:-1] + (pad,), F32)
    zn = jnp.zeros(uq.shape[:-1] + (MLA_NOPE,), F32)
    q_ext = jnp.concatenate([uq, zp], axis=-1).reshape(DEPTH, Q_LORA, MLA_QK_W)
    q_sw = jnp.concatenate([zn, _rope_swap(uq[..., MLA_NOPE:]), zp], axis=-1).reshape(q_ext.shape)
    ukv = w_ukv.reshape(DEPTH, KV_LORA, MLA_HEADS, MLA_NOPE + MLA_V)
    zk = jnp.zeros(ukv.shape[:-1] + (MLA_QK_PAD - MLA_NOPE,), F32)
    wka = jnp.concatenate([ukv[..., :MLA_NOPE], zk], axis=-1).reshape(DEPTH, KV_LORA, MLA_QK_W)
    eye = jnp.concatenate([jnp.zeros((MLA_ROPE, MLA_NOPE), F32), jnp.eye(MLA_ROPE, dtype=F32),
                           jnp.zeros((MLA_ROPE, pad), F32)], axis=-1)
    wkb = jnp.broadcast_to(jnp.tile(eye, (1, MLA_HEADS))[None], (DEPTH, MLA_ROPE, MLA_QK_W))
    wuv = ukv[..., MLA_NOPE:].reshape(DEPTH, KV_LORA, HD)
    b = lambda a: a.astype(BF16)
    return dict(wt=b(wt), wbt=b(wbt),
                wq_lat=b(jnp.concatenate([q_ext, q_sw], axis=-1)), wq_ctx=b(q_ext),
                wkat=b(t_last(wka)), wkbt=b(t_last(wkb)), wuvt=b(t_last(wuv)))


def kernel(x_prompt, x_sample, cache_na_k, cache_na_v, cache_mla_ckv, cache_mla_krope, c, c_ctx,
           w_ada, b_ada, norm_g, w_ffn1_gate, w_ffn1_up, w_ffn1_down, w_ffn2_gate, w_ffn2_up, w_ffn2_down,
           w_in, conv_w, conv_b, na_rpb, mla_qnorm, w_uq, mla_kvnorm, w_ukv,
           w_conv_out, w_na_out, w_mla_out, w_o, final_g):
    b16 = lambda a: a.astype(BF16)
    t_last = lambda a: jnp.swapaxes(a, -1, -2)
    w_int = t_last(w_in)
    packed = _pack_weights(w_int, w_uq, w_ukv)
    shared = dict(conv_w=conv_w, conv_b=conv_b.reshape(DEPTH, 1, CONV_DIM),
                  qnorm=mla_qnorm.reshape(DEPTH, 1, Q_LORA), kvnorm=mla_kvnorm.reshape(DEPTH, 1, KV_LORA),
                  ng1=norm_g[:, 1:2],
                  **{k: packed[k] for k in ("wt", "wbt", "wkat", "wkbt", "wuvt")})
    ffn1_f32 = (w_ffn1_gate, w_ffn1_up, w_ffn1_down)
    ffn2_f32 = (w_ffn2_gate, w_ffn2_up, w_ffn2_down)
    ffn1_w = {0: tuple(b16(w[0]) for w in ffn1_f32)}
    ffn2_w = {}
    mixer_w = {}
    mixer_srcs = ((w_int, 0, W_IN_KV), (w_int, W_IN_GATE, 3 * D_MODEL),
                  (w_conv_out, 0, None), (w_na_out, 0, None), (w_mla_out, 0, None), (w_o, 0, None))
    final_row = final_g.reshape(1, D_MODEL)

    c_all = jnp.concatenate([c_ctx[None], c, jnp.zeros((MOD_ROWS - 1 - DEC_BATCH, D_MODEL), F32)], axis=0)
    mod = _modulation(c_all, w_ada, b_ada).reshape(DEPTH, MOD_ROWS, N_MOD, D_MODEL)

    tables = _rope_tables()
    na_bias = _na_bias(na_rpb)
    ctx_k_na = t_last(cache_na_k)
    ctx_v_na = t_last(cache_na_v)
    ctx_krt = t_last(cache_mla_krope)

    xp = x_prompt.reshape(N_CTX_ROWS, D_MODEL)
    xs = x_sample.reshape(N_ROWS - N_CTX_ROWS, D_MODEL)
    caches = None
    x_all = None
    for l in range(DEPTH):
        last = l == DEPTH - 1
        jobs = [_cast_job(w, l) for w in ffn2_f32]
        if l == 0:
            jobs += [_cast_job(a, ll, r0, n) for ll in range(DEPTH) for a, r0, n in mixer_srcs]
            xp, cast = _ffn(xp, mod[l], norm_g[l, 0:1], *ffn1_w[l], 0, n_rows=N_CTX_ROWS, cast_jobs=jobs, tm=512)
            xs, _ = _ffn(xs, mod[l], norm_g[l, 0:1], *ffn1_w[l], 0, row0=N_CTX_ROWS)
            x1 = (xp, xs)
        else:
            x1, cast = _ffn(x_all, mod[l], norm_g[l, 0:1], *ffn1_w[l], 0, cast_jobs=jobs)
            xp = xs = x1
        ffn2_w[l] = tuple(cast[:3])
        for ll in range(DEPTH if l == 0 else 0):
            names = ("wat", "wgt", "wco", "wno", "wmo", "wo")
            mixer_w[ll] = dict(zip(names, cast[3 + len(names) * ll:3 + len(names) * (ll + 1)]))
        w_ctx = dict(shared, wq=packed["wq_ctx"], **mixer_w[l])
        w_lat = dict(shared, wq=packed["wq_lat"], **mixer_w[l])
        yc_p, qna, knat, vnat, qm, kmt, vmt, ckv, krt = _proj(xp, mod[l], w_ctx, l, SEQ, False, None, caches, 1024)
        caches = (knat, vnat, ckv, krt)
        ona_p, om_p = _attn_ctx(qna, knat, vnat, qm, kmt, vmt, l, SEQ)
        yc_s, qna, knat, vnat, qm, kmt, vmt = _proj(xs, mod[l], w_lat, l, DEC_SEQ, True, tables, None, DEC_SEQ)
        ona_s, om_s = _attn_lat(qna, knat, vnat, ctx_k_na, ctx_v_na, na_bias, qm, kmt, vmt, cache_mla_ckv, ctx_krt,
                                packed["wkat"], packed["wkbt"], packed["wuvt"], l)
        x2 = _mix(x1, (yc_p, yc_s), (ona_p, ona_s), (om_p, om_s), mod[l], w_ctx, l)
        if not last:
            x_all, cast = _ffn(x2, mod[l], norm_g[l, 2:3], *ffn2_w[l], 6,
                               cast_jobs=[_cast_job(w, l + 1) for w in ffn1_f32])
            ffn1_w[l + 1] = tuple(cast)
        else:
            yp, _ = _ffn(x2, mod[l], norm_g[l, 2:3], *ffn2_w[l], 6, n_rows=N_CTX_ROWS, final_g=final_row)
            ys, _ = _ffn(x2, mod[l], norm_g[l, 2:3], *ffn2_w[l], 6, row0=N_CTX_ROWS, final_g=final_row)
    new_kt, new_vt, new_ckv, new_krt = caches
    return (yp.reshape(BATCH, SEQ, D_MODEL), ys.reshape(DEC_BATCH, DEC_SEQ, D_MODEL),
            t_last(new_kt), t_last(new_vt), new_ckv, t_last(new_krt))
```

```python
import functools
import math

import jax
import jax.numpy as jnp
import numpy as np
from jax import lax
from jax.experimental import pallas as pl
from jax.experimental.pallas import tpu as pltpu

D_MODEL = 1024
BATCH = 32
SEQ = 256
DEPTH = 2
DEC_BATCH = 8
DEC_SEQ = 1024
PAST_LEN = 256
GRID_W = 64
CONV_DIM = 512
CONV_K = 3
NA_HEADS = 8
NA_HD = 64
NA_WIN_R = 8
NA_WIN_C = 16
MLA_HEADS = 8
MLA_NOPE = 64
MLA_ROPE = 32
MLA_V = 64
Q_LORA = 256
KV_LORA = 128
FFN_DIM = 2816
N_MOD = 9
ROPE_BASE = 10000.0
EPS = 1e-6
NEG_INF = -1e30
LOG2_E = 1.4426950408889634
MLA_SCALE = (MLA_NOPE + MLA_ROPE) ** -0.5 * LOG2_E
NA_SCALE = NA_HD ** -0.5 * LOG2_E

N_CTX_ROWS = BATCH * SEQ
N_ROWS = N_CTX_ROWS + DEC_BATCH * DEC_SEQ
HEADS = 8
HEAD_V = 64
HD = HEADS * NA_HD
W_IN_KV = 3 * CONV_DIM + HD
W_IN_LORA = W_IN_KV + 2 * HD
W_IN_KR = W_IN_LORA + Q_LORA + KV_LORA
W_IN_GATE = W_IN_KR + MLA_ROPE
MLA_QK_PAD = 128
MLA_QK_W = HEADS * MLA_QK_PAD
FFN_CHUNK = 256
CONV_CHUNK = 256
MIX_CHUNK = 256
Q_CHUNK = 256
MLA_Q_ROWS = 128
LAT_HEADS = 4
MOD_ROWS = 16
VMEM_LIMIT = 56 * 1024 * 1024
NA_WINDOWS = ((0, 512), (0, 768), (256, 768), (512, 512))
NA_DR_MASKED = 2 * NA_WIN_R - 1


def _na_block_pairs():
    rows = DEC_SEQ // GRID_W
    r_start = np.clip(np.arange(rows) - NA_WIN_R // 2, 0, rows - NA_WIN_R)
    pairs, index = [], []
    for c, (start, count) in enumerate(NA_WINDOWS):
        index.append([])
        for rl in range(Q_CHUNK // GRID_W):
            r = c * (Q_CHUNK // GRID_W) + rl
            assert start // GRID_W <= r_start[r] and r_start[r] + NA_WIN_R <= (start + count) // GRID_W
            index[c].append([])
            for kp in range(count // (2 * GRID_W)):
                pair = []
                for rk in (start // GRID_W + 2 * kp, start // GRID_W + 2 * kp + 1):
                    inside = r_start[r] <= rk < r_start[r] + NA_WIN_R
                    pair.append(int(rk - r + NA_WIN_R - 1) if inside else NA_DR_MASKED)
                pair = tuple(pair)
                if pair not in pairs:
                    pairs.append(pair)
                index[c][rl].append(pairs.index(pair))
    return tuple(pairs), index


NA_BLOCK_PAIRS, NA_BLOCK_INDEX = _na_block_pairs()

BF16 = jnp.bfloat16
F32 = jnp.float32


def _dot(a, b):
    return jnp.dot(a, b, preferred_element_type=F32)


def _dot_nt(a, b):
    return lax.dot_general(a, b, (((1,), (1,)), ((), ())), preferred_element_type=F32)


def _rms(x, g):
    return x * lax.rsqrt(jnp.mean(x * x, axis=-1, keepdims=True) + EPS) * g


def _params(n_axes):
    return pltpu.CompilerParams(dimension_semantics=("arbitrary",) * n_axes,
                                vmem_limit_bytes=VMEM_LIMIT)


def _resident(shape, index):
    return pl.BlockSpec(shape, lambda *_: index, pipeline_mode=pl.Buffered(1))


def _mod_body(c_ref, w_ref, b_ref, o_ref):
    c = c_ref[...]
    a = c * jax.nn.sigmoid(c)
    o_ref[...] = _dot(a.astype(BF16), w_ref[...].astype(BF16)) + b_ref[...]


def _modulation(c_all, w_ada, b_ada):
    n_col = N_MOD * D_MODEL
    tn = n_col // 4
    return pl.pallas_call(
        _mod_body,
        out_shape=jax.ShapeDtypeStruct((DEPTH, MOD_ROWS, n_col), F32),
        grid=(DEPTH, n_col // tn),
        in_specs=[pl.BlockSpec((MOD_ROWS, D_MODEL), lambda l, j: (0, 0)),
                  pl.BlockSpec((None, D_MODEL, tn), lambda l, j: (l, 0, j)),
                  pl.BlockSpec((None, 1, tn), lambda l, j: (l, 0, j))],
        out_specs=pl.BlockSpec((None, MOD_ROWS, tn), lambda l, j: (l, 0, j)),
        compiler_params=_params(2),
        name="modulation",
    )(c_all, w_ada, b_ada.reshape(DEPTH, 1, n_col))


def _rows_specs(x, tm, tile0):
    if isinstance(x, tuple):
        assert tile0 == 0
        n_ctx = x[0].shape[0] // tm
        width = x[0].shape[1]
        return [pl.BlockSpec((tm, width), lambda i: (jnp.minimum(i, n_ctx - 1), 0)),
                pl.BlockSpec((tm, width), lambda i: (jnp.maximum(i - n_ctx, 0), 0))], list(x)
    return [pl.BlockSpec((tm, x.shape[1]), lambda i: (i + tile0, 0))], [x]


def _rows_value(refs, tm):
    if len(refs) == 1:
        return refs[0][...]
    return jnp.where(pl.program_id(0) < N_CTX_ROWS // tm, refs[0][...], refs[1][...])


def _mod_spec(tm, tile0):
    def index(i):
        g = (i + tile0) * tm
        return jnp.where(g < N_CTX_ROWS, 0, 1 + (g - N_CTX_ROWS) // DEC_SEQ), 0, 0
    return pl.BlockSpec((None, N_MOD, D_MODEL), index)


def _ffn_body(*refs, mod_off, final, n_jobs, n_x):
    x_refs, (mod_ref, ng_ref, wg_ref, wu_ref, wd_ref), rest = refs[:n_x], refs[n_x:n_x + 5], refs[n_x + 5:]
    if final:
        fg_ref, rest = rest[0], rest[1:]
    job_in, o_ref, job_out, a_scr = rest[:n_jobs], rest[n_jobs], rest[n_jobs + 1:2 * n_jobs + 1], rest[-1]
    x = _rows_value(x_refs, o_ref.shape[0])
    mod = mod_ref[...]
    shift = mod[mod_off:mod_off + 1]
    scale = mod[mod_off + 1:mod_off + 2]
    gate = mod[mod_off + 2:mod_off + 3]
    h = (_rms(x, ng_ref[...]) * (1 + scale) + shift).astype(BF16)
    for f in range(FFN_DIM // FFN_CHUNK):
        cols = slice(f * FFN_CHUNK, (f + 1) * FFN_CHUNK)
        g = _dot(h, wg_ref[:, cols])
        u = _dot(h, wu_ref[:, cols])
        a_scr[:, cols] = (g * jax.nn.sigmoid(g) * u).astype(BF16)
    y = _dot(a_scr[...], wd_ref[...])
    out = x + 0.5 * gate * y
    if final:
        out = _rms(out, fg_ref[...])
    o_ref[...] = out
    for src, dst in zip(job_in, job_out):
        dst[...] = src[...].reshape(dst.shape).astype(BF16)


def _cast_job(arr, layer, row_start=0, n_rows=None):
    return arr, layer, row_start, arr.shape[1] if n_rows is None else n_rows


def _ffn(x, mod, ng, wg, wu, wd, mod_off, row0=0, n_rows=None, final_g=None, cast_jobs=(), tm=1024):
    t = N_ROWS - row0 if n_rows is None else n_rows
    n_steps = t // tm
    final = final_g is not None
    whole = isinstance(x, tuple) or x.shape[0] == N_ROWS
    x_specs, x_args = _rows_specs(x, tm, row0 // tm if whole else 0)
    in_specs = x_specs + [_mod_spec(tm, row0 // tm),
                          _resident((1, D_MODEL), (0, 0)),
                          _resident((D_MODEL, FFN_DIM), (0, 0)),
                          _resident((D_MODEL, FFN_DIM), (0, 0)),
                          _resident((FFN_DIM, D_MODEL), (0, 0))]
    args = x_args + [mod, ng, wg, wu, wd]
    if final:
        in_specs.append(_resident((1, D_MODEL), (0, 0)))
        args.append(final_g)
    out_shape = [jax.ShapeDtypeStruct((t, D_MODEL), F32)]
    out_specs = [pl.BlockSpec((tm, D_MODEL), lambda i: (i, 0))]
    for arr, layer, row_start, rows in cast_jobs:
        width = arr.shape[2]
        blk = rows // n_steps
        assert blk * n_steps == rows and blk % 16 == 0 and row_start % 16 == 0
        if row_start % blk == 0:
            spec = pl.BlockSpec((None, blk, width), lambda i, l=layer, b0=row_start // blk: (l, b0 + i, 0))
        else:
            g = math.gcd(row_start, blk)
            spec = pl.BlockSpec((pl.Element(1), pl.Element(blk), pl.Element(width)),
                                lambda i, l=layer, r0=row_start // g, n=blk // g, g=g: (l, (r0 + n * i) * g, 0))
        in_specs.append(spec)
        args.append(arr)
        out_shape.append(jax.ShapeDtypeStruct((rows, width), BF16))
        out_specs.append(pl.BlockSpec((blk, width), lambda i: (i, 0)))
    outs = pl.pallas_call(
        functools.partial(_ffn_body, mod_off=mod_off, final=final, n_jobs=len(cast_jobs), n_x=len(x_args)),
        out_shape=out_shape,
        grid=(n_steps,),
        in_specs=in_specs,
        out_specs=out_specs,
        scratch_shapes=[pltpu.VMEM((tm, FFN_DIM), BF16)],
        compiler_params=_params(1),
        name="ffn",
    )(*args)
    return outs[0], list(outs[1:])


def _proj_body(x_ref, mod_ref, ng_ref, wat_ref, wbt_ref, wt_ref, cw_ref, cb_ref, qn_ref, kvn_ref,
               wq_ref, wkat_ref, wkbt_ref, wuvt_ref, *rest, seq_len, latent, n_alias, own_slot):
    rest = rest[n_alias:]

    def put(ref, b, value):
        for k in range(ref.shape[1]):
            ref[b, k] = value if k == own_slot else jnp.zeros_like(value)

    if latent:
        (qc_ref, qs_ref, kct_ref, kst_ref,
         yc_ref, qna_ref, knat_ref, vnat_ref, qm_ref, kmt_ref, vmt_ref) = rest
    else:
        (yc_ref, qna_ref, knat_ref, vnat_ref, qm_ref, kmt_ref, vmt_ref, ckv_ref, krt_ref) = rest
    tm = x_ref.shape[0]
    x = x_ref[...]
    mod = mod_ref[...]
    h = (_rms(x, ng_ref[...]) * (1 + mod[4:5]) + mod[3:4]).astype(BF16)

    pos = lax.broadcasted_iota(jnp.int32, (tm, 1), 0) % seq_len
    cw = cw_ref[...]
    for ch in range(CONV_DIM // CONV_CHUNK):
        cols = slice(ch * CONV_CHUNK, (ch + 1) * CONV_CHUNK)
        part = lambda k: _dot_nt(h, wat_ref[k * CONV_DIM + ch * CONV_CHUNK:k * CONV_DIM + (ch + 1) * CONV_CHUNK, :])
        v = part(1) * part(2)
        v_prev = jnp.where(pos == 0, 0.0, pltpu.roll(v, 1, 0))
        v_next = jnp.where(pos == seq_len - 1, 0.0, pltpu.roll(v, tm - 1, 0))
        y = cb_ref[:, cols] + v_prev * cw[0:1, cols]
        y = y + v * cw[1:2, cols]
        y = y + v_next * cw[2:3, cols]
        yc_ref[:, cols] = (part(0) * y).astype(BF16)

    qna_ref[...] = (_dot_nt(h, wat_ref[3 * CONV_DIM:3 * CONV_DIM + HD, :]) * NA_SCALE).astype(BF16)

    kt_na = _dot_nt(wt_ref[0:HD, :], h)
    ut = _dot_nt(wt_ref[HD:, :], h)
    vt_na = ut[0:HD]
    krt = ut[HD:HD + MLA_ROPE]

    u = _dot_nt(h, wbt_ref[...])
    cq = _rms(u[:, 0:Q_LORA], qn_ref[...]).astype(BF16)
    ckv = _rms(u[:, Q_LORA:Q_LORA + KV_LORA], kvn_ref[...])
    ckv_b = ckv.astype(BF16)
    if latent:
        knat_ref[...] = kt_na.astype(BF16)
        vnat_ref[...] = vt_na.astype(BF16)
        krt = krt * kct_ref[...] + ut[HD + MLA_ROPE:HD + 2 * MLA_ROPE] * kst_ref[...]
        qc = jnp.concatenate([qc_ref[...]] * 2, axis=1)
        qs = jnp.concatenate([qs_ref[...]] * 2, axis=1)
        for hp in range(HEADS // 2):
            cols = slice(hp * 2 * MLA_QK_PAD, (hp + 1) * 2 * MLA_QK_PAD)
            sw_cols = slice(MLA_QK_W + hp * 2 * MLA_QK_PAD, MLA_QK_W + (hp + 1) * 2 * MLA_QK_PAD)
            q_rot = _dot(cq, wq_ref[:, cols]) * qc + _dot(cq, wq_ref[:, sw_cols]) * qs
            qm_ref[:, cols] = (q_rot * MLA_SCALE).astype(BF16)
    else:
        qm_ref[...] = (_dot(cq, wq_ref[...]) * MLA_SCALE).astype(BF16)
    kmt = _dot_nt(wkat_ref[...], ckv_b) + _dot(wkbt_ref[...], krt.astype(BF16))
    vmt = _dot_nt(wuvt_ref[...], ckv_b)
    if latent:
        kmt_ref[...] = kmt.astype(BF16)
        vmt_ref[...] = vmt.astype(BF16)
    else:
        for b in range(tm // seq_len):
            rows = slice(b * seq_len, (b + 1) * seq_len)
            put(knat_ref, b, kt_na[:, rows].reshape(HEADS, NA_HD, seq_len))
            put(vnat_ref, b, vt_na[:, rows].reshape(HEADS, NA_HD, seq_len))
            put(ckv_ref, b, ckv[rows])
            put(krt_ref, b, krt[:, rows])
            kmt_ref[b] = kmt[:, rows].astype(BF16)
            vmt_ref[b] = vmt[:, rows].astype(BF16)


def _proj(x, mod, w, layer, seq_len, latent, tables, caches, tm):
    row0 = N_CTX_ROWS if latent else 0
    t = N_ROWS - N_CTX_ROWS if latent else N_CTX_ROWS
    x_tile0 = row0 // tm if x.shape[0] == N_ROWS else 0
    n_seq = t // seq_len
    seq_per_tile = tm // seq_len
    wq_cols = w["wq"].shape[-1]
    wt_rows = w["wt"].shape[-2]
    row = lambda n: pl.BlockSpec((tm, n), lambda i: (i, 0))
    in_specs = [pl.BlockSpec((tm, D_MODEL), lambda i: (i + x_tile0, 0)),
                _mod_spec(tm, row0 // tm),
                _resident((None, 1, D_MODEL), (layer, 0, 0)),
                _resident((3 * CONV_DIM + HD, D_MODEL), (0, 0)),
                _resident((None, Q_LORA + KV_LORA, D_MODEL), (layer, 0, 0)),
                _resident((None, wt_rows, D_MODEL), (layer, 0, 0)),
                _resident((None, CONV_K, CONV_DIM), (layer, 0, 0)),
                _resident((None, 1, CONV_DIM), (layer, 0, 0)),
                _resident((None, 1, Q_LORA), (layer, 0, 0)),
                _resident((None, 1, KV_LORA), (layer, 0, 0)),
                _resident((None, Q_LORA, wq_cols), (layer, 0, 0)),
                _resident((None, MLA_QK_W, KV_LORA), (layer, 0, 0)),
                _resident((None, MLA_QK_W, MLA_ROPE), (layer, 0, 0)),
                _resident((None, HD, KV_LORA), (layer, 0, 0))]
    args = [x, mod, w["ng1"], w["wat"], w["wbt"], w["wt"], w["conv_w"], w["conv_b"], w["qnorm"], w["kvnorm"],
            w["wq"], w["wkat"], w["wkbt"], w["wuvt"]]
    out_shape = [jax.ShapeDtypeStruct((t, CONV_DIM), BF16),
                 jax.ShapeDtypeStruct((t, HD), BF16)]
    out_specs = [row(CONV_DIM), row(HD)]
    aliases = {}
    n_alias = 0
    own_slot = 0
    if latent:
        assert tm == seq_len
        in_specs += [_resident((seq_len, MLA_QK_PAD), (0, 0)), _resident((seq_len, MLA_QK_PAD), (0, 0)),
                     _resident((MLA_ROPE, seq_len), (0, 0)), _resident((MLA_ROPE, seq_len), (0, 0))]
        args += list(tables)
        seq_blk = lambda n: pl.BlockSpec((None, n, seq_len), lambda i: (i, 0, 0))
        out_shape += [jax.ShapeDtypeStruct((n_seq, HD, seq_len), BF16),
                      jax.ShapeDtypeStruct((n_seq, HD, seq_len), BF16),
                      jax.ShapeDtypeStruct((t, MLA_QK_W), BF16),
                      jax.ShapeDtypeStruct((n_seq, MLA_QK_W, seq_len), BF16),
                      jax.ShapeDtypeStruct((n_seq, HD, seq_len), BF16)]
        out_specs += [seq_blk(HD), seq_blk(HD), row(MLA_QK_W), seq_blk(MLA_QK_W), seq_blk(HD)]
    else:
        if caches is not None:
            n_alias = len(caches)
            in_specs += [pl.BlockSpec(memory_space=pl.ANY)] * n_alias
            args += list(caches)
            aliases = {len(args) - n_alias + k: 2 + (0, 1, 5, 6)[k] for k in range(n_alias)}
            n_slots, first_slot = 1, layer
        else:
            assert layer == 0
            n_slots, first_slot, own_slot = DEPTH, 0, layer
        cache_blk = lambda *dims: pl.BlockSpec((seq_per_tile, n_slots) + dims,
                                               lambda i: (i, first_slot) + (0,) * len(dims))
        seq_blk = lambda n: pl.BlockSpec((seq_per_tile, n, seq_len), lambda i: (i, 0, 0))
        out_shape += [jax.ShapeDtypeStruct((n_seq, DEPTH, HEADS, NA_HD, seq_len), F32),
                      jax.ShapeDtypeStruct((n_seq, DEPTH, HEADS, NA_HD, seq_len), F32),
                      jax.ShapeDtypeStruct((t, MLA_QK_W), BF16),
                      jax.ShapeDtypeStruct((n_seq, MLA_QK_W, seq_len), BF16),
                      jax.ShapeDtypeStruct((n_seq, HD, seq_len), BF16),
                      jax.ShapeDtypeStruct((n_seq, DEPTH, seq_len, KV_LORA), F32),
                      jax.ShapeDtypeStruct((n_seq, DEPTH, MLA_ROPE, seq_len), F32)]
        out_specs += [cache_blk(HEADS, NA_HD, seq_len), cache_blk(HEADS, NA_HD, seq_len), row(MLA_QK_W),
                      seq_blk(MLA_QK_W), seq_blk(HD), cache_blk(seq_len, KV_LORA), cache_blk(MLA_ROPE, seq_len)]
    return pl.pallas_call(
        functools.partial(_proj_body, seq_len=seq_len, latent=latent, n_alias=n_alias, own_slot=own_slot),
        out_shape=out_shape,
        grid=(t // tm,),
        in_specs=in_specs,
        out_specs=out_specs,
        input_output_aliases=aliases,
        compiler_params=_params(1),
        name="mixer_proj",
    )(*args)


def _softmax_pv(s, vt):
    m = jnp.max(s, axis=-1, keepdims=True)
    p = jnp.exp2(s - m)
    den = jnp.sum(p, axis=-1, keepdims=True)
    return _dot_nt(p.astype(BF16), vt) / den


def _pair_slot(x, j):
    z = jnp.zeros_like(x)
    return jnp.concatenate([x, z] if j == 0 else [z, x], axis=0)


def _attn_ctx_body(qna_ref, knat_ref, vnat_ref, qm_ref, kmt_ref, vmt_ref, ona_ref, om_ref, *, seq_len):
    for b in range(qna_ref.shape[0] // seq_len):
        rows = slice(b * seq_len, (b + 1) * seq_len)
        for hp in range(HEADS // 2):
            pair = slice(hp * 2 * HEAD_V, (hp + 1) * 2 * HEAD_V)
            q = qna_ref[rows, pair]
            o_na = o_m = None
            for j in range(2):
                hh = 2 * hp + j
                kt = _pair_slot(knat_ref[b, hh].astype(BF16), j)
                vt = _pair_slot(vnat_ref[b, hh].astype(BF16), j)
                o = _softmax_pv(_dot(q, kt), vt)
                o_na = o if j == 0 else o_na + o
                qk = slice(hh * MLA_QK_PAD, (hh + 1) * MLA_QK_PAD)
                vt = _pair_slot(vmt_ref[b, hh * HEAD_V:(hh + 1) * HEAD_V, :], j)
                o = _softmax_pv(_dot(qm_ref[rows, qk], kmt_ref[b, qk, :]), vt)
                o_m = o if j == 0 else o_m + o
            ona_ref[rows, pair] = o_na.astype(BF16)
            om_ref[rows, pair] = o_m.astype(BF16)


def _attn_ctx(qna, knat, vnat, qm, kmt, vmt, layer, seq_len, tm=1024):
    t = qna.shape[0]
    nb = tm // seq_len
    row = lambda n: pl.BlockSpec((tm, n), lambda i: (i, 0))
    cache_blk = pl.BlockSpec((nb, None, HEADS, NA_HD, seq_len), lambda i: (i, layer, 0, 0, 0))
    seq_blk = lambda n: pl.BlockSpec((nb, n, seq_len), lambda i: (i, 0, 0))
    return pl.pallas_call(
        functools.partial(_attn_ctx_body, seq_len=seq_len),
        out_shape=[jax.ShapeDtypeStruct((t, HD), BF16), jax.ShapeDtypeStruct((t, HD), BF16)],
        grid=(t // tm,),
        in_specs=[row(HD), cache_blk, cache_blk, row(MLA_QK_W), seq_blk(MLA_QK_W), seq_blk(HD)],
        out_specs=[row(HD), row(HD)],
        compiler_params=_params(1),
        name="attn_ctx",
    )(qna, knat, vnat, qm, kmt, vmt)


def _attn_lat_body(qna_ref, knat_ref, vnat_ref, kctx_ref, vctx_ref, bias_ref,
                   qm_ref, kmt_ref, vmt_ref, cckv_ref, ckrt_ref, wkat_ref, wkbt_ref, wuvt_ref, ona_ref, om_ref):
    cat = lambda *a: jnp.concatenate(a, axis=1)
    head = lambda h: slice(h * HEAD_V, (h + 1) * HEAD_V)
    cckv = cckv_ref[...].astype(BF16)
    ckrt = ckrt_ref[...].astype(BF16)
    for pr in range(LAT_HEADS // 2):
        pair = slice(pr * 2 * HEAD_V, (pr + 1) * 2 * HEAD_V)
        kc = [_pair_slot(kctx_ref[2 * pr + j].astype(BF16), j) for j in range(2)]
        vc = [_pair_slot(vctx_ref[2 * pr + j].astype(BF16), j) for j in range(2)]
        for c, (start, count) in enumerate(NA_WINDOWS):
            rows = slice(c * Q_CHUNK, (c + 1) * Q_CHUNK)
            keys = slice(start, start + count)
            q = qna_ref[rows, pair]
            for j in range(2):
                h = 2 * pr + j
                bias = jnp.concatenate([cat(*[bias_ref[h, p] for p in NA_BLOCK_INDEX[c][rl]])
                                        for rl in range(Q_CHUNK // GRID_W)], axis=0)
                s = cat(_dot(q, kc[j]), _dot(q, _pair_slot(knat_ref[head(h), keys], j)) + bias)
                o = _softmax_pv(s, cat(vc[j], _pair_slot(vnat_ref[head(h), keys], j)))
                o_na = o if j == 0 else o_na + o
            ona_ref[rows, pair] = o_na.astype(BF16)
        qk_pair = slice(pr * 2 * MLA_QK_PAD, (pr + 1) * 2 * MLA_QK_PAD)
        km_ctx = (_dot_nt(wkat_ref[qk_pair, :], cckv) + _dot(wkbt_ref[qk_pair, :], ckrt)).astype(BF16)
        vm_ctx = _dot_nt(wuvt_ref[pair, :], cckv).astype(BF16)
        kt, vt = [], []
        for j in range(2):
            h = 2 * pr + j
            qk = slice(h * MLA_QK_PAD, (h + 1) * MLA_QK_PAD)
            kt.append(cat(km_ctx[j * MLA_QK_PAD:(j + 1) * MLA_QK_PAD, :], kmt_ref[qk, :]))
            vt.append(_pair_slot(cat(vm_ctx[head(j), :], vmt_ref[head(h), :]), j))
        for c in range(DEC_SEQ // MLA_Q_ROWS):
            rows = slice(c * MLA_Q_ROWS, (c + 1) * MLA_Q_ROWS)
            for j in range(2):
                h = 2 * pr + j
                o = _softmax_pv(_dot(qm_ref[rows, h * MLA_QK_PAD:(h + 1) * MLA_QK_PAD], kt[j]), vt[j])
                o_m = o if j == 0 else o_m + o
            om_ref[rows, pair] = o_m.astype(BF16)


def _attn_lat(qna, knat, vnat, kctx, vctx, bias, qm, kmt, vmt, cache_ckv, cache_krt, wkat, wkbt, wuvt, layer):
    t = qna.shape[0]
    s = DEC_SEQ
    n = LAT_HEADS
    return pl.pallas_call(
        _attn_lat_body,
        out_shape=[jax.ShapeDtypeStruct((t, HD), BF16), jax.ShapeDtypeStruct((t, HD), BF16)],
        grid=(HEADS // n, t // s),
        in_specs=[pl.BlockSpec((s, n * NA_HD), lambda hg, b: (b, hg)),
                  pl.BlockSpec((None, n * NA_HD, s), lambda hg, b: (b, hg, 0)),
                  pl.BlockSpec((None, n * HEAD_V, s), lambda hg, b: (b, hg, 0)),
                  pl.BlockSpec((None, None, n, NA_HD, PAST_LEN), lambda hg, b: (b, layer, hg, 0, 0)),
                  pl.BlockSpec((None, None, n, NA_HD, PAST_LEN), lambda hg, b: (b, layer, hg, 0, 0)),
                  pl.BlockSpec((None, n, len(NA_BLOCK_PAIRS), GRID_W, 2 * GRID_W),
                               lambda hg, b: (layer, hg, 0, 0, 0)),
                  pl.BlockSpec((s, n * MLA_QK_PAD), lambda hg, b: (b, hg)),
                  pl.BlockSpec((None, n * MLA_QK_PAD, s), lambda hg, b: (b, hg, 0)),
                  pl.BlockSpec((None, n * HEAD_V, s), lambda hg, b: (b, hg, 0)),
                  pl.BlockSpec((None, None, PAST_LEN, KV_LORA), lambda hg, b: (b, layer, 0, 0)),
                  pl.BlockSpec((None, None, MLA_ROPE, PAST_LEN), lambda hg, b: (b, layer, 0, 0)),
                  pl.BlockSpec((None, n * MLA_QK_PAD, KV_LORA), lambda hg, b: (layer, hg, 0)),
                  pl.BlockSpec((None, n * MLA_QK_PAD, MLA_ROPE), lambda hg, b: (layer, hg, 0)),
                  pl.BlockSpec((None, n * HEAD_V, KV_LORA), lambda hg, b: (layer, hg, 0))],
        out_specs=[pl.BlockSpec((s, n * HEAD_V), lambda hg, b: (b, hg)),
                   pl.BlockSpec((s, n * HEAD_V), lambda hg, b: (b, hg))],
        compiler_params=_params(2),
        name="attn_lat",
    )(qna, knat, vnat, kctx, vctx, bias, qm, kmt, vmt, cache_ckv, cache_krt, wkat, wkbt, wuvt)


def _mix_body(*refs, n_parts):
    tm = refs[-1].shape[0]
    parts, pos = [], 0
    for n in n_parts:
        parts.append(_rows_value(refs[pos:pos + n], tm))
        pos += n
    x, yc, ona, om = parts
    mod_ref, ng_ref, wgt_ref, wco_ref, wno_ref, wmo_ref, wo_ref, o_ref, z_scr = refs[pos:]
    mod = mod_ref[...]
    h = (_rms(x, ng_ref[...]) * (1 + mod[4:5]) + mod[3:4]).astype(BF16)
    for ch in range(D_MODEL // MIX_CHUNK):
        cols = slice(ch * MIX_CHUNK, (ch + 1) * MIX_CHUNK)
        gate = lambda k: jax.nn.sigmoid(
            _dot_nt(h, wgt_ref[k * D_MODEL + ch * MIX_CHUNK:k * D_MODEL + (ch + 1) * MIX_CHUNK, :]))
        z = gate(0) * _dot(yc, wco_ref[:, cols])
        z = z + gate(1) * _dot(ona, wno_ref[:, cols])
        z = z + gate(2) * _dot(om, wmo_ref[:, cols])
        z_scr[:, cols] = z.astype(BF16)
    o_ref[...] = x + mod[5:6] * _dot(z_scr[...], wo_ref[...])


def _mix(x, yc, ona, om, mod, w, layer):
    tm = 512 if isinstance(x, tuple) else 1024
    in_specs, args, n_parts = [], [], []
    for a in (x, yc, ona, om):
        specs, ops = _rows_specs(a, tm, 0)
        in_specs += specs
        args += ops
        n_parts.append(len(ops))
    in_specs += [_mod_spec(tm, 0),
                 _resident((None, 1, D_MODEL), (layer, 0, 0)),
                 _resident((3 * D_MODEL, D_MODEL), (0, 0)),
                 _resident((CONV_DIM, D_MODEL), (0, 0)),
                 _resident((HD, D_MODEL), (0, 0)),
                 _resident((HD, D_MODEL), (0, 0)),
                 _resident((D_MODEL, D_MODEL), (0, 0))]
    args += [mod, w["ng1"], w["wgt"], w["wco"], w["wno"], w["wmo"], w["wo"]]
    return pl.pallas_call(
        functools.partial(_mix_body, n_parts=tuple(n_parts)),
        out_shape=jax.ShapeDtypeStruct((N_ROWS, D_MODEL), F32),
        grid=(N_ROWS // tm,),
        in_specs=in_specs,
        out_specs=pl.BlockSpec((tm, D_MODEL), lambda i: (i, 0)),
        scratch_shapes=[pltpu.VMEM((tm, D_MODEL), BF16)],
        compiler_params=_params(1),
        name="mixer_out",
    )(*args)


def _rope_tables():
    f32 = np.float32
    half = MLA_ROPE // 2
    nf = half // 2
    inv = (f32(1.0) / (f32(ROPE_BASE) ** (np.arange(nf, dtype=f32) / f32(nf)))).astype(f32)
    t = np.arange(DEC_SEQ)
    rows = (t // GRID_W).astype(f32)[:, None] * inv[None, :]
    cols = (t % GRID_W).astype(f32)[:, None] * inv[None, :]
    cos = np.concatenate([np.cos(rows), np.cos(rows), np.cos(cols), np.cos(cols)], axis=-1).astype(f32)
    sin = np.concatenate([np.sin(rows), np.sin(rows), np.sin(cols), np.sin(cols)], axis=-1).astype(f32)
    pad = MLA_QK_PAD - MLA_NOPE - MLA_ROPE
    q_cos = np.concatenate([np.ones((DEC_SEQ, MLA_NOPE), f32), cos, np.zeros((DEC_SEQ, pad), f32)], axis=-1)
    q_sin = np.concatenate([np.zeros((DEC_SEQ, MLA_NOPE), f32), sin, np.zeros((DEC_SEQ, pad), f32)], axis=-1)
    return tuple(jnp.asarray(a) for a in (q_cos, q_sin, np.ascontiguousarray(cos.T), np.ascontiguousarray(sin.T)))


def _rope_swap(w):
    nf = MLA_ROPE // 4
    a, b, c, d = (w[..., i * nf:(i + 1) * nf] for i in range(4))
    return jnp.concatenate([-b, a, -d, c], axis=-1)


def _na_bias(rpb):
    n_dc = 2 * NA_WIN_C - 1
    col = np.arange(GRID_W)
    c_start = np.clip(col - NA_WIN_C // 2, 0, GRID_W - NA_WIN_C)
    c_in = (col[None, :] >= c_start[:, None]) & (col[None, :] < c_start[:, None] + NA_WIN_C)
    dc = np.clip(col[None, :] - col[:, None] + (NA_WIN_C - 1), 0, n_dc - 1)
    pick_dc = (dc[None] == np.arange(n_dc)[:, None, None]).astype(np.float32)
    n_pairs = len(NA_BLOCK_PAIRS)
    pick_dr = np.zeros((n_pairs, 2, NA_DR_MASKED), np.float32)
    for p, pair in enumerate(NA_BLOCK_PAIRS):
        for side, d in enumerate(pair):
            if d != NA_DR_MASKED:
                pick_dr[p, side, d] = 1.0
    keep = pick_dr.sum(-1).astype(bool)[:, None, :, None] & c_in[None, :, None, :]
    keep = keep.reshape(n_pairs, GRID_W, 2 * GRID_W)
    pick_side_dc = np.zeros((2, n_dc, GRID_W, 2, GRID_W), np.float32)
    for side in range(2):
        pick_side_dc[side, :, :, side, :] = pick_dc
    pick_side_dc = pick_side_dc.reshape(2, n_dc, GRID_W, 2 * GRID_W)
    hi = lax.Precision.HIGHEST
    by_row = jnp.einsum("psd,lhdj->lhpsj", jnp.asarray(pick_dr), rpb, precision=hi)
    blocks = jnp.einsum("lhpsj,sjqn->lhpqn", by_row, jnp.asarray(pick_side_dc), precision=hi)
    return jnp.where(jnp.asarray(keep), blocks * LOG2_E, NEG_INF)


def _pack_weights(w_int, w_uq, w_ukv):
    t_last = lambda a: jnp.swapaxes(a, -1, -2)
    w_mid = lax.optimization_barrier(w_int[:, W_IN_KV:W_IN_GATE]).astype(BF16)
    w_kvt = w_mid[:, :W_IN_LORA - W_IN_KV]
    w_krt = w_mid[:, W_IN_KR - W_IN_KV:]
    wt = jnp.concatenate([w_kvt, w_krt, t_last(_rope_swap(t_last(w_krt)))], axis=1)
    wbt = w_mid[:, W_IN_LORA - W_IN_KV:W_IN_KR - W_IN_KV]
    uq = w_uq.reshape(DEPTH, Q_LORA, MLA_HEADS, MLA_NOPE + MLA_ROPE)
    pad = MLA_QK_PAD - MLA_NOPE - MLA_ROPE
    zp = jnp.zeros(uq.shape[:-1] + (pad,), F32)
    zn = jnp.zeros(uq.shape[:-1] + (MLA_NOPE,), F32)
    q_ext = jnp.concatenate([uq, zp], axis=-1).reshape(DEPTH, Q_LORA, MLA_QK_W)
    q_sw = jnp.concatenate([zn, _rope_swap(uq[..., MLA_NOPE:]), zp], axis=-1).reshape(q_ext.shape)
    ukv = w_ukv.reshape(DEPTH, KV_LORA, MLA_HEADS, MLA_NOPE + MLA_V)
    zk = jnp.zeros(ukv.shape[:-1] + (MLA_QK_PAD - MLA_NOPE,), F32)
    wka = jnp.concatenate([ukv[..., :MLA_NOPE], zk], axis=-1).reshape(DEPTH, KV_LORA, MLA_QK_W)
    eye = jnp.concatenate([jnp.zeros((MLA_ROPE, MLA_NOPE), F32), jnp.eye(MLA_ROPE, dtype=F32),
                           jnp.zeros((MLA_ROPE, pad), F32)], axis=-1)
    wkb = jnp.broadcast_to(jnp.tile(eye, (1, MLA_HEADS))[None], (DEPTH, MLA_ROPE, MLA_QK_W))
    wuv = ukv[..., MLA_NOPE:].reshape(DEPTH, KV_LORA, HD)
    b = lambda a: a.astype(BF16)
    return dict(wt=b(wt), wbt=b(wbt),
                wq_lat=b(jnp.concatenate([q_ext, q_sw], axis=-1)), wq_ctx=b(q_ext),
                wkat=b(t_last(wka)), wkbt=b(t_last(wkb)), wuvt=b(t_last(wuv)))


def kernel(x_prompt, x_sample, cache_na_k, cache_na_v, cache_mla_ckv, cache_mla_krope, c, c_ctx,
           w_ada, b_ada, norm_g, w_ffn1_gate, w_ffn1_up, w_ffn1_down, w_ffn2_gate, w_ffn2_up, w_ffn2_down,
           w_in, conv_w, conv_b, na_rpb, mla_qnorm, w_uq, mla_kvnorm, w_ukv,
           w_conv_out, w_na_out, w_mla_out, w_o, final_g):
    b16 = lambda a: a.astype(BF16)
    t_last = lambda a: jnp.swapaxes(a, -1, -2)
    w_int = t_last(w_in)
    packed = _pack_weights(w_int, w_uq, w_ukv)
    shared = dict(conv_w=conv_w, conv_b=conv_b.reshape(DEPTH, 1, CONV_DIM),
                  qnorm=mla_qnorm.reshape(DEPTH, 1, Q_LORA), kvnorm=mla_kvnorm.reshape(DEPTH, 1, KV_LORA),
                  ng1=norm_g[:, 1:2],
                  **{k: packed[k] for k in ("wt", "wbt", "wkat", "wkbt", "wuvt")})
    ffn1_f32 = (w_ffn1_gate, w_ffn1_up, w_ffn1_down)
    ffn2_f32 = (w_ffn2_gate, w_ffn2_up, w_ffn2_down)
    ffn1_w = {0: tuple(b16(w[0]) for w in ffn1_f32)}
    ffn2_w = {}
    mixer_w = {}
    mixer_srcs = ((w_int, 0, W_IN_KV), (w_int, W_IN_GATE, 3 * D_MODEL),
                  (w_conv_out, 0, None), (w_na_out, 0, None), (w_mla_out, 0, None), (w_o, 0, None))
    final_row = final_g.reshape(1, D_MODEL)

    c_all = jnp.concatenate([c_ctx[None], c, jnp.zeros((MOD_ROWS - 1 - DEC_BATCH, D_MODEL), F32)], axis=0)
    mod = _modulation(c_all, w_ada, b_ada).reshape(DEPTH, MOD_ROWS, N_MOD, D_MODEL)

    tables = _rope_tables()
    na_bias = _na_bias(na_rpb)
    ctx_k_na = t_last(cache_na_k)
    ctx_v_na = t_last(cache_na_v)
    ctx_krt = t_last(cache_mla_krope)

    xp = x_prompt.reshape(N_CTX_ROWS, D_MODEL)
    xs = x_sample.reshape(N_ROWS - N_CTX_ROWS, D_MODEL)
    caches = None
    x_all = None
    for l in range(DEPTH):
        last = l == DEPTH - 1
        jobs = [_cast_job(w, l) for w in ffn2_f32]
        if l == 0:
            jobs += [_cast_job(a, ll, r0, n) for ll in range(DEPTH) for a, r0, n in mixer_srcs]
            xp, cast = _ffn(xp, mod[l], norm_g[l, 0:1], *ffn1_w[l], 0, n_rows=N_CTX_ROWS, cast_jobs=jobs, tm=512)
            xs, _ = _ffn(xs, mod[l], norm_g[l, 0:1], *ffn1_w[l], 0, row0=N_CTX_ROWS)
            x1 = (xp, xs)
        else:
            x1, cast = _ffn(x_all, mod[l], norm_g[l, 0:1], *ffn1_w[l], 0, cast_jobs=jobs)
            xp = xs = x1
        ffn2_w[l] = tuple(cast[:3])
        for ll in range(DEPTH if l == 0 else 0):
            names = ("wat", "wgt", "wco", "wno", "wmo", "wo")
            mixer_w[ll] = dict(zip(names, cast[3 + len(names) * ll:3 + len(names) * (ll + 1)]))
        w_ctx = dict(shared, wq=packed["wq_ctx"], **mixer_w[l])
        w_lat = dict(shared, wq=packed["wq_lat"], **mixer_w[l])
        yc_p, qna, knat, vnat, qm, kmt, vmt, ckv, krt = _proj(xp, mod[l], w_ctx, l, SEQ, False, None, caches, 1024)
        caches = (knat, vnat, ckv, krt)
        ona_p, om_p = _attn_ctx(qna, knat, vnat, qm, kmt, vmt, l, SEQ)
        yc_s, qna, knat, vnat, qm, kmt, vmt = _proj(xs, mod[l], w_lat, l, DEC_SEQ, True, tables, None, DEC_SEQ)
        ona_s, om_s = _attn_lat(qna, knat, vnat, ctx_k_na, ctx_v_na, na_bias, qm, kmt, vmt, cache_mla_ckv, ctx_krt,
                                packed["wkat"], packed["wkbt"], packed["wuvt"], l)
        x2 = _mix(x1, (yc_p, yc_s), (ona_p, ona_s), (om_p, om_s), mod[l], w_ctx, l)
        if not last:
            x_all, cast = _ffn(x2, mod[l], norm_g[l, 2:3], *ffn2_w[l], 6,
                               cast_jobs=[_cast_job(w, l + 1) for w in ffn1_f32])
            ffn1_w[l + 1] = tuple(cast)
        else:
            yp, _ = _ffn(x2, mod[l], norm_g[l, 2:3], *ffn2_w[l], 6, n_rows=N_CTX_ROWS, final_g=final_row)
            ys, _ = _ffn(x2, mod[l], norm_g[l, 2:3], *ffn2_w[l], 6, row0=N_CTX_ROWS, final_g=final_row)
    new_kt, new_vt, new_ckv, new_krt = caches
    return (yp.reshape(BATCH, SEQ, D_MODEL), ys.reshape(DEC_BATCH, DEC_SEQ, D_MODEL),
            t_last(new_kt), t_last(new_vt), new_ckv, t_last(new_krt))
```

```python
import functools
import math

import jax
import jax.numpy as jnp
import numpy as np
from jax import lax
from jax.experimental import pallas as pl
from jax.experimental.pallas import tpu as pltpu

D_MODEL = 1024
BATCH = 32
SEQ = 256
DEPTH = 2
DEC_BATCH = 8
DEC_SEQ = 1024
PAST_LEN = 256
GRID_W = 64
CONV_DIM = 512
CONV_K = 3
NA_HEADS = 8
NA_HD = 64
NA_WIN_R = 8
NA_WIN_C = 16
MLA_HEADS = 8
MLA_NOPE = 64
MLA_ROPE = 32
MLA_V = 64
Q_LORA = 256
KV_LORA = 128
FFN_DIM = 2816
N_MOD = 9
ROPE_BASE = 10000.0
EPS = 1e-6
NEG_INF = -1e30
LOG2_E = 1.4426950408889634
MLA_SCALE = (MLA_NOPE + MLA_ROPE) ** -0.5 * LOG2_E
NA_SCALE = NA_HD ** -0.5 * LOG2_E

N_CTX_ROWS = BATCH * SEQ
N_ROWS = N_CTX_ROWS + DEC_BATCH * DEC_SEQ
HEADS = 8
HEAD_V = 64
HD = HEADS * NA_HD
W_IN_KV = 3 * CONV_DIM + HD
W_IN_LORA = W_IN_KV + 2 * HD
W_IN_KR = W_IN_LORA + Q_LORA + KV_LORA
W_IN_GATE = W_IN_KR + MLA_ROPE
MLA_QK_PAD = 128
MLA_QK_W = HEADS * MLA_QK_PAD
FFN_CHUNK = 256
CONV_CHUNK = 256
MIX_CHUNK = 256
Q_CHUNK = 256
MLA_Q_ROWS = 128
LAT_HEADS = 4
MOD_ROWS = 16
VMEM_LIMIT = 56 * 1024 * 1024
NA_WINDOWS = ((0, 512), (0, 768), (256, 768), (512, 512))
NA_DR_MASKED = 2 * NA_WIN_R - 1


def _na_block_pairs():
    rows = DEC_SEQ // GRID_W
    r_start = np.clip(np.arange(rows) - NA_WIN_R // 2, 0, rows - NA_WIN_R)
    pairs, index = [], []
    for c, (start, count) in enumerate(NA_WINDOWS):
        index.append([])
        for rl in range(Q_CHUNK // GRID_W):
            r = c * (Q_CHUNK // GRID_W) + rl
            assert start // GRID_W <= r_start[r] and r_start[r] + NA_WIN_R <= (start + count) // GRID_W
            index[c].append([])
            for kp in range(count // (2 * GRID_W)):
                pair = []
                for rk in (start // GRID_W + 2 * kp, start // GRID_W + 2 * kp + 1):
                    inside = r_start[r] <= rk < r_start[r] + NA_WIN_R
                    pair.append(int(rk - r + NA_WIN_R - 1) if inside else NA_DR_MASKED)
                pair = tuple(pair)
                if pair not in pairs:
                    pairs.append(pair)
                index[c][rl].append(pairs.index(pair))
    return tuple(pairs), index


NA_BLOCK_PAIRS, NA_BLOCK_INDEX = _na_block_pairs()

BF16 = jnp.bfloat16
F32 = jnp.float32


def _dot(a, b):
    return jnp.dot(a, b, preferred_element_type=F32)


def _dot_nt(a, b):
    return lax.dot_general(a, b, (((1,), (1,)), ((), ())), preferred_element_type=F32)


def _rms(x, g):
    return x * lax.rsqrt(jnp.mean(x * x, axis=-1, keepdims=True) + EPS) * g


def _params(n_axes):
    return pltpu.CompilerParams(dimension_semantics=("arbitrary",) * n_axes,
                                vmem_limit_bytes=VMEM_LIMIT)


def _resident(shape, index):
    return pl.BlockSpec(shape, lambda *_: index, pipeline_mode=pl.Buffered(1))


def _mod_body(c_ref, w_ref, b_ref, o_ref):
    c = c_ref[...]
    a = c * jax.nn.sigmoid(c)
    o_ref[...] = _dot(a.astype(BF16), w_ref[...].astype(BF16)) + b_ref[...]


def _mod_specs(layer, n_blocks):
    tn = N_MOD * D_MODEL // n_blocks
    assert tn % 128 == 0
    return ([pl.BlockSpec((MOD_ROWS, D_MODEL), lambda j: (0, 0)),
             pl.BlockSpec((None, D_MODEL, tn), lambda j: (layer, 0, j)),
             pl.BlockSpec((None, 1, tn), lambda j: (layer, 0, j))],
            pl.BlockSpec((MOD_ROWS, tn), lambda j: (0, j)))


def _modulation(c_all, w_ada, b_ada, layer):
    n_blocks = 4
    in_specs, out_spec = _mod_specs(layer, n_blocks)
    return pl.pallas_call(
        _mod_body,
        out_shape=jax.ShapeDtypeStruct((MOD_ROWS, N_MOD * D_MODEL), F32),
        grid=(n_blocks,),
        in_specs=in_specs,
        out_specs=out_spec,
        compiler_params=_params(1),
        name="modulation",
    )(c_all, w_ada, b_ada)


def _rows_specs(x, tm, tile0):
    if isinstance(x, tuple):
        assert tile0 == 0
        n_ctx = x[0].shape[0] // tm
        width = x[0].shape[1]
        return [pl.BlockSpec((tm, width), lambda i: (jnp.minimum(i, n_ctx - 1), 0)),
                pl.BlockSpec((tm, width), lambda i: (jnp.maximum(i - n_ctx, 0), 0))], list(x)
    return [pl.BlockSpec((tm, x.shape[1]), lambda i: (i + tile0, 0))], [x]


def _rows_value(refs, tm):
    if len(refs) == 1:
        return refs[0][...]
    return jnp.where(pl.program_id(0) < N_CTX_ROWS // tm, refs[0][...], refs[1][...])


def _mod_spec(tm, tile0):
    def index(i):
        g = (i + tile0) * tm
        return jnp.where(g < N_CTX_ROWS, 0, 1 + (g - N_CTX_ROWS) // DEC_SEQ), 0, 0
    return pl.BlockSpec((None, N_MOD, D_MODEL), index)


def _ffn_body(*refs, mod_off, final, n_jobs, n_x, mod_job):
    x_refs, (mod_ref, ng_ref, wg_ref, wu_ref, wd_ref), rest = refs[:n_x], refs[n_x:n_x + 5], refs[n_x + 5:]
    if final:
        fg_ref, rest = rest[0], rest[1:]
    job_in, rest = rest[:n_jobs], rest[n_jobs:]
    if mod_job:
        next_mod_in, rest = rest[:3], rest[3:]
    o_ref, job_out, a_scr = rest[0], rest[1:n_jobs + 1], rest[-1]
    x = _rows_value(x_refs, o_ref.shape[0])
    mod = mod_ref[...]
    shift = mod[mod_off:mod_off + 1]
    scale = mod[mod_off + 1:mod_off + 2]
    gate = mod[mod_off + 2:mod_off + 3]
    h = (_rms(x, ng_ref[...]) * (1 + scale) + shift).astype(BF16)
    for f in range(FFN_DIM // FFN_CHUNK):
        cols = slice(f * FFN_CHUNK, (f + 1) * FFN_CHUNK)
        g = _dot(h, wg_ref[:, cols])
        u = _dot(h, wu_ref[:, cols])
        a_scr[:, cols] = (g * jax.nn.sigmoid(g) * u).astype(BF16)
    y = _dot(a_scr[...], wd_ref[...])
    out = x + 0.5 * gate * y
    if final:
        out = _rms(out, fg_ref[...])
    o_ref[...] = out
    for src, dst in zip(job_in, job_out):
        dst[...] = src[...].reshape(dst.shape).astype(BF16)
    if mod_job:
        _mod_body(*next_mod_in, rest[n_jobs + 1])


def _cast_job(arr, layer, row_start=0, n_rows=None):
    return arr, layer, row_start, arr.shape[1] if n_rows is None else n_rows


def _ffn(x, mod, ng, wg, wu, wd, mod_off, row0=0, n_rows=None, final_g=None, cast_jobs=(), next_mod=None, tm=1024):
    t = N_ROWS - row0 if n_rows is None else n_rows
    n_steps = t // tm
    final = final_g is not None
    whole = isinstance(x, tuple) or x.shape[0] == N_ROWS
    x_specs, x_args = _rows_specs(x, tm, row0 // tm if whole else 0)
    in_specs = x_specs + [_mod_spec(tm, row0 // tm),
                          _resident((1, D_MODEL), (0, 0)),
                          _resident((D_MODEL, FFN_DIM), (0, 0)),
                          _resident((D_MODEL, FFN_DIM), (0, 0)),
                          _resident((FFN_DIM, D_MODEL), (0, 0))]
    args = x_args + [mod, ng, wg, wu, wd]
    if final:
        in_specs.append(_resident((1, D_MODEL), (0, 0)))
        args.append(final_g)
    out_shape = [jax.ShapeDtypeStruct((t, D_MODEL), F32)]
    out_specs = [pl.BlockSpec((tm, D_MODEL), lambda i: (i, 0))]
    for arr, layer, row_start, rows in cast_jobs:
        width = arr.shape[2]
        blk = rows // n_steps
        assert blk * n_steps == rows and blk % 16 == 0 and row_start % 16 == 0
        if row_start % blk == 0:
            spec = pl.BlockSpec((None, blk, width), lambda i, l=layer, b0=row_start // blk: (l, b0 + i, 0))
        else:
            g = math.gcd(row_start, blk)
            spec = pl.BlockSpec((pl.Element(1), pl.Element(blk), pl.Element(width)),
                                lambda i, l=layer, r0=row_start // g, n=blk // g, g=g: (l, (r0 + n * i) * g, 0))
        in_specs.append(spec)
        args.append(arr)
        out_shape.append(jax.ShapeDtypeStruct((rows, width), BF16))
        out_specs.append(pl.BlockSpec((blk, width), lambda i: (i, 0)))
    if next_mod is not None:
        mod_in, mod_out = _mod_specs(next_mod[3], n_steps)
        in_specs += mod_in
        args += list(next_mod[:3])
        out_shape.append(jax.ShapeDtypeStruct((MOD_ROWS, N_MOD * D_MODEL), F32))
        out_specs.append(mod_out)
    outs = pl.pallas_call(
        functools.partial(_ffn_body, mod_off=mod_off, final=final, n_jobs=len(cast_jobs), n_x=len(x_args),
                          mod_job=next_mod is not None),
        out_shape=out_shape,
        grid=(n_steps,),
        in_specs=in_specs,
        out_specs=out_specs,
        scratch_shapes=[pltpu.VMEM((tm, FFN_DIM), BF16)],
        compiler_params=_params(1),
        name="ffn",
    )(*args)
    return outs[0], list(outs[1:])


def _proj_body(x_ref, mod_ref, ng_ref, wat_ref, wbt_ref, wt_ref, cw_ref, cb_ref, qn_ref, kvn_ref,
               wq_ref, wkat_ref, wkbt_ref, wuvt_ref, *rest, seq_len, latent, n_alias, own_slot):
    rest = rest[n_alias:]

    def put(ref, b, value):
        for k in range(ref.shape[1]):
            ref[b, k] = value if k == own_slot else jnp.zeros_like(value)

    if latent:
        (qc_ref, qs_ref, kct_ref, kst_ref,
         yc_ref, qna_ref, knat_ref, vnat_ref, qm_ref, kmt_ref, vmt_ref) = rest
    else:
        (yc_ref, qna_ref, knat_ref, vnat_ref, qm_ref, kmt_ref, vmt_ref, ckv_ref, krt_ref) = rest
    tm = x_ref.shape[0]
    x = x_ref[...]
    mod = mod_ref[...]
    h = (_rms(x, ng_ref[...]) * (1 + mod[4:5]) + mod[3:4]).astype(BF16)

    pos = lax.broadcasted_iota(jnp.int32, (tm, 1), 0) % seq_len
    cw = cw_ref[...]
    for ch in range(CONV_DIM // CONV_CHUNK):
        cols = slice(ch * CONV_CHUNK, (ch + 1) * CONV_CHUNK)
        part = lambda k: _dot_nt(h, wat_ref[k * CONV_DIM + ch * CONV_CHUNK:k * CONV_DIM + (ch + 1) * CONV_CHUNK, :])
        v = part(1) * part(2)
        v_prev = jnp.where(pos == 0, 0.0, pltpu.roll(v, 1, 0))
        v_next = jnp.where(pos == seq_len - 1, 0.0, pltpu.roll(v, tm - 1, 0))
        y = cb_ref[:, cols] + v_prev * cw[0:1, cols]
        y = y + v * cw[1:2, cols]
        y = y + v_next * cw[2:3, cols]
        yc_ref[:, cols] = (part(0) * y).astype(BF16)

    qna_ref[...] = (_dot_nt(h, wat_ref[3 * CONV_DIM:3 * CONV_DIM + HD, :]) * NA_SCALE).astype(BF16)

    kt_na = _dot_nt(wt_ref[0:HD, :], h)
    ut = _dot_nt(wt_ref[HD:, :], h)
    vt_na = ut[0:HD]
    krt = ut[HD:HD + MLA_ROPE]

    u = _dot_nt(h, wbt_ref[...])
    cq = _rms(u[:, 0:Q_LORA], qn_ref[...]).astype(BF16)
    ckv = _rms(u[:, Q_LORA:Q_LORA + KV_LORA], kvn_ref[...])
    ckv_b = ckv.astype(BF16)
    if latent:
        knat_ref[...] = kt_na.astype(BF16)
        vnat_ref[...] = vt_na.astype(BF16)
        krt = krt * kct_ref[...] + ut[HD + MLA_ROPE:HD + 2 * MLA_ROPE] * kst_ref[...]
        qc = jnp.concatenate([qc_ref[...]] * 2, axis=1)
        qs = jnp.concatenate([qs_ref[...]] * 2, axis=1)
        for hp in range(HEADS // 2):
            cols = slice(hp * 2 * MLA_QK_PAD, (hp + 1) * 2 * MLA_QK_PAD)
            sw_cols = slice(MLA_QK_W + hp * 2 * MLA_QK_PAD, MLA_QK_W + (hp + 1) * 2 * MLA_QK_PAD)
            q_rot = _dot(cq, wq_ref[:, cols]) * qc + _dot(cq, wq_ref[:, sw_cols]) * qs
            qm_ref[:, cols] = (q_rot * MLA_SCALE).astype(BF16)
    else:
        qm_ref[...] = (_dot(cq, wq_ref[...]) * MLA_SCALE).astype(BF16)
    kmt = _dot_nt(wkat_ref[...], ckv_b) + _dot(wkbt_ref[...], krt.astype(BF16))
    vmt = _dot_nt(wuvt_ref[...], ckv_b)
    if latent:
        kmt_ref[...] = kmt.astype(BF16)
        vmt_ref[...] = vmt.astype(BF16)
    else:
        for b in range(tm // seq_len):
            rows = slice(b * seq_len, (b + 1) * seq_len)
            put(knat_ref, b, kt_na[:, rows].reshape(HEADS, NA_HD, seq_len))
            put(vnat_ref, b, vt_na[:, rows].reshape(HEADS, NA_HD, seq_len))
            put(ckv_ref, b, ckv[rows])
            put(krt_ref, b, krt[:, rows])
            kmt_ref[b] = kmt[:, rows].astype(BF16)
            vmt_ref[b] = vmt[:, rows].astype(BF16)


def _proj(x, mod, w, layer, seq_len, latent, tables, caches, tm):
    row0 = N_CTX_ROWS if latent else 0
    t = N_ROWS - N_CTX_ROWS if latent else N_CTX_ROWS
    x_tile0 = row0 // tm if x.shape[0] == N_ROWS else 0
    n_seq = t // seq_len
    seq_per_tile = tm // seq_len
    wq_cols = w["wq"].shape[-1]
    wt_rows = w["wt"].shape[-2]
    row = lambda n: pl.BlockSpec((tm, n), lambda i: (i, 0))
    in_specs = [pl.BlockSpec((tm, D_MODEL), lambda i: (i + x_tile0, 0)),
                _mod_spec(tm, row0 // tm),
                _resident((None, 1, D_MODEL), (layer, 0, 0)),
                _resident((3 * CONV_DIM + HD, D_MODEL), (0, 0)),
                _resident((None, Q_LORA + KV_LORA, D_MODEL), (layer, 0, 0)),
                _resident((None, wt_rows, D_MODEL), (layer, 0, 0)),
                _resident((None, CONV_K, CONV_DIM), (layer, 0, 0)),
                _resident((None, 1, CONV_DIM), (layer, 0, 0)),
                _resident((None, 1, Q_LORA), (layer, 0, 0)),
                _resident((None, 1, KV_LORA), (layer, 0, 0)),
                _resident((None, Q_LORA, wq_cols), (layer, 0, 0)),
                _resident((None, MLA_QK_W, KV_LORA), (layer, 0, 0)),
                _resident((None, MLA_QK_W, MLA_ROPE), (layer, 0, 0)),
                _resident((None, HD, KV_LORA), (layer, 0, 0))]
    args = [x, mod, w["ng1"], w["wat"], w["wbt"], w["wt"], w["conv_w"], w["conv_b"], w["qnorm"], w["kvnorm"],
            w["wq"], w["wkat"], w["wkbt"], w["wuvt"]]
    out_shape = [jax.ShapeDtypeStruct((t, CONV_DIM), BF16),
                 jax.ShapeDtypeStruct((t, HD), BF16)]
    out_specs = [row(CONV_DIM), row(HD)]
    aliases = {}
    n_alias = 0
    own_slot = 0
    if latent:
        assert tm == seq_len
        in_specs += [_resident((seq_len, MLA_QK_PAD), (0, 0)), _resident((seq_len, MLA_QK_PAD), (0, 0)),
                     _resident((MLA_ROPE, seq_len), (0, 0)), _resident((MLA_ROPE, seq_len), (0, 0))]
        args += list(tables)
        seq_blk = lambda n: pl.BlockSpec((None, n, seq_len), lambda i: (i, 0, 0))
        out_shape += [jax.ShapeDtypeStruct((n_seq, HD, seq_len), BF16),
                      jax.ShapeDtypeStruct((n_seq, HD, seq_len), BF16),
                      jax.ShapeDtypeStruct((t, MLA_QK_W), BF16),
                      jax.ShapeDtypeStruct((n_seq, MLA_QK_W, seq_len), BF16),
                      jax.ShapeDtypeStruct((n_seq, HD, seq_len), BF16)]
        out_specs += [seq_blk(HD), seq_blk(HD), row(MLA_QK_W), seq_blk(MLA_QK_W), seq_blk(HD)]
    else:
        if caches is not None:
            n_alias = len(caches)
            in_specs += [pl.BlockSpec(memory_space=pl.ANY)] * n_alias
            args += list(caches)
            aliases = {len(args) - n_alias + k: 2 + (0, 1, 5, 6)[k] for k in range(n_alias)}
            n_slots, first_slot = 1, layer
        else:
            assert layer == 0
            n_slots, first_slot, own_slot = DEPTH, 0, layer
        cache_blk = lambda *dims: pl.BlockSpec((seq_per_tile, n_slots) + dims,
                                               lambda i: (i, first_slot) + (0,) * len(dims))
        seq_blk = lambda n: pl.BlockSpec((seq_per_tile, n, seq_len), lambda i: (i, 0, 0))
        out_shape += [jax.ShapeDtypeStruct((n_seq, DEPTH, HEADS, NA_HD, seq_len), F32),
                      jax.ShapeDtypeStruct((n_seq, DEPTH, HEADS, NA_HD, seq_len), F32),
                      jax.ShapeDtypeStruct((t, MLA_QK_W), BF16),
                      jax.ShapeDtypeStruct((n_seq, MLA_QK_W, seq_len), BF16),
                      jax.ShapeDtypeStruct((n_seq, HD, seq_len), BF16),
                      jax.ShapeDtypeStruct((n_seq, DEPTH, seq_len, KV_LORA), F32),
                      jax.ShapeDtypeStruct((n_seq, DEPTH, MLA_ROPE, seq_len), F32)]
        out_specs += [cache_blk(HEADS, NA_HD, seq_len), cache_blk(HEADS, NA_HD, seq_len), row(MLA_QK_W),
                      seq_blk(MLA_QK_W), seq_blk(HD), cache_blk(seq_len, KV_LORA), cache_blk(MLA_ROPE, seq_len)]
    return pl.pallas_call(
        functools.partial(_proj_body, seq_len=seq_len, latent=latent, n_alias=n_alias, own_slot=own_slot),
        out_shape=out_shape,
        grid=(t // tm,),
        in_specs=in_specs,
        out_specs=out_specs,
        input_output_aliases=aliases,
        compiler_params=_params(1),
        name="mixer_proj",
    )(*args)


def _softmax_pv(s, vt):
    m = jnp.max(s, axis=-1, keepdims=True)
    p = jnp.exp2(s - m)
    den = jnp.sum(p, axis=-1, keepdims=True)
    return _dot_nt(p.astype(BF16), vt) / den


def _pair_slot(x, j):
    z = jnp.zeros_like(x)
    return jnp.concatenate([x, z] if j == 0 else [z, x], axis=0)


def _attn_ctx_body(qna_ref, knat_ref, vnat_ref, qm_ref, kmt_ref, vmt_ref, ona_ref, om_ref, *, seq_len):
    for b in range(qna_ref.shape[0] // seq_len):
        rows = slice(b * seq_len, (b + 1) * seq_len)
        for hp in range(HEADS // 2):
            pair = slice(hp * 2 * HEAD_V, (hp + 1) * 2 * HEAD_V)
            q = qna_ref[rows, pair]
            o_na = o_m = None
            for j in range(2):
                hh = 2 * hp + j
                kt = _pair_slot(knat_ref[b, hh].astype(BF16), j)
                vt = _pair_slot(vnat_ref[b, hh].astype(BF16), j)
                o = _softmax_pv(_dot(q, kt), vt)
                o_na = o if j == 0 else o_na + o
                qk = slice(hh * MLA_QK_PAD, (hh + 1) * MLA_QK_PAD)
                vt = _pair_slot(vmt_ref[b, hh * HEAD_V:(hh + 1) * HEAD_V, :], j)
                o = _softmax_pv(_dot(qm_ref[rows, qk], kmt_ref[b, qk, :]), vt)
                o_m = o if j == 0 else o_m + o
            ona_ref[rows, pair] = o_na.astype(BF16)
            om_ref[rows, pair] = o_m.astype(BF16)


def _attn_ctx(qna, knat, vnat, qm, kmt, vmt, layer, seq_len, tm=1024):
    t = qna.shape[0]
    nb = tm // seq_len
    row = lambda n: pl.BlockSpec((tm, n), lambda i: (i, 0))
    cache_blk = pl.BlockSpec((nb, None, HEADS, NA_HD, seq_len), lambda i: (i, layer, 0, 0, 0))
    seq_blk = lambda n: pl.BlockSpec((nb, n, seq_len), lambda i: (i, 0, 0))
    return pl.pallas_call(
        functools.partial(_attn_ctx_body, seq_len=seq_len),
        out_shape=[jax.ShapeDtypeStruct((t, HD), BF16), jax.ShapeDtypeStruct((t, HD), BF16)],
        grid=(t // tm,),
        in_specs=[row(HD), cache_blk, cache_blk, row(MLA_QK_W), seq_blk(MLA_QK_W), seq_blk(HD)],
        out_specs=[row(HD), row(HD)],
        compiler_params=_params(1),
        name="attn_ctx",
    )(qna, knat, vnat, qm, kmt, vmt)


def _attn_lat_body(qna_ref, knat_ref, vnat_ref, kctx_ref, vctx_ref, bias_ref,
                   qm_ref, kmt_ref, vmt_ref, cckv_ref, ckrt_ref, wkat_ref, wkbt_ref, wuvt_ref, ona_ref, om_ref):
    cat = lambda *a: jnp.concatenate(a, axis=1)
    head = lambda h: slice(h * HEAD_V, (h + 1) * HEAD_V)
    cckv = cckv_ref[...].astype(BF16)
    ckrt = ckrt_ref[...].astype(BF16)
    for pr in range(LAT_HEADS // 2):
        pair = slice(pr * 2 * HEAD_V, (pr + 1) * 2 * HEAD_V)
        kc = [_pair_slot(kctx_ref[2 * pr + j].astype(BF16), j) for j in range(2)]
        vc = [_pair_slot(vctx_ref[2 * pr + j].astype(BF16), j) for j in range(2)]
        for c, (start, count) in enumerate(NA_WINDOWS):
            rows = slice(c * Q_CHUNK, (c + 1) * Q_CHUNK)
            keys = slice(start, start + count)
            q = qna_ref[rows, pair]
            for j in range(2):
                h = 2 * pr + j
                bias = jnp.concatenate([cat(*[bias_ref[h, p] for p in NA_BLOCK_INDEX[c][rl]])
                                        for rl in range(Q_CHUNK // GRID_W)], axis=0)
                s = cat(_dot(q, kc[j]), _dot(q, _pair_slot(knat_ref[head(h), keys], j)) + bias)
                o = _softmax_pv(s, cat(vc[j], _pair_slot(vnat_ref[head(h), keys], j)))
                o_na = o if j == 0 else o_na + o
            ona_ref[rows, pair] = o_na.astype(BF16)
        qk_pair = slice(pr * 2 * MLA_QK_PAD, (pr + 1) * 2 * MLA_QK_PAD)
        km_ctx = (_dot_nt(wkat_ref[qk_pair, :], cckv) + _dot(wkbt_ref[qk_pair, :], ckrt)).astype(BF16)
        vm_ctx = _dot_nt(wuvt_ref[pair, :], cckv).astype(BF16)
        kt, vt = [], []
        for j in range(2):
            h = 2 * pr + j
            qk = slice(h * MLA_QK_PAD, (h + 1) * MLA_QK_PAD)
            kt.append(cat(km_ctx[j * MLA_QK_PAD:(j + 1) * MLA_QK_PAD, :], kmt_ref[qk, :]))
            vt.append(_pair_slot(cat(vm_ctx[head(j), :], vmt_ref[head(h), :]), j))
        for c in range(DEC_SEQ // MLA_Q_ROWS):
            rows = slice(c * MLA_Q_ROWS, (c + 1) * MLA_Q_ROWS)
            for j in range(2):
                h = 2 * pr + j
                o = _softmax_pv(_dot(qm_ref[rows, h * MLA_QK_PAD:(h + 1) * MLA_QK_PAD], kt[j]), vt[j])
                o_m = o if j == 0 else o_m + o
            om_ref[rows, pair] = o_m.astype(BF16)


def _attn_lat(qna, knat, vnat, kctx, vctx, bias, qm, kmt, vmt, cache_ckv, cache_krt, wkat, wkbt, wuvt, layer):
    t = qna.shape[0]
    s = DEC_SEQ
    n = LAT_HEADS
    return pl.pallas_call(
        _attn_lat_body,
        out_shape=[jax.ShapeDtypeStruct((t, HD), BF16), jax.ShapeDtypeStruct((t, HD), BF16)],
        grid=(HEADS // n, t // s),
        in_specs=[pl.BlockSpec((s, n * NA_HD), lambda hg, b: (b, hg)),
                  pl.BlockSpec((None, n * NA_HD, s), lambda hg, b: (b, hg, 0)),
                  pl.BlockSpec((None, n * HEAD_V, s), lambda hg, b: (b, hg, 0)),
                  pl.BlockSpec((None, None, n, NA_HD, PAST_LEN), lambda hg, b: (b, layer, hg, 0, 0)),
                  pl.BlockSpec((None, None, n, NA_HD, PAST_LEN), lambda hg, b: (b, layer, hg, 0, 0)),
                  pl.BlockSpec((None, n, len(NA_BLOCK_PAIRS), GRID_W, 2 * GRID_W),
                               lambda hg, b: (layer, hg, 0, 0, 0)),
                  pl.BlockSpec((s, n * MLA_QK_PAD), lambda hg, b: (b, hg)),
                  pl.BlockSpec((None, n * MLA_QK_PAD, s), lambda hg, b: (b, hg, 0)),
                  pl.BlockSpec((None, n * HEAD_V, s), lambda hg, b: (b, hg, 0)),
                  pl.BlockSpec((None, None, PAST_LEN, KV_LORA), lambda hg, b: (b, layer, 0, 0)),
                  pl.BlockSpec((None, None, MLA_ROPE, PAST_LEN), lambda hg, b: (b, layer, 0, 0)),
                  pl.BlockSpec((None, n * MLA_QK_PAD, KV_LORA), lambda hg, b: (layer, hg, 0)),
                  pl.BlockSpec((None, n * MLA_QK_PAD, MLA_ROPE), lambda hg, b: (layer, hg, 0)),
                  pl.BlockSpec((None, n * HEAD_V, KV_LORA), lambda hg, b: (layer, hg, 0))],
        out_specs=[pl.BlockSpec((s, n * HEAD_V), lambda hg, b: (b, hg)),
                   pl.BlockSpec((s, n * HEAD_V), lambda hg, b: (b, hg))],
        compiler_params=_params(2),
        name="attn_lat",
    )(qna, knat, vnat, kctx, vctx, bias, qm, kmt, vmt, cache_ckv, cache_krt, wkat, wkbt, wuvt)


def _mix_body(*refs, n_parts):
    tm = refs[-1].shape[0]
    parts, pos = [], 0
    for n in n_parts:
        parts.append(_rows_value(refs[pos:pos + n], tm))
        pos += n
    x, yc, ona, om = parts
    mod_ref, ng_ref, wgt_ref, wco_ref, wno_ref, wmo_ref, wo_ref, o_ref, z_scr = refs[pos:]
    mod = mod_ref[...]
    h = (_rms(x, ng_ref[...]) * (1 + mod[4:5]) + mod[3:4]).astype(BF16)
    for ch in range(D_MODEL // MIX_CHUNK):
        cols = slice(ch * MIX_CHUNK, (ch + 1) * MIX_CHUNK)
        gate = lambda k: jax.nn.sigmoid(
            _dot_nt(h, wgt_ref[k * D_MODEL + ch * MIX_CHUNK:k * D_MODEL + (ch + 1) * MIX_CHUNK, :]))
        z = gate(0) * _dot(yc, wco_ref[:, cols])
        z = z + gate(1) * _dot(ona, wno_ref[:, cols])
        z = z + gate(2) * _dot(om, wmo_ref[:, cols])
        z_scr[:, cols] = z.astype(BF16)
    o_ref[...] = x + mod[5:6] * _dot(z_scr[...], wo_ref[...])


def _mix(x, yc, ona, om, mod, w, layer):
    tm = 512 if isinstance(x, tuple) else 1024
    in_specs, args, n_parts = [], [], []
    for a in (x, yc, ona, om):
        specs, ops = _rows_specs(a, tm, 0)
        in_specs += specs
        args += ops
        n_parts.append(len(ops))
    in_specs += [_mod_spec(tm, 0),
                 _resident((None, 1, D_MODEL), (layer, 0, 0)),
                 _resident((3 * D_MODEL, D_MODEL), (0, 0)),
                 _resident((CONV_DIM, D_MODEL), (0, 0)),
                 _resident((HD, D_MODEL), (0, 0)),
                 _resident((HD, D_MODEL), (0, 0)),
                 _resident((D_MODEL, D_MODEL), (0, 0))]
    args += [mod, w["ng1"], w["wgt"], w["wco"], w["wno"], w["wmo"], w["wo"]]
    return pl.pallas_call(
        functools.partial(_mix_body, n_parts=tuple(n_parts)),
        out_shape=jax.ShapeDtypeStruct((N_ROWS, D_MODEL), F32),
        grid=(N_ROWS // tm,),
        in_specs=in_specs,
        out_specs=pl.BlockSpec((tm, D_MODEL), lambda i: (i, 0)),
        scratch_shapes=[pltpu.VMEM((tm, D_MODEL), BF16)],
        compiler_params=_params(1),
        name="mixer_out",
    )(*args)


def _rope_tables():
    f32 = np.float32
    half = MLA_ROPE // 2
    nf = half // 2
    inv = (f32(1.0) / (f32(ROPE_BASE) ** (np.arange(nf, dtype=f32) / f32(nf)))).astype(f32)
    t = np.arange(DEC_SEQ)
    rows = (t // GRID_W).astype(f32)[:, None] * inv[None, :]
    cols = (t % GRID_W).astype(f32)[:, None] * inv[None, :]
    cos = np.concatenate([np.cos(rows), np.cos(rows), np.cos(cols), np.cos(cols)], axis=-1).astype(f32)
    sin = np.concatenate([np.sin(rows), np.sin(rows), np.sin(cols), np.sin(cols)], axis=-1).astype(f32)
    pad = MLA_QK_PAD - MLA_NOPE - MLA_ROPE
    q_cos = np.concatenate([np.ones((DEC_SEQ, MLA_NOPE), f32), cos, np.zeros((DEC_SEQ, pad), f32)], axis=-1)
    q_sin = np.concatenate([np.zeros((DEC_SEQ, MLA_NOPE), f32), sin, np.zeros((DEC_SEQ, pad), f32)], axis=-1)
    return tuple(jnp.asarray(a) for a in (q_cos, q_sin, np.ascontiguousarray(cos.T), np.ascontiguousarray(sin.T)))


def _rope_swap(w):
    nf = MLA_ROPE // 4
    a, b, c, d = (w[..., i * nf:(i + 1) * nf] for i in range(4))
    return jnp.concatenate([-b, a, -d, c], axis=-1)


def _na_bias(rpb):
    n_dc = 2 * NA_WIN_C - 1
    col = np.arange(GRID_W)
    c_start = np.clip(col - NA_WIN_C // 2, 0, GRID_W - NA_WIN_C)
    c_in = (col[None, :] >= c_start[:, None]) & (col[None, :] < c_start[:, None] + NA_WIN_C)
    dc = np.clip(col[None, :] - col[:, None] + (NA_WIN_C - 1), 0, n_dc - 1)
    pick_dc = (dc[None] == np.arange(n_dc)[:, None, None]).astype(np.float32)
    n_pairs = len(NA_BLOCK_PAIRS)
    pick_dr = np.zeros((n_pairs, 2, NA_DR_MASKED), np.float32)
    for p, pair in enumerate(NA_BLOCK_PAIRS):
        for side, d in enumerate(pair):
            if d != NA_DR_MASKED:
                pick_dr[p, side, d] = 1.0
    keep = pick_dr.sum(-1).astype(bool)[:, None, :, None] & c_in[None, :, None, :]
    keep = keep.reshape(n_pairs, GRID_W, 2 * GRID_W)
    pick_side_dc = np.zeros((2, n_dc, GRID_W, 2, GRID_W), np.float32)
    for side in range(2):
        pick_side_dc[side, :, :, side, :] = pick_dc
    pick_side_dc = pick_side_dc.reshape(2, n_dc, GRID_W, 2 * GRID_W)
    hi = lax.Precision.HIGHEST
    by_row = jnp.einsum("psd,lhdj->lhpsj", jnp.asarray(pick_dr), rpb, precision=hi)
    blocks = jnp.einsum("lhpsj,sjqn->lhpqn", by_row, jnp.asarray(pick_side_dc), precision=hi)
    return jnp.where(jnp.asarray(keep), blocks * LOG2_E, NEG_INF)


def _pack_weights(w_int, w_uq, w_ukv):
    t_last = lambda a: jnp.swapaxes(a, -1, -2)
    w_mid = lax.optimization_barrier(w_int[:, W_IN_KV:W_IN_GATE]).astype(BF16)
    w_kvt = w_mid[:, :W_IN_LORA - W_IN_KV]
    w_krt = w_mid[:, W_IN_KR - W_IN_KV:]
    wt = jnp.concatenate([w_kvt, w_krt, t_last(_rope_swap(t_last(w_krt)))], axis=1)
    wbt = w_mid[:, W_IN_LORA - W_IN_KV:W_IN_KR - W_IN_KV]
    uq = w_uq.reshape(DEPTH, Q_LORA, MLA_HEADS, MLA_NOPE + MLA_ROPE)
    pad = MLA_QK_PAD - MLA_NOPE - MLA_ROPE
    zp = jnp.zeros(uq.shape[:-1] + (pad,), F32)
    zn = jnp.zeros(uq.shape[:-1] + (MLA_NOPE,), F32)
    q_ext = jnp.concatenate([uq, zp], axis=-1).reshape(DEPTH, Q_LORA, MLA_QK_W)
    q_sw = jnp.concatenate([zn, _rope_swap(uq[..., MLA_NOPE:]), zp], axis=-1).reshape(q_ext.shape)
    ukv = w_ukv.reshape(DEPTH, KV_LORA, MLA_HEADS, MLA_NOPE + MLA_V)
    zk = jnp.zeros(ukv.shape[:-1] + (MLA_QK_PAD - MLA_NOPE,), F32)
    wka = jnp.concatenate([ukv[..., :MLA_NOPE], zk], axis=-1).reshape(DEPTH, KV_LORA, MLA_QK_W)
    eye = jnp.concatenate([jnp.zeros((MLA_ROPE, MLA_NOPE), F32), jnp.eye(MLA_ROPE, dtype=F32),
                           jnp.zeros((MLA_ROPE, pad), F32)], axis=-1)
    wkb = jnp.broadcast_to(jnp.tile(eye, (1, MLA_HEADS))[None], (DEPTH, MLA_ROPE, MLA_QK_W))
    wuv = ukv[..., MLA_NOPE:].reshape(DEPTH, KV_LORA, HD)
    b = lambda a: a.astype(BF16)
    return dict(wt=b(wt), wbt=b(wbt),
                wq_lat=b(jnp.concatenate([q_ext, q_sw], axis=-1)), wq_ctx=b(q_ext),
                wkat=b(t_last(wka)), wkbt=b(t_last(wkb)), wuvt=b(t_last(wuv)))


def kernel(x_prompt, x_sample, cache_na_k, cache_na_v, cache_mla_ckv, cache_mla_krope, c, c_ctx,
           w_ada, b_ada, norm_g, w_ffn1_gate, w_ffn1_up, w_ffn1_down, w_ffn2_gate, w_ffn2_up, w_ffn2_down,
           w_in, conv_w, conv_b, na_rpb, mla_qnorm, w_uq, mla_kvnorm, w_ukv,
           w_conv_out, w_na_out, w_mla_out, w_o, final_g):
    b16 = lambda a: a.astype(BF16)
    t_last = lambda a: jnp.swapaxes(a, -1, -2)
    w_int = t_last(w_in)
    packed = _pack_weights(w_int, w_uq, w_ukv)
    shared = dict(conv_w=conv_w, conv_b=conv_b.reshape(DEPTH, 1, CONV_DIM),
                  qnorm=mla_qnorm.reshape(DEPTH, 1, Q_LORA), kvnorm=mla_kvnorm.reshape(DEPTH, 1, KV_LORA),
                  ng1=norm_g[:, 1:2],
                  **{k: packed[k] for k in ("wt", "wbt", "wkat", "wkbt", "wuvt")})
    ffn1_f32 = (w_ffn1_gate, w_ffn1_up, w_ffn1_down)
    ffn2_f32 = (w_ffn2_gate, w_ffn2_up, w_ffn2_down)
    ffn1_w = {0: tuple(b16(w[0]) for w in ffn1_f32)}
    ffn2_w = {}
    mixer_w = {}
    mixer_srcs = ((w_int, 0, W_IN_KV), (w_int, W_IN_GATE, 3 * D_MODEL),
                  (w_conv_out, 0, None), (w_na_out, 0, None), (w_mla_out, 0, None), (w_o, 0, None))
    final_row = final_g.reshape(1, D_MODEL)

    c_all = jnp.concatenate([c_ctx[None], c, jnp.zeros((MOD_ROWS - 1 - DEC_BATCH, D_MODEL), F32)], axis=0)
    b_ada3 = b_ada.reshape(DEPTH, 1, N_MOD * D_MODEL)
    as_table = lambda m: m.reshape(MOD_ROWS, N_MOD, D_MODEL)
    mod = {0: as_table(_modulation(c_all, w_ada, b_ada3, 0))}

    tables = _rope_tables()
    na_bias = _na_bias(na_rpb)
    ctx_k_na = t_last(cache_na_k)
    ctx_v_na = t_last(cache_na_v)
    ctx_krt = t_last(cache_mla_krope)

    xp = x_prompt.reshape(N_CTX_ROWS, D_MODEL)
    xs = x_sample.reshape(N_ROWS - N_CTX_ROWS, D_MODEL)
    caches = None
    x_all = None
    for l in range(DEPTH):
        last = l == DEPTH - 1
        jobs = [_cast_job(w, l) for w in ffn2_f32]
        if l == 0:
            jobs += [_cast_job(a, ll, r0, n) for ll in range(DEPTH) for a, r0, n in mixer_srcs]
            xp, cast = _ffn(xp, mod[l], norm_g[l, 0:1], *ffn1_w[l], 0, n_rows=N_CTX_ROWS, cast_jobs=jobs, tm=512)
            xs, side = _ffn(xs, mod[l], norm_g[l, 0:1], *ffn1_w[l], 0, row0=N_CTX_ROWS,
                            next_mod=(c_all, w_ada, b_ada3, 1) if DEPTH > 1 else None)
            if side:
                mod[1] = as_table(side[0])
            x1 = (xp, xs)
        else:
            if l not in mod:
                mod[l] = as_table(_modulation(c_all, w_ada, b_ada3, l))
            x1, cast = _ffn(x_all, mod[l], norm_g[l, 0:1], *ffn1_w[l], 0, cast_jobs=jobs)
            xp = xs = x1
        ffn2_w[l] = tuple(cast[:3])
        for ll in range(DEPTH if l == 0 else 0):
            names = ("wat", "wgt", "wco", "wno", "wmo", "wo")
            mixer_w[ll] = dict(zip(names, cast[3 + len(names) * ll:3 + len(names) * (ll + 1)]))
        w_ctx = dict(shared, wq=packed["wq_ctx"], **mixer_w[l])
        w_lat = dict(shared, wq=packed["wq_lat"], **mixer_w[l])
        yc_p, qna, knat, vnat, qm, kmt, vmt, ckv, krt = _proj(xp, mod[l], w_ctx, l, SEQ, False, None, caches, 1024)
        caches = (knat, vnat, ckv, krt)
        ona_p, om_p = _attn_ctx(qna, knat, vnat, qm, kmt, vmt, l, SEQ)
        yc_s, qna, knat, vnat, qm, kmt, vmt = _proj(xs, mod[l], w_lat, l, DEC_SEQ, True, tables, None, DEC_SEQ)
        ona_s, om_s = _attn_lat(qna, knat, vnat, ctx_k_na, ctx_v_na, na_bias, qm, kmt, vmt, cache_mla_ckv, ctx_krt,
                                packed["wkat"], packed["wkbt"], packed["wuvt"], l)
        x2 = _mix(x1, (yc_p, yc_s), (ona_p, ona_s), (om_p, om_s), mod[l], w_ctx, l)
        if not last:
            x_all, cast = _ffn(x2, mod[l], norm_g[l, 2:3], *ffn2_w[l], 6,
                               cast_jobs=[_cast_job(w, l + 1) for w in ffn1_f32])
            ffn1_w[l + 1] = tuple(cast)
        else:
            yp, _ = _ffn(x2, mod[l], norm_g[l, 2:3], *ffn2_w[l], 6, n_rows=N_CTX_ROWS, final_g=final_row)
            ys, _ = _ffn(x2, mod[l], norm_g[l, 2:3], *ffn2_w[l], 6, row0=N_CTX_ROWS, final_g=final_row)
    new_kt, new_vt, new_ckv, new_krt = caches
    return (yp.reshape(BATCH, SEQ, D_MODEL), ys.reshape(DEC_BATCH, DEC_SEQ, D_MODEL),
            t_last(new_kt), t_last(new_vt), new_ckv, t_last(new_krt))
```

```python
import functools
import math

import jax
import jax.numpy as jnp
import numpy as np
from jax import lax
from jax.experimental import pallas as pl
from jax.experimental.pallas import tpu as pltpu

D_MODEL = 1024
BATCH = 32
SEQ = 256
DEPTH = 2
DEC_BATCH = 8
DEC_SEQ = 1024
PAST_LEN = 256
GRID_W = 64
CONV_DIM = 512
CONV_K = 3
NA_HEADS = 8
NA_HD = 64
NA_WIN_R = 8
NA_WIN_C = 16
MLA_HEADS = 8
MLA_NOPE = 64
MLA_ROPE = 32
MLA_V = 64
Q_LORA = 256
KV_LORA = 128
FFN_DIM = 2816
N_MOD = 9
ROPE_BASE = 10000.0
EPS = 1e-6
NEG_INF = -1e30
LOG2_E = 1.4426950408889634
MLA_SCALE = (MLA_NOPE + MLA_ROPE) ** -0.5 * LOG2_E
NA_SCALE = NA_HD ** -0.5 * LOG2_E

N_CTX_ROWS = BATCH * SEQ
N_ROWS = N_CTX_ROWS + DEC_BATCH * DEC_SEQ
HEADS = 8
HEAD_V = 64
HD = HEADS * NA_HD
W_IN_KV = 3 * CONV_DIM + HD
W_IN_LORA = W_IN_KV + 2 * HD
W_IN_KR = W_IN_LORA + Q_LORA + KV_LORA
W_IN_GATE = W_IN_KR + MLA_ROPE
MLA_QK_PAD = 128
MLA_QK_W = HEADS * MLA_QK_PAD
FFN_CHUNK = 256
CONV_CHUNK = 256
MIX_CHUNK = 256
Q_CHUNK = 256
MLA_Q_ROWS = 128
LAT_HEADS = 4
MOD_ROWS = 16
VMEM_LIMIT = 56 * 1024 * 1024
NA_WINDOWS = ((0, 512), (0, 768), (256, 768), (512, 512))
NA_DR_MASKED = 2 * NA_WIN_R - 1


def _na_block_pairs():
    rows = DEC_SEQ // GRID_W
    r_start = np.clip(np.arange(rows) - NA_WIN_R // 2, 0, rows - NA_WIN_R)
    pairs, index = [], []
    for c, (start, count) in enumerate(NA_WINDOWS):
        index.append([])
        for rl in range(Q_CHUNK // GRID_W):
            r = c * (Q_CHUNK // GRID_W) + rl
            assert start // GRID_W <= r_start[r] and r_start[r] + NA_WIN_R <= (start + count) // GRID_W
            index[c].append([])
            for kp in range(count // (2 * GRID_W)):
                pair = []
                for rk in (start // GRID_W + 2 * kp, start // GRID_W + 2 * kp + 1):
                    inside = r_start[r] <= rk < r_start[r] + NA_WIN_R
                    pair.append(int(rk - r + NA_WIN_R - 1) if inside else NA_DR_MASKED)
                pair = tuple(pair)
                if pair not in pairs:
                    pairs.append(pair)
                index[c][rl].append(pairs.index(pair))
    return tuple(pairs), index


NA_BLOCK_PAIRS, NA_BLOCK_INDEX = _na_block_pairs()

BF16 = jnp.bfloat16
F32 = jnp.float32


def _dot(a, b):
    return jnp.dot(a, b, preferred_element_type=F32)


def _dot_nt(a, b):
    return lax.dot_general(a, b, (((1,), (1,)), ((), ())), preferred_element_type=F32)


def _rms(x, g):
    return x * lax.rsqrt(jnp.mean(x * x, axis=-1, keepdims=True) + EPS) * g


def _params(n_axes):
    return pltpu.CompilerParams(dimension_semantics=("arbitrary",) * n_axes,
                                vmem_limit_bytes=VMEM_LIMIT)


def _resident(shape, index):
    return pl.BlockSpec(shape, lambda *_: index, pipeline_mode=pl.Buffered(1))


def _mod_body(c_ref, w_ref, b_ref, o_ref):
    c = c_ref[...]
    a = c * jax.nn.sigmoid(c)
    o_ref[...] = _dot(a.astype(BF16), w_ref[...].astype(BF16)) + b_ref[...]


def _mod_specs(layer, n_blocks):
    tn = N_MOD * D_MODEL // n_blocks
    assert tn % 128 == 0
    return ([pl.BlockSpec((MOD_ROWS, D_MODEL), lambda j: (0, 0)),
             pl.BlockSpec((None, D_MODEL, tn), lambda j: (layer, 0, j)),
             pl.BlockSpec((None, 1, tn), lambda j: (layer, 0, j))],
            pl.BlockSpec((MOD_ROWS, tn), lambda j: (0, j)))


def _modulation_body(c_ref, w_ref, b_ref, *rest, n_jobs):
    _mod_body(c_ref, w_ref, b_ref, rest[n_jobs])
    _cast_blocks(rest[:n_jobs], rest[n_jobs + 1:])


def _modulation(c_all, w_ada, b_ada, layer, cast_jobs=()):
    n_blocks = 4
    in_specs, out_spec = _mod_specs(layer, n_blocks)
    job_in, job_args, job_shapes, job_out = _job_specs(cast_jobs, n_blocks)
    outs = pl.pallas_call(
        functools.partial(_modulation_body, n_jobs=len(cast_jobs)),
        out_shape=[jax.ShapeDtypeStruct((MOD_ROWS, N_MOD * D_MODEL), F32)] + job_shapes,
        grid=(n_blocks,),
        in_specs=in_specs + job_in,
        out_specs=[out_spec] + job_out,
        compiler_params=_params(1),
        name="modulation",
    )(c_all, w_ada, b_ada, *job_args)
    return outs[0], list(outs[1:])


def _rows_specs(x, tm, tile0):
    if isinstance(x, tuple):
        assert tile0 == 0
        n_ctx = x[0].shape[0] // tm
        width = x[0].shape[1]
        return [pl.BlockSpec((tm, width), lambda i: (jnp.minimum(i, n_ctx - 1), 0)),
                pl.BlockSpec((tm, width), lambda i: (jnp.maximum(i - n_ctx, 0), 0))], list(x)
    return [pl.BlockSpec((tm, x.shape[1]), lambda i: (i + tile0, 0))], [x]


def _rows_value(refs, tm):
    if len(refs) == 1:
        return refs[0][...]
    return jnp.where(pl.program_id(0) < N_CTX_ROWS // tm, refs[0][...], refs[1][...])


def _mod_spec(tm, tile0):
    def index(i):
        g = (i + tile0) * tm
        return jnp.where(g < N_CTX_ROWS, 0, 1 + (g - N_CTX_ROWS) // DEC_SEQ), 0, 0
    return pl.BlockSpec((None, N_MOD, D_MODEL), index)


def _ffn_body(*refs, mod_off, final, n_jobs, n_x, mod_job):
    x_refs, (mod_ref, ng_ref, wg_ref, wu_ref, wd_ref), rest = refs[:n_x], refs[n_x:n_x + 5], refs[n_x + 5:]
    if final:
        fg_ref, rest = rest[0], rest[1:]
    job_in, rest = rest[:n_jobs], rest[n_jobs:]
    if mod_job:
        next_mod_in, rest = rest[:3], rest[3:]
    o_ref, job_out, a_scr = rest[0], rest[1:n_jobs + 1], rest[-1]
    x = _rows_value(x_refs, o_ref.shape[0])
    mod = mod_ref[...]
    shift = mod[mod_off:mod_off + 1]
    scale = mod[mod_off + 1:mod_off + 2]
    gate = mod[mod_off + 2:mod_off + 3]
    h = (_rms(x, ng_ref[...]) * (1 + scale) + shift).astype(BF16)
    for f in range(FFN_DIM // FFN_CHUNK):
        cols = slice(f * FFN_CHUNK, (f + 1) * FFN_CHUNK)
        g = _dot(h, wg_ref[:, cols])
        u = _dot(h, wu_ref[:, cols])
        a_scr[:, cols] = (g * jax.nn.sigmoid(g) * u).astype(BF16)
    y = _dot(a_scr[...], wd_ref[...])
    out = x + 0.5 * gate * y
    if final:
        out = _rms(out, fg_ref[...])
    o_ref[...] = out
    _cast_blocks(job_in, job_out)
    if mod_job:
        _mod_body(*next_mod_in, rest[n_jobs + 1])


def _cast_job(arr, layer, row_start=0, n_rows=None):
    return arr, layer, row_start, arr.shape[1] if n_rows is None else n_rows


def _job_specs(cast_jobs, n_steps):
    in_specs, args, out_shape, out_specs = [], [], [], []
    for arr, layer, row_start, rows in cast_jobs:
        width = arr.shape[2]
        blk = rows // n_steps
        assert blk * n_steps == rows and blk % 16 == 0 and row_start % 16 == 0
        if row_start % blk == 0:
            spec = pl.BlockSpec((None, blk, width), lambda i, l=layer, b0=row_start // blk: (l, b0 + i, 0))
        else:
            g = math.gcd(row_start, blk)
            spec = pl.BlockSpec((pl.Element(1), pl.Element(blk), pl.Element(width)),
                                lambda i, l=layer, r0=row_start // g, n=blk // g, g=g: (l, (r0 + n * i) * g, 0))
        in_specs.append(spec)
        args.append(arr)
        out_shape.append(jax.ShapeDtypeStruct((rows, width), BF16))
        out_specs.append(pl.BlockSpec((blk, width), lambda i: (i, 0)))
    return in_specs, args, out_shape, out_specs


def _cast_blocks(job_in, job_out):
    for src, dst in zip(job_in, job_out):
        dst[...] = src[...].reshape(dst.shape).astype(BF16)


def _ffn(x, mod, ng, wg, wu, wd, mod_off, row0=0, n_rows=None, final_g=None, cast_jobs=(), next_mod=None, tm=1024):
    t = N_ROWS - row0 if n_rows is None else n_rows
    n_steps = t // tm
    final = final_g is not None
    whole = isinstance(x, tuple) or x.shape[0] == N_ROWS
    x_specs, x_args = _rows_specs(x, tm, row0 // tm if whole else 0)
    in_specs = x_specs + [_mod_spec(tm, row0 // tm),
                          _resident((1, D_MODEL), (0, 0)),
                          _resident((D_MODEL, FFN_DIM), (0, 0)),
                          _resident((D_MODEL, FFN_DIM), (0, 0)),
                          _resident((FFN_DIM, D_MODEL), (0, 0))]
    args = x_args + [mod, ng, wg, wu, wd]
    if final:
        in_specs.append(_resident((1, D_MODEL), (0, 0)))
        args.append(final_g)
    out_shape = [jax.ShapeDtypeStruct((t, D_MODEL), F32)]
    out_specs = [pl.BlockSpec((tm, D_MODEL), lambda i: (i, 0))]
    job_in, job_args, job_shapes, job_out = _job_specs(cast_jobs, n_steps)
    in_specs += job_in
    args += job_args
    out_shape += job_shapes
    out_specs += job_out
    if next_mod is not None:
        mod_in, mod_out = _mod_specs(next_mod[3], n_steps)
        in_specs += mod_in
        args += list(next_mod[:3])
        out_shape.append(jax.ShapeDtypeStruct((MOD_ROWS, N_MOD * D_MODEL), F32))
        out_specs.append(mod_out)
    outs = pl.pallas_call(
        functools.partial(_ffn_body, mod_off=mod_off, final=final, n_jobs=len(cast_jobs), n_x=len(x_args),
                          mod_job=next_mod is not None),
        out_shape=out_shape,
        grid=(n_steps,),
        in_specs=in_specs,
        out_specs=out_specs,
        scratch_shapes=[pltpu.VMEM((tm, FFN_DIM), BF16)],
        compiler_params=_params(1),
        name="ffn",
    )(*args)
    return outs[0], list(outs[1:])


def _proj_body(x_ref, mod_ref, ng_ref, wat_ref, wbt_ref, wt_ref, cw_ref, cb_ref, qn_ref, kvn_ref,
               wq_ref, wkat_ref, wkbt_ref, wuvt_ref, *rest, seq_len, latent, n_alias, own_slot):
    rest = rest[n_alias:]

    def put(ref, b, value):
        for k in range(ref.shape[1]):
            ref[b, k] = value if k == own_slot else jnp.zeros_like(value)

    if latent:
        (qc_ref, qs_ref, kct_ref, kst_ref,
         yc_ref, qna_ref, knat_ref, vnat_ref, qm_ref, kmt_ref, vmt_ref) = rest
    else:
        (yc_ref, qna_ref, knat_ref, vnat_ref, qm_ref, kmt_ref, vmt_ref, ckv_ref, krt_ref) = rest
    tm = x_ref.shape[0]
    x = x_ref[...]
    mod = mod_ref[...]
    h = (_rms(x, ng_ref[...]) * (1 + mod[4:5]) + mod[3:4]).astype(BF16)

    pos = lax.broadcasted_iota(jnp.int32, (tm, 1), 0) % seq_len
    cw = cw_ref[...]
    for ch in range(CONV_DIM // CONV_CHUNK):
        cols = slice(ch * CONV_CHUNK, (ch + 1) * CONV_CHUNK)
        part = lambda k: _dot_nt(h, wat_ref[k * CONV_DIM + ch * CONV_CHUNK:k * CONV_DIM + (ch + 1) * CONV_CHUNK, :])
        v = part(1) * part(2)
        v_prev = jnp.where(pos == 0, 0.0, pltpu.roll(v, 1, 0))
        v_next = jnp.where(pos == seq_len - 1, 0.0, pltpu.roll(v, tm - 1, 0))
        y = cb_ref[:, cols] + v_prev * cw[0:1, cols]
        y = y + v * cw[1:2, cols]
        y = y + v_next * cw[2:3, cols]
        yc_ref[:, cols] = (part(0) * y).astype(BF16)

    qna_ref[...] = (_dot_nt(h, wat_ref[3 * CONV_DIM:3 * CONV_DIM + HD, :]) * NA_SCALE).astype(BF16)

    kt_na = _dot_nt(wt_ref[0:HD, :], h)
    ut = _dot_nt(wt_ref[HD:, :], h)
    vt_na = ut[0:HD]
    krt = ut[HD:HD + MLA_ROPE]

    u = _dot_nt(h, wbt_ref[...])
    cq = _rms(u[:, 0:Q_LORA], qn_ref[...]).astype(BF16)
    ckv = _rms(u[:, Q_LORA:Q_LORA + KV_LORA], kvn_ref[...])
    ckv_b = ckv.astype(BF16)
    if latent:
        knat_ref[...] = kt_na.astype(BF16)
        vnat_ref[...] = vt_na.astype(BF16)
        krt = krt * kct_ref[...] + ut[HD + MLA_ROPE:HD + 2 * MLA_ROPE] * kst_ref[...]
        qc = jnp.concatenate([qc_ref[...]] * 2, axis=1)
        qs = jnp.concatenate([qs_ref[...]] * 2, axis=1)
        for hp in range(HEADS // 2):
            cols = slice(hp * 2 * MLA_QK_PAD, (hp + 1) * 2 * MLA_QK_PAD)
            sw_cols = slice(MLA_QK_W + hp * 2 * MLA_QK_PAD, MLA_QK_W + (hp + 1) * 2 * MLA_QK_PAD)
            q_rot = _dot(cq, wq_ref[:, cols]) * qc + _dot(cq, wq_ref[:, sw_cols]) * qs
            qm_ref[:, cols] = (q_rot * MLA_SCALE).astype(BF16)
    else:
        qm_ref[...] = (_dot(cq, wq_ref[...]) * MLA_SCALE).astype(BF16)
    kmt = _dot_nt(wkat_ref[...], ckv_b) + _dot(wkbt_ref[...], krt.astype(BF16))
    vmt = _dot_nt(wuvt_ref[...], ckv_b)
    if latent:
        kmt_ref[...] = kmt.astype(BF16)
        vmt_ref[...] = vmt.astype(BF16)
    else:
        for b in range(tm // seq_len):
            rows = slice(b * seq_len, (b + 1) * seq_len)
            put(knat_ref, b, kt_na[:, rows].reshape(HEADS, NA_HD, seq_len))
            put(vnat_ref, b, vt_na[:, rows].reshape(HEADS, NA_HD, seq_len))
            put(ckv_ref, b, ckv[rows])
            put(krt_ref, b, krt[:, rows])
            kmt_ref[b] = kmt[:, rows].astype(BF16)
            vmt_ref[b] = vmt[:, rows].astype(BF16)


def _proj(x, mod, w, layer, seq_len, latent, tables, caches, tm):
    row0 = N_CTX_ROWS if latent else 0
    t = N_ROWS - N_CTX_ROWS if latent else N_CTX_ROWS
    x_tile0 = row0 // tm if x.shape[0] == N_ROWS else 0
    n_seq = t // seq_len
    seq_per_tile = tm // seq_len
    wq_cols = w["wq"].shape[-1]
    wt_rows = w["wt"].shape[-2]
    row = lambda n: pl.BlockSpec((tm, n), lambda i: (i, 0))
    in_specs = [pl.BlockSpec((tm, D_MODEL), lambda i: (i + x_tile0, 0)),
                _mod_spec(tm, row0 // tm),
                _resident((None, 1, D_MODEL), (layer, 0, 0)),
                _resident((3 * CONV_DIM + HD, D_MODEL), (0, 0)),
                _resident((None, Q_LORA + KV_LORA, D_MODEL), (layer, 0, 0)),
                _resident((None, wt_rows, D_MODEL), (layer, 0, 0)),
                _resident((None, CONV_K, CONV_DIM), (layer, 0, 0)),
                _resident((None, 1, CONV_DIM), (layer, 0, 0)),
                _resident((None, 1, Q_LORA), (layer, 0, 0)),
                _resident((None, 1, KV_LORA), (layer, 0, 0)),
                _resident((None, Q_LORA, wq_cols), (layer, 0, 0)),
                _resident((None, MLA_QK_W, KV_LORA), (layer, 0, 0)),
                _resident((None, MLA_QK_W, MLA_ROPE), (layer, 0, 0)),
                _resident((None, HD, KV_LORA), (layer, 0, 0))]
    args = [x, mod, w["ng1"], w["wat"], w["wbt"], w["wt"], w["conv_w"], w["conv_b"], w["qnorm"], w["kvnorm"],
            w["wq"], w["wkat"], w["wkbt"], w["wuvt"]]
    out_shape = [jax.ShapeDtypeStruct((t, CONV_DIM), BF16),
                 jax.ShapeDtypeStruct((t, HD), BF16)]
    out_specs = [row(CONV_DIM), row(HD)]
    aliases = {}
    n_alias = 0
    own_slot = 0
    if latent:
        assert tm == seq_len
        in_specs += [_resident((seq_len, MLA_QK_PAD), (0, 0)), _resident((seq_len, MLA_QK_PAD), (0, 0)),
                     _resident((MLA_ROPE, seq_len), (0, 0)), _resident((MLA_ROPE, seq_len), (0, 0))]
        args += list(tables)
        seq_blk = lambda n: pl.BlockSpec((None, n, seq_len), lambda i: (i, 0, 0))
        out_shape += [jax.ShapeDtypeStruct((n_seq, HD, seq_len), BF16),
                      jax.ShapeDtypeStruct((n_seq, HD, seq_len), BF16),
                      jax.ShapeDtypeStruct((t, MLA_QK_W), BF16),
                      jax.ShapeDtypeStruct((n_seq, MLA_QK_W, seq_len), BF16),
                      jax.ShapeDtypeStruct((n_seq, HD, seq_len), BF16)]
        out_specs += [seq_blk(HD), seq_blk(HD), row(MLA_QK_W), seq_blk(MLA_QK_W), seq_blk(HD)]
    else:
        if caches is not None:
            n_alias = len(caches)
            in_specs += [pl.BlockSpec(memory_space=pl.ANY)] * n_alias
            args += list(caches)
            aliases = {len(args) - n_alias + k: 2 + (0, 1, 5, 6)[k] for k in range(n_alias)}
            n_slots, first_slot = 1, layer
        else:
            assert layer == 0
            n_slots, first_slot, own_slot = DEPTH, 0, layer
        cache_blk = lambda *dims: pl.BlockSpec((seq_per_tile, n_slots) + dims,
                                               lambda i: (i, first_slot) + (0,) * len(dims))
        seq_blk = lambda n: pl.BlockSpec((seq_per_tile, n, seq_len), lambda i: (i, 0, 0))
        out_shape += [jax.ShapeDtypeStruct((n_seq, DEPTH, HEADS, NA_HD, seq_len), F32),
                      jax.ShapeDtypeStruct((n_seq, DEPTH, HEADS, NA_HD, seq_len), F32),
                      jax.ShapeDtypeStruct((t, MLA_QK_W), BF16),
                      jax.ShapeDtypeStruct((n_seq, MLA_QK_W, seq_len), BF16),
                      jax.ShapeDtypeStruct((n_seq, HD, seq_len), BF16),
                      jax.ShapeDtypeStruct((n_seq, DEPTH, seq_len, KV_LORA), F32),
                      jax.ShapeDtypeStruct((n_seq, DEPTH, MLA_ROPE, seq_len), F32)]
        out_specs += [cache_blk(HEADS, NA_HD, seq_len), cache_blk(HEADS, NA_HD, seq_len), row(MLA_QK_W),
                      seq_blk(MLA_QK_W), seq_blk(HD), cache_blk(seq_len, KV_LORA), cache_blk(MLA_ROPE, seq_len)]
    return pl.pallas_call(
        functools.partial(_proj_body, seq_len=seq_len, latent=latent, n_alias=n_alias, own_slot=own_slot),
        out_shape=out_shape,
        grid=(t // tm,),
        in_specs=in_specs,
        out_specs=out_specs,
        input_output_aliases=aliases,
        compiler_params=_params(1),
        name="mixer_proj",
    )(*args)


def _softmax_pv(s, vt):
    m = jnp.max(s, axis=-1, keepdims=True)
    p = jnp.exp2(s - m)
    den = jnp.sum(p, axis=-1, keepdims=True)
    return _dot_nt(p.astype(BF16), vt) / den


def _pair_slot(x, j):
    z = jnp.zeros_like(x)
    return jnp.concatenate([x, z] if j == 0 else [z, x], axis=0)


def _attn_ctx_body(qna_ref, knat_ref, vnat_ref, qm_ref, kmt_ref, vmt_ref, ona_ref, om_ref, *, seq_len):
    for b in range(qna_ref.shape[0] // seq_len):
        rows = slice(b * seq_len, (b + 1) * seq_len)
        for hp in range(HEADS // 2):
            pair = slice(hp * 2 * HEAD_V, (hp + 1) * 2 * HEAD_V)
            q = qna_ref[rows, pair]
            o_na = o_m = None
            for j in range(2):
                hh = 2 * hp + j
                kt = _pair_slot(knat_ref[b, hh].astype(BF16), j)
                vt = _pair_slot(vnat_ref[b, hh].astype(BF16), j)
                o = _softmax_pv(_dot(q, kt), vt)
                o_na = o if j == 0 else o_na + o
                qk = slice(hh * MLA_QK_PAD, (hh + 1) * MLA_QK_PAD)
                vt = _pair_slot(vmt_ref[b, hh * HEAD_V:(hh + 1) * HEAD_V, :], j)
                o = _softmax_pv(_dot(qm_ref[rows, qk], kmt_ref[b, qk, :]), vt)
                o_m = o if j == 0 else o_m + o
            ona_ref[rows, pair] = o_na.astype(BF16)
            om_ref[rows, pair] = o_m.astype(BF16)


def _attn_ctx(qna, knat, vnat, qm, kmt, vmt, layer, seq_len, tm=1024):
    t = qna.shape[0]
    nb = tm // seq_len
    row = lambda n: pl.BlockSpec((tm, n), lambda i: (i, 0))
    cache_blk = pl.BlockSpec((nb, None, HEADS, NA_HD, seq_len), lambda i: (i, layer, 0, 0, 0))
    seq_blk = lambda n: pl.BlockSpec((nb, n, seq_len), lambda i: (i, 0, 0))
    return pl.pallas_call(
        functools.partial(_attn_ctx_body, seq_len=seq_len),
        out_shape=[jax.ShapeDtypeStruct((t, HD), BF16), jax.ShapeDtypeStruct((t, HD), BF16)],
        grid=(t // tm,),
        in_specs=[row(HD), cache_blk, cache_blk, row(MLA_QK_W), seq_blk(MLA_QK_W), seq_blk(HD)],
        out_specs=[row(HD), row(HD)],
        compiler_params=_params(1),
        name="attn_ctx",
    )(qna, knat, vnat, qm, kmt, vmt)


def _attn_lat_body(qna_ref, knat_ref, vnat_ref, kctx_ref, vctx_ref, bias_ref,
                   qm_ref, kmt_ref, vmt_ref, cckv_ref, ckrt_ref, wkat_ref, wkbt_ref, wuvt_ref, ona_ref, om_ref):
    cat = lambda *a: jnp.concatenate(a, axis=1)
    head = lambda h: slice(h * HEAD_V, (h + 1) * HEAD_V)
    cckv = cckv_ref[...].astype(BF16)
    ckrt = ckrt_ref[...].astype(BF16)
    for pr in range(LAT_HEADS // 2):
        pair = slice(pr * 2 * HEAD_V, (pr + 1) * 2 * HEAD_V)
        kc = [_pair_slot(kctx_ref[2 * pr + j].astype(BF16), j) for j in range(2)]
        vc = [_pair_slot(vctx_ref[2 * pr + j].astype(BF16), j) for j in range(2)]
        for c, (start, count) in enumerate(NA_WINDOWS):
            rows = slice(c * Q_CHUNK, (c + 1) * Q_CHUNK)
            keys = slice(start, start + count)
            q = qna_ref[rows, pair]
            for j in range(2):
                h = 2 * pr + j
                bias = jnp.concatenate([cat(*[bias_ref[h, p] for p in NA_BLOCK_INDEX[c][rl]])
                                        for rl in range(Q_CHUNK // GRID_W)], axis=0)
                s = cat(_dot(q, kc[j]), _dot(q, _pair_slot(knat_ref[head(h), keys], j)) + bias)
                o = _softmax_pv(s, cat(vc[j], _pair_slot(vnat_ref[head(h), keys], j)))
                o_na = o if j == 0 else o_na + o
            ona_ref[rows, pair] = o_na.astype(BF16)
        qk_pair = slice(pr * 2 * MLA_QK_PAD, (pr + 1) * 2 * MLA_QK_PAD)
        km_ctx = (_dot_nt(wkat_ref[qk_pair, :], cckv) + _dot(wkbt_ref[qk_pair, :], ckrt)).astype(BF16)
        vm_ctx = _dot_nt(wuvt_ref[pair, :], cckv).astype(BF16)
        kt, vt = [], []
        for j in range(2):
            h = 2 * pr + j
            qk = slice(h * MLA_QK_PAD, (h + 1) * MLA_QK_PAD)
            kt.append(cat(km_ctx[j * MLA_QK_PAD:(j + 1) * MLA_QK_PAD, :], kmt_ref[qk, :]))
            vt.append(_pair_slot(cat(vm_ctx[head(j), :], vmt_ref[head(h), :]), j))
        for c in range(DEC_SEQ // MLA_Q_ROWS):
            rows = slice(c * MLA_Q_ROWS, (c + 1) * MLA_Q_ROWS)
            for j in range(2):
                h = 2 * pr + j
                o = _softmax_pv(_dot(qm_ref[rows, h * MLA_QK_PAD:(h + 1) * MLA_QK_PAD], kt[j]), vt[j])
                o_m = o if j == 0 else o_m + o
            om_ref[rows, pair] = o_m.astype(BF16)


def _attn_lat(qna, knat, vnat, kctx, vctx, bias, qm, kmt, vmt, cache_ckv, cache_krt, wkat, wkbt, wuvt, layer):
    t = qna.shape[0]
    s = DEC_SEQ
    n = LAT_HEADS
    return pl.pallas_call(
        _attn_lat_body,
        out_shape=[jax.ShapeDtypeStruct((t, HD), BF16), jax.ShapeDtypeStruct((t, HD), BF16)],
        grid=(HEADS // n, t // s),
        in_specs=[pl.BlockSpec((s, n * NA_HD), lambda hg, b: (b, hg)),
                  pl.BlockSpec((None, n * NA_HD, s), lambda hg, b: (b, hg, 0)),
                  pl.BlockSpec((None, n * HEAD_V, s), lambda hg, b: (b, hg, 0)),
                  pl.BlockSpec((None, None, n, NA_HD, PAST_LEN), lambda hg, b: (b, layer, hg, 0, 0)),
                  pl.BlockSpec((None, None, n, NA_HD, PAST_LEN), lambda hg, b: (b, layer, hg, 0, 0)),
                  pl.BlockSpec((None, n, len(NA_BLOCK_PAIRS), GRID_W, 2 * GRID_W),
                               lambda hg, b: (layer, hg, 0, 0, 0)),
                  pl.BlockSpec((s, n * MLA_QK_PAD), lambda hg, b: (b, hg)),
                  pl.BlockSpec((None, n * MLA_QK_PAD, s), lambda hg, b: (b, hg, 0)),
                  pl.BlockSpec((None, n * HEAD_V, s), lambda hg, b: (b, hg, 0)),
                  pl.BlockSpec((None, None, PAST_LEN, KV_LORA), lambda hg, b: (b, layer, 0, 0)),
                  pl.BlockSpec((None, None, MLA_ROPE, PAST_LEN), lambda hg, b: (b, layer, 0, 0)),
                  pl.BlockSpec((None, n * MLA_QK_PAD, KV_LORA), lambda hg, b: (layer, hg, 0)),
                  pl.BlockSpec((None, n * MLA_QK_PAD, MLA_ROPE), lambda hg, b: (layer, hg, 0)),
                  pl.BlockSpec((None, n * HEAD_V, KV_LORA), lambda hg, b: (layer, hg, 0))],
        out_specs=[pl.BlockSpec((s, n * HEAD_V), lambda hg, b: (b, hg)),
                   pl.BlockSpec((s, n * HEAD_V), lambda hg, b: (b, hg))],
        compiler_params=_params(2),
        name="attn_lat",
    )(qna, knat, vnat, kctx, vctx, bias, qm, kmt, vmt, cache_ckv, cache_krt, wkat, wkbt, wuvt)


def _mix_body(*refs, n_parts):
    tm = refs[-1].shape[0]
    parts, pos = [], 0
    for n in n_parts:
        parts.append(_rows_value(refs[pos:pos + n], tm))
        pos += n
    x, yc, ona, om = parts
    mod_ref, ng_ref, wgt_ref, wco_ref, wno_ref, wmo_ref, wo_ref, o_ref, z_scr = refs[pos:]
    mod = mod_ref[...]
    h = (_rms(x, ng_ref[...]) * (1 + mod[4:5]) + mod[3:4]).astype(BF16)
    for ch in range(D_MODEL // MIX_CHUNK):
        cols = slice(ch * MIX_CHUNK, (ch + 1) * MIX_CHUNK)
        gate = lambda k: jax.nn.sigmoid(
            _dot_nt(h, wgt_ref[k * D_MODEL + ch * MIX_CHUNK:k * D_MODEL + (ch + 1) * MIX_CHUNK, :]))
        z = gate(0) * _dot(yc, wco_ref[:, cols])
        z = z + gate(1) * _dot(ona, wno_ref[:, cols])
        z = z + gate(2) * _dot(om, wmo_ref[:, cols])
        z_scr[:, cols] = z.astype(BF16)
    o_ref[...] = x + mod[5:6] * _dot(z_scr[...], wo_ref[...])


def _mix(x, yc, ona, om, mod, w, layer):
    tm = 512 if isinstance(x, tuple) else 1024
    in_specs, args, n_parts = [], [], []
    for a in (x, yc, ona, om):
        specs, ops = _rows_specs(a, tm, 0)
        in_specs += specs
        args += ops
        n_parts.append(len(ops))
    in_specs += [_mod_spec(tm, 0),
                 _resident((None, 1, D_MODEL), (layer, 0, 0)),
                 _resident((3 * D_MODEL, D_MODEL), (0, 0)),
                 _resident((CONV_DIM, D_MODEL), (0, 0)),
                 _resident((HD, D_MODEL), (0, 0)),
                 _resident((HD, D_MODEL), (0, 0)),
                 _resident((D_MODEL, D_MODEL), (0, 0))]
    args += [mod, w["ng1"], w["wgt"], w["wco"], w["wno"], w["wmo"], w["wo"]]
    return pl.pallas_call(
        functools.partial(_mix_body, n_parts=tuple(n_parts)),
        out_shape=jax.ShapeDtypeStruct((N_ROWS, D_MODEL), F32),
        grid=(N_ROWS // tm,),
        in_specs=in_specs,
        out_specs=pl.BlockSpec((tm, D_MODEL), lambda i: (i, 0)),
        scratch_shapes=[pltpu.VMEM((tm, D_MODEL), BF16)],
        compiler_params=_params(1),
        name="mixer_out",
    )(*args)


def _rope_tables():
    f32 = np.float32
    half = MLA_ROPE // 2
    nf = half // 2
    inv = (f32(1.0) / (f32(ROPE_BASE) ** (np.arange(nf, dtype=f32) / f32(nf)))).astype(f32)
    t = np.arange(DEC_SEQ)
    rows = (t // GRID_W).astype(f32)[:, None] * inv[None, :]
    cols = (t % GRID_W).astype(f32)[:, None] * inv[None, :]
    cos = np.concatenate([np.cos(rows), np.cos(rows), np.cos(cols), np.cos(cols)], axis=-1).astype(f32)
    sin = np.concatenate([np.sin(rows), np.sin(rows), np.sin(cols), np.sin(cols)], axis=-1).astype(f32)
    pad = MLA_QK_PAD - MLA_NOPE - MLA_ROPE
    q_cos = np.concatenate([np.ones((DEC_SEQ, MLA_NOPE), f32), cos, np.zeros((DEC_SEQ, pad), f32)], axis=-1)
    q_sin = np.concatenate([np.zeros((DEC_SEQ, MLA_NOPE), f32), sin, np.zeros((DEC_SEQ, pad), f32)], axis=-1)
    return tuple(jnp.asarray(a) for a in (q_cos, q_sin, np.ascontiguousarray(cos.T), np.ascontiguousarray(sin.T)))


def _rope_swap(w):
    nf = MLA_ROPE // 4
    a, b, c, d = (w[..., i * nf:(i + 1) * nf] for i in range(4))
    return jnp.concatenate([-b, a, -d, c], axis=-1)


def _na_bias(rpb):
    n_dc = 2 * NA_WIN_C - 1
    col = np.arange(GRID_W)
    c_start = np.clip(col - NA_WIN_C // 2, 0, GRID_W - NA_WIN_C)
    c_in = (col[None, :] >= c_start[:, None]) & (col[None, :] < c_start[:, None] + NA_WIN_C)
    dc = np.clip(col[None, :] - col[:, None] + (NA_WIN_C - 1), 0, n_dc - 1)
    pick_dc = (dc[None] == np.arange(n_dc)[:, None, None]).astype(np.float32)
    n_pairs = len(NA_BLOCK_PAIRS)
    pick_dr = np.zeros((n_pairs, 2, NA_DR_MASKED), np.float32)
    for p, pair in enumerate(NA_BLOCK_PAIRS):
        for side, d in enumerate(pair):
            if d != NA_DR_MASKED:
                pick_dr[p, side, d] = 1.0
    keep = pick_dr.sum(-1).astype(bool)[:, None, :, None] & c_in[None, :, None, :]
    keep = keep.reshape(n_pairs, GRID_W, 2 * GRID_W)
    pick_side_dc = np.zeros((2, n_dc, GRID_W, 2, GRID_W), np.float32)
    for side in range(2):
        pick_side_dc[side, :, :, side, :] = pick_dc
    pick_side_dc = pick_side_dc.reshape(2, n_dc, GRID_W, 2 * GRID_W)
    hi = lax.Precision.HIGHEST
    by_row = jnp.einsum("psd,lhdj->lhpsj", jnp.asarray(pick_dr), rpb, precision=hi)
    blocks = jnp.einsum("lhpsj,sjqn->lhpqn", by_row, jnp.asarray(pick_side_dc), precision=hi)
    return jnp.where(jnp.asarray(keep), blocks * LOG2_E, NEG_INF)


def _pack_weights(w_int, w_uq, w_ukv):
    t_last = lambda a: jnp.swapaxes(a, -1, -2)
    w_mid = lax.optimization_barrier(w_int[:, W_IN_KV:W_IN_GATE]).astype(BF16)
    w_kvt = w_mid[:, :W_IN_LORA - W_IN_KV]
    w_krt = w_mid[:, W_IN_KR - W_IN_KV:]
    wt = jnp.concatenate([w_kvt, w_krt, t_last(_rope_swap(t_last(w_krt)))], axis=1)
    wbt = w_mid[:, W_IN_LORA - W_IN_KV:W_IN_KR - W_IN_KV]
    uq = w_uq.reshape(DEPTH, Q_LORA, MLA_HEADS, MLA_NOPE + MLA_ROPE)
    pad = MLA_QK_PAD - MLA_NOPE - MLA_ROPE
    zp = jnp.zeros(uq.shape[:-1] + (pad,), F32)
    zn = jnp.zeros(uq.shape[:-1] + (MLA_NOPE,), F32)
    q_ext = jnp.concatenate([uq, zp], axis=-1).reshape(DEPTH, Q_LORA, MLA_QK_W)
    q_sw = jnp.concatenate([zn, _rope_swap(uq[..., MLA_NOPE:]), zp], axis=-1).reshape(q_ext.shape)
    ukv = w_ukv.reshape(DEPTH, KV_LORA, MLA_HEADS, MLA_NOPE + MLA_V)
    zk = jnp.zeros(ukv.shape[:-1] + (MLA_QK_PAD - MLA_NOPE,), F32)
    wka = jnp.concatenate([ukv[..., :MLA_NOPE], zk], axis=-1).reshape(DEPTH, KV_LORA, MLA_QK_W)
    eye = jnp.concatenate([jnp.zeros((MLA_ROPE, MLA_NOPE), F32), jnp.eye(MLA_ROPE, dtype=F32),
                           jnp.zeros((MLA_ROPE, pad), F32)], axis=-1)
    wkb = jnp.broadcast_to(jnp.tile(eye, (1, MLA_HEADS))[None], (DEPTH, MLA_ROPE, MLA_QK_W))
    wuv = ukv[..., MLA_NOPE:].reshape(DEPTH, KV_LORA, HD)
    b = lambda a: a.astype(BF16)
    return dict(wt=b(wt), wbt=b(wbt),
                wq_lat=b(jnp.concatenate([q_ext, q_sw], axis=-1)), wq_ctx=b(q_ext),
                wkat=b(t_last(wka)), wkbt=b(t_last(wkb)), wuvt=b(t_last(wuv)))


def kernel(x_prompt, x_sample, cache_na_k, cache_na_v, cache_mla_ckv, cache_mla_krope, c, c_ctx,
           w_ada, b_ada, norm_g, w_ffn1_gate, w_ffn1_up, w_ffn1_down, w_ffn2_gate, w_ffn2_up, w_ffn2_down,
           w_in, conv_w, conv_b, na_rpb, mla_qnorm, w_uq, mla_kvnorm, w_ukv,
           w_conv_out, w_na_out, w_mla_out, w_o, final_g):
    t_last = lambda a: jnp.swapaxes(a, -1, -2)
    w_int = t_last(w_in)
    packed = _pack_weights(w_int, w_uq, w_ukv)
    shared = dict(conv_w=conv_w, conv_b=conv_b.reshape(DEPTH, 1, CONV_DIM),
                  qnorm=mla_qnorm.reshape(DEPTH, 1, Q_LORA), kvnorm=mla_kvnorm.reshape(DEPTH, 1, KV_LORA),
                  ng1=norm_g[:, 1:2],
                  **{k: packed[k] for k in ("wt", "wbt", "wkat", "wkbt", "wuvt")})
    ffn1_f32 = (w_ffn1_gate, w_ffn1_up, w_ffn1_down)
    ffn2_f32 = (w_ffn2_gate, w_ffn2_up, w_ffn2_down)
    ffn1_w = {}
    ffn2_w = {}
    mixer_w = {}
    mixer_srcs = ((w_int, 0, W_IN_KV), (w_int, W_IN_GATE, 3 * D_MODEL),
                  (w_conv_out, 0, None), (w_na_out, 0, None), (w_mla_out, 0, None), (w_o, 0, None))
    final_row = final_g.reshape(1, D_MODEL)

    c_all = jnp.concatenate([c_ctx[None], c, jnp.zeros((MOD_ROWS - 1 - DEC_BATCH, D_MODEL), F32)], axis=0)
    b_ada3 = b_ada.reshape(DEPTH, 1, N_MOD * D_MODEL)
    as_table = lambda m: m.reshape(MOD_ROWS, N_MOD, D_MODEL)
    mod0, cast = _modulation(c_all, w_ada, b_ada3, 0, [_cast_job(w, 0) for w in ffn1_f32])
    mod = {0: as_table(mod0)}
    ffn1_w[0] = tuple(cast)

    tables = _rope_tables()
    na_bias = _na_bias(na_rpb)
    ctx_k_na = t_last(cache_na_k)
    ctx_v_na = t_last(cache_na_v)
    ctx_krt = t_last(cache_mla_krope)

    xp = x_prompt.reshape(N_CTX_ROWS, D_MODEL)
    xs = x_sample.reshape(N_ROWS - N_CTX_ROWS, D_MODEL)
    caches = None
    x_all = None
    for l in range(DEPTH):
        last = l == DEPTH - 1
        jobs = [_cast_job(w, l) for w in ffn2_f32]
        if l == 0:
            jobs += [_cast_job(a, ll, r0, n) for ll in range(DEPTH) for a, r0, n in mixer_srcs]
            xp, cast = _ffn(xp, mod[l], norm_g[l, 0:1], *ffn1_w[l], 0, n_rows=N_CTX_ROWS, cast_jobs=jobs, tm=512)
            xs, side = _ffn(xs, mod[l], norm_g[l, 0:1], *ffn1_w[l], 0, row0=N_CTX_ROWS,
                            next_mod=(c_all, w_ada, b_ada3, 1) if DEPTH > 1 else None)
            if side:
                mod[1] = as_table(side[0])
            x1 = (xp, xs)
        else:
            if l not in mod:
                mod[l] = as_table(_modulation(c_all, w_ada, b_ada3, l)[0])
            x1, cast = _ffn(x_all, mod[l], norm_g[l, 0:1], *ffn1_w[l], 0, cast_jobs=jobs)
            xp = xs = x1
        ffn2_w[l] = tuple(cast[:3])
        for ll in range(DEPTH if l == 0 else 0):
            names = ("wat", "wgt", "wco", "wno", "wmo", "wo")
            mixer_w[ll] = dict(zip(names, cast[3 + len(names) * ll:3 + len(names) * (ll + 1)]))
        w_ctx = dict(shared, wq=packed["wq_ctx"], **mixer_w[l])
        w_lat = dict(shared, wq=packed["wq_lat"], **mixer_w[l])
        yc_p, qna, knat, vnat, qm, kmt, vmt, ckv, krt = _proj(xp, mod[l], w_ctx, l, SEQ, False, None, caches, 1024)
        caches = (knat, vnat, ckv, krt)
        ona_p, om_p = _attn_ctx(qna, knat, vnat, qm, kmt, vmt, l, SEQ)
        yc_s, qna, knat, vnat, qm, kmt, vmt = _proj(xs, mod[l], w_lat, l, DEC_SEQ, True, tables, None, DEC_SEQ)
        ona_s, om_s = _attn_lat(qna, knat, vnat, ctx_k_na, ctx_v_na, na_bias, qm, kmt, vmt, cache_mla_ckv, ctx_krt,
                                packed["wkat"], packed["wkbt"], packed["wuvt"], l)
        x2 = _mix(x1, (yc_p, yc_s), (ona_p, ona_s), (om_p, om_s), mod[l], w_ctx, l)
        if not last:
            x_all, cast = _ffn(x2, mod[l], norm_g[l, 2:3], *ffn2_w[l], 6,
                               cast_jobs=[_cast_job(w, l + 1) for w in ffn1_f32])
            ffn1_w[l + 1] = tuple(cast)
        else:
            yp, _ = _ffn(x2, mod[l], norm_g[l, 2:3], *ffn2_w[l], 6, n_rows=N_CTX_ROWS, final_g=final_row)
            ys, _ = _ffn(x2, mod[l], norm_g[l, 2:3], *ffn2_w[l], 6, row0=N_CTX_ROWS, final_g=final_row)
    new_kt, new_vt, new_ckv, new_krt = caches
    return (yp.reshape(BATCH, SEQ, D_MODEL), ys.reshape(DEC_BATCH, DEC_SEQ, D_MODEL),
            t_last(new_kt), t_last(new_vt), new_ckv, t_last(new_krt))
```

```python
import functools
import math

import jax
import jax.numpy as jnp
import numpy as np
from jax import lax
from jax.experimental import pallas as pl
from jax.experimental.pallas import tpu as pltpu

D_MODEL = 1024
BATCH = 32
SEQ = 256
DEPTH = 2
DEC_BATCH = 8
DEC_SEQ = 1024
PAST_LEN = 256
GRID_W = 64
CONV_DIM = 512
CONV_K = 3
NA_HEADS = 8
NA_HD = 64
NA_WIN_R = 8
NA_WIN_C = 16
MLA_HEADS = 8
MLA_NOPE = 64
MLA_ROPE = 32
MLA_V = 64
Q_LORA = 256
KV_LORA = 128
FFN_DIM = 2816
N_MOD = 9
ROPE_BASE = 10000.0
EPS = 1e-6
NEG_INF = -1e30
LOG2_E = 1.4426950408889634
MLA_SCALE = (MLA_NOPE + MLA_ROPE) ** -0.5 * LOG2_E
NA_SCALE = NA_HD ** -0.5 * LOG2_E

N_CTX_ROWS = BATCH * SEQ
N_ROWS = N_CTX_ROWS + DEC_BATCH * DEC_SEQ
HEADS = 8
HEAD_V = 64
HD = HEADS * NA_HD
W_IN_KV = 3 * CONV_DIM + HD
W_IN_LORA = W_IN_KV + 2 * HD
W_IN_KR = W_IN_LORA + Q_LORA + KV_LORA
W_IN_GATE = W_IN_KR + MLA_ROPE
MLA_QK_PAD = 128
MLA_QK_W = HEADS * MLA_QK_PAD
FFN_CHUNK = 256
CONV_CHUNK = 256
MIX_CHUNK = 256
Q_CHUNK = 256
MLA_Q_ROWS = 128
LAT_HEADS = 4
MOD_ROWS = 16
VMEM_LIMIT = 56 * 1024 * 1024
NA_WINDOWS = ((0, 512), (0, 768), (256, 768), (512, 512))
NA_DR_MASKED = 2 * NA_WIN_R - 1


def _na_block_pairs():
    rows = DEC_SEQ // GRID_W
    r_start = np.clip(np.arange(rows) - NA_WIN_R // 2, 0, rows - NA_WIN_R)
    pairs, index = [], []
    for c, (start, count) in enumerate(NA_WINDOWS):
        index.append([])
        for rl in range(Q_CHUNK // GRID_W):
            r = c * (Q_CHUNK // GRID_W) + rl
            assert start // GRID_W <= r_start[r] and r_start[r] + NA_WIN_R <= (start + count) // GRID_W
            index[c].append([])
            for kp in range(count // (2 * GRID_W)):
                pair = []
                for rk in (start // GRID_W + 2 * kp, start // GRID_W + 2 * kp + 1):
                    inside = r_start[r] <= rk < r_start[r] + NA_WIN_R
                    pair.append(int(rk - r + NA_WIN_R - 1) if inside else NA_DR_MASKED)
                pair = tuple(pair)
                if pair not in pairs:
                    pairs.append(pair)
                index[c][rl].append(pairs.index(pair))
    return tuple(pairs), index


NA_BLOCK_PAIRS, NA_BLOCK_INDEX = _na_block_pairs()

BF16 = jnp.bfloat16
F32 = jnp.float32


def _dot(a, b):
    return jnp.dot(a, b, preferred_element_type=F32)


def _dot_nt(a, b):
    return lax.dot_general(a, b, (((1,), (1,)), ((), ())), preferred_element_type=F32)


def _rms(x, g):
    return x * lax.rsqrt(jnp.mean(x * x, axis=-1, keepdims=True) + EPS) * g


def _params(n_axes):
    return pltpu.CompilerParams(dimension_semantics=("arbitrary",) * n_axes,
                                vmem_limit_bytes=VMEM_LIMIT)


def _resident(shape, index):
    return pl.BlockSpec(shape, lambda *_: index, pipeline_mode=pl.Buffered(1))


def _mod_body(c_ref, w_ref, b_ref, o_ref):
    c = c_ref[...]
    a = c * jax.nn.sigmoid(c)
    o_ref[...] = _dot(a.astype(BF16), w_ref[...].astype(BF16)) + b_ref[...]


def _mod_specs(layer, n_blocks):
    tn = N_MOD * D_MODEL // n_blocks
    assert tn % 128 == 0
    return ([pl.BlockSpec((MOD_ROWS, D_MODEL), lambda j: (0, 0)),
             pl.BlockSpec((None, D_MODEL, tn), lambda j: (layer, 0, j)),
             pl.BlockSpec((None, 1, tn), lambda j: (layer, 0, j))],
            pl.BlockSpec((MOD_ROWS, tn), lambda j: (0, j)))


def _modulation_body(c_ref, w_ref, b_ref, *rest, n_jobs):
    _mod_body(c_ref, w_ref, b_ref, rest[n_jobs])
    _cast_blocks(rest[:n_jobs], rest[n_jobs + 1:])


def _modulation(c_all, w_ada, b_ada, layer, cast_jobs=()):
    n_blocks = 4
    in_specs, out_spec = _mod_specs(layer, n_blocks)
    job_in, job_args, job_shapes, job_out = _job_specs(cast_jobs, n_blocks)
    outs = pl.pallas_call(
        functools.partial(_modulation_body, n_jobs=len(cast_jobs)),
        out_shape=[jax.ShapeDtypeStruct((MOD_ROWS, N_MOD * D_MODEL), F32)] + job_shapes,
        grid=(n_blocks,),
        in_specs=in_specs + job_in,
        out_specs=[out_spec] + job_out,
        compiler_params=_params(1),
        name="modulation",
    )(c_all, w_ada, b_ada, *job_args)
    return outs[0], list(outs[1:])


def _rows_specs(x, tm, tile0):
    if isinstance(x, tuple):
        assert tile0 == 0
        n_ctx = x[0].shape[0] // tm
        width = x[0].shape[1]
        return [pl.BlockSpec((tm, width), lambda i: (jnp.minimum(i, n_ctx - 1), 0)),
                pl.BlockSpec((tm, width), lambda i: (jnp.maximum(i - n_ctx, 0), 0))], list(x)
    return [pl.BlockSpec((tm, x.shape[1]), lambda i: (i + tile0, 0))], [x]


def _rows_value(refs, tm):
    if len(refs) == 1:
        return refs[0][...]
    return jnp.where(pl.program_id(0) < N_CTX_ROWS // tm, refs[0][...], refs[1][...])


def _mod_spec(tm, tile0):
    def index(i):
        g = (i + tile0) * tm
        return jnp.where(g < N_CTX_ROWS, 0, 1 + (g - N_CTX_ROWS) // DEC_SEQ), 0, 0
    return pl.BlockSpec((None, N_MOD, D_MODEL), index)


def _ffn_body(*refs, mod_off, final, n_jobs, n_x, mod_job):
    x_refs, (mod_ref, ng_ref, wg_ref, wu_ref, wd_ref), rest = refs[:n_x], refs[n_x:n_x + 5], refs[n_x + 5:]
    if final:
        fg_ref, rest = rest[0], rest[1:]
    job_in, rest = rest[:n_jobs], rest[n_jobs:]
    if mod_job:
        next_mod_in, rest = rest[:3], rest[3:]
    o_ref, job_out, a_scr = rest[0], rest[1:n_jobs + 1], rest[-1]
    x = _rows_value(x_refs, o_ref.shape[0])
    mod = mod_ref[...]
    shift = mod[mod_off:mod_off + 1]
    scale = mod[mod_off + 1:mod_off + 2]
    gate = mod[mod_off + 2:mod_off + 3]
    h = (_rms(x, ng_ref[...]) * (1 + scale) + shift).astype(BF16)
    for f in range(FFN_DIM // FFN_CHUNK):
        cols = slice(f * FFN_CHUNK, (f + 1) * FFN_CHUNK)
        g = _dot(h, wg_ref[:, cols])
        u = _dot(h, wu_ref[:, cols])
        a_scr[:, cols] = (g * jax.nn.sigmoid(g) * u).astype(BF16)
    y = _dot(a_scr[...], wd_ref[...])
    out = x + 0.5 * gate * y
    if final:
        out = _rms(out, fg_ref[...])
    o_ref[...] = out
    _cast_blocks(job_in, job_out)
    if mod_job:
        _mod_body(*next_mod_in, rest[n_jobs + 1])


def _cast_job(arr, layer, row_start=0, n_rows=None):
    return arr, layer, row_start, arr.shape[1] if n_rows is None else n_rows


def _job_specs(cast_jobs, n_steps):
    in_specs, args, out_shape, out_specs = [], [], [], []
    for arr, layer, row_start, rows in cast_jobs:
        width = arr.shape[2]
        blk = rows // n_steps
        assert blk * n_steps == rows and blk % 16 == 0 and row_start % 16 == 0
        if row_start % blk == 0:
            spec = pl.BlockSpec((None, blk, width), lambda i, l=layer, b0=row_start // blk: (l, b0 + i, 0))
        else:
            g = math.gcd(row_start, blk)
            spec = pl.BlockSpec((pl.Element(1), pl.Element(blk), pl.Element(width)),
                                lambda i, l=layer, r0=row_start // g, n=blk // g, g=g: (l, (r0 + n * i) * g, 0))
        in_specs.append(spec)
        args.append(arr)
        out_shape.append(jax.ShapeDtypeStruct((rows, width), BF16))
        out_specs.append(pl.BlockSpec((blk, width), lambda i: (i, 0)))
    return in_specs, args, out_shape, out_specs


def _cast_blocks(job_in, job_out):
    for src, dst in zip(job_in, job_out):
        dst[...] = src[...].reshape(dst.shape).astype(BF16)


def _ffn(x, mod, ng, wg, wu, wd, mod_off, row0=0, n_rows=None, final_g=None, cast_jobs=(), next_mod=None, tm=1024):
    t = N_ROWS - row0 if n_rows is None else n_rows
    n_steps = t // tm
    final = final_g is not None
    whole = isinstance(x, tuple) or x.shape[0] == N_ROWS
    x_specs, x_args = _rows_specs(x, tm, row0 // tm if whole else 0)
    in_specs = x_specs + [_mod_spec(tm, row0 // tm),
                          _resident((1, D_MODEL), (0, 0)),
                          _resident((D_MODEL, FFN_DIM), (0, 0)),
                          _resident((D_MODEL, FFN_DIM), (0, 0)),
                          _resident((FFN_DIM, D_MODEL), (0, 0))]
    args = x_args + [mod, ng, wg, wu, wd]
    if final:
        in_specs.append(_resident((1, D_MODEL), (0, 0)))
        args.append(final_g)
    out_shape = [jax.ShapeDtypeStruct((t, D_MODEL), F32)]
    out_specs = [pl.BlockSpec((tm, D_MODEL), lambda i: (i, 0))]
    job_in, job_args, job_shapes, job_out = _job_specs(cast_jobs, n_steps)
    in_specs += job_in
    args += job_args
    out_shape += job_shapes
    out_specs += job_out
    if next_mod is not None:
        mod_in, mod_out = _mod_specs(next_mod[3], n_steps)
        in_specs += mod_in
        args += list(next_mod[:3])
        out_shape.append(jax.ShapeDtypeStruct((MOD_ROWS, N_MOD * D_MODEL), F32))
        out_specs.append(mod_out)
    outs = pl.pallas_call(
        functools.partial(_ffn_body, mod_off=mod_off, final=final, n_jobs=len(cast_jobs), n_x=len(x_args),
                          mod_job=next_mod is not None),
        out_shape=out_shape,
        grid=(n_steps,),
        in_specs=in_specs,
        out_specs=out_specs,
        scratch_shapes=[pltpu.VMEM((tm, FFN_DIM), BF16)],
        compiler_params=_params(1),
        name="ffn",
    )(*args)
    return outs[0], list(outs[1:])


def _proj_body(x_ref, mod_ref, ng_ref, wat_ref, wbt_ref, wt_ref, cw_ref, cb_ref, qn_ref, kvn_ref,
               wq_ref, wkat_ref, wkbt_ref, wuvt_ref, *rest, seq_len, latent, n_alias, own_slot):
    rest = rest[n_alias:]

    def put(ref, b, value):
        for k in range(ref.shape[1]):
            ref[b, k] = value if k == own_slot else jnp.zeros_like(value)

    if latent:
        (qc_ref, qs_ref, kct_ref, kst_ref,
         yc_ref, qna_ref, knat_ref, vnat_ref, qm_ref, kmt_ref, vmt_ref) = rest
    else:
        (yc_ref, knat_ref, vnat_ref, ckv_ref, krt_ref, ona_ref, om_ref, qna_ref, qm_ref, kmt_ref, vmt_ref) = rest
    tm = x_ref.shape[0]
    x = x_ref[...]
    mod = mod_ref[...]
    h = (_rms(x, ng_ref[...]) * (1 + mod[4:5]) + mod[3:4]).astype(BF16)

    pos = lax.broadcasted_iota(jnp.int32, (tm, 1), 0) % seq_len
    cw = cw_ref[...]
    for ch in range(CONV_DIM // CONV_CHUNK):
        cols = slice(ch * CONV_CHUNK, (ch + 1) * CONV_CHUNK)
        part = lambda k: _dot_nt(h, wat_ref[k * CONV_DIM + ch * CONV_CHUNK:k * CONV_DIM + (ch + 1) * CONV_CHUNK, :])
        v = part(1) * part(2)
        v_prev = jnp.where(pos == 0, 0.0, pltpu.roll(v, 1, 0))
        v_next = jnp.where(pos == seq_len - 1, 0.0, pltpu.roll(v, tm - 1, 0))
        y = cb_ref[:, cols] + v_prev * cw[0:1, cols]
        y = y + v * cw[1:2, cols]
        y = y + v_next * cw[2:3, cols]
        yc_ref[:, cols] = (part(0) * y).astype(BF16)

    qna_ref[...] = (_dot_nt(h, wat_ref[3 * CONV_DIM:3 * CONV_DIM + HD, :]) * NA_SCALE).astype(BF16)

    kt_na = _dot_nt(wt_ref[0:HD, :], h)
    ut = _dot_nt(wt_ref[HD:, :], h)
    vt_na = ut[0:HD]
    krt = ut[HD:HD + MLA_ROPE]

    u = _dot_nt(h, wbt_ref[...])
    cq = _rms(u[:, 0:Q_LORA], qn_ref[...]).astype(BF16)
    ckv = _rms(u[:, Q_LORA:Q_LORA + KV_LORA], kvn_ref[...])
    ckv_b = ckv.astype(BF16)
    if latent:
        knat_ref[...] = kt_na.astype(BF16)
        vnat_ref[...] = vt_na.astype(BF16)
        krt = krt * kct_ref[...] + ut[HD + MLA_ROPE:HD + 2 * MLA_ROPE] * kst_ref[...]
        qc = jnp.concatenate([qc_ref[...]] * 2, axis=1)
        qs = jnp.concatenate([qs_ref[...]] * 2, axis=1)
        for hp in range(HEADS // 2):
            cols = slice(hp * 2 * MLA_QK_PAD, (hp + 1) * 2 * MLA_QK_PAD)
            sw_cols = slice(MLA_QK_W + hp * 2 * MLA_QK_PAD, MLA_QK_W + (hp + 1) * 2 * MLA_QK_PAD)
            q_rot = _dot(cq, wq_ref[:, cols]) * qc + _dot(cq, wq_ref[:, sw_cols]) * qs
            qm_ref[:, cols] = (q_rot * MLA_SCALE).astype(BF16)
    else:
        qm_ref[...] = (_dot(cq, wq_ref[...]) * MLA_SCALE).astype(BF16)
    kmt = _dot_nt(wkat_ref[...], ckv_b) + _dot(wkbt_ref[...], krt.astype(BF16))
    vmt = _dot_nt(wuvt_ref[...], ckv_b)
    if latent:
        kmt_ref[...] = kmt.astype(BF16)
        vmt_ref[...] = vmt.astype(BF16)
    else:
        for b in range(tm // seq_len):
            rows = slice(b * seq_len, (b + 1) * seq_len)
            put(knat_ref, b, kt_na[:, rows].reshape(HEADS, NA_HD, seq_len))
            put(vnat_ref, b, vt_na[:, rows].reshape(HEADS, NA_HD, seq_len))
            put(ckv_ref, b, ckv[rows])
            put(krt_ref, b, krt[:, rows])
            kmt_ref[b] = kmt[:, rows].astype(BF16)
            vmt_ref[b] = vmt[:, rows].astype(BF16)
        _ctx_attention(qna_ref, knat_ref, vnat_ref, qm_ref, kmt_ref, vmt_ref, ona_ref, om_ref, seq_len, own_slot)


def _proj(x, mod, w, layer, seq_len, latent, tables, caches, tm):
    row0 = N_CTX_ROWS if latent else 0
    t = N_ROWS - N_CTX_ROWS if latent else N_CTX_ROWS
    x_tile0 = row0 // tm if x.shape[0] == N_ROWS else 0
    n_seq = t // seq_len
    seq_per_tile = tm // seq_len
    wq_cols = w["wq"].shape[-1]
    wt_rows = w["wt"].shape[-2]
    row = lambda n: pl.BlockSpec((tm, n), lambda i: (i, 0))
    in_specs = [pl.BlockSpec((tm, D_MODEL), lambda i: (i + x_tile0, 0)),
                _mod_spec(tm, row0 // tm),
                _resident((None, 1, D_MODEL), (layer, 0, 0)),
                _resident((3 * CONV_DIM + HD, D_MODEL), (0, 0)),
                _resident((None, Q_LORA + KV_LORA, D_MODEL), (layer, 0, 0)),
                _resident((None, wt_rows, D_MODEL), (layer, 0, 0)),
                _resident((None, CONV_K, CONV_DIM), (layer, 0, 0)),
                _resident((None, 1, CONV_DIM), (layer, 0, 0)),
                _resident((None, 1, Q_LORA), (layer, 0, 0)),
                _resident((None, 1, KV_LORA), (layer, 0, 0)),
                _resident((None, Q_LORA, wq_cols), (layer, 0, 0)),
                _resident((None, MLA_QK_W, KV_LORA), (layer, 0, 0)),
                _resident((None, MLA_QK_W, MLA_ROPE), (layer, 0, 0)),
                _resident((None, HD, KV_LORA), (layer, 0, 0))]
    args = [x, mod, w["ng1"], w["wat"], w["wbt"], w["wt"], w["conv_w"], w["conv_b"], w["qnorm"], w["kvnorm"],
            w["wq"], w["wkat"], w["wkbt"], w["wuvt"]]
    out_shape = [jax.ShapeDtypeStruct((t, CONV_DIM), BF16)]
    out_specs = [row(CONV_DIM)]
    scratch = []
    aliases = {}
    n_alias = 0
    own_slot = 0
    if latent:
        out_shape.append(jax.ShapeDtypeStruct((t, HD), BF16))
        out_specs.append(row(HD))
        assert tm == seq_len
        in_specs += [_resident((seq_len, MLA_QK_PAD), (0, 0)), _resident((seq_len, MLA_QK_PAD), (0, 0)),
                     _resident((MLA_ROPE, seq_len), (0, 0)), _resident((MLA_ROPE, seq_len), (0, 0))]
        args += list(tables)
        seq_blk = lambda n: pl.BlockSpec((None, n, seq_len), lambda i: (i, 0, 0))
        out_shape += [jax.ShapeDtypeStruct((n_seq, HD, seq_len), BF16),
                      jax.ShapeDtypeStruct((n_seq, HD, seq_len), BF16),
                      jax.ShapeDtypeStruct((t, MLA_QK_W), BF16),
                      jax.ShapeDtypeStruct((n_seq, MLA_QK_W, seq_len), BF16),
                      jax.ShapeDtypeStruct((n_seq, HD, seq_len), BF16)]
        out_specs += [seq_blk(HD), seq_blk(HD), row(MLA_QK_W), seq_blk(MLA_QK_W), seq_blk(HD)]
    else:
        if caches is not None:
            n_alias = len(caches)
            in_specs += [pl.BlockSpec(memory_space=pl.ANY)] * n_alias
            args += list(caches)
            aliases = {len(args) - n_alias + k: 1 + k for k in range(n_alias)}
            n_slots, first_slot = 1, layer
        else:
            assert layer == 0
            n_slots, first_slot, own_slot = DEPTH, 0, layer
        cache_blk = lambda *dims: pl.BlockSpec((seq_per_tile, n_slots) + dims,
                                               lambda i: (i, first_slot) + (0,) * len(dims))
        out_shape += [jax.ShapeDtypeStruct((n_seq, DEPTH, HEADS, NA_HD, seq_len), F32),
                      jax.ShapeDtypeStruct((n_seq, DEPTH, HEADS, NA_HD, seq_len), F32),
                      jax.ShapeDtypeStruct((n_seq, DEPTH, seq_len, KV_LORA), F32),
                      jax.ShapeDtypeStruct((n_seq, DEPTH, MLA_ROPE, seq_len), F32),
                      jax.ShapeDtypeStruct((t, HD), BF16),
                      jax.ShapeDtypeStruct((t, HD), BF16)]
        out_specs += [cache_blk(HEADS, NA_HD, seq_len), cache_blk(HEADS, NA_HD, seq_len),
                      cache_blk(seq_len, KV_LORA), cache_blk(MLA_ROPE, seq_len), row(HD), row(HD)]
        scratch = [pltpu.VMEM((tm, HD), BF16), pltpu.VMEM((tm, MLA_QK_W), BF16),
                   pltpu.VMEM((seq_per_tile, MLA_QK_W, seq_len), BF16), pltpu.VMEM((seq_per_tile, HD, seq_len), BF16)]
    return pl.pallas_call(
        functools.partial(_proj_body, seq_len=seq_len, latent=latent, n_alias=n_alias, own_slot=own_slot),
        out_shape=out_shape,
        grid=(t // tm,),
        in_specs=in_specs,
        out_specs=out_specs,
        scratch_shapes=scratch,
        input_output_aliases=aliases,
        compiler_params=_params(1),
        name="mixer_proj",
    )(*args)


def _softmax_pv(s, vt):
    m = jnp.max(s, axis=-1, keepdims=True)
    p = jnp.exp2(s - m)
    den = jnp.sum(p, axis=-1, keepdims=True)
    return _dot_nt(p.astype(BF16), vt) / den


def _pair_slot(x, j):
    z = jnp.zeros_like(x)
    return jnp.concatenate([x, z] if j == 0 else [z, x], axis=0)


def _ctx_attention(qna_ref, knat_ref, vnat_ref, qm_ref, kmt_ref, vmt_ref, ona_ref, om_ref, seq_len, slot):
    for b in range(qna_ref.shape[0] // seq_len):
        rows = slice(b * seq_len, (b + 1) * seq_len)
        for hp in range(HEADS // 2):
            pair = slice(hp * 2 * HEAD_V, (hp + 1) * 2 * HEAD_V)
            q = qna_ref[rows, pair]
            o_na = o_m = None
            for j in range(2):
                hh = 2 * hp + j
                kt = _pair_slot(knat_ref[b, slot, hh].astype(BF16), j)
                vt = _pair_slot(vnat_ref[b, slot, hh].astype(BF16), j)
                o = _softmax_pv(_dot(q, kt), vt)
                o_na = o if j == 0 else o_na + o
                qk = slice(hh * MLA_QK_PAD, (hh + 1) * MLA_QK_PAD)
                vt = _pair_slot(vmt_ref[b, hh * HEAD_V:(hh + 1) * HEAD_V, :], j)
                o = _softmax_pv(_dot(qm_ref[rows, qk], kmt_ref[b, qk, :]), vt)
                o_m = o if j == 0 else o_m + o
            ona_ref[rows, pair] = o_na.astype(BF16)
            om_ref[rows, pair] = o_m.astype(BF16)


def _attn_lat_body(qna_ref, knat_ref, vnat_ref, kctx_ref, vctx_ref, bias_ref,
                   qm_ref, kmt_ref, vmt_ref, cckv_ref, ckrt_ref, wkat_ref, wkbt_ref, wuvt_ref, ona_ref, om_ref):
    cat = lambda *a: jnp.concatenate(a, axis=1)
    head = lambda h: slice(h * HEAD_V, (h + 1) * HEAD_V)
    cckv = cckv_ref[...].astype(BF16)
    ckrt = ckrt_ref[...].astype(BF16)
    for pr in range(LAT_HEADS // 2):
        pair = slice(pr * 2 * HEAD_V, (pr + 1) * 2 * HEAD_V)
        kc = [_pair_slot(kctx_ref[2 * pr + j].astype(BF16), j) for j in range(2)]
        vc = [_pair_slot(vctx_ref[2 * pr + j].astype(BF16), j) for j in range(2)]
        for c, (start, count) in enumerate(NA_WINDOWS):
            rows = slice(c * Q_CHUNK, (c + 1) * Q_CHUNK)
            keys = slice(start, start + count)
            q = qna_ref[rows, pair]
            for j in range(2):
                h = 2 * pr + j
                bias = jnp.concatenate([cat(*[bias_ref[h, p] for p in NA_BLOCK_INDEX[c][rl]])
                                        for rl in range(Q_CHUNK // GRID_W)], axis=0)
                s = cat(_dot(q, kc[j]), _dot(q, _pair_slot(knat_ref[head(h), keys], j)) + bias)
                o = _softmax_pv(s, cat(vc[j], _pair_slot(vnat_ref[head(h), keys], j)))
                o_na = o if j == 0 else o_na + o
            ona_ref[rows, pair] = o_na.astype(BF16)
        qk_pair = slice(pr * 2 * MLA_QK_PAD, (pr + 1) * 2 * MLA_QK_PAD)
        km_ctx = (_dot_nt(wkat_ref[qk_pair, :], cckv) + _dot(wkbt_ref[qk_pair, :], ckrt)).astype(BF16)
        vm_ctx = _dot_nt(wuvt_ref[pair, :], cckv).astype(BF16)
        kt, vt = [], []
        for j in range(2):
            h = 2 * pr + j
            qk = slice(h * MLA_QK_PAD, (h + 1) * MLA_QK_PAD)
            kt.append(cat(km_ctx[j * MLA_QK_PAD:(j + 1) * MLA_QK_PAD, :], kmt_ref[qk, :]))
            vt.append(_pair_slot(cat(vm_ctx[head(j), :], vmt_ref[head(h), :]), j))
        for c in range(DEC_SEQ // MLA_Q_ROWS):
            rows = slice(c * MLA_Q_ROWS, (c + 1) * MLA_Q_ROWS)
            for j in range(2):
                h = 2 * pr + j
                o = _softmax_pv(_dot(qm_ref[rows, h * MLA_QK_PAD:(h + 1) * MLA_QK_PAD], kt[j]), vt[j])
                o_m = o if j == 0 else o_m + o
            om_ref[rows, pair] = o_m.astype(BF16)


def _attn_lat(qna, knat, vnat, kctx, vctx, bias, qm, kmt, vmt, cache_ckv, cache_krt, wkat, wkbt, wuvt, layer):
    t = qna.shape[0]
    s = DEC_SEQ
    n = LAT_HEADS
    return pl.pallas_call(
        _attn_lat_body,
        out_shape=[jax.ShapeDtypeStruct((t, HD), BF16), jax.ShapeDtypeStruct((t, HD), BF16)],
        grid=(HEADS // n, t // s),
        in_specs=[pl.BlockSpec((s, n * NA_HD), lambda hg, b: (b, hg)),
                  pl.BlockSpec((None, n * NA_HD, s), lambda hg, b: (b, hg, 0)),
                  pl.BlockSpec((None, n * HEAD_V, s), lambda hg, b: (b, hg, 0)),
                  pl.BlockSpec((None, None, n, NA_HD, PAST_LEN), lambda hg, b: (b, layer, hg, 0, 0)),
                  pl.BlockSpec((None, None, n, NA_HD, PAST_LEN), lambda hg, b: (b, layer, hg, 0, 0)),
                  pl.BlockSpec((None, n, len(NA_BLOCK_PAIRS), GRID_W, 2 * GRID_W),
                               lambda hg, b: (layer, hg, 0, 0, 0)),
                  pl.BlockSpec((s, n * MLA_QK_PAD), lambda hg, b: (b, hg)),
                  pl.BlockSpec((None, n * MLA_QK_PAD, s), lambda hg, b: (b, hg, 0)),
                  pl.BlockSpec((None, n * HEAD_V, s), lambda hg, b: (b, hg, 0)),
                  pl.BlockSpec((None, None, PAST_LEN, KV_LORA), lambda hg, b: (b, layer, 0, 0)),
                  pl.BlockSpec((None, None, MLA_ROPE, PAST_LEN), lambda hg, b: (b, layer, 0, 0)),
                  pl.BlockSpec((None, n * MLA_QK_PAD, KV_LORA), lambda hg, b: (layer, hg, 0)),
                  pl.BlockSpec((None, n * MLA_QK_PAD, MLA_ROPE), lambda hg, b: (layer, hg, 0)),
                  pl.BlockSpec((None, n * HEAD_V, KV_LORA), lambda hg, b: (layer, hg, 0))],
        out_specs=[pl.BlockSpec((s, n * HEAD_V), lambda hg, b: (b, hg)),
                   pl.BlockSpec((s, n * HEAD_V), lambda hg, b: (b, hg))],
        compiler_params=_params(2),
        name="attn_lat",
    )(qna, knat, vnat, kctx, vctx, bias, qm, kmt, vmt, cache_ckv, cache_krt, wkat, wkbt, wuvt)


def _mix_body(*refs, n_parts):
    tm = refs[-1].shape[0]
    parts, pos = [], 0
    for n in n_parts:
        parts.append(_rows_value(refs[pos:pos + n], tm))
        pos += n
    x, yc, ona, om = parts
    mod_ref, ng_ref, wgt_ref, wco_ref, wno_ref, wmo_ref, wo_ref, o_ref, z_scr = refs[pos:]
    mod = mod_ref[...]
    h = (_rms(x, ng_ref[...]) * (1 + mod[4:5]) + mod[3:4]).astype(BF16)
    for ch in range(D_MODEL // MIX_CHUNK):
        cols = slice(ch * MIX_CHUNK, (ch + 1) * MIX_CHUNK)
        gate = lambda k: jax.nn.sigmoid(
            _dot_nt(h, wgt_ref[k * D_MODEL + ch * MIX_CHUNK:k * D_MODEL + (ch + 1) * MIX_CHUNK, :]))
        z = gate(0) * _dot(yc, wco_ref[:, cols])
        z = z + gate(1) * _dot(ona, wno_ref[:, cols])
        z = z + gate(2) * _dot(om, wmo_ref[:, cols])
        z_scr[:, cols] = z.astype(BF16)
    o_ref[...] = x + mod[5:6] * _dot(z_scr[...], wo_ref[...])


def _mix(x, yc, ona, om, mod, w, layer):
    tm = 512 if isinstance(x, tuple) else 1024
    in_specs, args, n_parts = [], [], []
    for a in (x, yc, ona, om):
        specs, ops = _rows_specs(a, tm, 0)
        in_specs += specs
        args += ops
        n_parts.append(len(ops))
    in_specs += [_mod_spec(tm, 0),
                 _resident((None, 1, D_MODEL), (layer, 0, 0)),
                 _resident((3 * D_MODEL, D_MODEL), (0, 0)),
                 _resident((CONV_DIM, D_MODEL), (0, 0)),
                 _resident((HD, D_MODEL), (0, 0)),
                 _resident((HD, D_MODEL), (0, 0)),
                 _resident((D_MODEL, D_MODEL), (0, 0))]
    args += [mod, w["ng1"], w["wgt"], w["wco"], w["wno"], w["wmo"], w["wo"]]
    return pl.pallas_call(
        functools.partial(_mix_body, n_parts=tuple(n_parts)),
        out_shape=jax.ShapeDtypeStruct((N_ROWS, D_MODEL), F32),
        grid=(N_ROWS // tm,),
        in_specs=in_specs,
        out_specs=pl.BlockSpec((tm, D_MODEL), lambda i: (i, 0)),
        scratch_shapes=[pltpu.VMEM((tm, D_MODEL), BF16)],
        compiler_params=_params(1),
        name="mixer_out",
    )(*args)


def _rope_tables():
    f32 = np.float32
    half = MLA_ROPE // 2
    nf = half // 2
    inv = (f32(1.0) / (f32(ROPE_BASE) ** (np.arange(nf, dtype=f32) / f32(nf)))).astype(f32)
    t = np.arange(DEC_SEQ)
    rows = (t // GRID_W).astype(f32)[:, None] * inv[None, :]
    cols = (t % GRID_W).astype(f32)[:, None] * inv[None, :]
    cos = np.concatenate([np.cos(rows), np.cos(rows), np.cos(cols), np.cos(cols)], axis=-1).astype(f32)
    sin = np.concatenate([np.sin(rows), np.sin(rows), np.sin(cols), np.sin(cols)], axis=-1).astype(f32)
    pad = MLA_QK_PAD - MLA_NOPE - MLA_ROPE
    q_cos = np.concatenate([np.ones((DEC_SEQ, MLA_NOPE), f32), cos, np.zeros((DEC_SEQ, pad), f32)], axis=-1)
    q_sin = np.concatenate([np.zeros((DEC_SEQ, MLA_NOPE), f32), sin, np.zeros((DEC_SEQ, pad), f32)], axis=-1)
    return tuple(jnp.asarray(a) for a in (q_cos, q_sin, np.ascontiguousarray(cos.T), np.ascontiguousarray(sin.T)))


def _rope_swap(w):
    nf = MLA_ROPE // 4
    a, b, c, d = (w[..., i * nf:(i + 1) * nf] for i in range(4))
    return jnp.concatenate([-b, a, -d, c], axis=-1)


def _na_bias(rpb):
    n_dc = 2 * NA_WIN_C - 1
    col = np.arange(GRID_W)
    c_start = np.clip(col - NA_WIN_C // 2, 0, GRID_W - NA_WIN_C)
    c_in = (col[None, :] >= c_start[:, None]) & (col[None, :] < c_start[:, None] + NA_WIN_C)
    dc = np.clip(col[None, :] - col[:, None] + (NA_WIN_C - 1), 0, n_dc - 1)
    pick_dc = (dc[None] == np.arange(n_dc)[:, None, None]).astype(np.float32)
    n_pairs = len(NA_BLOCK_PAIRS)
    pick_dr = np.zeros((n_pairs, 2, NA_DR_MASKED), np.float32)
    for p, pair in enumerate(NA_BLOCK_PAIRS):
        for side, d in enumerate(pair):
            if d != NA_DR_MASKED:
                pick_dr[p, side, d] = 1.0
    keep = pick_dr.sum(-1).astype(bool)[:, None, :, None] & c_in[None, :, None, :]
    keep = keep.reshape(n_pairs, GRID_W, 2 * GRID_W)
    pick_side_dc = np.zeros((2, n_dc, GRID_W, 2, GRID_W), np.float32)
    for side in range(2):
        pick_side_dc[side, :, :, side, :] = pick_dc
    pick_side_dc = pick_side_dc.reshape(2, n_dc, GRID_W, 2 * GRID_W)
    hi = lax.Precision.HIGHEST
    by_row = jnp.einsum("psd,lhdj->lhpsj", jnp.asarray(pick_dr), rpb, precision=hi)
    blocks = jnp.einsum("lhpsj,sjqn->lhpqn", by_row, jnp.asarray(pick_side_dc), precision=hi)
    return jnp.where(jnp.asarray(keep), blocks * LOG2_E, NEG_INF)


def _pack_weights(w_int, w_uq, w_ukv):
    t_last = lambda a: jnp.swapaxes(a, -1, -2)
    w_mid = lax.optimization_barrier(w_int[:, W_IN_KV:W_IN_GATE]).astype(BF16)
    w_kvt = w_mid[:, :W_IN_LORA - W_IN_KV]
    w_krt = w_mid[:, W_IN_KR - W_IN_KV:]
    wt = jnp.concatenate([w_kvt, w_krt, t_last(_rope_swap(t_last(w_krt)))], axis=1)
    wbt = w_mid[:, W_IN_LORA - W_IN_KV:W_IN_KR - W_IN_KV]
    uq = w_uq.reshape(DEPTH, Q_LORA, MLA_HEADS, MLA_NOPE + MLA_ROPE)
    pad = MLA_QK_PAD - MLA_NOPE - MLA_ROPE
    zp = jnp.zeros(uq.shape[:-1] + (pad,), F32)
    zn = jnp.zeros(uq.shape[:-1] + (MLA_NOPE,), F32)
    q_ext = jnp.concatenate([uq, zp], axis=-1).reshape(DEPTH, Q_LORA, MLA_QK_W)
    q_sw = jnp.concatenate([zn, _rope_swap(uq[..., MLA_NOPE:]), zp], axis=-1).reshape(q_ext.shape)
    ukv = w_ukv.reshape(DEPTH, KV_LORA, MLA_HEADS, MLA_NOPE + MLA_V)
    zk = jnp.zeros(ukv.shape[:-1] + (MLA_QK_PAD - MLA_NOPE,), F32)
    wka = jnp.concatenate([ukv[..., :MLA_NOPE], zk], axis=-1).reshape(DEPTH, KV_LORA, MLA_QK_W)
    eye = jnp.concatenate([jnp.zeros((MLA_ROPE, MLA_NOPE), F32), jnp.eye(MLA_ROPE, dtype=F32),
                           jnp.zeros((MLA_ROPE, pad), F32)], axis=-1)
    wkb = jnp.broadcast_to(jnp.tile(eye, (1, MLA_HEADS))[None], (DEPTH, MLA_ROPE, MLA_QK_W))
    wuv = ukv[..., MLA_NOPE:].reshape(DEPTH, KV_LORA, HD)
    b = lambda a: a.astype(BF16)
    return dict(wt=b(wt), wbt=b(wbt),
                wq_lat=b(jnp.concatenate([q_ext, q_sw], axis=-1)), wq_ctx=b(q_ext),
                wkat=b(t_last(wka)), wkbt=b(t_last(wkb)), wuvt=b(t_last(wuv)))


def kernel(x_prompt, x_sample, cache_na_k, cache_na_v, cache_mla_ckv, cache_mla_krope, c, c_ctx,
           w_ada, b_ada, norm_g, w_ffn1_gate, w_ffn1_up, w_ffn1_down, w_ffn2_gate, w_ffn2_up, w_ffn2_down,
           w_in, conv_w, conv_b, na_rpb, mla_qnorm, w_uq, mla_kvnorm, w_ukv,
           w_conv_out, w_na_out, w_mla_out, w_o, final_g):
    t_last = lambda a: jnp.swapaxes(a, -1, -2)
    w_int = t_last(w_in)
    packed = _pack_weights(w_int, w_uq, w_ukv)
    shared = dict(conv_w=conv_w, conv_b=conv_b.reshape(DEPTH, 1, CONV_DIM),
                  qnorm=mla_qnorm.reshape(DEPTH, 1, Q_LORA), kvnorm=mla_kvnorm.reshape(DEPTH, 1, KV_LORA),
                  ng1=norm_g[:, 1:2],
                  **{k: packed[k] for k in ("wt", "wbt", "wkat", "wkbt", "wuvt")})
    ffn1_f32 = (w_ffn1_gate, w_ffn1_up, w_ffn1_down)
    ffn2_f32 = (w_ffn2_gate, w_ffn2_up, w_ffn2_down)
    ffn1_w = {}
    ffn2_w = {}
    mixer_w = {}
    mixer_srcs = ((w_int, 0, W_IN_KV), (w_int, W_IN_GATE, 3 * D_MODEL),
                  (w_conv_out, 0, None), (w_na_out, 0, None), (w_mla_out, 0, None), (w_o, 0, None))
    final_row = final_g.reshape(1, D_MODEL)

    c_all = jnp.concatenate([c_ctx[None], c, jnp.zeros((MOD_ROWS - 1 - DEC_BATCH, D_MODEL), F32)], axis=0)
    b_ada3 = b_ada.reshape(DEPTH, 1, N_MOD * D_MODEL)
    as_table = lambda m: m.reshape(MOD_ROWS, N_MOD, D_MODEL)
    mod0, cast = _modulation(c_all, w_ada, b_ada3, 0, [_cast_job(w, 0) for w in ffn1_f32])
    mod = {0: as_table(mod0)}
    ffn1_w[0] = tuple(cast)

    tables = _rope_tables()
    na_bias = _na_bias(na_rpb)
    ctx_k_na = t_last(cache_na_k)
    ctx_v_na = t_last(cache_na_v)
    ctx_krt = t_last(cache_mla_krope)

    xp = x_prompt.reshape(N_CTX_ROWS, D_MODEL)
    xs = x_sample.reshape(N_ROWS - N_CTX_ROWS, D_MODEL)
    caches = None
    x_all = None
    for l in range(DEPTH):
        last = l == DEPTH - 1
        jobs = [_cast_job(w, l) for w in ffn2_f32]
        if l == 0:
            jobs += [_cast_job(a, ll, r0, n) for ll in range(DEPTH) for a, r0, n in mixer_srcs]
            xp, cast = _ffn(xp, mod[l], norm_g[l, 0:1], *ffn1_w[l], 0, n_rows=N_CTX_ROWS, cast_jobs=jobs, tm=512)
            xs, side = _ffn(xs, mod[l], norm_g[l, 0:1], *ffn1_w[l], 0, row0=N_CTX_ROWS,
                            next_mod=(c_all, w_ada, b_ada3, 1) if DEPTH > 1 else None)
            if side:
                mod[1] = as_table(side[0])
            x1 = (xp, xs)
        else:
            if l not in mod:
                mod[l] = as_table(_modulation(c_all, w_ada, b_ada3, l)[0])
            x1, cast = _ffn(x_all, mod[l], norm_g[l, 0:1], *ffn1_w[l], 0, cast_jobs=jobs)
            xp = xs = x1
        ffn2_w[l] = tuple(cast[:3])
        for ll in range(DEPTH if l == 0 else 0):
            names = ("wat", "wgt", "wco", "wno", "wmo", "wo")
            mixer_w[ll] = dict(zip(names, cast[3 + len(names) * ll:3 + len(names) * (ll + 1)]))
        w_ctx = dict(shared, wq=packed["wq_ctx"], **mixer_w[l])
        w_lat = dict(shared, wq=packed["wq_lat"], **mixer_w[l])
        yc_p, knat, vnat, ckv, krt, ona_p, om_p = _proj(xp, mod[l], w_ctx, l, SEQ, False, None, caches, 1024)
        caches = (knat, vnat, ckv, krt)
        yc_s, qna, knat, vnat, qm, kmt, vmt = _proj(xs, mod[l], w_lat, l, DEC_SEQ, True, tables, None, DEC_SEQ)
        ona_s, om_s = _attn_lat(qna, knat, vnat, ctx_k_na, ctx_v_na, na_bias, qm, kmt, vmt, cache_mla_ckv, ctx_krt,
                                packed["wkat"], packed["wkbt"], packed["wuvt"], l)
        x2 = _mix(x1, (yc_p, yc_s), (ona_p, ona_s), (om_p, om_s), mod[l], w_ctx, l)
        if not last:
            x_all, cast = _ffn(x2, mod[l], norm_g[l, 2:3], *ffn2_w[l], 6,
                               cast_jobs=[_cast_job(w, l + 1) for w in ffn1_f32])
            ffn1_w[l + 1] = tuple(cast)
        else:
            yp, _ = _ffn(x2, mod[l], norm_g[l, 2:3], *ffn2_w[l], 6, n_rows=N_CTX_ROWS, final_g=final_row)
            ys, _ = _ffn(x2, mod[l], norm_g[l, 2:3], *ffn2_w[l], 6, row0=N_CTX_ROWS, final_g=final_row)
    new_kt, new_vt, new_ckv, new_krt = caches
    return (yp.reshape(BATCH, SEQ, D_MODEL), ys.reshape(DEC_BATCH, DEC_SEQ, D_MODEL),
            t_last(new_kt), t_last(new_vt), new_ckv, t_last(new_krt))
```

```python
import functools
import math

import jax
import jax.numpy as jnp
import numpy as np
from jax import lax
from jax.experimental import pallas as pl
from jax.experimental.pallas import tpu as pltpu

D_MODEL = 1024
BATCH = 32
SEQ = 256
DEPTH = 2
DEC_BATCH = 8
DEC_SEQ = 1024
PAST_LEN = 256
GRID_W = 64
CONV_DIM = 512
CONV_K = 3
NA_HEADS = 8
NA_HD = 64
NA_WIN_R = 8
NA_WIN_C = 16
MLA_HEADS = 8
MLA_NOPE = 64
MLA_ROPE = 32
MLA_V = 64
Q_LORA = 256
KV_LORA = 128
FFN_DIM = 2816
N_MOD = 9
ROPE_BASE = 10000.0
EPS = 1e-6
NEG_INF = -1e30
LOG2_E = 1.4426950408889634
MLA_SCALE = (MLA_NOPE + MLA_ROPE) ** -0.5 * LOG2_E
NA_SCALE = NA_HD ** -0.5 * LOG2_E

N_CTX_ROWS = BATCH * SEQ
N_ROWS = N_CTX_ROWS + DEC_BATCH * DEC_SEQ
HEADS = 8
HEAD_V = 64
HD = HEADS * NA_HD
W_IN_KV = 3 * CONV_DIM + HD
W_IN_LORA = W_IN_KV + 2 * HD
W_IN_KR = W_IN_LORA + Q_LORA + KV_LORA
W_IN_GATE = W_IN_KR + MLA_ROPE
MLA_QK_PAD = 128
MLA_QK_W = HEADS * MLA_QK_PAD
FFN_CHUNK = 256
CONV_CHUNK = 256
MIX_CHUNK = 256
Q_CHUNK = 256
MLA_Q_ROWS = 128
LAT_HEADS = 4
MOD_ROWS = 16
VMEM_LIMIT = 56 * 1024 * 1024
NA_WINDOWS = ((0, 512), (0, 768), (256, 768), (512, 512))
NA_DR_MASKED = 2 * NA_WIN_R - 1


def _na_block_pairs():
    rows = DEC_SEQ // GRID_W
    r_start = np.clip(np.arange(rows) - NA_WIN_R // 2, 0, rows - NA_WIN_R)
    pairs, index = [], []
    for c, (start, count) in enumerate(NA_WINDOWS):
        index.append([])
        for rl in range(Q_CHUNK // GRID_W):
            r = c * (Q_CHUNK // GRID_W) + rl
            assert start // GRID_W <= r_start[r] and r_start[r] + NA_WIN_R <= (start + count) // GRID_W
            index[c].append([])
            for kp in range(count // (2 * GRID_W)):
                pair = []
                for rk in (start // GRID_W + 2 * kp, start // GRID_W + 2 * kp + 1):
                    inside = r_start[r] <= rk < r_start[r] + NA_WIN_R
                    pair.append(int(rk - r + NA_WIN_R - 1) if inside else NA_DR_MASKED)
                pair = tuple(pair)
                if pair not in pairs:
                    pairs.append(pair)
                index[c][rl].append(pairs.index(pair))
    return tuple(pairs), index


NA_BLOCK_PAIRS, NA_BLOCK_INDEX = _na_block_pairs()

BF16 = jnp.bfloat16
F32 = jnp.float32


def _dot(a, b):
    return jnp.dot(a, b, preferred_element_type=F32)


def _dot_nt(a, b):
    return lax.dot_general(a, b, (((1,), (1,)), ((), ())), preferred_element_type=F32)


def _rms(x, g):
    return x * lax.rsqrt(jnp.mean(x * x, axis=-1, keepdims=True) + EPS) * g


def _params(n_axes):
    return pltpu.CompilerParams(dimension_semantics=("arbitrary",) * n_axes,
                                vmem_limit_bytes=VMEM_LIMIT)


def _resident(shape, index):
    return pl.BlockSpec(shape, lambda *_: index, pipeline_mode=pl.Buffered(1))


def _mod_body(c_ref, w_ref, b_ref, o_ref):
    c = c_ref[...]
    a = c * jax.nn.sigmoid(c)
    o_ref[...] = _dot(a.astype(BF16), w_ref[...].astype(BF16)) + b_ref[...]


def _mod_specs(layer, n_blocks):
    tn = N_MOD * D_MODEL // n_blocks
    assert tn % 128 == 0
    return ([pl.BlockSpec((MOD_ROWS, D_MODEL), lambda j: (0, 0)),
             pl.BlockSpec((None, D_MODEL, tn), lambda j: (layer, 0, j)),
             pl.BlockSpec((None, 1, tn), lambda j: (layer, 0, j))],
            pl.BlockSpec((MOD_ROWS, tn), lambda j: (0, j)))


def _modulation_body(c_ref, w_ref, b_ref, *rest, n_jobs):
    _mod_body(c_ref, w_ref, b_ref, rest[n_jobs])
    _cast_blocks(rest[:n_jobs], rest[n_jobs + 1:])


def _modulation(c_all, w_ada, b_ada, layer, cast_jobs=()):
    n_blocks = 4
    in_specs, out_spec = _mod_specs(layer, n_blocks)
    job_in, job_args, job_shapes, job_out = _job_specs(cast_jobs, n_blocks)
    outs = pl.pallas_call(
        functools.partial(_modulation_body, n_jobs=len(cast_jobs)),
        out_shape=[jax.ShapeDtypeStruct((MOD_ROWS, N_MOD * D_MODEL), F32)] + job_shapes,
        grid=(n_blocks,),
        in_specs=in_specs + job_in,
        out_specs=[out_spec] + job_out,
        compiler_params=_params(1),
        name="modulation",
    )(c_all, w_ada, b_ada, *job_args)
    return outs[0], list(outs[1:])


def _rows_specs(x, tm, tile0):
    if isinstance(x, tuple):
        assert tile0 == 0
        n_ctx = x[0].shape[0] // tm
        width = x[0].shape[1]
        return [pl.BlockSpec((tm, width), lambda i: (jnp.minimum(i, n_ctx - 1), 0)),
                pl.BlockSpec((tm, width), lambda i: (jnp.maximum(i - n_ctx, 0), 0))], list(x)
    return [pl.BlockSpec((tm, x.shape[1]), lambda i: (i + tile0, 0))], [x]


def _rows_value(refs, tm):
    if len(refs) == 1:
        return refs[0][...]
    return jnp.where(pl.program_id(0) < N_CTX_ROWS // tm, refs[0][...], refs[1][...])


def _mod_spec(tm, tile0):
    def index(i):
        g = (i + tile0) * tm
        return jnp.where(g < N_CTX_ROWS, 0, 1 + (g - N_CTX_ROWS) // DEC_SEQ), 0, 0
    return pl.BlockSpec((None, N_MOD, D_MODEL), index)


def _ffn_body(*refs, mod_off, final, n_jobs, n_x, mod_job):
    x_refs, (mod_ref, ng_ref, wg_ref, wu_ref, wd_ref), rest = refs[:n_x], refs[n_x:n_x + 5], refs[n_x + 5:]
    if final:
        fg_ref, rest = rest[0], rest[1:]
    job_in, rest = rest[:n_jobs], rest[n_jobs:]
    if mod_job:
        next_mod_in, rest = rest[:3], rest[3:]
    o_ref, job_out, a_scr = rest[0], rest[1:n_jobs + 1], rest[-1]
    x = _rows_value(x_refs, o_ref.shape[0])
    mod = mod_ref[...]
    shift = mod[mod_off:mod_off + 1]
    scale = mod[mod_off + 1:mod_off + 2]
    gate = mod[mod_off + 2:mod_off + 3]
    h = (_rms(x, ng_ref[...]) * (1 + scale) + shift).astype(BF16)
    for f in range(FFN_DIM // FFN_CHUNK):
        cols = slice(f * FFN_CHUNK, (f + 1) * FFN_CHUNK)
        g = _dot(h, wg_ref[:, cols])
        u = _dot(h, wu_ref[:, cols])
        a_scr[:, cols] = (g * jax.nn.sigmoid(g) * u).astype(BF16)
    y = _dot(a_scr[...], wd_ref[...])
    out = x + 0.5 * gate * y
    if final:
        out = _rms(out, fg_ref[...])
    o_ref[...] = out
    _cast_blocks(job_in, job_out)
    if mod_job:
        _mod_body(*next_mod_in, rest[n_jobs + 1])


def _cast_job(arr, layer, row_start=0, n_rows=None):
    return arr, layer, row_start, arr.shape[1] if n_rows is None else n_rows


def _job_specs(cast_jobs, n_steps):
    in_specs, args, out_shape, out_specs = [], [], [], []
    for arr, layer, row_start, rows in cast_jobs:
        width = arr.shape[2]
        blk = rows // n_steps
        assert blk * n_steps == rows and blk % 16 == 0 and row_start % 16 == 0
        if row_start % blk == 0:
            spec = pl.BlockSpec((None, blk, width), lambda i, l=layer, b0=row_start // blk: (l, b0 + i, 0))
        else:
            g = math.gcd(row_start, blk)
            spec = pl.BlockSpec((pl.Element(1), pl.Element(blk), pl.Element(width)),
                                lambda i, l=layer, r0=row_start // g, n=blk // g, g=g: (l, (r0 + n * i) * g, 0))
        in_specs.append(spec)
        args.append(arr)
        out_shape.append(jax.ShapeDtypeStruct((rows, width), BF16))
        out_specs.append(pl.BlockSpec((blk, width), lambda i: (i, 0)))
    return in_specs, args, out_shape, out_specs


def _cast_blocks(job_in, job_out):
    for src, dst in zip(job_in, job_out):
        dst[...] = src[...].reshape(dst.shape).astype(BF16)


def _ffn(x, mod, ng, wg, wu, wd, mod_off, row0=0, n_rows=None, final_g=None, cast_jobs=(), next_mod=None, tm=1024):
    t = N_ROWS - row0 if n_rows is None else n_rows
    n_steps = t // tm
    final = final_g is not None
    whole = isinstance(x, tuple) or x.shape[0] == N_ROWS
    x_specs, x_args = _rows_specs(x, tm, row0 // tm if whole else 0)
    in_specs = x_specs + [_mod_spec(tm, row0 // tm),
                          _resident((1, D_MODEL), (0, 0)),
                          _resident((D_MODEL, FFN_DIM), (0, 0)),
                          _resident((D_MODEL, FFN_DIM), (0, 0)),
                          _resident((FFN_DIM, D_MODEL), (0, 0))]
    args = x_args + [mod, ng, wg, wu, wd]
    if final:
        in_specs.append(_resident((1, D_MODEL), (0, 0)))
        args.append(final_g)
    out_shape = [jax.ShapeDtypeStruct((t, D_MODEL), F32)]
    out_specs = [pl.BlockSpec((tm, D_MODEL), lambda i: (i, 0))]
    job_in, job_args, job_shapes, job_out = _job_specs(cast_jobs, n_steps)
    in_specs += job_in
    args += job_args
    out_shape += job_shapes
    out_specs += job_out
    if next_mod is not None:
        mod_in, mod_out = _mod_specs(next_mod[3], n_steps)
        in_specs += mod_in
        args += list(next_mod[:3])
        out_shape.append(jax.ShapeDtypeStruct((MOD_ROWS, N_MOD * D_MODEL), F32))
        out_specs.append(mod_out)
    outs = pl.pallas_call(
        functools.partial(_ffn_body, mod_off=mod_off, final=final, n_jobs=len(cast_jobs), n_x=len(x_args),
                          mod_job=next_mod is not None),
        out_shape=out_shape,
        grid=(n_steps,),
        in_specs=in_specs,
        out_specs=out_specs,
        scratch_shapes=[pltpu.VMEM((tm, FFN_DIM), BF16)],
        compiler_params=_params(1),
        name="ffn",
    )(*args)
    return outs[0], list(outs[1:])


def _proj_body(x_ref, mod_ref, ng_ref, wat_ref, wbt_ref, wt_ref, cw_ref, cb_ref, qn_ref, kvn_ref,
               wq_ref, wkat_ref, wkbt_ref, wuvt_ref, *rest, seq_len, latent, n_alias, own_slot):
    rest = rest[n_alias:]

    def put(ref, b, value):
        for k in range(ref.shape[1]):
            ref[b, k] = value if k == own_slot else jnp.zeros_like(value)

    if latent:
        (qc_ref, qs_ref, kct_ref, kst_ref, wqs_ref,
         yc_ref, qna_ref, knat_ref, vnat_ref, qm_ref, kmt_ref, vmt_ref) = rest
    else:
        (yc_ref, knat_ref, vnat_ref, ckv_ref, krt_ref, ona_ref, om_ref, qna_ref, qm_ref, kmt_ref, vmt_ref) = rest
    tm = x_ref.shape[0]
    x = x_ref[...]
    mod = mod_ref[...]
    h = (_rms(x, ng_ref[...]) * (1 + mod[4:5]) + mod[3:4]).astype(BF16)

    pos = lax.broadcasted_iota(jnp.int32, (tm, 1), 0) % seq_len
    cw = cw_ref[...]
    for ch in range(CONV_DIM // CONV_CHUNK):
        cols = slice(ch * CONV_CHUNK, (ch + 1) * CONV_CHUNK)
        part = lambda k: _dot_nt(h, wat_ref[k * CONV_DIM + ch * CONV_CHUNK:k * CONV_DIM + (ch + 1) * CONV_CHUNK, :])
        v = part(1) * part(2)
        v_prev = jnp.where(pos == 0, 0.0, pltpu.roll(v, 1, 0))
        v_next = jnp.where(pos == seq_len - 1, 0.0, pltpu.roll(v, tm - 1, 0))
        y = cb_ref[:, cols] + v_prev * cw[0:1, cols]
        y = y + v * cw[1:2, cols]
        y = y + v_next * cw[2:3, cols]
        yc_ref[:, cols] = (part(0) * y).astype(BF16)

    qna_ref[...] = (_dot_nt(h, wat_ref[3 * CONV_DIM:3 * CONV_DIM + HD, :]) * NA_SCALE).astype(BF16)

    kt_na = _dot_nt(wt_ref[0:HD, :], h)
    ut = _dot_nt(wt_ref[HD:, :], h)
    vt_na = ut[0:HD]
    krt = ut[HD:HD + MLA_ROPE]

    u = _dot_nt(h, wbt_ref[...])
    cq = _rms(u[:, 0:Q_LORA], qn_ref[...]).astype(BF16)
    ckv = _rms(u[:, Q_LORA:Q_LORA + KV_LORA], kvn_ref[...])
    ckv_b = ckv.astype(BF16)
    if latent:
        knat_ref[...] = kt_na.astype(BF16)
        vnat_ref[...] = vt_na.astype(BF16)
        krt = krt * kct_ref[...] + ut[HD + MLA_ROPE:HD + 2 * MLA_ROPE] * kst_ref[...]
        qc = jnp.concatenate([qc_ref[...]] * 2, axis=1)
        qs = jnp.concatenate([qs_ref[...]] * 2, axis=1)
        for hp in range(HEADS // 2):
            cols = slice(hp * 2 * MLA_QK_PAD, (hp + 1) * 2 * MLA_QK_PAD)
            q_rot = _dot(cq, wq_ref[:, cols]) * qc + _dot(cq, wqs_ref[:, cols]) * qs
            qm_ref[:, cols] = (q_rot * MLA_SCALE).astype(BF16)
    else:
        qm_ref[...] = (_dot(cq, wq_ref[...]) * MLA_SCALE).astype(BF16)
    kmt = _dot_nt(wkat_ref[...], ckv_b) + _dot(wkbt_ref[...], krt.astype(BF16))
    vmt = _dot_nt(wuvt_ref[...], ckv_b)
    if latent:
        kmt_ref[...] = kmt.astype(BF16)
        vmt_ref[...] = vmt.astype(BF16)
    else:
        for b in range(tm // seq_len):
            rows = slice(b * seq_len, (b + 1) * seq_len)
            put(knat_ref, b, kt_na[:, rows].reshape(HEADS, NA_HD, seq_len))
            put(vnat_ref, b, vt_na[:, rows].reshape(HEADS, NA_HD, seq_len))
            put(ckv_ref, b, ckv[rows])
            put(krt_ref, b, krt[:, rows])
            kmt_ref[b] = kmt[:, rows].astype(BF16)
            vmt_ref[b] = vmt[:, rows].astype(BF16)
        _ctx_attention(qna_ref, knat_ref, vnat_ref, qm_ref, kmt_ref, vmt_ref, ona_ref, om_ref, seq_len, own_slot)


def _proj(x, mod, w, layer, seq_len, latent, tables, caches, tm):
    row0 = N_CTX_ROWS if latent else 0
    t = N_ROWS - N_CTX_ROWS if latent else N_CTX_ROWS
    x_tile0 = row0 // tm if x.shape[0] == N_ROWS else 0
    n_seq = t // seq_len
    seq_per_tile = tm // seq_len
    wq_cols = w["wq"].shape[-1]
    wt_rows = w["wt"].shape[-2]
    row = lambda n: pl.BlockSpec((tm, n), lambda i: (i, 0))
    in_specs = [pl.BlockSpec((tm, D_MODEL), lambda i: (i + x_tile0, 0)),
                _mod_spec(tm, row0 // tm),
                _resident((None, 1, D_MODEL), (layer, 0, 0)),
                _resident((3 * CONV_DIM + HD, D_MODEL), (0, 0)),
                _resident((None, Q_LORA + KV_LORA, D_MODEL), (layer, 0, 0)),
                _resident((None, wt_rows, D_MODEL), (layer, 0, 0)),
                _resident((None, CONV_K, CONV_DIM), (layer, 0, 0)),
                _resident((None, 1, CONV_DIM), (layer, 0, 0)),
                _resident((None, 1, Q_LORA), (layer, 0, 0)),
                _resident((None, 1, KV_LORA), (layer, 0, 0)),
                _resident((None, Q_LORA, wq_cols), (layer, 0, 0)),
                _resident((None, MLA_QK_W, KV_LORA), (layer, 0, 0)),
                _resident((None, MLA_QK_W, MLA_ROPE), (layer, 0, 0)),
                _resident((None, HD, KV_LORA), (layer, 0, 0))]
    args = [x, mod, w["ng1"], w["wat"], w["wbt"], w["wt"], w["conv_w"], w["conv_b"], w["qnorm"], w["kvnorm"],
            w["wq"], w["wkat"], w["wkbt"], w["wuvt"]]
    out_shape = [jax.ShapeDtypeStruct((t, CONV_DIM), BF16)]
    out_specs = [row(CONV_DIM)]
    scratch = []
    aliases = {}
    n_alias = 0
    own_slot = 0
    if latent:
        out_shape.append(jax.ShapeDtypeStruct((t, HD), BF16))
        out_specs.append(row(HD))
        assert tm == seq_len
        in_specs += [_resident((seq_len, MLA_QK_PAD), (0, 0)), _resident((seq_len, MLA_QK_PAD), (0, 0)),
                     _resident((MLA_ROPE, seq_len), (0, 0)), _resident((MLA_ROPE, seq_len), (0, 0)),
                     _resident((None, Q_LORA, MLA_QK_W), (layer, 0, 0))]
        args += list(tables) + [w["wqs"]]
        seq_blk = lambda n: pl.BlockSpec((None, n, seq_len), lambda i: (i, 0, 0))
        out_shape += [jax.ShapeDtypeStruct((n_seq, HD, seq_len), BF16),
                      jax.ShapeDtypeStruct((n_seq, HD, seq_len), BF16),
                      jax.ShapeDtypeStruct((t, MLA_QK_W), BF16),
                      jax.ShapeDtypeStruct((n_seq, MLA_QK_W, seq_len), BF16),
                      jax.ShapeDtypeStruct((n_seq, HD, seq_len), BF16)]
        out_specs += [seq_blk(HD), seq_blk(HD), row(MLA_QK_W), seq_blk(MLA_QK_W), seq_blk(HD)]
    else:
        if caches is not None:
            n_alias = len(caches)
            in_specs += [pl.BlockSpec(memory_space=pl.ANY)] * n_alias
            args += list(caches)
            aliases = {len(args) - n_alias + k: 1 + k for k in range(n_alias)}
            n_slots, first_slot = 1, layer
        else:
            assert layer == 0
            n_slots, first_slot, own_slot = DEPTH, 0, layer
        cache_blk = lambda *dims: pl.BlockSpec((seq_per_tile, n_slots) + dims,
                                               lambda i: (i, first_slot) + (0,) * len(dims))
        out_shape += [jax.ShapeDtypeStruct((n_seq, DEPTH, HEADS, NA_HD, seq_len), F32),
                      jax.ShapeDtypeStruct((n_seq, DEPTH, HEADS, NA_HD, seq_len), F32),
                      jax.ShapeDtypeStruct((n_seq, DEPTH, seq_len, KV_LORA), F32),
                      jax.ShapeDtypeStruct((n_seq, DEPTH, MLA_ROPE, seq_len), F32),
                      jax.ShapeDtypeStruct((t, HD), BF16),
                      jax.ShapeDtypeStruct((t, HD), BF16)]
        out_specs += [cache_blk(HEADS, NA_HD, seq_len), cache_blk(HEADS, NA_HD, seq_len),
                      cache_blk(seq_len, KV_LORA), cache_blk(MLA_ROPE, seq_len), row(HD), row(HD)]
        scratch = [pltpu.VMEM((tm, HD), BF16), pltpu.VMEM((tm, MLA_QK_W), BF16),
                   pltpu.VMEM((seq_per_tile, MLA_QK_W, seq_len), BF16), pltpu.VMEM((seq_per_tile, HD, seq_len), BF16)]
    return pl.pallas_call(
        functools.partial(_proj_body, seq_len=seq_len, latent=latent, n_alias=n_alias, own_slot=own_slot),
        out_shape=out_shape,
        grid=(t // tm,),
        in_specs=in_specs,
        out_specs=out_specs,
        scratch_shapes=scratch,
        input_output_aliases=aliases,
        compiler_params=_params(1),
        name="mixer_proj",
    )(*args)


def _softmax_pv(s, vt):
    m = jnp.max(s, axis=-1, keepdims=True)
    p = jnp.exp2(s - m)
    den = jnp.sum(p, axis=-1, keepdims=True)
    return _dot_nt(p.astype(BF16), vt) / den


def _pair_slot(x, j):
    z = jnp.zeros_like(x)
    return jnp.concatenate([x, z] if j == 0 else [z, x], axis=0)


def _ctx_attention(qna_ref, knat_ref, vnat_ref, qm_ref, kmt_ref, vmt_ref, ona_ref, om_ref, seq_len, slot):
    for b in range(qna_ref.shape[0] // seq_len):
        rows = slice(b * seq_len, (b + 1) * seq_len)
        for hp in range(HEADS // 2):
            pair = slice(hp * 2 * HEAD_V, (hp + 1) * 2 * HEAD_V)
            q = qna_ref[rows, pair]
            o_na = o_m = None
            for j in range(2):
                hh = 2 * hp + j
                kt = _pair_slot(knat_ref[b, slot, hh].astype(BF16), j)
                vt = _pair_slot(vnat_ref[b, slot, hh].astype(BF16), j)
                o = _softmax_pv(_dot(q, kt), vt)
                o_na = o if j == 0 else o_na + o
                qk = slice(hh * MLA_QK_PAD, (hh + 1) * MLA_QK_PAD)
                vt = _pair_slot(vmt_ref[b, hh * HEAD_V:(hh + 1) * HEAD_V, :], j)
                o = _softmax_pv(_dot(qm_ref[rows, qk], kmt_ref[b, qk, :]), vt)
                o_m = o if j == 0 else o_m + o
            ona_ref[rows, pair] = o_na.astype(BF16)
            om_ref[rows, pair] = o_m.astype(BF16)


def _attn_lat_body(qna_ref, knat_ref, vnat_ref, kctx_ref, vctx_ref, bias_ref,
                   qm_ref, kmt_ref, vmt_ref, cckv_ref, ckrt_ref, wkat_ref, wkbt_ref, wuvt_ref, ona_ref, om_ref):
    cat = lambda *a: jnp.concatenate(a, axis=1)
    head = lambda h: slice(h * HEAD_V, (h + 1) * HEAD_V)
    cckv = cckv_ref[...].astype(BF16)
    ckrt = ckrt_ref[...].astype(BF16)
    for pr in range(LAT_HEADS // 2):
        pair = slice(pr * 2 * HEAD_V, (pr + 1) * 2 * HEAD_V)
        kc = [_pair_slot(kctx_ref[2 * pr + j].astype(BF16), j) for j in range(2)]
        vc = [_pair_slot(vctx_ref[2 * pr + j].astype(BF16), j) for j in range(2)]
        for c, (start, count) in enumerate(NA_WINDOWS):
            rows = slice(c * Q_CHUNK, (c + 1) * Q_CHUNK)
            keys = slice(start, start + count)
            q = qna_ref[rows, pair]
            for j in range(2):
                h = 2 * pr + j
                bias = jnp.concatenate([cat(*[bias_ref[h, p] for p in NA_BLOCK_INDEX[c][rl]])
                                        for rl in range(Q_CHUNK // GRID_W)], axis=0)
                s = cat(_dot(q, kc[j]), _dot(q, _pair_slot(knat_ref[head(h), keys], j)) + bias)
                o = _softmax_pv(s, cat(vc[j], _pair_slot(vnat_ref[head(h), keys], j)))
                o_na = o if j == 0 else o_na + o
            ona_ref[rows, pair] = o_na.astype(BF16)
        qk_pair = slice(pr * 2 * MLA_QK_PAD, (pr + 1) * 2 * MLA_QK_PAD)
        km_ctx = (_dot_nt(wkat_ref[qk_pair, :], cckv) + _dot(wkbt_ref[qk_pair, :], ckrt)).astype(BF16)
        vm_ctx = _dot_nt(wuvt_ref[pair, :], cckv).astype(BF16)
        kt, vt = [], []
        for j in range(2):
            h = 2 * pr + j
            qk = slice(h * MLA_QK_PAD, (h + 1) * MLA_QK_PAD)
            kt.append(cat(km_ctx[j * MLA_QK_PAD:(j + 1) * MLA_QK_PAD, :], kmt_ref[qk, :]))
            vt.append(_pair_slot(cat(vm_ctx[head(j), :], vmt_ref[head(h), :]), j))
        for c in range(DEC_SEQ // MLA_Q_ROWS):
            rows = slice(c * MLA_Q_ROWS, (c + 1) * MLA_Q_ROWS)
            for j in range(2):
                h = 2 * pr + j
                o = _softmax_pv(_dot(qm_ref[rows, h * MLA_QK_PAD:(h + 1) * MLA_QK_PAD], kt[j]), vt[j])
                o_m = o if j == 0 else o_m + o
            om_ref[rows, pair] = o_m.astype(BF16)


def _attn_lat(qna, knat, vnat, kctx, vctx, bias, qm, kmt, vmt, cache_ckv, cache_krt, wkat, wkbt, wuvt, layer):
    t = qna.shape[0]
    s = DEC_SEQ
    n = LAT_HEADS
    return pl.pallas_call(
        _attn_lat_body,
        out_shape=[jax.ShapeDtypeStruct((t, HD), BF16), jax.ShapeDtypeStruct((t, HD), BF16)],
        grid=(HEADS // n, t // s),
        in_specs=[pl.BlockSpec((s, n * NA_HD), lambda hg, b: (b, hg)),
                  pl.BlockSpec((None, n * NA_HD, s), lambda hg, b: (b, hg, 0)),
                  pl.BlockSpec((None, n * HEAD_V, s), lambda hg, b: (b, hg, 0)),
                  pl.BlockSpec((None, None, n, NA_HD, PAST_LEN), lambda hg, b: (b, layer, hg, 0, 0)),
                  pl.BlockSpec((None, None, n, NA_HD, PAST_LEN), lambda hg, b: (b, layer, hg, 0, 0)),
                  pl.BlockSpec((None, n, len(NA_BLOCK_PAIRS), GRID_W, 2 * GRID_W),
                               lambda hg, b: (layer, hg, 0, 0, 0)),
                  pl.BlockSpec((s, n * MLA_QK_PAD), lambda hg, b: (b, hg)),
                  pl.BlockSpec((None, n * MLA_QK_PAD, s), lambda hg, b: (b, hg, 0)),
                  pl.BlockSpec((None, n * HEAD_V, s), lambda hg, b: (b, hg, 0)),
                  pl.BlockSpec((None, None, PAST_LEN, KV_LORA), lambda hg, b: (b, layer, 0, 0)),
                  pl.BlockSpec((None, None, MLA_ROPE, PAST_LEN), lambda hg, b: (b, layer, 0, 0)),
                  pl.BlockSpec((None, n * MLA_QK_PAD, KV_LORA), lambda hg, b: (layer, hg, 0)),
                  pl.BlockSpec((None, n * MLA_QK_PAD, MLA_ROPE), lambda hg, b: (layer, hg, 0)),
                  pl.BlockSpec((None, n * HEAD_V, KV_LORA), lambda hg, b: (layer, hg, 0))],
        out_specs=[pl.BlockSpec((s, n * HEAD_V), lambda hg, b: (b, hg)),
                   pl.BlockSpec((s, n * HEAD_V), lambda hg, b: (b, hg))],
        compiler_params=_params(2),
        name="attn_lat",
    )(qna, knat, vnat, kctx, vctx, bias, qm, kmt, vmt, cache_ckv, cache_krt, wkat, wkbt, wuvt)


def _mix_body(*refs, n_parts):
    tm = refs[-1].shape[0]
    parts, pos = [], 0
    for n in n_parts:
        parts.append(_rows_value(refs[pos:pos + n], tm))
        pos += n
    x, yc, ona, om = parts
    mod_ref, ng_ref, wgt_ref, wco_ref, wno_ref, wmo_ref, wo_ref, o_ref, z_scr = refs[pos:]
    mod = mod_ref[...]
    h = (_rms(x, ng_ref[...]) * (1 + mod[4:5]) + mod[3:4]).astype(BF16)
    for ch in range(D_MODEL // MIX_CHUNK):
        cols = slice(ch * MIX_CHUNK, (ch + 1) * MIX_CHUNK)
        gate = lambda k: jax.nn.sigmoid(
            _dot_nt(h, wgt_ref[k * D_MODEL + ch * MIX_CHUNK:k * D_MODEL + (ch + 1) * MIX_CHUNK, :]))
        z = gate(0) * _dot(yc, wco_ref[:, cols])
        z = z + gate(1) * _dot(ona, wno_ref[:, cols])
        z = z + gate(2) * _dot(om, wmo_ref[:, cols])
        z_scr[:, cols] = z.astype(BF16)
    o_ref[...] = x + mod[5:6] * _dot(z_scr[...], wo_ref[...])


def _mix(x, yc, ona, om, mod, w, layer):
    tm = 512 if isinstance(x, tuple) else 1024
    in_specs, args, n_parts = [], [], []
    for a in (x, yc, ona, om):
        specs, ops = _rows_specs(a, tm, 0)
        in_specs += specs
        args += ops
        n_parts.append(len(ops))
    in_specs += [_mod_spec(tm, 0),
                 _resident((None, 1, D_MODEL), (layer, 0, 0)),
                 _resident((3 * D_MODEL, D_MODEL), (0, 0)),
                 _resident((CONV_DIM, D_MODEL), (0, 0)),
                 _resident((HD, D_MODEL), (0, 0)),
                 _resident((HD, D_MODEL), (0, 0)),
                 _resident((D_MODEL, D_MODEL), (0, 0))]
    args += [mod, w["ng1"], w["wgt"], w["wco"], w["wno"], w["wmo"], w["wo"]]
    return pl.pallas_call(
        functools.partial(_mix_body, n_parts=tuple(n_parts)),
        out_shape=jax.ShapeDtypeStruct((N_ROWS, D_MODEL), F32),
        grid=(N_ROWS // tm,),
        in_specs=in_specs,
        out_specs=pl.BlockSpec((tm, D_MODEL), lambda i: (i, 0)),
        scratch_shapes=[pltpu.VMEM((tm, D_MODEL), BF16)],
        compiler_params=_params(1),
        name="mixer_out",
    )(*args)


def _rope_tables():
    f32 = np.float32
    half = MLA_ROPE // 2
    nf = half // 2
    inv = (f32(1.0) / (f32(ROPE_BASE) ** (np.arange(nf, dtype=f32) / f32(nf)))).astype(f32)
    t = np.arange(DEC_SEQ)
    rows = (t // GRID_W).astype(f32)[:, None] * inv[None, :]
    cols = (t % GRID_W).astype(f32)[:, None] * inv[None, :]
    cos = np.concatenate([np.cos(rows), np.cos(rows), np.cos(cols), np.cos(cols)], axis=-1).astype(f32)
    sin = np.concatenate([np.sin(rows), np.sin(rows), np.sin(cols), np.sin(cols)], axis=-1).astype(f32)
    pad = MLA_QK_PAD - MLA_NOPE - MLA_ROPE
    q_cos = np.concatenate([np.ones((DEC_SEQ, MLA_NOPE), f32), cos, np.zeros((DEC_SEQ, pad), f32)], axis=-1)
    q_sin = np.concatenate([np.zeros((DEC_SEQ, MLA_NOPE), f32), sin, np.zeros((DEC_SEQ, pad), f32)], axis=-1)
    return tuple(jnp.asarray(a) for a in (q_cos, q_sin, np.ascontiguousarray(cos.T), np.ascontiguousarray(sin.T)))


def _rope_swap(w):
    nf = MLA_ROPE // 4
    a, b, c, d = (w[..., i * nf:(i + 1) * nf] for i in range(4))
    return jnp.concatenate([-b, a, -d, c], axis=-1)


def _na_bias(rpb):
    n_dc = 2 * NA_WIN_C - 1
    col = np.arange(GRID_W)
    c_start = np.clip(col - NA_WIN_C // 2, 0, GRID_W - NA_WIN_C)
    c_in = (col[None, :] >= c_start[:, None]) & (col[None, :] < c_start[:, None] + NA_WIN_C)
    dc = np.clip(col[None, :] - col[:, None] + (NA_WIN_C - 1), 0, n_dc - 1)
    pick_dc = (dc[None] == np.arange(n_dc)[:, None, None]).astype(np.float32)
    n_pairs = len(NA_BLOCK_PAIRS)
    pick_dr = np.zeros((n_pairs, 2, NA_DR_MASKED), np.float32)
    for p, pair in enumerate(NA_BLOCK_PAIRS):
        for side, d in enumerate(pair):
            if d != NA_DR_MASKED:
                pick_dr[p, side, d] = 1.0
    keep = pick_dr.sum(-1).astype(bool)[:, None, :, None] & c_in[None, :, None, :]
    keep = keep.reshape(n_pairs, GRID_W, 2 * GRID_W)
    pick_side_dc = np.zeros((2, n_dc, GRID_W, 2, GRID_W), np.float32)
    for side in range(2):
        pick_side_dc[side, :, :, side, :] = pick_dc
    pick_side_dc = pick_side_dc.reshape(2, n_dc, GRID_W, 2 * GRID_W)
    hi = lax.Precision.HIGHEST
    by_row = jnp.einsum("psd,lhdj->lhpsj", jnp.asarray(pick_dr), rpb, precision=hi)
    blocks = jnp.einsum("lhpsj,sjqn->lhpqn", by_row, jnp.asarray(pick_side_dc), precision=hi)
    return jnp.where(jnp.asarray(keep), blocks * LOG2_E, NEG_INF)


def _pack_weights(w_int, w_uq, w_ukv):
    t_last = lambda a: jnp.swapaxes(a, -1, -2)
    w_mid = lax.optimization_barrier(w_int[:, W_IN_KV:W_IN_GATE]).astype(BF16)
    w_kvt = w_mid[:, :W_IN_LORA - W_IN_KV]
    w_krt = w_mid[:, W_IN_KR - W_IN_KV:]
    wt = jnp.concatenate([w_kvt, w_krt, t_last(_rope_swap(t_last(w_krt)))], axis=1)
    wbt = w_mid[:, W_IN_LORA - W_IN_KV:W_IN_KR - W_IN_KV]
    uq = w_uq.reshape(DEPTH, Q_LORA, MLA_HEADS, MLA_NOPE + MLA_ROPE)
    pad = MLA_QK_PAD - MLA_NOPE - MLA_ROPE
    zp = jnp.zeros(uq.shape[:-1] + (pad,), F32)
    zn = jnp.zeros(uq.shape[:-1] + (MLA_NOPE,), F32)
    q_ext = jnp.concatenate([uq, zp], axis=-1).reshape(DEPTH, Q_LORA, MLA_QK_W)
    q_sw = jnp.concatenate([zn, _rope_swap(uq[..., MLA_NOPE:]), zp], axis=-1).reshape(q_ext.shape)
    ukv = w_ukv.reshape(DEPTH, KV_LORA, MLA_HEADS, MLA_NOPE + MLA_V)
    zk = jnp.zeros(ukv.shape[:-1] + (MLA_QK_PAD - MLA_NOPE,), F32)
    wka = jnp.concatenate([ukv[..., :MLA_NOPE], zk], axis=-1).reshape(DEPTH, KV_LORA, MLA_QK_W)
    eye = jnp.concatenate([jnp.zeros((MLA_ROPE, MLA_NOPE), F32), jnp.eye(MLA_ROPE, dtype=F32),
                           jnp.zeros((MLA_ROPE, pad), F32)], axis=-1)
    wkb = jnp.broadcast_to(jnp.tile(eye, (1, MLA_HEADS))[None], (DEPTH, MLA_ROPE, MLA_QK_W))
    wuv = ukv[..., MLA_NOPE:].reshape(DEPTH, KV_LORA, HD)
    b = lambda a: a.astype(BF16)
    return dict(wt=b(wt), wbt=b(wbt),
                wq=b(q_ext), wqs=b(q_sw),
                wkat=b(t_last(wka)), wkbt=b(t_last(wkb)), wuvt=b(t_last(wuv)))


def kernel(x_prompt, x_sample, cache_na_k, cache_na_v, cache_mla_ckv, cache_mla_krope, c, c_ctx,
           w_ada, b_ada, norm_g, w_ffn1_gate, w_ffn1_up, w_ffn1_down, w_ffn2_gate, w_ffn2_up, w_ffn2_down,
           w_in, conv_w, conv_b, na_rpb, mla_qnorm, w_uq, mla_kvnorm, w_ukv,
           w_conv_out, w_na_out, w_mla_out, w_o, final_g):
    t_last = lambda a: jnp.swapaxes(a, -1, -2)
    w_int = t_last(w_in)
    packed = _pack_weights(w_int, w_uq, w_ukv)
    shared = dict(conv_w=conv_w, conv_b=conv_b.reshape(DEPTH, 1, CONV_DIM),
                  qnorm=mla_qnorm.reshape(DEPTH, 1, Q_LORA), kvnorm=mla_kvnorm.reshape(DEPTH, 1, KV_LORA),
                  ng1=norm_g[:, 1:2],
                  **{k: packed[k] for k in ("wt", "wbt", "wkat", "wkbt", "wuvt")})
    ffn1_f32 = (w_ffn1_gate, w_ffn1_up, w_ffn1_down)
    ffn2_f32 = (w_ffn2_gate, w_ffn2_up, w_ffn2_down)
    ffn1_w = {}
    ffn2_w = {}
    mixer_w = {}
    mixer_srcs = ((w_int, 0, W_IN_KV), (w_int, W_IN_GATE, 3 * D_MODEL),
                  (w_conv_out, 0, None), (w_na_out, 0, None), (w_mla_out, 0, None), (w_o, 0, None))
    final_row = final_g.reshape(1, D_MODEL)

    c_all = jnp.concatenate([c_ctx[None], c, jnp.zeros((MOD_ROWS - 1 - DEC_BATCH, D_MODEL), F32)], axis=0)
    b_ada3 = b_ada.reshape(DEPTH, 1, N_MOD * D_MODEL)
    as_table = lambda m: m.reshape(MOD_ROWS, N_MOD, D_MODEL)
    mod0, cast = _modulation(c_all, w_ada, b_ada3, 0, [_cast_job(w, 0) for w in ffn1_f32])
    mod = {0: as_table(mod0)}
    ffn1_w[0] = tuple(cast)

    tables = _rope_tables()
    na_bias = _na_bias(na_rpb)
    ctx_k_na = t_last(cache_na_k)
    ctx_v_na = t_last(cache_na_v)
    ctx_krt = t_last(cache_mla_krope)

    xp = x_prompt.reshape(N_CTX_ROWS, D_MODEL)
    xs = x_sample.reshape(N_ROWS - N_CTX_ROWS, D_MODEL)
    caches = None
    x_all = None
    for l in range(DEPTH):
        last = l == DEPTH - 1
        jobs = [_cast_job(w, l) for w in ffn2_f32]
        if l == 0:
            jobs += [_cast_job(a, ll, r0, n) for ll in range(DEPTH) for a, r0, n in mixer_srcs]
            xp, cast = _ffn(xp, mod[l], norm_g[l, 0:1], *ffn1_w[l], 0, n_rows=N_CTX_ROWS, cast_jobs=jobs, tm=512)
            xs, side = _ffn(xs, mod[l], norm_g[l, 0:1], *ffn1_w[l], 0, row0=N_CTX_ROWS,
                            next_mod=(c_all, w_ada, b_ada3, 1) if DEPTH > 1 else None)
            if side:
                mod[1] = as_table(side[0])
            x1 = (xp, xs)
        else:
            if l not in mod:
                mod[l] = as_table(_modulation(c_all, w_ada, b_ada3, l)[0])
            x1, cast = _ffn(x_all, mod[l], norm_g[l, 0:1], *ffn1_w[l], 0, cast_jobs=jobs)
            xp = xs = x1
        ffn2_w[l] = tuple(cast[:3])
        for ll in range(DEPTH if l == 0 else 0):
            names = ("wat", "wgt", "wco", "wno", "wmo", "wo")
            mixer_w[ll] = dict(zip(names, cast[3 + len(names) * ll:3 + len(names) * (ll + 1)]))
        w_ctx = dict(shared, wq=packed["wq"], **mixer_w[l])
        w_lat = dict(w_ctx, wqs=packed["wqs"])
        yc_p, knat, vnat, ckv, krt, ona_p, om_p = _proj(xp, mod[l], w_ctx, l, SEQ, False, None, caches, 1024)
        caches = (knat, vnat, ckv, krt)
        yc_s, qna, knat, vnat, qm, kmt, vmt = _proj(xs, mod[l], w_lat, l, DEC_SEQ, True, tables, None, DEC_SEQ)
        ona_s, om_s = _attn_lat(qna, knat, vnat, ctx_k_na, ctx_v_na, na_bias, qm, kmt, vmt, cache_mla_ckv, ctx_krt,
                                packed["wkat"], packed["wkbt"], packed["wuvt"], l)
        x2 = _mix(x1, (yc_p, yc_s), (ona_p, ona_s), (om_p, om_s), mod[l], w_ctx, l)
        if not last:
            x_all, cast = _ffn(x2, mod[l], norm_g[l, 2:3], *ffn2_w[l], 6,
                               cast_jobs=[_cast_job(w, l + 1) for w in ffn1_f32])
            ffn1_w[l + 1] = tuple(cast)
        else:
            yp, _ = _ffn(x2, mod[l], norm_g[l, 2:3], *ffn2_w[l], 6, n_rows=N_CTX_ROWS, final_g=final_row)
            ys, _ = _ffn(x2, mod[l], norm_g[l, 2:3], *ffn2_w[l], 6, row0=N_CTX_ROWS, final_g=final_row)
    new_kt, new_vt, new_ckv, new_krt = caches
    return (yp.reshape(BATCH, SEQ, D_MODEL), ys.reshape(DEC_BATCH, DEC_SEQ, D_MODEL),
            t_last(new_kt), t_last(new_vt), new_ckv, t_last(new_krt))
```
